```python
import math
import jax, jax.numpy as jnp
from jax import lax
import numpy as np

D_MODEL = 1024
BATCH = 8
SEQ = 8192
DEPTH = 4

N_MIXERS = 3
D_FF = 2816
RMS_EPS = 1e-6
S5_WIDTH = D_MODEL
S5_GROUP = 16
S5_GROUPS = S5_WIDTH // S5_GROUP
S5_STATE = 64
S5_CHUNK = 128
S5_DT_MIN = 1e-3
S5_DT_MAX = 1e-1
SB_HEAD_DIM = 64
SB_HEADS = D_MODEL // SB_HEAD_DIM
SB_BLOCK = 128
LRU_WIDTH = D_MODEL
LRU_BLOCK_WIDTH = 256
LRU_BLOCKS = LRU_WIDTH // LRU_BLOCK_WIDTH
LRU_CONV = 4
LRU_C = 8.0
N_S5 = len(range(0, DEPTH, N_MIXERS))
N_SB = len(range(1, DEPTH, N_MIXERS))
N_LRU = len(range(2, DEPTH, N_MIXERS))

kernel_name = "hybrid_s5_stickbreak_rglru_macaron"


def rms_norm(x, g):
    xf = x.astype(jnp.float32)
    y = xf * lax.rsqrt(jnp.mean(xf * xf, axis=-1, keepdims=True) + RMS_EPS)
    return (y * g.astype(jnp.float32)).astype(x.dtype)


def swiglu_ffn(h, w_in, w_out):
    gate, up = jnp.split(h @ w_in, 2, axis=-1)
    return (jax.nn.silu(gate) * up) @ w_out


def _linear_recurrence_op(left, right):
    a_l, b_l = left
    a_r, b_r = right
    return a_r * a_l, a_r * b_l + b_r


def s5_mixer(h, w_in, lam_re, lam_im, log_dt, b_re, b_im, c_re, c_im, d_skip, w_out):
    f32 = jnp.float32
    bsz, seq, _ = h.shape
    u = (h @ w_in).astype(f32)
    lam = lax.complex(lam_re.astype(f32), lam_im.astype(f32))
    lam_dt = lam * jnp.exp(log_dt.astype(f32))[:, None]
    lam_bar = jnp.exp(lam_dt)
    b_bar = ((lam_bar - 1.0) / lam)[:, :, None] * lax.complex(b_re.astype(f32), b_im.astype(f32))
    c = lax.complex(c_re.astype(f32), c_im.astype(f32))
    decay_pow = jnp.exp(lam_dt[None] * jnp.arange(1, S5_CHUNK + 1, dtype=f32)[:, None, None])
    n_chunks = seq // S5_CHUNK
    u_blocks = u.reshape(bsz, n_chunks, S5_CHUNK, S5_GROUPS, S5_GROUP).transpose(1, 0, 2, 3, 4)
    a_elems = jnp.broadcast_to(lam_bar, (bsz, S5_CHUNK, S5_GROUPS, S5_STATE))

    def chunk_step(state, u_blk):
        bu = jnp.einsum('gph,btgh->btgp', b_bar, u_blk.astype(jnp.complex64))
        _, s = lax.associative_scan(_linear_recurrence_op, (a_elems, bu), axis=1)
        s = s + decay_pow[None] * state[:, None]
        y_blk = jnp.einsum('ghp,btgp->btgh', c, s).real
        return s[:, -1], y_blk

    state0 = jnp.zeros((bsz, S5_GROUPS, S5_STATE), jnp.complex64)
    _, y = lax.scan(chunk_step, state0, u_blocks)
    y = y.transpose(1, 0, 2, 3, 4).reshape(bsz, seq, S5_WIDTH) + d_skip.astype(f32) * u
    z = jax.nn.gelu(y).astype(h.dtype)
    val, gate = jnp.split(z @ w_out, 2, axis=-1)
    return val * jax.nn.sigmoid(gate)


def stick_breaking_mixer(h, w_qkv, w_out):
    f32 = jnp.float32
    bsz, seq, _ = h.shape
    qkv = (h @ w_qkv).reshape(bsz, seq, 3, SB_HEADS, SB_HEAD_DIM)
    q = qkv[:, :, 0].astype(f32) * (SB_HEAD_DIM ** -0.5)
    k = qkv[:, :, 1].astype(f32)
    v = qkv[:, :, 2].astype(f32)
    n_blocks = seq // SB_BLOCK
    q_blocks = q.reshape(bsz, n_blocks, SB_BLOCK, SB_HEADS, SB_HEAD_DIM).transpose(1, 0, 2, 3, 4)
    key_pos = jnp.arange(seq)

    def attend(args):
        q_blk, blk = args
        logits = jnp.einsum('bqhd,bkhd->bhqk', q_blk, k)
        query_pos = blk * SB_BLOCK + jnp.arange(SB_BLOCK)
        causal = key_pos[None, :] < query_pos[:, None]
        log_keep = jnp.where(causal, jax.nn.log_sigmoid(-logits), 0.0)
        log_remaining = lax.cumsum(log_keep, axis=3, reverse=True) - log_keep
        weights = jnp.where(causal, jnp.exp(jax.nn.log_sigmoid(logits) + log_remaining), 0.0)
        return jnp.einsum('bhqk,bkhd->bqhd', weights, v)

    o = lax.map(attend, (q_blocks, jnp.arange(n_blocks)))
    o = o.transpose(1, 0, 2, 3, 4).reshape(bsz, seq, D_MODEL).astype(h.dtype)
    return o @ w_out


def rglru_mixer(h, w_in, conv_w, conv_b, w_a, b_a, w_x, b_x, lam, w_out):
    f32 = jnp.float32
    bsz, seq, _ = h.shape
    branch_gelu, branch_rnn = jnp.split(h @ w_in, 2, axis=-1)
    xc = lax.conv_general_dilated(
        branch_rnn, conv_w[:, None, :], window_strides=(1,), padding=[(LRU_CONV - 1, 0)],
        dimension_numbers=('NWC', 'WIO', 'NWC'), feature_group_count=LRU_WIDTH) + conv_b
    xb = xc.reshape(bsz, seq, LRU_BLOCKS, LRU_BLOCK_WIDTH)
    r = jax.nn.sigmoid(jnp.einsum('blnc,ncd->blnd', xb, w_a) + b_a).reshape(bsz, seq, LRU_WIDTH)
    i = jax.nn.sigmoid(jnp.einsum('blnc,ncd->blnd', xb, w_x) + b_x).reshape(bsz, seq, LRU_WIDTH)
    log_a = (-LRU_C * r.astype(f32)) * jax.nn.softplus(-lam.astype(f32))
    a = jnp.exp(log_a)
    gated_x = (i * xc).astype(f32) * jnp.sqrt(-jnp.expm1(2.0 * log_a))
    _, hseq = lax.associative_scan(_linear_recurrence_op, (a, gated_x), axis=1)
    y = (jax.nn.gelu(branch_gelu.astype(f32)) * hseq).astype(h.dtype)
    return y @ w_out


def _fwd_setup_inputs(seed: int = 0) -> dict:
    key = jax.random.key(seed)
    ks = iter(jax.random.split(key, 40))

    def dense(shape, fan_in):
        return jax.random.normal(next(ks), shape, jnp.float32) * (fan_in ** -0.5)

    def gain(shape):
        return 1.0 + 0.02 * jax.random.normal(next(ks), shape, jnp.float32)

    def small(shape):
        return 0.01 * jax.random.normal(next(ks), shape, jnp.float32)

    x = jax.random.normal(next(ks), (BATCH, SEQ, D_MODEL), jnp.float32)
    inp = {"x": x}
    inp["ffn1_norm"] = gain((DEPTH, D_MODEL))
    inp["ffn1_w_in"] = dense((DEPTH, D_MODEL, 2 * D_FF), D_MODEL)
    inp["ffn1_w_out"] = dense((DEPTH, D_FF, D_MODEL), D_FF)
    inp["mix_norm"] = gain((DEPTH, D_MODEL))
    inp["ffn2_norm"] = gain((DEPTH, D_MODEL))
    inp["ffn2_w_in"] = dense((DEPTH, D_MODEL, 2 * D_FF), D_MODEL)
    inp["ffn2_w_out"] = dense((DEPTH, D_FF, D_MODEL), D_FF)
    inp["final_norm"] = gain((D_MODEL,))
    inp["s5_w_in"] = dense((N_S5, D_MODEL, S5_WIDTH), D_MODEL)
    inp["s5_lam_re"] = -0.5 + small((N_S5, S5_GROUPS, S5_STATE))
    inp["s5_lam_im"] = (math.pi * jnp.arange(S5_STATE, dtype=jnp.float32))[None, None, :] + small((N_S5, S5_GROUPS, S5_STATE))
    inp["s5_log_dt"] = jax.random.uniform(next(ks), (N_S5, S5_GROUPS), jnp.float32,
                                          math.log(S5_DT_MIN), math.log(S5_DT_MAX))
    inp["s5_b_re"] = dense((N_S5, S5_GROUPS, S5_STATE, S5_GROUP), 2 * S5_GROUP)
    inp["s5_b_im"] = dense((N_S5, S5_GROUPS, S5_STATE, S5_GROUP), 2 * S5_GROUP)
    inp["s5_c_re"] = dense((N_S5, S5_GROUPS, S5_GROUP, S5_STATE), S5_STATE)
    inp["s5_c_im"] = dense((N_S5, S5_GROUPS, S5_GROUP, S5_STATE), S5_STATE)
    inp["s5_d"] = jax.random.normal(next(ks), (N_S5, S5_WIDTH), jnp.float32)
    inp["s5_w_out"] = dense((N_S5, S5_WIDTH, 2 * D_MODEL), S5_WIDTH)
    inp["sb_w_qkv"] = dense((N_SB, D_MODEL, 3 * D_MODEL), D_MODEL)
    inp["sb_w_out"] = dense((N_SB, D_MODEL, D_MODEL), D_MODEL)
    inp["lru_w_in"] = dense((N_LRU, D_MODEL, 2 * LRU_WIDTH), D_MODEL)
    inp["lru_conv_w"] = dense((N_LRU, LRU_CONV, LRU_WIDTH), LRU_CONV)
    inp["lru_conv_b"] = small((N_LRU, LRU_WIDTH))
    inp["lru_w_a"] = dense((N_LRU, LRU_BLOCKS, LRU_BLOCK_WIDTH, LRU_BLOCK_WIDTH), LRU_BLOCK_WIDTH)
    inp["lru_b_a"] = small((N_LRU, LRU_BLOCKS, LRU_BLOCK_WIDTH))
    inp["lru_w_x"] = dense((N_LRU, LRU_BLOCKS, LRU_BLOCK_WIDTH, LRU_BLOCK_WIDTH), LRU_BLOCK_WIDTH)
    inp["lru_b_x"] = small((N_LRU, LRU_BLOCKS, LRU_BLOCK_WIDTH))
    a0 = jax.random.uniform(next(ks), (N_LRU, LRU_WIDTH), jnp.float32, 0.9, 0.999)
    p = jnp.exp(jnp.log(a0) / LRU_C)
    inp["lru_lambda"] = jnp.log(p) - jnp.log1p(-p)
    inp["lru_w_out"] = dense((N_LRU, LRU_WIDTH, D_MODEL), LRU_WIDTH)
    return inp


def _fwd_reference(x, ffn1_norm, ffn1_w_in, ffn1_w_out, mix_norm, ffn2_norm, ffn2_w_in, ffn2_w_out, final_norm,
              s5_w_in, s5_lam_re, s5_lam_im, s5_log_dt, s5_b_re, s5_b_im, s5_c_re, s5_c_im, s5_d, s5_w_out,
              sb_w_qkv, sb_w_out,
              lru_w_in, lru_conv_w, lru_conv_b, lru_w_a, lru_b_a, lru_w_x, lru_b_x, lru_lambda, lru_w_out):
    h = x
    for layer in range(DEPTH):
        h = h + 0.5 * swiglu_ffn(rms_norm(h, ffn1_norm[layer]), ffn1_w_in[layer], ffn1_w_out[layer])
        hn = rms_norm(h, mix_norm[layer])
        kind = layer % N_MIXERS
        j = layer // N_MIXERS
        if kind == 0:
            mixed = s5_mixer(hn, s5_w_in[j], s5_lam_re[j], s5_lam_im[j], s5_log_dt[j], s5_b_re[j], s5_b_im[j],
                             s5_c_re[j], s5_c_im[j], s5_d[j], s5_w_out[j])
        elif kind == 1:
            mixed = stick_breaking_mixer(hn, sb_w_qkv[j], sb_w_out[j])
        else:
            mixed = rglru_mixer(hn, lru_w_in[j], lru_conv_w[j], lru_conv_b[j], lru_w_a[j], lru_b_a[j],
                                lru_w_x[j], lru_b_x[j], lru_lambda[j], lru_w_out[j])
        h = h + mixed
        h = h + 0.5 * swiglu_ffn(rms_norm(h, ffn2_norm[layer]), ffn2_w_in[layer], ffn2_w_out[layer])
    return rms_norm(h, final_norm)


import jax as _jax
import jax.numpy as _jnp

TWIN_FORMAT = 'train_step'
FWD_PARAMS = ['x', 'ffn1_norm', 'ffn1_w_in', 'ffn1_w_out', 'mix_norm', 'ffn2_norm', 'ffn2_w_in', 'ffn2_w_out', 'final_norm', 's5_w_in', 's5_lam_re', 's5_lam_im', 's5_log_dt', 's5_b_re', 's5_b_im', 's5_c_re', 's5_c_im', 's5_d', 's5_w_out', 'sb_w_qkv', 'sb_w_out', 'lru_w_in', 'lru_conv_w', 'lru_conv_b', 'lru_w_a', 'lru_b_a', 'lru_w_x', 'lru_b_x', 'lru_lambda', 'lru_w_out']
TWIN_WEIGHTS = ['ffn1_norm', 'ffn1_w_in', 'ffn1_w_out', 'mix_norm', 'ffn2_norm', 'ffn2_w_in', 'ffn2_w_out', 'final_norm', 's5_w_in', 's5_lam_re', 's5_lam_im', 's5_log_dt', 's5_b_re', 's5_b_im', 's5_c_re', 's5_c_im', 's5_d', 's5_w_out', 'sb_w_qkv', 'sb_w_out', 'lru_w_in', 'lru_conv_w', 'lru_conv_b', 'lru_w_a', 'lru_b_a', 'lru_w_x', 'lru_b_x', 'lru_lambda', 'lru_w_out']
TWIN_DIFF_INPUT = 'x'
TWIN_INPUTS = ['x', 'ffn1_norm', 'ffn1_w_in', 'ffn1_w_out', 'mix_norm', 'ffn2_norm', 'ffn2_w_in', 'ffn2_w_out', 'final_norm', 's5_w_in', 's5_lam_re', 's5_lam_im', 's5_log_dt', 's5_b_re', 's5_b_im', 's5_c_re', 's5_c_im', 's5_d', 's5_w_out', 'sb_w_qkv', 'sb_w_out', 'lru_w_in', 'lru_conv_w', 'lru_conv_b', 'lru_w_a', 'lru_b_a', 'lru_w_x', 'lru_b_x', 'lru_lambda', 'lru_w_out', 'loss_target', 'm_ffn1_norm', 'm_ffn1_w_in', 'm_ffn1_w_out', 'm_mix_norm', 'm_ffn2_norm', 'm_ffn2_w_in', 'm_ffn2_w_out', 'm_final_norm', 'm_s5_w_in', 'm_s5_lam_re', 'm_s5_lam_im', 'm_s5_log_dt', 'm_s5_b_re', 'm_s5_b_im', 'm_s5_c_re', 'm_s5_c_im', 'm_s5_d', 'm_s5_w_out', 'm_sb_w_qkv', 'm_sb_w_out', 'm_lru_w_in', 'm_lru_conv_w', 'm_lru_conv_b', 'm_lru_w_a', 'm_lru_b_a', 'm_lru_w_x', 'm_lru_b_x', 'm_lru_lambda', 'm_lru_w_out', 'v_ffn1_norm', 'v_ffn1_w_in', 'v_ffn1_w_out', 'v_mix_norm', 'v_ffn2_norm', 'v_ffn2_w_in', 'v_ffn2_w_out', 'v_final_norm', 'v_s5_w_in', 'v_s5_lam_re', 'v_s5_lam_im', 'v_s5_log_dt', 'v_s5_b_re', 'v_s5_b_im', 'v_s5_c_re', 'v_s5_c_im', 'v_s5_d', 'v_s5_w_out', 'v_sb_w_qkv', 'v_sb_w_out', 'v_lru_w_in', 'v_lru_conv_w', 'v_lru_conv_b', 'v_lru_w_a', 'v_lru_b_a', 'v_lru_w_x', 'v_lru_b_x', 'v_lru_lambda', 'v_lru_w_out']
TWIN_OUTPUTS = ['loss', 'grad_x', 'grad_ffn1_norm', 'grad_ffn1_w_in', 'grad_ffn1_w_out', 'grad_mix_norm', 'grad_ffn2_norm', 'grad_ffn2_w_in', 'grad_ffn2_w_out', 'grad_final_norm', 'grad_s5_w_in', 'grad_s5_lam_re', 'grad_s5_lam_im', 'grad_s5_log_dt', 'grad_s5_b_re', 'grad_s5_b_im', 'grad_s5_c_re', 'grad_s5_c_im', 'grad_s5_d', 'grad_s5_w_out', 'grad_sb_w_qkv', 'grad_sb_w_out', 'grad_lru_w_in', 'grad_lru_conv_w', 'grad_lru_conv_b', 'grad_lru_w_a', 'grad_lru_b_a', 'grad_lru_w_x', 'grad_lru_b_x', 'grad_lru_lambda', 'grad_lru_w_out', 'delta_ffn1_norm', 'delta_ffn1_w_in', 'delta_ffn1_w_out', 'delta_mix_norm', 'delta_ffn2_norm', 'delta_ffn2_w_in', 'delta_ffn2_w_out', 'delta_final_norm', 'delta_s5_w_in', 'delta_s5_lam_re', 'delta_s5_lam_im', 'delta_s5_log_dt', 'delta_s5_b_re', 'delta_s5_b_im', 'delta_s5_c_re', 'delta_s5_c_im', 'delta_s5_d', 'delta_s5_w_out', 'delta_sb_w_qkv', 'delta_sb_w_out', 'delta_lru_w_in', 'delta_lru_conv_w', 'delta_lru_conv_b', 'delta_lru_w_a', 'delta_lru_b_a', 'delta_lru_w_x', 'delta_lru_b_x', 'delta_lru_lambda', 'delta_lru_w_out', 'new_m_ffn1_norm', 'new_m_ffn1_w_in', 'new_m_ffn1_w_out', 'new_m_mix_norm', 'new_m_ffn2_norm', 'new_m_ffn2_w_in', 'new_m_ffn2_w_out', 'new_m_final_norm', 'new_m_s5_w_in', 'new_m_s5_lam_re', 'new_m_s5_lam_im', 'new_m_s5_log_dt', 'new_m_s5_b_re', 'new_m_s5_b_im', 'new_m_s5_c_re', 'new_m_s5_c_im', 'new_m_s5_d', 'new_m_s5_w_out', 'new_m_sb_w_qkv', 'new_m_sb_w_out', 'new_m_lru_w_in', 'new_m_lru_conv_w', 'new_m_lru_conv_b', 'new_m_lru_w_a', 'new_m_lru_b_a', 'new_m_lru_w_x', 'new_m_lru_b_x', 'new_m_lru_lambda', 'new_m_lru_w_out', 'new_v_ffn1_norm', 'new_v_ffn1_w_in', 'new_v_ffn1_w_out', 'new_v_mix_norm', 'new_v_ffn2_norm', 'new_v_ffn2_w_in', 'new_v_ffn2_w_out', 'new_v_final_norm', 'new_v_s5_w_in', 'new_v_s5_lam_re', 'new_v_s5_lam_im', 'new_v_s5_log_dt', 'new_v_s5_b_re', 'new_v_s5_b_im', 'new_v_s5_c_re', 'new_v_s5_c_im', 'new_v_s5_d', 'new_v_s5_w_out', 'new_v_sb_w_qkv', 'new_v_sb_w_out', 'new_v_lru_w_in', 'new_v_lru_conv_w', 'new_v_lru_conv_b', 'new_v_lru_w_a', 'new_v_lru_b_a', 'new_v_lru_w_x', 'new_v_lru_b_x', 'new_v_lru_lambda', 'new_v_lru_w_out']
TWIN_LEAF_KINDS = {'loss': 'loss', 'grad_x': 'grad_x', 'grad_ffn1_norm': 'grad_w', 'grad_ffn1_w_in': 'grad_w', 'grad_ffn1_w_out': 'grad_w', 'grad_mix_norm': 'grad_w', 'grad_ffn2_norm': 'grad_w', 'grad_ffn2_w_in': 'grad_w', 'grad_ffn2_w_out': 'grad_w', 'grad_final_norm': 'grad_w', 'grad_s5_w_in': 'grad_w', 'grad_s5_lam_re': 'grad_w', 'grad_s5_lam_im': 'grad_w', 'grad_s5_log_dt': 'grad_w', 'grad_s5_b_re': 'grad_w', 'grad_s5_b_im': 'grad_w', 'grad_s5_c_re': 'grad_w', 'grad_s5_c_im': 'grad_w', 'grad_s5_d': 'grad_w', 'grad_s5_w_out': 'grad_w', 'grad_sb_w_qkv': 'grad_w', 'grad_sb_w_out': 'grad_w', 'grad_lru_w_in': 'grad_w', 'grad_lru_conv_w': 'grad_w', 'grad_lru_conv_b': 'grad_w', 'grad_lru_w_a': 'grad_w', 'grad_lru_b_a': 'grad_w', 'grad_lru_w_x': 'grad_w', 'grad_lru_b_x': 'grad_w', 'grad_lru_lambda': 'grad_w', 'grad_lru_w_out': 'grad_w', 'delta_ffn1_norm': 'delta_w', 'delta_ffn1_w_in': 'delta_w', 'delta_ffn1_w_out': 'delta_w', 'delta_mix_norm': 'delta_w', 'delta_ffn2_norm': 'delta_w', 'delta_ffn2_w_in': 'delta_w', 'delta_ffn2_w_out': 'delta_w', 'delta_final_norm': 'delta_w', 'delta_s5_w_in': 'delta_w', 'delta_s5_lam_re': 'delta_w', 'delta_s5_lam_im': 'delta_w', 'delta_s5_log_dt': 'delta_w', 'delta_s5_b_re': 'delta_w', 'delta_s5_b_im': 'delta_w', 'delta_s5_c_re': 'delta_w', 'delta_s5_c_im': 'delta_w', 'delta_s5_d': 'delta_w', 'delta_s5_w_out': 'delta_w', 'delta_sb_w_qkv': 'delta_w', 'delta_sb_w_out': 'delta_w', 'delta_lru_w_in': 'delta_w', 'delta_lru_conv_w': 'delta_w', 'delta_lru_conv_b': 'delta_w', 'delta_lru_w_a': 'delta_w', 'delta_lru_b_a': 'delta_w', 'delta_lru_w_x': 'delta_w', 'delta_lru_b_x': 'delta_w', 'delta_lru_lambda': 'delta_w', 'delta_lru_w_out': 'delta_w', 'new_m_ffn1_norm': 'new_m', 'new_m_ffn1_w_in': 'new_m', 'new_m_ffn1_w_out': 'new_m', 'new_m_mix_norm': 'new_m', 'new_m_ffn2_norm': 'new_m', 'new_m_ffn2_w_in': 'new_m', 'new_m_ffn2_w_out': 'new_m', 'new_m_final_norm': 'new_m', 'new_m_s5_w_in': 'new_m', 'new_m_s5_lam_re': 'new_m', 'new_m_s5_lam_im': 'new_m', 'new_m_s5_log_dt': 'new_m', 'new_m_s5_b_re': 'new_m', 'new_m_s5_b_im': 'new_m', 'new_m_s5_c_re': 'new_m', 'new_m_s5_c_im': 'new_m', 'new_m_s5_d': 'new_m', 'new_m_s5_w_out': 'new_m', 'new_m_sb_w_qkv': 'new_m', 'new_m_sb_w_out': 'new_m', 'new_m_lru_w_in': 'new_m', 'new_m_lru_conv_w': 'new_m', 'new_m_lru_conv_b': 'new_m', 'new_m_lru_w_a': 'new_m', 'new_m_lru_b_a': 'new_m', 'new_m_lru_w_x': 'new_m', 'new_m_lru_b_x': 'new_m', 'new_m_lru_lambda': 'new_m', 'new_m_lru_w_out': 'new_m', 'new_v_ffn1_norm': 'new_v', 'new_v_ffn1_w_in': 'new_v', 'new_v_ffn1_w_out': 'new_v', 'new_v_mix_norm': 'new_v', 'new_v_ffn2_norm': 'new_v', 'new_v_ffn2_w_in': 'new_v', 'new_v_ffn2_w_out': 'new_v', 'new_v_final_norm': 'new_v', 'new_v_s5_w_in': 'new_v', 'new_v_s5_lam_re': 'new_v', 'new_v_s5_lam_im': 'new_v', 'new_v_s5_log_dt': 'new_v', 'new_v_s5_b_re': 'new_v', 'new_v_s5_b_im': 'new_v', 'new_v_s5_c_re': 'new_v', 'new_v_s5_c_im': 'new_v', 'new_v_s5_d': 'new_v', 'new_v_s5_w_out': 'new_v', 'new_v_sb_w_qkv': 'new_v', 'new_v_sb_w_out': 'new_v', 'new_v_lru_w_in': 'new_v', 'new_v_lru_conv_w': 'new_v', 'new_v_lru_conv_b': 'new_v', 'new_v_lru_w_a': 'new_v', 'new_v_lru_b_a': 'new_v', 'new_v_lru_w_x': 'new_v', 'new_v_lru_b_x': 'new_v', 'new_v_lru_lambda': 'new_v', 'new_v_lru_w_out': 'new_v'}


def _forward(args):
    return _fwd_reference(*[args[k] for k in FWD_PARAMS])


def _output_shape():
    def fwd():
        inp = _fwd_setup_inputs(0)
        return _fwd_reference(*[inp[k] for k in FWD_PARAMS])
    out = _jax.eval_shape(fwd)
    return out.shape, out.dtype

N_MICROBATCH = 1
ADAM_LR = 0.001
ADAM_B1 = 0.9
ADAM_B2 = 0.999
ADAM_EPS = 1e-08
ADAM_WD = 0.01
ADAM_STEP = 10
PER_EXAMPLE_BATCH_AXIS = {'x': 0, 'loss_target': 0}
SHARED_INPUTS = []
_WEIGHT_DTYPES = {'ffn1_norm': _jnp.float32, 'ffn1_w_in': _jnp.float32, 'ffn1_w_out': _jnp.float32, 'mix_norm': _jnp.float32, 'ffn2_norm': _jnp.float32, 'ffn2_w_in': _jnp.float32, 'ffn2_w_out': _jnp.float32, 'final_norm': _jnp.float32, 's5_w_in': _jnp.float32, 's5_lam_re': _jnp.float32, 's5_lam_im': _jnp.float32, 's5_log_dt': _jnp.float32, 's5_b_re': _jnp.float32, 's5_b_im': _jnp.float32, 's5_c_re': _jnp.float32, 's5_c_im': _jnp.float32, 's5_d': _jnp.float32, 's5_w_out': _jnp.float32, 'sb_w_qkv': _jnp.float32, 'sb_w_out': _jnp.float32, 'lru_w_in': _jnp.float32, 'lru_conv_w': _jnp.float32, 'lru_conv_b': _jnp.float32, 'lru_w_a': _jnp.float32, 'lru_b_a': _jnp.float32, 'lru_w_x': _jnp.float32, 'lru_b_x': _jnp.float32, 'lru_lambda': _jnp.float32, 'lru_w_out': _jnp.float32}
MOMENT_SCALE = {'ffn1_norm': 1.033074e-01, 'ffn1_w_in': 4.387373e-02, 'ffn1_w_out': 7.159282e-02, 'mix_norm': 1.367322e-01, 'ffn2_norm': 9.121427e-02, 'ffn2_w_in': 3.781797e-02, 'ffn2_w_out': 6.169771e-02, 'final_norm': 6.420489e+01, 's5_w_in': 8.903872e-02, 's5_lam_re': 7.443808e-03, 's5_lam_im': 6.957270e-03, 's5_log_dt': 4.465652e+00, 's5_b_re': 4.462729e-03, 's5_b_im': 4.541392e-03, 's5_c_re': 6.260103e-03, 's5_c_im': 6.459412e-03, 's5_d': 1.396365e-01, 's5_w_out': 8.889015e-02, 'sb_w_qkv': 1.067617e-01, 'sb_w_out': 1.622320e-01, 'lru_w_in': 1.410344e-01, 'lru_conv_w': 1.381492e-01, 'lru_conv_b': 8.837800e-01, 'lru_w_a': 1.827340e-02, 'lru_b_a': 3.046821e-02, 'lru_w_x': 3.347239e-02, 'lru_b_x': 5.910306e-02, 'lru_lambda': 7.031365e-02, 'lru_w_out': 1.412849e-01}


def _to_microbatches(a, axis):
    t = _jnp.moveaxis(a, axis, 0)
    t = t.reshape((N_MICROBATCH, t.shape[0] // N_MICROBATCH) + t.shape[1:])
    return _jnp.moveaxis(t, 1, axis + 1)


def setup_inputs(seed: int = 0) -> dict:
    inp = _fwd_setup_inputs(seed)
    key = _jax.random.fold_in(_jax.random.key(seed), 7919)
    shape, _ = _output_shape()
    out = dict(inp)
    out["loss_target"] = _jax.random.normal(_jax.random.fold_in(key, 0), shape, _jnp.float32)
    for i, name in enumerate(TWIN_WEIGHTS):
        w = inp[name].astype(_jnp.float32)
        if MOMENT_SCALE is None:
            s = _jnp.sqrt(_jnp.mean(_jnp.square(w)) + 1e-30)
        else:
            s = MOMENT_SCALE[name]
        km, kv = _jax.random.split(_jax.random.fold_in(key, i + 1))
        out[name] = w
        out["m_" + name] = s * _jax.random.normal(km, w.shape, _jnp.float32)
        out["v_" + name] = (s * s) * _jax.random.uniform(kv, w.shape, _jnp.float32, 0.5, 1.5)
    if N_MICROBATCH > 1:
        for name, axis in PER_EXAMPLE_BATCH_AXIS.items():
            out[name] = _to_microbatches(out[name], axis)
    return {'x': out['x'], 'ffn1_norm': out['ffn1_norm'], 'ffn1_w_in': out['ffn1_w_in'], 'ffn1_w_out': out['ffn1_w_out'], 'mix_norm': out['mix_norm'], 'ffn2_norm': out['ffn2_norm'], 'ffn2_w_in': out['ffn2_w_in'], 'ffn2_w_out': out['ffn2_w_out'], 'final_norm': out['final_norm'], 's5_w_in': out['s5_w_in'], 's5_lam_re': out['s5_lam_re'], 's5_lam_im': out['s5_lam_im'], 's5_log_dt': out['s5_log_dt'], 's5_b_re': out['s5_b_re'], 's5_b_im': out['s5_b_im'], 's5_c_re': out['s5_c_re'], 's5_c_im': out['s5_c_im'], 's5_d': out['s5_d'], 's5_w_out': out['s5_w_out'], 'sb_w_qkv': out['sb_w_qkv'], 'sb_w_out': out['sb_w_out'], 'lru_w_in': out['lru_w_in'], 'lru_conv_w': out['lru_conv_w'], 'lru_conv_b': out['lru_conv_b'], 'lru_w_a': out['lru_w_a'], 'lru_b_a': out['lru_b_a'], 'lru_w_x': out['lru_w_x'], 'lru_b_x': out['lru_b_x'], 'lru_lambda': out['lru_lambda'], 'lru_w_out': out['lru_w_out'], 'loss_target': out['loss_target'], 'm_ffn1_norm': out['m_ffn1_norm'], 'm_ffn1_w_in': out['m_ffn1_w_in'], 'm_ffn1_w_out': out['m_ffn1_w_out'], 'm_mix_norm': out['m_mix_norm'], 'm_ffn2_norm': out['m_ffn2_norm'], 'm_ffn2_w_in': out['m_ffn2_w_in'], 'm_ffn2_w_out': out['m_ffn2_w_out'], 'm_final_norm': out['m_final_norm'], 'm_s5_w_in': out['m_s5_w_in'], 'm_s5_lam_re': out['m_s5_lam_re'], 'm_s5_lam_im': out['m_s5_lam_im'], 'm_s5_log_dt': out['m_s5_log_dt'], 'm_s5_b_re': out['m_s5_b_re'], 'm_s5_b_im': out['m_s5_b_im'], 'm_s5_c_re': out['m_s5_c_re'], 'm_s5_c_im': out['m_s5_c_im'], 'm_s5_d': out['m_s5_d'], 'm_s5_w_out': out['m_s5_w_out'], 'm_sb_w_qkv': out['m_sb_w_qkv'], 'm_sb_w_out': out['m_sb_w_out'], 'm_lru_w_in': out['m_lru_w_in'], 'm_lru_conv_w': out['m_lru_conv_w'], 'm_lru_conv_b': out['m_lru_conv_b'], 'm_lru_w_a': out['m_lru_w_a'], 'm_lru_b_a': out['m_lru_b_a'], 'm_lru_w_x': out['m_lru_w_x'], 'm_lru_b_x': out['m_lru_b_x'], 'm_lru_lambda': out['m_lru_lambda'], 'm_lru_w_out': out['m_lru_w_out'], 'v_ffn1_norm': out['v_ffn1_norm'], 'v_ffn1_w_in': out['v_ffn1_w_in'], 'v_ffn1_w_out': out['v_ffn1_w_out'], 'v_mix_norm': out['v_mix_norm'], 'v_ffn2_norm': out['v_ffn2_norm'], 'v_ffn2_w_in': out['v_ffn2_w_in'], 'v_ffn2_w_out': out['v_ffn2_w_out'], 'v_final_norm': out['v_final_norm'], 'v_s5_w_in': out['v_s5_w_in'], 'v_s5_lam_re': out['v_s5_lam_re'], 'v_s5_lam_im': out['v_s5_lam_im'], 'v_s5_log_dt': out['v_s5_log_dt'], 'v_s5_b_re': out['v_s5_b_re'], 'v_s5_b_im': out['v_s5_b_im'], 'v_s5_c_re': out['v_s5_c_re'], 'v_s5_c_im': out['v_s5_c_im'], 'v_s5_d': out['v_s5_d'], 'v_s5_w_out': out['v_s5_w_out'], 'v_sb_w_qkv': out['v_sb_w_qkv'], 'v_sb_w_out': out['v_sb_w_out'], 'v_lru_w_in': out['v_lru_w_in'], 'v_lru_conv_w': out['v_lru_conv_w'], 'v_lru_conv_b': out['v_lru_conv_b'], 'v_lru_w_a': out['v_lru_w_a'], 'v_lru_b_a': out['v_lru_b_a'], 'v_lru_w_x': out['v_lru_w_x'], 'v_lru_b_x': out['v_lru_b_x'], 'v_lru_lambda': out['v_lru_lambda'], 'v_lru_w_out': out['v_lru_w_out']}


def _loss(weights, diff, rest, loss_target):
    with _jax.named_scope("forward"):
        args = {**rest, TWIN_DIFF_INPUT: diff, **{k: w.astype(_WEIGHT_DTYPES[k]) for k, w in weights.items()}}
        y = _forward(args)
    with _jax.named_scope("loss_head"):
        err = _jnp.square(y.astype(_jnp.float32) - loss_target)
        return 0.5 * _jnp.sum(_jnp.mean(err, axis=-1)) if err.ndim else 0.5 * err


def _adamw(w, g, m, v):
    m = ADAM_B1 * m + (1.0 - ADAM_B1) * g
    v = ADAM_B2 * v + (1.0 - ADAM_B2) * _jnp.square(g)
    m_hat = m / (1.0 - ADAM_B1 ** ADAM_STEP)
    v_hat = v / (1.0 - ADAM_B2 ** ADAM_STEP)
    delta = -ADAM_LR * (m_hat / (_jnp.sqrt(v_hat) + ADAM_EPS) + ADAM_WD * w)
    return delta, m, v


def reference(x, ffn1_norm, ffn1_w_in, ffn1_w_out, mix_norm, ffn2_norm, ffn2_w_in, ffn2_w_out, final_norm, s5_w_in, s5_lam_re, s5_lam_im, s5_log_dt, s5_b_re, s5_b_im, s5_c_re, s5_c_im, s5_d, s5_w_out, sb_w_qkv, sb_w_out, lru_w_in, lru_conv_w, lru_conv_b, lru_w_a, lru_b_a, lru_w_x, lru_b_x, lru_lambda, lru_w_out, loss_target, m_ffn1_norm, m_ffn1_w_in, m_ffn1_w_out, m_mix_norm, m_ffn2_norm, m_ffn2_w_in, m_ffn2_w_out, m_final_norm, m_s5_w_in, m_s5_lam_re, m_s5_lam_im, m_s5_log_dt, m_s5_b_re, m_s5_b_im, m_s5_c_re, m_s5_c_im, m_s5_d, m_s5_w_out, m_sb_w_qkv, m_sb_w_out, m_lru_w_in, m_lru_conv_w, m_lru_conv_b, m_lru_w_a, m_lru_b_a, m_lru_w_x, m_lru_b_x, m_lru_lambda, m_lru_w_out, v_ffn1_norm, v_ffn1_w_in, v_ffn1_w_out, v_mix_norm, v_ffn2_norm, v_ffn2_w_in, v_ffn2_w_out, v_final_norm, v_s5_w_in, v_s5_lam_re, v_s5_lam_im, v_s5_log_dt, v_s5_b_re, v_s5_b_im, v_s5_c_re, v_s5_c_im, v_s5_d, v_s5_w_out, v_sb_w_qkv, v_sb_w_out, v_lru_w_in, v_lru_conv_w, v_lru_conv_b, v_lru_w_a, v_lru_b_a, v_lru_w_x, v_lru_b_x, v_lru_lambda, v_lru_w_out):
    given = dict(x=x, ffn1_norm=ffn1_norm, ffn1_w_in=ffn1_w_in, ffn1_w_out=ffn1_w_out, mix_norm=mix_norm, ffn2_norm=ffn2_norm, ffn2_w_in=ffn2_w_in, ffn2_w_out=ffn2_w_out, final_norm=final_norm, s5_w_in=s5_w_in, s5_lam_re=s5_lam_re, s5_lam_im=s5_lam_im, s5_log_dt=s5_log_dt, s5_b_re=s5_b_re, s5_b_im=s5_b_im, s5_c_re=s5_c_re, s5_c_im=s5_c_im, s5_d=s5_d, s5_w_out=s5_w_out, sb_w_qkv=sb_w_qkv, sb_w_out=sb_w_out, lru_w_in=lru_w_in, lru_conv_w=lru_conv_w, lru_conv_b=lru_conv_b, lru_w_a=lru_w_a, lru_b_a=lru_b_a, lru_w_x=lru_w_x, lru_b_x=lru_b_x, lru_lambda=lru_lambda, lru_w_out=lru_w_out, loss_target=loss_target, m_ffn1_norm=m_ffn1_norm, m_ffn1_w_in=m_ffn1_w_in, m_ffn1_w_out=m_ffn1_w_out, m_mix_norm=m_mix_norm, m_ffn2_norm=m_ffn2_norm, m_ffn2_w_in=m_ffn2_w_in, m_ffn2_w_out=m_ffn2_w_out, m_final_norm=m_final_norm, m_s5_w_in=m_s5_w_in, m_s5_lam_re=m_s5_lam_re, m_s5_lam_im=m_s5_lam_im, m_s5_log_dt=m_s5_log_dt, m_s5_b_re=m_s5_b_re, m_s5_b_im=m_s5_b_im, m_s5_c_re=m_s5_c_re, m_s5_c_im=m_s5_c_im, m_s5_d=m_s5_d, m_s5_w_out=m_s5_w_out, m_sb_w_qkv=m_sb_w_qkv, m_sb_w_out=m_sb_w_out, m_lru_w_in=m_lru_w_in, m_lru_conv_w=m_lru_conv_w, m_lru_conv_b=m_lru_conv_b, m_lru_w_a=m_lru_w_a, m_lru_b_a=m_lru_b_a, m_lru_w_x=m_lru_w_x, m_lru_b_x=m_lru_b_x, m_lru_lambda=m_lru_lambda, m_lru_w_out=m_lru_w_out, v_ffn1_norm=v_ffn1_norm, v_ffn1_w_in=v_ffn1_w_in, v_ffn1_w_out=v_ffn1_w_out, v_mix_norm=v_mix_norm, v_ffn2_norm=v_ffn2_norm, v_ffn2_w_in=v_ffn2_w_in, v_ffn2_w_out=v_ffn2_w_out, v_final_norm=v_final_norm, v_s5_w_in=v_s5_w_in, v_s5_lam_re=v_s5_lam_re, v_s5_lam_im=v_s5_lam_im, v_s5_log_dt=v_s5_log_dt, v_s5_b_re=v_s5_b_re, v_s5_b_im=v_s5_b_im, v_s5_c_re=v_s5_c_re, v_s5_c_im=v_s5_c_im, v_s5_d=v_s5_d, v_s5_w_out=v_s5_w_out, v_sb_w_qkv=v_sb_w_qkv, v_sb_w_out=v_sb_w_out, v_lru_w_in=v_lru_w_in, v_lru_conv_w=v_lru_conv_w, v_lru_conv_b=v_lru_conv_b, v_lru_w_a=v_lru_w_a, v_lru_b_a=v_lru_b_a, v_lru_w_x=v_lru_w_x, v_lru_b_x=v_lru_b_x, v_lru_lambda=v_lru_lambda, v_lru_w_out=v_lru_w_out)
    weights = {n: given[n] for n in TWIN_WEIGHTS}
    shared = {n: given[n] for n in SHARED_INPUTS}
    per_example = {n: given[n] for n in ['x']}
    grad_fn = _jax.value_and_grad(_loss, argnums=(0, 1))

    def one_microbatch(ex, loss_target):
        ex = dict(ex)
        diff = ex.pop(TWIN_DIFF_INPUT)
        return grad_fn(weights, diff, {**shared, **ex}, loss_target)

    if N_MICROBATCH == 1:
        loss, (grad_w, grad_x) = one_microbatch(per_example, given["loss_target"])
    else:
        def body(carry, xs):
            loss_sum, grad_sum = carry
            l_k, (gw_k, gx_k) = one_microbatch(xs[0], xs[1])
            with _jax.named_scope("update"):
                return (loss_sum + l_k, _jax.tree.map(_jnp.add, grad_sum, gw_k)), gx_k

        init = (_jnp.zeros((), _jnp.float32), _jax.tree.map(_jnp.zeros_like, weights))
        (loss, grad_w), grad_x = _jax.lax.scan(body, init, (per_example, given["loss_target"]))
    with _jax.named_scope("update"):
        delta_w, new_m, new_v = {}, {}, {}
        for n in TWIN_WEIGHTS:
            delta_w[n], new_m[n], new_v[n] = _adamw(weights[n], grad_w[n], given["m_" + n], given["v_" + n])
    return (loss, grad_x, *[grad_w[n] for n in TWIN_WEIGHTS], *[delta_w[n] for n in TWIN_WEIGHTS],
            *[new_m[n] for n in TWIN_WEIGHTS], *[new_v[n] for n in TWIN_WEIGHTS])
```

```python
import functools
import math

import jax
import jax.numpy as jnp
from jax import lax
from jax.experimental import pallas as pl
from jax.experimental.pallas import tpu as pltpu

F32 = jnp.float32
BF16 = jnp.bfloat16

VMEM_LIMIT_BYTES = 56 * 1024 * 1024

RMS_EPS = 1e-6
D_FF = 2816
S5_GROUP = 16
S5_STATE = 64
SB_HEAD_DIM = 64
LRU_BLOCK_WIDTH = 256
LRU_CONV = 4
LRU_C = 8.0
N_MIXERS = 3

ADAM_LR = 0.001
ADAM_B1 = 0.9
ADAM_B2 = 0.999
ADAM_EPS = 1e-08
ADAM_WD = 0.01
ADAM_STEP = 10


def _params(semantics):
    return pltpu.CompilerParams(dimension_semantics=semantics, vmem_limit_bytes=VMEM_LIMIT_BYTES)


def _tile(dim, prefs):
    for t in prefs:
        if t <= dim and dim % t == 0:
            return t
    return dim


def _mm_call(name, grid, a, a_spec, b, b_spec, dims, extras, outs, epilogue, acc_shape):
    n_extra, n_out, n_k = len(extras), len(outs), grid[-1]

    def body(a_ref, b_ref, *rest):
        extra_refs = rest[:n_extra]
        out_refs = rest[n_extra:n_extra + n_out]
        acc_ref = rest[n_extra + n_out]
        k = pl.program_id(len(grid) - 1)
        part = lax.dot_general(a_ref[...].astype(BF16), b_ref[...].astype(BF16), dims, preferred_element_type=F32)

        @pl.when(k == 0)
        def _():
            acc_ref[...] = part

        @pl.when(k > 0)
        def _():
            acc_ref[...] += part

        @pl.when(k == n_k - 1)
        def _():
            res = epilogue(acc_ref[...], *[r[...] for r in extra_refs])
            if not isinstance(res, (tuple, list)):
                res = (res,)
            for o_ref, r in zip(out_refs, res):
                o_ref[...] = r.astype(o_ref.dtype)

    sem = ("parallel",) * (len(grid) - 1) + ("arbitrary",)
    res = pl.pallas_call(
        body,
        name=name,
        grid=grid,
        in_specs=[a_spec, b_spec] + [s for _, s in extras],
        out_specs=[s for _, s in outs],
        out_shape=[s for s, _ in outs],
        scratch_shapes=[pltpu.VMEM(acc_shape, F32)],
        compiler_params=_params(sem),
    )(a, b, *[x for x, _ in extras])
    return res


def mm(a, b, *, name, ta=False, tb=False, extras=(), epilogue=None, out_dtypes=(F32,), tm=512, tn=512, tk=1024):
    m, kdim = (a.shape[1], a.shape[0]) if ta else a.shape
    n = b.shape[0] if tb else b.shape[1]
    tm = _tile(m, (tm, 256, 128))
    tn = _tile(n, (tn, 256, 128))
    tk = _tile(kdim, (tk, 512, 256, 128))
    grid = (m // tm, n // tn, kdim // tk)
    a_spec = pl.BlockSpec((tk, tm), lambda i, j, k: (k, i)) if ta else pl.BlockSpec((tm, tk), lambda i, j, k: (i, k))
    b_spec = pl.BlockSpec((tn, tk), lambda i, j, k: (j, k)) if tb else pl.BlockSpec((tk, tn), lambda i, j, k: (k, j))
    dims = (((0 if ta else 1,), (1 if tb else 0,)), ((), ()))
    o_spec = pl.BlockSpec((tm, tn), lambda i, j, k: (i, j))
    if epilogue is None:
        epilogue = lambda acc: acc
    res = _mm_call(name, grid, a, a_spec, b, b_spec, dims, [(x, o_spec) for x in extras],
                   [(jax.ShapeDtypeStruct((m, n), dt), o_spec) for dt in out_dtypes], epilogue, (tm, tn))
    return res[0] if len(res) == 1 else res


def rowwise(fn, rows, consts, outs, sums=(), *, name, tm=256):
    m = rows[0].shape[0]
    tm = _tile(m, (tm, 128, 64, 32, 16, 8))
    n_rows, n_consts, n_outs = len(rows), len(consts), len(outs)

    def body(*refs):
        in_vals = [r[...] for r in refs[:n_rows + n_consts]]
        out_refs = refs[n_rows + n_consts:n_rows + n_consts + n_outs]
        sum_refs = refs[n_rows + n_consts + n_outs:]
        res = fn(*in_vals)
        if not isinstance(res, (tuple, list)):
            res = (res,)
        for o_ref, r in zip(out_refs, res[:n_outs]):
            o_ref[...] = r.astype(o_ref.dtype)
        if sum_refs:
            first = pl.program_id(0) == 0

            @pl.when(first)
            def _():
                for s_ref, r in zip(sum_refs, res[n_outs:]):
                    s_ref[...] = r.astype(F32)

            @pl.when(jnp.logical_not(first))
            def _():
                for s_ref, r in zip(sum_refs, res[n_outs:]):
                    s_ref[...] += r.astype(F32)

    def whole(shape):
        nd = len(shape)
        return pl.BlockSpec(shape, lambda i: (0,) * nd)

    def row_spec(x):
        if x.ndim == 3:
            return pl.BlockSpec((x.shape[0], tm, x.shape[2]), lambda i: (0, i, 0))
        return pl.BlockSpec((tm, x.shape[1]), lambda i: (i, 0))

    in_specs = [row_spec(x) for x in rows] + [whole(c.shape) for c in consts]
    out_specs = [pl.BlockSpec((tm, nc), lambda i: (i, 0)) for nc, _ in outs] + [whole(tuple(s)) for s in sums]
    out_shape = [jax.ShapeDtypeStruct((m, nc), dt) for nc, dt in outs] + [jax.ShapeDtypeStruct(tuple(s), F32) for s in sums]
    res = pl.pallas_call(
        body,
        name=name,
        grid=(m // tm,),
        in_specs=in_specs,
        out_specs=out_specs,
        out_shape=out_shape,
        compiler_params=_params(("arbitrary",) if sums else ("parallel",)),
    )(*rows, *consts)
    return res[0] if len(res) == 1 else res


def _rms(h, g):
    return h * lax.rsqrt(jnp.mean(h * h, axis=-1, keepdims=True) + RMS_EPS) * g


def _sigmoid(x):
    return 1.0 / (1.0 + jnp.exp(-x))


def _silu_mul(g, u):
    return g * _sigmoid(g) * u


def _gelu(x):
    return 0.5 * x * (1.0 + jnp.tanh(math.sqrt(2.0 / math.pi) * (x + 0.044715 * (x * x * x))))


def _softplus(x):
    return jnp.maximum(x, 0.0) + jnp.log(1.0 + jnp.exp(-jnp.abs(x)))


def rms_fwd(h, g, name):
    return rowwise(lambda x, gg: _rms(x, gg), [h], [g.reshape(1, -1)], [(h.shape[1], BF16)], name=name)


def rms_bwd(h, g, dhn, dres, name):
    def fn(x, dy, dr, gg):
        _, vjp = jax.vjp(_rms, x, gg)
        dx, dg = vjp(dy.astype(F32))
        return dr + dx, dg

    d = h.shape[1]
    dh, dg = rowwise(fn, [h, dhn, dres], [g.reshape(1, -1)], [(d, F32)], [(1, d)], name=name)
    return dh, dg.reshape(-1)


def ffn_fwd(h, g, w_in, w_out, tag):
    hn = rms_fwd(h, g, f"{tag}_norm")
    gu = mm(hn, w_in, name=f"{tag}_in", out_dtypes=(BF16,))
    f = w_out.shape[0]
    act = rowwise(lambda x: _silu_mul(x[:, :f].astype(F32), x[:, f:].astype(F32)), [gu], [], [(f, BF16)], name=f"{tag}_act")
    h_new = mm(act, w_out, name=f"{tag}_out", extras=(h,), epilogue=lambda acc, res: res + 0.5 * acc, tk=2816)
    return h_new, (h, hn, gu, act)


def ffn_bwd(dh_new, saved, g, w_in, w_out, tag):
    h, hn, gu, act = saved
    f = w_out.shape[0]
    dact = mm(dh_new, w_out, tb=True, name=f"{tag}_dact", out_dtypes=(BF16,), epilogue=lambda acc: 0.5 * acc)

    def act_bwd(x, da):
        gg, uu = x[:, :f].astype(F32), x[:, f:].astype(F32)
        _, vjp = jax.vjp(_silu_mul, gg, uu)
        dg_, du_ = vjp(da.astype(F32))
        return jnp.concatenate([dg_, du_], axis=1)

    dgu = rowwise(act_bwd, [gu, dact], [], [(2 * f, BF16)], name=f"{tag}_dgu")
    dw_out = mm(act, dh_new, ta=True, name=f"{tag}_dwout", epilogue=lambda acc: 0.5 * acc, tm=256, tn=1024, tk=1024)
    dw_in = mm(hn, dgu, ta=True, name=f"{tag}_dwin", tm=512, tn=512, tk=1024)
    dhn = mm(dgu, w_in, tb=True, name=f"{tag}_dhn", out_dtypes=(BF16,), tk=5632)
    dh, dg = rms_bwd(h, g, dhn, dh_new, f"{tag}_dnorm")
    return dh, dg, dw_in, dw_out


def loss_fwd_bwd(h, g, target):
    d = h.shape[1]

    def fn(x, t, gg):
        y, vjp = jax.vjp(_rms, x, gg)
        err = y - t
        dx, dg = vjp(err * (1.0 / d))
        part = 0.5 * jnp.sum(jnp.sum(err * err, axis=1, keepdims=True), axis=0, keepdims=True) * (1.0 / d)
        return dx, dg, jnp.broadcast_to(part, (1, 128))

    dh, dg, loss = rowwise(fn, [h, target], [g.reshape(1, -1)], [(d, F32)], [(1, d), (1, 128)], name="loss_head")
    return loss[0, 0], dh, dg.reshape(-1)


def _shift_down(v, d, fill):
    rows = lax.broadcasted_iota(jnp.int32, v.shape, 0)
    return jnp.where(rows < d, fill, pltpu.roll(v, d, 0))


def _shift_up(v, d, fill):
    t = v.shape[0]
    rows = lax.broadcasted_iota(jnp.int32, v.shape, 0)
    return jnp.where(rows >= t - d, fill, pltpu.roll(v, t - d, 0))


def _scan_fwd(a, x):
    d = 1
    while d < a.shape[0]:
        x = x + a * _shift_down(x, d, 0.0)
        a = a * _shift_down(a, d, 1.0)
        d *= 2
    return a, x


def _scan_bwd(b, x):
    d = 1
    while d < b.shape[0]:
        x = x + b * _shift_up(x, d, 0.0)
        b = b * _shift_up(b, d, 1.0)
        d *= 2
    return b, x


def _rows_before(cur, prev8, s):
    r = pltpu.roll(cur, s, 0)
    p = pltpu.roll(prev8, s, 0)
    rows = lax.broadcasted_iota(jnp.int32, prev8.shape, 0)
    return jnp.concatenate([jnp.where(rows < s, p, r[:8]), r[8:]], axis=0)


def _rows_after(cur, next8, s):
    t = cur.shape[0]
    r = pltpu.roll(cur, t - s, 0)
    p = pltpu.roll(next8, 8 - s, 0)
    rows = lax.broadcasted_iota(jnp.int32, next8.shape, 0)
    return jnp.concatenate([r[:t - 8], jnp.where(rows >= 8 - s, p, r[t - 8:])], axis=0)


LRU_CHUNK = 256


def _neg_expm1(y):
    small = -y * (1.0 + 0.5 * y * (1.0 + y * (1.0 / 3.0)))
    return jnp.where(y > -0.01, small, 1.0 - jnp.exp(y))


def _lru_gate(xc, pre_a, pre_x, lam):
    r = _sigmoid(pre_a)
    ig = _sigmoid(pre_x)
    log_a = (-LRU_C * r) * _softplus(-lam)
    return jnp.exp(log_a), (ig * xc) * jnp.sqrt(_neg_expm1(2.0 * log_a))


def _lru_conv(br, prev8, conv_w, conv_b):
    taps = [br] + [_rows_before(br, prev8, s) for s in range(1, LRU_CONV)]
    xc = conv_b
    for k in range(LRU_CONV):
        xc = xc + conv_w[k:k + 1, :] * taps[LRU_CONV - 1 - k]
    return xc, taps


def _lru_pre(xcb, w_ref, bias):
    nb = w_ref.shape[0]
    bw = w_ref.shape[1]
    return jnp.concatenate(
        [jnp.dot(xcb[:, n * bw:(n + 1) * bw], w_ref[n], preferred_element_type=F32) for n in range(nb)], axis=1) + bias


def lru_scan_fwd(bgr, conv_w, conv_b, wa, ba, wx, bx, lam):
    l, w2 = bgr.shape
    w = w2 // 2
    t = _tile(l, (LRU_CHUNK, 128, 64, 32, 16, 8))

    def body(bg_ref, br_ref, cw_ref, cb_ref, wa_ref, ba_ref, wx_ref, bx_ref, lam_ref, y_ref, h_ref, tail_ref, hprev_ref):
        @pl.when(pl.program_id(0) == 0)
        def _():
            tail_ref[...] = jnp.zeros_like(tail_ref)
            hprev_ref[...] = jnp.zeros_like(hprev_ref)

        br = br_ref[...]
        xc, _ = _lru_conv(br, tail_ref[...], cw_ref[...], cb_ref[...])
        xcb = xc.astype(BF16)
        a, gx = _lru_gate(xc, _lru_pre(xcb, wa_ref, ba_ref[...]), _lru_pre(xcb, wx_ref, bx_ref[...]), lam_ref[...])
        acum, x = _scan_fwd(a, gx)
        h = x + acum * hprev_ref[pl.ds(7, 1), :]
        y_ref[...] = (_gelu(bg_ref[...]) * h).astype(y_ref.dtype)
        h_ref[...] = h
        tail_ref[...] = br[t - 8:, :]
        hprev_ref[...] = h[t - 8:, :]

    def whole(x):
        nd = x.ndim
        return pl.BlockSpec(x.shape, lambda i: (0,) * nd)

    consts = [conv_w, conv_b, wa, ba, wx, bx, lam]
    return pl.pallas_call(
        body,
        name="lru_scan_fwd",
        grid=(l // t,),
        in_specs=[pl.BlockSpec((t, w), lambda i: (i, 0)), pl.BlockSpec((t, w), lambda i: (i, 1))] + [whole(c) for c in consts],
        out_specs=[pl.BlockSpec((t, w), lambda i: (i, 0)), pl.BlockSpec((t, w), lambda i: (i, 0))],
        out_shape=[jax.ShapeDtypeStruct((l, w), BF16), jax.ShapeDtypeStruct((l, w), F32)],
        scratch_shapes=[pltpu.VMEM((8, w), F32), pltpu.VMEM((8, w), F32)],
        compiler_params=_params(("arbitrary",)),
    )(bgr, bgr, *consts)


def lru_scan_bwd(dy, bgr, hseq, conv_w, conv_b, wa, ba, wx, bx, lam):
    l, w2 = bgr.shape
    w = w2 // 2
    t = _tile(l, (LRU_CHUNK, 128, 64, 32, 16, 8))
    nc = l // t
    nb, bw = wa.shape[0], wa.shape[1]

    def body(dy_ref, bg_ref, br_ref, brh_ref, h_ref, hh_ref, cw_ref, cb_ref, wa_ref, ba_ref, wx_ref, bx_ref, lam_ref,
             dbgr_ref, dcw_ref, dcb_ref, dwa_ref, dba_ref, dwx_ref, dbx_ref, dlam_ref, dxcn_ref, carry_ref):
        i = pl.program_id(0)
        has_prev = (i < nc - 1).astype(F32)

        @pl.when(i == 0)
        def _():
            dxcn_ref[...] = jnp.zeros_like(dxcn_ref)
            carry_ref[...] = jnp.zeros_like(carry_ref)

        br = br_ref[...]
        cw = cw_ref[...]
        xc, taps = _lru_conv(br, brh_ref[...] * has_prev, cw, cb_ref[...])
        xcb = xc.astype(BF16)
        (a, _), gate_vjp = jax.vjp(_lru_gate, xc, _lru_pre(xcb, wa_ref, ba_ref[...]), _lru_pre(xcb, wx_ref, bx_ref[...]),
                                   lam_ref[...])
        hs = h_ref[...]
        _, out_vjp = jax.vjp(lambda g_, h_: _gelu(g_) * h_, bg_ref[...], hs)
        dbg, dhs = out_vjp(dy_ref[...])
        bcum, x = _scan_bwd(_shift_up(a, 1, 1.0), dhs)
        dh = x + bcum * carry_ref[pl.ds(0, 1), :]
        da = dh * _rows_before(hs, hh_ref[...] * has_prev, 1)
        dxc, dpa, dpx, dlam = gate_vjp((da, dh))
        dpab, dpxb = dpa.astype(BF16), dpx.astype(BF16)
        nt = (((1,), (1,)), ((), ()))
        tn = (((0,), (0,)), ((), ()))
        dxb, dwa, dwx = [], [], []
        for n in range(nb):
            sl = slice(n * bw, (n + 1) * bw)
            dxb.append(lax.dot_general(dpab[:, sl], wa_ref[n], nt, preferred_element_type=F32)
                       + lax.dot_general(dpxb[:, sl], wx_ref[n], nt, preferred_element_type=F32))
            dwa.append(lax.dot_general(xcb[:, sl], dpab[:, sl], tn, preferred_element_type=F32))
            dwx.append(lax.dot_general(xcb[:, sl], dpxb[:, sl], tn, preferred_element_type=F32))
        dxc = dxc + jnp.concatenate(dxb, axis=1)
        ups = [dxc] + [_rows_after(dxc, dxcn_ref[...], s) for s in range(1, LRU_CONV)]
        dbr = cw[LRU_CONV - 1:LRU_CONV, :] * ups[0]
        for k in range(LRU_CONV - 1):
            dbr = dbr + cw[k:k + 1, :] * ups[LRU_CONV - 1 - k]
        dbgr_ref[:, :w] = dbg.astype(dbgr_ref.dtype)
        dbgr_ref[:, w:] = dbr.astype(dbgr_ref.dtype)
        dcw = jnp.concatenate([jnp.sum(dxc * taps[LRU_CONV - 1 - k], axis=0, keepdims=True) for k in range(LRU_CONV)], axis=0)
        sums = [(dcw_ref, dcw), (dcb_ref, jnp.sum(dxc, axis=0, keepdims=True)), (dwa_ref, jnp.stack(dwa)),
                (dba_ref, jnp.sum(dpa, axis=0, keepdims=True)), (dwx_ref, jnp.stack(dwx)),
                (dbx_ref, jnp.sum(dpx, axis=0, keepdims=True)), (dlam_ref, dlam)]

        @pl.when(i == 0)
        def _():
            for ref, val in sums:
                ref[...] = val

        @pl.when(i > 0)
        def _():
            for ref, val in sums:
                ref[...] += val

        dxcn_ref[...] = dxc[:8, :]
        carry_ref[...] = (a * dh)[:8, :]

    def whole(shape):
        nd = len(shape)
        return pl.BlockSpec(tuple(shape), lambda i: (0,) * nd)

    consts = [conv_w, conv_b, wa, ba, wx, bx, lam]
    t8 = t // 8
    rev = lambda i: nc - 1 - i
    halo = lambda i: jnp.maximum(rev(i) * t8 - 1, 0)
    in_specs = [
        pl.BlockSpec((t, w), lambda i: (rev(i), 0)),
        pl.BlockSpec((t, w), lambda i: (rev(i), 0)),
        pl.BlockSpec((t, w), lambda i: (rev(i), 1)),
        pl.BlockSpec((8, w), lambda i: (halo(i), 1)),
        pl.BlockSpec((t, w), lambda i: (rev(i), 0)),
        pl.BlockSpec((8, w), lambda i: (halo(i), 0)),
    ] + [whole(c.shape) for c in consts]
    sum_shapes = [conv_w.shape, conv_b.shape, wa.shape, ba.shape, wx.shape, bx.shape, lam.shape]
    res = pl.pallas_call(
        body,
        name="lru_scan_bwd",
        grid=(nc,),
        in_specs=in_specs,
        out_specs=[pl.BlockSpec((t, w2), lambda i: (rev(i), 0))] + [whole(s) for s in sum_shapes],
        out_shape=[jax.ShapeDtypeStruct((l, w2), BF16)] + [jax.ShapeDtypeStruct(tuple(s), F32) for s in sum_shapes],
        scratch_shapes=[pltpu.VMEM((8, w), F32), pltpu.VMEM((8, w), F32)],
        compiler_params=_params(("arbitrary",)),
    )(dy, bgr, bgr, bgr, hseq, hseq, *consts)
    return res[0], res[1:]


def lru_fwd(hn, p):
    bgr = mm(hn, p["w_in"], name="lru_in")
    y, hseq = lru_scan_fwd(bgr, p["conv_w"], p["conv_b"], p["wa"], p["ba"], p["wx"], p["bx"], p["lam"])
    return y, (hn, bgr, hseq, y)


def lru_bwd(dmixed, saved, p):
    hn, bgr, hseq, y = saved
    dy = mm(dmixed, p["w_out"], tb=True, name="lru_dy")
    dw_out = mm(y, dmixed, ta=True, name="lru_dwout")
    dbgr, (dcw, dcb, dwa, dba, dwx, dbx, dlam) = lru_scan_bwd(dy, bgr, hseq, p["conv_w"], p["conv_b"], p["wa"], p["ba"],
                                                               p["wx"], p["bx"], p["lam"])
    dw_in = mm(hn, dbgr, ta=True, name="lru_dwin")
    dhn = mm(dbgr, p["w_in"], tb=True, name="lru_dhn", out_dtypes=(BF16,), tk=2048)
    grads = dict(w_in=dw_in, conv_w=dcw, conv_b=dcb, wa=dwa, ba=dba, wx=dwx, bx=dbx, lam=dlam, w_out=dw_out)
    return dhn, grads


SB_BLOCK = 256
_NT = (((1,), (1,)), ((), ()))
_TN = (((0,), (0,)), ((), ()))


def _dot_hilo(x, tri):
    hi = x.astype(BF16)
    lo = (x - hi.astype(F32)).astype(BF16)
    return jnp.dot(hi, tri, preferred_element_type=F32) + jnp.dot(lo, tri, preferred_element_type=F32)


def _tri(n, cmp):
    r = lax.broadcasted_iota(jnp.int32, (n, n), 0)
    c = lax.broadcasted_iota(jnp.int32, (n, n), 1)
    return cmp(r, c).astype(BF16)


def sb_attn_fwd(qkv):
    _, nh, l, hd = qkv.shape
    blk = _tile(l, (SB_BLOCK, 128))
    scale = hd ** -0.5
    t_suf = _tri(blk, lambda r, c: r > c)

    def body(q_ref, k_ref, v_ref, tsuf_ref, o_ref, ltot_ref):
        i = pl.program_id(1)
        q = q_ref[...]
        tsuf = tsuf_ref[...]

        def tile(j, carry, masked):
            c_r, acc = carry
            rows = pl.ds(pl.multiple_of(j * blk, blk), blk)
            k = k_ref[rows, :]
            v = v_ref[rows, :]
            z = lax.dot_general(q, k, _NT, preferred_element_type=F32) * scale
            lk = -_softplus(z)
            if masked:
                causal = lax.broadcasted_iota(jnp.int32, z.shape, 1) < lax.broadcasted_iota(jnp.int32, z.shape, 0)
                lk = jnp.where(causal, lk, 0.0)
            w = jnp.exp(z + lk + c_r + _dot_hilo(lk, tsuf))
            if masked:
                w = jnp.where(causal, w, 0.0)
            acc = acc + jnp.dot(w.astype(BF16), v, preferred_element_type=F32)
            return c_r + jnp.sum(lk, axis=1, keepdims=True), acc

        carry = tile(i, (jnp.zeros((blk, 1), F32), jnp.zeros((blk, hd), F32)), True)
        c_r, acc = lax.fori_loop(0, i, lambda jj, c: tile(i - 1 - jj, c, False), carry)
        o_ref[...] = acc.astype(o_ref.dtype)
        ltot_ref[...] = c_r

    return pl.pallas_call(
        body,
        name="sb_attn_fwd",
        grid=(nh, l // blk),
        in_specs=[
            pl.BlockSpec((None, None, blk, hd), lambda h, i: (0, h, i, 0)),
            pl.BlockSpec((None, None, l, hd), lambda h, i: (1, h, 0, 0)),
            pl.BlockSpec((None, None, l, hd), lambda h, i: (2, h, 0, 0)),
            pl.BlockSpec((blk, blk), lambda h, i: (0, 0)),
        ],
        out_specs=[pl.BlockSpec((None, blk, hd), lambda h, i: (h, i, 0)), pl.BlockSpec((None, blk, 1), lambda h, i: (h, i, 0))],
        out_shape=[jax.ShapeDtypeStruct((nh, l, hd), BF16), jax.ShapeDtypeStruct((nh, l, 1), F32)],
        compiler_params=_params(("parallel", "parallel")),
    )(qkv, qkv, qkv, t_suf)


def sb_attn_bwd(qkv, do, ltot):
    _, nh, l, hd = qkv.shape
    blk = _tile(l, (SB_BLOCK, 128))
    scale = hd ** -0.5
    t_inc = _tri(blk, lambda r, c: r <= c)
    t_exc = _tri(blk, lambda r, c: r < c)

    def body(q_ref, k_ref, v_ref, do_ref, ltot_ref, tinc_ref, texc_ref, dq_ref, dk_ref, dv_ref):
        i = pl.program_id(1)

        @pl.when(i == 0)
        def _():
            dk_ref[...] = jnp.zeros_like(dk_ref)
            dv_ref[...] = jnp.zeros_like(dv_ref)

        q = q_ref[...]
        d_o = do_ref[...]
        ltot = ltot_ref[...]
        tinc = tinc_ref[...]
        texc = texc_ref[...]

        def tile(j, carry, masked):
            c_l, c_p, dq = carry
            rows = pl.ds(pl.multiple_of(j * blk, blk), blk)
            k = k_ref[rows, :]
            v = v_ref[rows, :]
            z = lax.dot_general(q, k, _NT, preferred_element_type=F32) * scale
            lk = -_softplus(z)
            if masked:
                causal = lax.broadcasted_iota(jnp.int32, z.shape, 1) < lax.broadcasted_iota(jnp.int32, z.shape, 0)
                lk = jnp.where(causal, lk, 0.0)
            log_beta = z + lk
            w = jnp.exp(log_beta + (ltot - c_l) - _dot_hilo(lk, tinc))
            if masked:
                w = jnp.where(causal, w, 0.0)
            g = w * lax.dot_general(d_o, v, _NT, preferred_element_type=F32)
            dz = g - jnp.exp(log_beta) * (g + c_p + _dot_hilo(g, texc))
            if masked:
                dz = jnp.where(causal, dz, 0.0)
            dzb = (dz * scale).astype(BF16)
            dq = dq + jnp.dot(dzb, k, preferred_element_type=F32)
            dk_ref[rows, :] += lax.dot_general(dzb, q, _TN, preferred_element_type=F32)
            dv_ref[rows, :] += lax.dot_general(w.astype(BF16), d_o, _TN, preferred_element_type=F32)
            return c_l + jnp.sum(lk, axis=1, keepdims=True), c_p + jnp.sum(g, axis=1, keepdims=True), dq

        zero = jnp.zeros((blk, 1), F32)
        carry = lax.fori_loop(0, i, lambda j, c: tile(j, c, False), (zero, zero, jnp.zeros((blk, hd), F32)))
        dq_ref[...] = tile(i, carry, True)[2]

    q_spec = pl.BlockSpec((None, None, blk, hd), lambda h, i: (0, h, i, 0))
    full = lambda s: pl.BlockSpec((None, None, l, hd), lambda h, i: (s, h, 0, 0))
    tri_spec = pl.BlockSpec((blk, blk), lambda h, i: (0, 0))
    dq, dk, dv = pl.pallas_call(
        body,
        name="sb_attn_bwd",
        grid=(nh, l // blk),
        in_specs=[q_spec, full(1), full(2), pl.BlockSpec((None, blk, hd), lambda h, i: (h, i, 0)),
                  pl.BlockSpec((None, blk, 1), lambda h, i: (h, i, 0)), tri_spec, tri_spec],
        out_specs=[pl.BlockSpec((None, blk, hd), lambda h, i: (h, i, 0)), pl.BlockSpec((None, l, hd), lambda h, i: (h, 0, 0)),
                   pl.BlockSpec((None, l, hd), lambda h, i: (h, 0, 0))],
        out_shape=[jax.ShapeDtypeStruct((nh, l, hd), F32)] * 3,
        compiler_params=_params(("parallel", "arbitrary")),
    )(qkv, qkv, qkv, do, ltot, t_inc, t_exc)
    return jnp.stack([dq, dk, dv])


def _heads_major(x, n):
    l = x.shape[0]
    return x.reshape(l, n, -1, SB_HEAD_DIM).transpose(1, 2, 0, 3)


def _heads_minor(x):
    n, nh, l, hd = x.shape
    return x.transpose(2, 0, 1, 3).reshape(l, n * nh * hd)


def sb_fwd(hn, p):
    qkv = _heads_major(mm(hn, p["w_qkv"], name="sb_qkv", out_dtypes=(BF16,)), 3)
    o, ltot = sb_attn_fwd(qkv)
    o_flat = _heads_minor(o[None])
    return o_flat, (hn, qkv, ltot, o_flat)


def sb_bwd(dmixed, saved, p):
    hn, qkv, ltot, o_flat = saved
    do = _heads_major(mm(dmixed, p["w_out"], tb=True, name="sb_do", out_dtypes=(BF16,)), 1)[0]
    dw_out = mm(o_flat, dmixed, ta=True, name="sb_dwout")
    dqkv = _heads_minor(sb_attn_bwd(qkv, do, ltot).astype(BF16))
    dw_qkv = mm(hn, dqkv, ta=True, name="sb_dwqkv")
    dhn = mm(dqkv, p["w_qkv"], tb=True, name="sb_dhn", out_dtypes=(BF16,), tk=3072)
    return dhn, dict(w_qkv=dw_qkv, w_out=dw_out)


S5_CHUNK = 128
S5_SLAB_GROUPS = 8


def _s5_discretise(lr, li, ldt, bre, bim):
    dt = jnp.exp(ldt)
    mag = jnp.exp(lr * dt)
    lbr = mag * jnp.cos(li * dt)
    lbi = mag * jnp.sin(li * dt)
    inv = 1.0 / (lr * lr + li * li)
    cr = ((lbr - 1.0) * lr + lbi * li) * inv
    ci = (lbi * lr - (lbr - 1.0) * li) * inv
    return lbr, lbi, cr * bre - ci * bim, cr * bim + ci * bre


def _s5_cols(lam_re, lam_im, log_dt, b_re, b_im):
    g, p = lam_re.shape
    col = lambda x: x.reshape(g * p, 1)
    ldt = jnp.broadcast_to(log_dt[:, None], (g, p))
    return col(lam_re), col(lam_im), col(ldt), b_re.reshape(g * p, -1), b_im.reshape(g * p, -1)


def _slab_b(bbar):
    sg = S5_SLAB_GROUPS
    gp, h = bbar.shape
    p = S5_STATE
    x = bbar.reshape(gp // (sg * p), sg, p, h)
    return jnp.einsum("kaph,ab->kahbp", x, jnp.eye(sg, dtype=x.dtype)).reshape(-1, sg * h, sg * p)


def _unslab_b(dslab):
    sg, p = S5_SLAB_GROUPS, S5_STATE
    nk, sh, _ = dslab.shape
    h = sh // sg
    x = dslab.reshape(nk, sg, h, sg, p)
    return jnp.einsum("kahbp,ab->kaph", x, jnp.eye(sg, dtype=x.dtype)).reshape(nk * sg * p, h)


def _slab_c(c):
    sg = S5_SLAB_GROUPS
    g, h, p = c.shape
    x = c.reshape(g // sg, sg, h, p)
    return jnp.einsum("kahp,ab->kapbh", x, jnp.eye(sg, dtype=x.dtype)).reshape(-1, sg * p, sg * h)


def _unslab_c(dslab):
    sg, p = S5_SLAB_GROUPS, S5_STATE
    nk, _, sh = dslab.shape
    h = sh // sg
    x = dslab.reshape(nk, sg, p, sg, h)
    return jnp.einsum("kapbh,ab->kahp", x, jnp.eye(sg, dtype=x.dtype)).reshape(nk * sg, h, p)


def _cmul_scan(pre, pim, xre, xim, shift):
    d = 1
    while d < xre.shape[0]:
        sre, sim = shift(xre, d, 0.0), shift(xim, d, 0.0)
        xre, xim = xre + pre * sre - pim * sim, xim + pre * sim + pim * sre
        pre, pim = pre * pre - pim * pim, 2.0 * pre * pim
        d *= 2
    return xre, xim


def s5_scan_fwd(u, lbr, lbi, bbd_re, bbd_im, cbd_re, cbd_imn, d_skip):
    l, w = u.shape
    n = lbr.shape[1]
    nk, cw, sw = bbd_re.shape
    t = _tile(l, (S5_CHUNK, 64, 32, 16, 8))

    def body(u_ref, lbr_ref, lbi_ref, bre_ref, bim_ref, cre_ref, cim_ref, d_ref, sre_ref, sim_ref, y_ref, z_ref, pre_ref, pim_ref):
        @pl.when(pl.program_id(0) == 0)
        def _():
            pre_ref[...] = jnp.zeros_like(pre_ref)
            pim_ref[...] = jnp.zeros_like(pim_ref)

        uu = u_ref[...]
        ub = uu.astype(BF16)
        lre, lim = lbr_ref[...], lbi_ref[...]
        xre = jnp.concatenate([jnp.dot(ub[:, k * cw:(k + 1) * cw], bre_ref[k], preferred_element_type=F32) for k in range(nk)], axis=1)
        xim = jnp.concatenate([jnp.dot(ub[:, k * cw:(k + 1) * cw], bim_ref[k], preferred_element_type=F32) for k in range(nk)], axis=1)
        first = lax.broadcasted_iota(jnp.int32, xre.shape, 0) == 0
        pr, pi = pre_ref[pl.ds(7, 1), :], pim_ref[pl.ds(7, 1), :]
        xre = xre + jnp.where(first, lre * pr - lim * pi, 0.0)
        xim = xim + jnp.where(first, lre * pi + lim * pr, 0.0)
        sre, sim = _cmul_scan(lre, lim, xre, xim, _shift_down)
        sre_ref[...] = sre
        sim_ref[...] = sim
        pre_ref[...] = sre[t - 8:, :]
        pim_ref[...] = sim[t - 8:, :]
        sreb, simb = sre.astype(BF16), sim.astype(BF16)
        y = jnp.concatenate(
            [jnp.dot(sreb[:, k * sw:(k + 1) * sw], cre_ref[k], preferred_element_type=F32)
             + jnp.dot(simb[:, k * sw:(k + 1) * sw], cim_ref[k], preferred_element_type=F32) for k in range(nk)], axis=1)
        y = y + d_ref[...] * uu
        y_ref[...] = y
        z_ref[...] = _gelu(y).astype(z_ref.dtype)

    def whole(x):
        nd = x.ndim
        return pl.BlockSpec(x.shape, lambda i: (0,) * nd)

    consts = [lbr, lbi, bbd_re, bbd_im, cbd_re, cbd_imn, d_skip]
    row = lambda c: pl.BlockSpec((t, c), lambda i: (i, 0))
    return pl.pallas_call(
        body,
        name="s5_scan_fwd",
        grid=(l // t,),
        in_specs=[row(w)] + [whole(c) for c in consts],
        out_specs=[row(n), row(n), row(w), row(w)],
        out_shape=[jax.ShapeDtypeStruct((l, n), F32), jax.ShapeDtypeStruct((l, n), F32), jax.ShapeDtypeStruct((l, w), F32),
                   jax.ShapeDtypeStruct((l, w), BF16)],
        scratch_shapes=[pltpu.VMEM((8, n), F32), pltpu.VMEM((8, n), F32)],
        compiler_params=_params(("arbitrary",)),
    )(u, *consts)


def s5_scan_bwd(dz, y, u, sre, sim, lbr, lbi, bbd_re, bbd_im, cbd_re, cbd_imn, d_skip):
    l, w = u.shape
    n = lbr.shape[1]
    nk, cw, sw = bbd_re.shape
    t = _tile(l, (S5_CHUNK, 64, 32, 16, 8))
    nc = l // t

    def body(dz_ref, y_ref, u_ref, sre_ref, sim_ref, hre_ref, him_ref, lbr_ref, lbi_ref, bre_ref, bim_ref, cre_ref, cim_ref,
             d_ref, du_ref, dlr_ref, dli_ref, dbre_ref, dbim_ref, dcre_ref, dcim_ref, dd_ref, nre_ref, nim_ref):
        i = pl.program_id(0)
        has_prev = (i < nc - 1).astype(F32)

        @pl.when(i == 0)
        def _():
            nre_ref[...] = jnp.zeros_like(nre_ref)
            nim_ref[...] = jnp.zeros_like(nim_ref)

        uu = u_ref[...]
        ub = uu.astype(BF16)
        lre, lim = lbr_ref[...], lbi_ref[...]
        _, gelu_vjp = jax.vjp(_gelu, y_ref[...])
        dy = gelu_vjp(dz_ref[...].astype(F32))[0]
        dyb = dy.astype(BF16)
        gre = jnp.concatenate([lax.dot_general(dyb[:, k * cw:(k + 1) * cw], cre_ref[k], _NT, preferred_element_type=F32)
                               for k in range(nk)], axis=1)
        gim = jnp.concatenate([lax.dot_general(dyb[:, k * cw:(k + 1) * cw], cim_ref[k], _NT, preferred_element_type=F32)
                               for k in range(nk)], axis=1)
        last = lax.broadcasted_iota(jnp.int32, gre.shape, 0) == t - 1
        nr, ni = nre_ref[pl.ds(0, 1), :], nim_ref[pl.ds(0, 1), :]
        gre = gre + jnp.where(last, lre * nr + lim * ni, 0.0)
        gim = gim + jnp.where(last, lre * ni - lim * nr, 0.0)
        dsre, dsim = _cmul_scan(lre, -lim, gre, gim, _shift_up)
        nre_ref[...] = dsre[:8, :]
        nim_ref[...] = dsim[:8, :]
        dsreb, dsimb = dsre.astype(BF16), dsim.astype(BF16)
        s_re, s_im = sre_ref[...], sim_ref[...]
        du = jnp.concatenate(
            [lax.dot_general(dsreb[:, k * sw:(k + 1) * sw], bre_ref[k], _NT, preferred_element_type=F32)
             + lax.dot_general(dsimb[:, k * sw:(k + 1) * sw], bim_ref[k], _NT, preferred_element_type=F32) for k in range(nk)],
            axis=1)
        du_ref[...] = (du + d_ref[...] * dy).astype(du_ref.dtype)
        pre = _rows_before(s_re, hre_ref[...] * has_prev, 1)
        pim = _rows_before(s_im, him_ref[...] * has_prev, 1)
        sreb, simb = s_re.astype(BF16), s_im.astype(BF16)
        sums = [
            (dlr_ref, jnp.sum(dsre * pre + dsim * pim, axis=0, keepdims=True)),
            (dli_ref, jnp.sum(dsim * pre - dsre * pim, axis=0, keepdims=True)),
            (dbre_ref, jnp.stack([lax.dot_general(ub[:, k * cw:(k + 1) * cw], dsreb[:, k * sw:(k + 1) * sw], _TN,
                                                  preferred_element_type=F32) for k in range(nk)])),
            (dbim_ref, jnp.stack([lax.dot_general(ub[:, k * cw:(k + 1) * cw], dsimb[:, k * sw:(k + 1) * sw], _TN,
                                                  preferred_element_type=F32) for k in range(nk)])),
            (dcre_ref, jnp.stack([lax.dot_general(sreb[:, k * sw:(k + 1) * sw], dyb[:, k * cw:(k + 1) * cw], _TN,
                                                  preferred_element_type=F32) for k in range(nk)])),
            (dcim_ref, jnp.stack([lax.dot_general(simb[:, k * sw:(k + 1) * sw], dyb[:, k * cw:(k + 1) * cw], _TN,
                                                  preferred_element_type=F32) for k in range(nk)])),
            (dd_ref, jnp.sum(dy * uu, axis=0, keepdims=True)),
        ]

        @pl.when(i == 0)
        def _():
            for ref, val in sums:
                ref[...] = val

        @pl.when(i > 0)
        def _():
            for ref, val in sums:
                ref[...] += val

    def whole(shape):
        nd = len(shape)
        return pl.BlockSpec(tuple(shape), lambda i: (0,) * nd)

    consts = [lbr, lbi, bbd_re, bbd_im, cbd_re, cbd_imn, d_skip]
    t8 = t // 8
    rev = lambda i: nc - 1 - i
    halo = lambda i: jnp.maximum(rev(i) * t8 - 1, 0)
    row = lambda c: pl.BlockSpec((t, c), lambda i: (rev(i), 0))
    sum_shapes = [lbr.shape, lbi.shape, bbd_re.shape, bbd_im.shape, cbd_re.shape, cbd_imn.shape, d_skip.shape]
    res = pl.pallas_call(
        body,
        name="s5_scan_bwd",
        grid=(nc,),
        in_specs=[row(w), row(w), row(w), row(n), row(n), pl.BlockSpec((8, n), lambda i: (halo(i), 0)),
                  pl.BlockSpec((8, n), lambda i: (halo(i), 0))] + [whole(c.shape) for c in consts],
        out_specs=[row(w)] + [whole(s) for s in sum_shapes],
        out_shape=[jax.ShapeDtypeStruct((l, w), BF16)] + [jax.ShapeDtypeStruct(tuple(s), F32) for s in sum_shapes],
        scratch_shapes=[pltpu.VMEM((8, n), F32), pltpu.VMEM((8, n), F32)],
        compiler_params=_params(("arbitrary",)),
    )(dz, y, u, sre, sim, sre, sim, *consts)
    return res[0], res[1:]


def _glu(vg):
    w = vg.shape[1] // 2
    return vg[:, :w] * _sigmoid(vg[:, w:])


def s5_fwd(hn, h, p):
    cols = _s5_cols(p["lam_re"], p["lam_im"], p["log_dt"], p["b_re"], p["b_im"])
    gp, hh = cols[3].shape
    lbr, lbi, bbr, bbi = rowwise(_s5_discretise, list(cols), [], [(1, F32), (1, F32), (hh, F32), (hh, F32)],
                                 name="s5_discretise", tm=512)
    consts = (lbr.reshape(1, gp), lbi.reshape(1, gp), _slab_b(bbr).astype(BF16), _slab_b(bbi).astype(BF16),
              _slab_c(p["c_re"]).astype(BF16), _slab_c(-p["c_im"]).astype(BF16), p["d"])
    u = mm(hn, p["w_in"], name="s5_in")
    sre, sim, y, z = s5_scan_fwd(u, *consts)
    vg = mm(z, p["w_out"], name="s5_out", out_dtypes=(BF16,))
    h_new = rowwise(lambda a, r: r + _glu(a.astype(F32)), [vg, h], [], [(h.shape[1], F32)], name="s5_glu")
    return h_new, (hn, u, sre, sim, y, z, vg, cols, consts)


def s5_bwd(dh_new, saved, p):
    hn, u, sre, sim, y, z, vg, cols, consts = saved

    def glu_bwd(a, dm):
        _, vjp = jax.vjp(_glu, a.astype(F32))
        return vjp(dm)[0]

    dvg = rowwise(glu_bwd, [vg, dh_new], [], [(vg.shape[1], BF16)], name="s5_dglu")
    dw_out = mm(z, dvg, ta=True, name="s5_dwout")
    dz = mm(dvg, p["w_out"], tb=True, name="s5_dz", out_dtypes=(BF16,), tk=2048)
    du, (dlbr, dlbi, dbbr, dbbi, dcre, dcimn, dd) = s5_scan_bwd(dz, y, u, sre, sim, *consts)
    dw_in = mm(hn, du, ta=True, name="s5_dwin")
    dhn = mm(du, p["w_in"], tb=True, name="s5_dhn", out_dtypes=(BF16,))
    gp = cols[0].shape[0]

    def disc_bwd(lr, li, ldt, bre, bim, g0, g1, g2, g3):
        _, vjp = jax.vjp(_s5_discretise, lr, li, ldt, bre, bim)
        return vjp((g0, g1, g2, g3))

    cot = (dlbr.reshape(gp, 1), dlbi.reshape(gp, 1), _unslab_b(dbbr), _unslab_b(dbbi))
    dlr, dli, dldt, dbre, dbim = rowwise(disc_bwd, list(cols + cot), [], [(c.shape[1], F32) for c in cols],
                                         name="s5_discretise_bwd", tm=512)
    g_, p_ = p["lam_re"].shape
    grads = dict(w_in=dw_in, lam_re=dlr.reshape(g_, p_), lam_im=dli.reshape(g_, p_), log_dt=dldt.reshape(g_, p_).sum(axis=1),
                 b_re=dbre.reshape(p["b_re"].shape), b_im=dbim.reshape(p["b_im"].shape), c_re=_unslab_c(dcre),
                 c_im=-_unslab_c(dcimn), d=dd, w_out=dw_out)
    return dhn, grads


MESH = pl.DeviceIdType.MESH
N_CHIPS = 4
N_DEVICES = 8


def _place():
    x, y, c = lax.axis_index("x"), lax.axis_index("y"), lax.axis_index("c")
    return x, y, c, [(1 - x, y), (x, 1 - y), (1 - x, 1 - y)]


def _hbm_call(body, name, ins, out_shapes, n_remote, n_local=0):
    hbm = pl.BlockSpec(memory_space=pltpu.HBM)
    scratch = [pltpu.SemaphoreType.DMA((n_remote,)), pltpu.SemaphoreType.DMA((n_remote,))]
    if n_local:
        scratch.append(pltpu.SemaphoreType.DMA((n_local,)))
    return pl.pallas_call(
        body,
        name=name,
        in_specs=[hbm] * len(ins),
        out_specs=[hbm] * len(out_shapes),
        out_shape=out_shapes,
        scratch_shapes=scratch,
    )(*ins)


def gather_chips(shards):
    n = len(shards)

    def body(*refs):
        ins, outs = refs[:n], refs[n:2 * n]
        send, recv, local = refs[2 * n:]
        x, y, c, chips = _place()
        me = 2 * x + y
        pending = []
        for t in range(n):
            own = pltpu.make_async_copy(ins[t], outs[t].at[me], local.at[t])
            own.start()
            pending.append(own)
            for k, (px, py) in enumerate(chips):
                out_going = pltpu.make_async_remote_copy(ins[t], outs[t].at[me], send.at[3 * t + k], recv.at[3 * t + k],
                                                         device_id=(px, py, c), device_id_type=MESH)
                out_going.start()
                pending.append(pltpu.make_async_remote_copy(ins[t], outs[t].at[2 * px + py], send.at[3 * t + k],
                                                            recv.at[3 * t + k], device_id=(px, py, c), device_id_type=MESH))
        for cp in pending:
            cp.wait()

    return _hbm_call(body, "gather_chips", shards, [jax.ShapeDtypeStruct((N_CHIPS,) + s.shape, s.dtype) for s in shards],
                     3 * n, n)


def scatter_chips(blocked):
    n = len(blocked)

    def body(*refs):
        ins, outs = refs[:n], refs[n:2 * n]
        send, recv = refs[2 * n:]
        x, y, c, chips = _place()
        pending = []
        for t in range(n):
            for k, (px, py) in enumerate(chips):
                cp = pltpu.make_async_remote_copy(ins[t].at[2 * px + py], outs[t].at[k], send.at[3 * t + k], recv.at[3 * t + k],
                                                  device_id=(px, py, c), device_id_type=MESH)
                cp.start()
                pending.append(cp)
        for cp in pending:
            cp.wait()

    return _hbm_call(body, "scatter_chips", blocked, [jax.ShapeDtypeStruct((3,) + b.shape[1:], b.dtype) for b in blocked], 3 * n)


def swap_cores(arrays):
    n = len(arrays)

    def body(*refs):
        ins, outs = refs[:n], refs[n:2 * n]
        send, recv = refs[2 * n:]
        x, y, c, _ = _place()
        pending = []
        for t in range(n):
            cp = pltpu.make_async_remote_copy(ins[t], outs[t], send.at[t], recv.at[t], device_id=(x, y, 1 - c),
                                              device_id_type=MESH)
            cp.start()
            pending.append(cp)
        for cp in pending:
            cp.wait()

    return _hbm_call(body, "swap_cores", arrays, [jax.ShapeDtypeStruct(a.shape, a.dtype) for a in arrays], n)


def gather_devices(buf):
    def body(in_ref, out_ref, send, recv, local):
        x, y, c, _ = _place()
        own = pltpu.make_async_copy(in_ref, out_ref.at[4 * x + 2 * y + c], local.at[0])
        own.start()
        pending = [own]
        for k in range(1, N_DEVICES):
            px = x ^ ((k >> 2) & 1)
            py = y ^ ((k >> 1) & 1)
            pc = c ^ (k & 1)
            going = pltpu.make_async_remote_copy(in_ref, out_ref.at[4 * x + 2 * y + c], send.at[k - 1], recv.at[k - 1],
                                                 device_id=(px, py, pc), device_id_type=MESH)
            going.start()
            pending.append(pltpu.make_async_remote_copy(in_ref, out_ref.at[4 * px + 2 * py + pc], send.at[k - 1], recv.at[k - 1],
                                                        device_id=(px, py, pc), device_id_type=MESH))
        for cp in pending:
            cp.wait()

    return _hbm_call(body, "gather_devices", [buf], [jax.ShapeDtypeStruct((N_DEVICES,) + buf.shape, buf.dtype)],
                     N_DEVICES - 1, 1)[0]


def _adamw(w, g, m, v):
    m = ADAM_B1 * m + (1.0 - ADAM_B1) * g
    v = ADAM_B2 * v + (1.0 - ADAM_B2) * (g * g)
    m_hat = m / (1.0 - ADAM_B1 ** ADAM_STEP)
    v_hat = v / (1.0 - ADAM_B2 ** ADAM_STEP)
    return -ADAM_LR * (m_hat / (jnp.sqrt(v_hat) + ADAM_EPS) + ADAM_WD * w), m, v


def _rows2d(a):
    return a.reshape(-1, a.shape[-1])


WEIGHTS = ["ffn1_norm", "ffn1_w_in", "ffn1_w_out", "mix_norm", "ffn2_norm", "ffn2_w_in", "ffn2_w_out", "final_norm",
           "s5_w_in", "s5_lam_re", "s5_lam_im", "s5_log_dt", "s5_b_re", "s5_b_im", "s5_c_re", "s5_c_im", "s5_d", "s5_w_out",
           "sb_w_qkv", "sb_w_out", "lru_w_in", "lru_conv_w", "lru_conv_b", "lru_w_a", "lru_b_a", "lru_w_x", "lru_b_x",
           "lru_lambda", "lru_w_out"]
INPUTS = ["x"] + WEIGHTS + ["loss_target"] + ["m_" + n for n in WEIGHTS] + ["v_" + n for n in WEIGHTS]
SHARDED_BIG = dict(ffn1_w_in=2, ffn1_w_out=1, ffn2_w_in=2, ffn2_w_out=1, s5_w_in=1, s5_w_out=2, sb_w_qkv=2, sb_w_out=1,
                   lru_w_in=2, lru_w_a=2, lru_w_x=2, lru_w_out=1)
SHARDED_SMALL = dict(s5_d=1, lru_conv_w=2, lru_conv_b=1, lru_b_a=2, lru_b_x=2, lru_lambda=1)
REPLICATED = [n for n in WEIGHTS if n not in SHARDED_BIG and n not in SHARDED_SMALL]
PACK_LANES = 128
PACK_ROW_ALIGN = 8


def _unblock(g, d):
    full = jnp.moveaxis(g, 0, d)
    return full.reshape(full.shape[:d] + (full.shape[d] * full.shape[d + 1],) + full.shape[d + 2:])


def _block(full, d):
    s = full.shape[d] // N_CHIPS
    return jnp.moveaxis(full.reshape(full.shape[:d] + (N_CHIPS, s) + full.shape[d + 1:]), d, 0)


def _pack(arrays, lead=()):
    nl = len(lead)
    flat = jnp.concatenate([a.reshape(lead + (-1,)) for a in arrays], axis=nl)
    quantum = PACK_LANES * PACK_ROW_ALIGN
    pad = (-flat.shape[nl]) % quantum
    flat = jnp.pad(flat, [(0, 0)] * nl + [(0, pad)])
    return flat.reshape(lead + (-1, PACK_LANES))


def _unpack(packed, shapes, lead=()):
    nl = len(lead)
    flat = packed.reshape(lead + (-1,))
    out, off = [], 0
    for s in shapes:
        size = math.prod(s)
        out.append(lax.slice_in_dim(flat, off, off + size, axis=nl).reshape(lead + tuple(s)))
        off += size
    return out


def _forward_backward(x, target, w, norms):
    depth = norms["ffn1_norm"].shape[0]
    n_s5 = w["s5_w_in"].shape[0]

    def mixer_params(layer):
        kind, j = layer % N_MIXERS, layer // N_MIXERS
        if kind == 0:
            return kind, j, dict(w_in=w["s5_w_in"][j], lam_re=w["s5_lam_re"][j], lam_im=w["s5_lam_im"][j], log_dt=w["s5_log_dt"][j],
                                 b_re=w["s5_b_re"][j], b_im=w["s5_b_im"][j], c_re=w["s5_c_re"][j], c_im=w["s5_c_im"][j],
                                 d=w["s5_d"][j].reshape(1, -1), w_out=w["s5_w_out"][j])
        if kind == 1:
            return kind, j, dict(w_qkv=w["sb_w_qkv"][j], w_out=w["sb_w_out"][j])
        return kind, j, dict(w_in=w["lru_w_in"][j], conv_w=w["lru_conv_w"][j], conv_b=w["lru_conv_b"][j].reshape(1, -1),
                             wa=w["lru_w_a"][j], ba=w["lru_b_a"][j].reshape(1, -1), wx=w["lru_w_x"][j],
                             bx=w["lru_b_x"][j].reshape(1, -1), lam=w["lru_lambda"][j].reshape(1, -1), w_out=w["lru_w_out"][j])

    h = x
    tape = []
    for layer in range(depth):
        h, s1 = ffn_fwd(h, norms["ffn1_norm"][layer], w["ffn1_w_in"][layer], w["ffn1_w_out"][layer], "ffn")
        kind, j, p = mixer_params(layer)
        h_mix_in = h
        hn = rms_fwd(h, norms["mix_norm"][layer], "mix_norm")
        if kind == 0:
            h, sm = s5_fwd(hn, h, p)
        elif kind == 1:
            o_flat, sm = sb_fwd(hn, p)
            h = mm(o_flat, p["w_out"], name="mix_out", extras=(h,), epilogue=lambda acc, res: res + acc)
        else:
            y, sm = lru_fwd(hn, p)
            h = mm(y, p["w_out"], name="mix_out", extras=(h,), epilogue=lambda acc, res: res + acc)
        h, s2 = ffn_fwd(h, norms["ffn2_norm"][layer], w["ffn2_w_in"][layer], w["ffn2_w_out"][layer], "ffn")
        tape.append((s1, h_mix_in, sm, s2))

    loss, dh, g_final = loss_fwd_bwd(h, norms["final_norm"], target)

    per_layer = {n: [None] * depth for n in ("ffn1_norm", "ffn1_w_in", "ffn1_w_out", "mix_norm", "ffn2_norm", "ffn2_w_in", "ffn2_w_out")}
    mix = {}
    for layer in reversed(range(depth)):
        s1, h_mix_in, sm, s2 = tape[layer]
        dh, dg, dwi, dwo = ffn_bwd(dh, s2, norms["ffn2_norm"][layer], w["ffn2_w_in"][layer], w["ffn2_w_out"][layer], "ffn")
        per_layer["ffn2_norm"][layer], per_layer["ffn2_w_in"][layer], per_layer["ffn2_w_out"][layer] = dg, dwi, dwo
        kind, j, p = mixer_params(layer)
        if kind == 0:
            dhn, g = s5_bwd(dh, sm, p)
            names = dict(w_in="s5_w_in", lam_re="s5_lam_re", lam_im="s5_lam_im", log_dt="s5_log_dt", b_re="s5_b_re", b_im="s5_b_im",
                         c_re="s5_c_re", c_im="s5_c_im", d="s5_d", w_out="s5_w_out")
        elif kind == 1:
            dhn, g = sb_bwd(dh, sm, p)
            names = dict(w_qkv="sb_w_qkv", w_out="sb_w_out")
        else:
            dhn, g = lru_bwd(dh, sm, p)
            names = dict(w_in="lru_w_in", conv_w="lru_conv_w", conv_b="lru_conv_b", wa="lru_w_a", ba="lru_b_a", wx="lru_w_x",
                         bx="lru_b_x", lam="lru_lambda", w_out="lru_w_out")
        for k, full_name in names.items():
            mix.setdefault(full_name, {})[j] = g[k].reshape(w[full_name].shape[1:])
        dh, dg = rms_bwd(h_mix_in, norms["mix_norm"][layer], dhn, dh, "mix_dnorm")
        per_layer["mix_norm"][layer] = dg
        dh, dg, dwi, dwo = ffn_bwd(dh, s1, norms["ffn1_norm"][layer], w["ffn1_w_in"][layer], w["ffn1_w_out"][layer], "ffn")
        per_layer["ffn1_norm"][layer], per_layer["ffn1_w_in"][layer], per_layer["ffn1_w_out"][layer] = dg, dwi, dwo

    grads = {n: jnp.stack(v) for n, v in per_layer.items()}
    grads["final_norm"] = g_final
    for n, by_j in mix.items():
        grads[n] = jnp.stack([by_j[j] for j in range(len(by_j))])
    return loss, dh, grads


def kernel(x, ffn1_norm, ffn1_w_in, ffn1_w_out, mix_norm, ffn2_norm, ffn2_w_in, ffn2_w_out, final_norm, s5_w_in,
           s5_lam_re, s5_lam_im, s5_log_dt, s5_b_re, s5_b_im, s5_c_re, s5_c_im, s5_d, s5_w_out, sb_w_qkv,
           sb_w_out, lru_w_in, lru_conv_w, lru_conv_b, lru_w_a, lru_b_a, lru_w_x, lru_b_x, lru_lambda,
           lru_w_out, loss_target, m_ffn1_norm, m_ffn1_w_in, m_ffn1_w_out, m_mix_norm, m_ffn2_norm,
           m_ffn2_w_in, m_ffn2_w_out, m_final_norm, m_s5_w_in, m_s5_lam_re, m_s5_lam_im, m_s5_log_dt,
           m_s5_b_re, m_s5_b_im, m_s5_c_re, m_s5_c_im, m_s5_d, m_s5_w_out, m_sb_w_qkv, m_sb_w_out, m_lru_w_in,
           m_lru_conv_w, m_lru_conv_b, m_lru_w_a, m_lru_b_a, m_lru_w_x, m_lru_b_x, m_lru_lambda, m_lru_w_out,
           v_ffn1_norm, v_ffn1_w_in, v_ffn1_w_out, v_mix_norm, v_ffn2_norm, v_ffn2_w_in, v_ffn2_w_out,
           v_final_norm, v_s5_w_in, v_s5_lam_re, v_s5_lam_im, v_s5_log_dt, v_s5_b_re, v_s5_b_im, v_s5_c_re,
           v_s5_c_im, v_s5_d, v_s5_w_out, v_sb_w_qkv, v_sb_w_out, v_lru_w_in, v_lru_conv_w, v_lru_conv_b,
           v_lru_w_a, v_lru_b_a, v_lru_w_x, v_lru_b_x, v_lru_lambda, v_lru_w_out):
    a = dict(locals())
    assert list(a) == INPUTS
    x, y, c, _ = _place()
    chip = 2 * x + y
    everyone = ("x", "y", "c")

    big = list(SHARDED_BIG)
    small = list(SHARDED_SMALL)
    small_packed = _pack([a[n] for n in small])
    gathered = gather_chips([a[n].astype(BF16) for n in big] + [small_packed])
    w = {n: _unblock(g, SHARDED_BIG[n] + 0) for n, g in zip(big, gathered[:-1])}
    small_blocks = _unpack(gathered[-1], [a[n].shape for n in small], lead=(N_CHIPS,))
    w.update({n: _unblock(g, SHARDED_SMALL[n]) for n, g in zip(small, small_blocks)})
    for n in ("s5_lam_re", "s5_lam_im", "s5_log_dt", "s5_b_re", "s5_b_im", "s5_c_re", "s5_c_im"):
        w[n] = a[n]
    norms = {n: a[n] for n in ("ffn1_norm", "mix_norm", "ffn2_norm", "final_norm")}

    loss, dx, grads = _forward_backward(a["x"][0], a["loss_target"][0], w, norms)
    loss = lax.psum(loss, everyone)

    own = {n: lax.dynamic_slice_in_dim(grads[n], chip * a[n].shape[d], a[n].shape[d], axis=d)
           for n, d in {**SHARDED_BIG, **SHARDED_SMALL}.items()}
    small_send = _pack([_block(grads[n], SHARDED_SMALL[n]) for n in small], lead=(N_CHIPS,))
    received = scatter_chips([_block(grads[n], SHARDED_BIG[n]).astype(BF16) for n in big] + [small_send])
    own_list = [own[n] for n in big] + [_pack([own[n] for n in small])]
    partial = []
    for mine, got in zip(own_list, received):
        rows = _rows2d(mine)
        got3 = got.reshape((3,) + rows.shape)
        partial.append(rowwise(lambda o, r: ((o + r[0].astype(F32)) + r[1].astype(F32)) + r[2].astype(F32), [rows, got3], [],
                               [(rows.shape[1], F32)], name="sum_chips"))
    other = swap_cores(partial)

    def adam_sharded(wv, ga, gb, m, v):
        shape = wv.shape
        res = rowwise(lambda w_, a_, b_, mm_, vv_: (a_ + b_,) + _adamw(w_, a_ + b_, mm_, vv_),
                      [_rows2d(wv), ga, gb, _rows2d(m), _rows2d(v)], [], [(shape[-1], F32)] * 4, name="adamw_sharded")
        return [r.reshape(shape) for r in res]

    out_grad, out_delta, out_m, out_v = {}, {}, {}, {}
    for i, n in enumerate(big):
        out_grad[n], out_delta[n], out_m[n], out_v[n] = adam_sharded(a[n], partial[i], other[i], a["m_" + n], a["v_" + n])
    small_shapes = [a[n].shape for n in small]
    sp = [_pack([a[pre + n] for n in small]) for pre in ("", "m_", "v_")]
    g_, d_, m_, v_ = adam_sharded(sp[0], partial[-1], other[-1], sp[1], sp[2])
    for n, gg, dd, mm_, vv in zip(small, _unpack(g_, small_shapes), _unpack(d_, small_shapes), _unpack(m_, small_shapes),
                                  _unpack(v_, small_shapes)):
        out_grad[n], out_delta[n], out_m[n], out_v[n] = gg, dd, mm_, vv

    rep_shapes = [a[n].shape for n in REPLICATED]
    rep_all = gather_devices(_pack([grads[n] for n in REPLICATED]))
    rp = [_pack([a[pre + n] for n in REPLICATED]) for pre in ("", "m_", "v_")]

    def adam_rep(w_, g8, mm_, vv_):
        g = g8[0]
        for k in range(1, N_DEVICES):
            g = g + g8[k]
        return (g,) + _adamw(w_, g, mm_, vv_)

    g_, d_, m_, v_ = rowwise(adam_rep, [rp[0], rep_all, rp[1], rp[2]], [], [(PACK_LANES, F32)] * 4, name="adamw_replicated")
    for n, gg, dd, mm_, vv in zip(REPLICATED, _unpack(g_, rep_shapes), _unpack(d_, rep_shapes), _unpack(m_, rep_shapes),
                                  _unpack(v_, rep_shapes)):
        out_grad[n], out_delta[n], out_m[n], out_v[n] = gg, dd, mm_, vv

    return (loss, dx[None], *[out_grad[n] for n in WEIGHTS], *[out_delta[n] for n in WEIGHTS], *[out_m[n] for n in WEIGHTS],
            *[out_v[n] for n in WEIGHTS])
```

```python
import functools
import math

import jax
import jax.numpy as jnp
from jax import lax
from jax.experimental import pallas as pl
from jax.experimental.pallas import tpu as pltpu

F32 = jnp.float32
BF16 = jnp.bfloat16

VMEM_LIMIT_BYTES = 56 * 1024 * 1024

RMS_EPS = 1e-6
D_FF = 2816
S5_GROUP = 16
S5_STATE = 64
SB_HEAD_DIM = 64
LRU_BLOCK_WIDTH = 256
LRU_CONV = 4
LRU_C = 8.0
N_MIXERS = 3

ADAM_LR = 0.001
ADAM_B1 = 0.9
ADAM_B2 = 0.999
ADAM_EPS = 1e-08
ADAM_WD = 0.01
ADAM_STEP = 10


def _params(semantics):
    return pltpu.CompilerParams(dimension_semantics=semantics, vmem_limit_bytes=VMEM_LIMIT_BYTES)


def _tile(dim, prefs):
    for t in prefs:
        if t <= dim and dim % t == 0:
            return t
    return dim


def _mm_call(name, grid, a, a_spec, b, b_spec, dims, extras, outs, epilogue, acc_shape):
    n_extra, n_out, n_k = len(extras), len(outs), grid[-1]

    def body(a_ref, b_ref, *rest):
        extra_refs = rest[:n_extra]
        out_refs = rest[n_extra:n_extra + n_out]
        acc_ref = rest[n_extra + n_out]
        k = pl.program_id(len(grid) - 1)
        part = lax.dot_general(a_ref[...].astype(BF16), b_ref[...].astype(BF16), dims, preferred_element_type=F32)

        @pl.when(k == 0)
        def _():
            acc_ref[...] = part

        @pl.when(k > 0)
        def _():
            acc_ref[...] += part

        @pl.when(k == n_k - 1)
        def _():
            res = epilogue(acc_ref[...], *[r[...] for r in extra_refs])
            if not isinstance(res, (tuple, list)):
                res = (res,)
            for o_ref, r in zip(out_refs, res):
                o_ref[...] = r.astype(o_ref.dtype)

    sem = ("parallel",) * (len(grid) - 1) + ("arbitrary",)
    res = pl.pallas_call(
        body,
        name=name,
        grid=grid,
        in_specs=[a_spec, b_spec] + [s for _, s in extras],
        out_specs=[s for _, s in outs],
        out_shape=[s for s, _ in outs],
        scratch_shapes=[pltpu.VMEM(acc_shape, F32)],
        compiler_params=_params(sem),
    )(a, b, *[x for x, _ in extras])
    return res


def _fit(dim, target, align=128):
    best = None
    for t in range(align, min(dim, target) + 1, align):
        if dim % t == 0:
            best = t
    return best or dim


def mm(a, b, *, name, ta=False, tb=False, extras=(), epilogue=None, out_dtypes=(F32,), tm=1024, tn=1024, tk=1024):
    m, kdim = (a.shape[1], a.shape[0]) if ta else a.shape
    n = b.shape[0] if tb else b.shape[1]
    tm = _fit(m, tm)
    tn = _fit(n, tn)
    tk = _fit(kdim, tk)
    grid = (m // tm, n // tn, kdim // tk)
    a_spec = pl.BlockSpec((tk, tm), lambda i, j, k: (k, i)) if ta else pl.BlockSpec((tm, tk), lambda i, j, k: (i, k))
    b_spec = pl.BlockSpec((tn, tk), lambda i, j, k: (j, k)) if tb else pl.BlockSpec((tk, tn), lambda i, j, k: (k, j))
    dims = (((0 if ta else 1,), (1 if tb else 0,)), ((), ()))
    o_spec = pl.BlockSpec((tm, tn), lambda i, j, k: (i, j))
    if epilogue is None:
        epilogue = lambda acc: acc
    res = _mm_call(name, grid, a, a_spec, b, b_spec, dims, [(x, o_spec) for x in extras],
                   [(jax.ShapeDtypeStruct((m, n), dt), o_spec) for dt in out_dtypes], epilogue, (tm, tn))
    return res[0] if len(res) == 1 else res


def rowwise(fn, rows, consts, outs, sums=(), *, name, tm=256):
    m = rows[0].shape[0]
    tm = _tile(m, (tm, 128, 64, 32, 16, 8))
    n_rows, n_consts, n_outs = len(rows), len(consts), len(outs)

    def body(*refs):
        in_vals = [r[...] for r in refs[:n_rows + n_consts]]
        out_refs = refs[n_rows + n_consts:n_rows + n_consts + n_outs]
        sum_refs = refs[n_rows + n_consts + n_outs:]
        res = fn(*in_vals)
        if not isinstance(res, (tuple, list)):
            res = (res,)
        for o_ref, r in zip(out_refs, res[:n_outs]):
            o_ref[...] = r.astype(o_ref.dtype)
        if sum_refs:
            first = pl.program_id(0) == 0

            @pl.when(first)
            def _():
                for s_ref, r in zip(sum_refs, res[n_outs:]):
                    s_ref[...] = r.astype(F32)

            @pl.when(jnp.logical_not(first))
            def _():
                for s_ref, r in zip(sum_refs, res[n_outs:]):
                    s_ref[...] += r.astype(F32)

    def whole(shape):
        nd = len(shape)
        return pl.BlockSpec(shape, lambda i: (0,) * nd)

    def row_spec(x):
        if x.ndim == 3:
            return pl.BlockSpec((x.shape[0], tm, x.shape[2]), lambda i: (0, i, 0))
        return pl.BlockSpec((tm, x.shape[1]), lambda i: (i, 0))

    in_specs = [row_spec(x) for x in rows] + [whole(c.shape) for c in consts]
    out_specs = [pl.BlockSpec((tm, nc), lambda i: (i, 0)) for nc, _ in outs] + [whole(tuple(s)) for s in sums]
    out_shape = [jax.ShapeDtypeStruct((m, nc), dt) for nc, dt in outs] + [jax.ShapeDtypeStruct(tuple(s), F32) for s in sums]
    res = pl.pallas_call(
        body,
        name=name,
        grid=(m // tm,),
        in_specs=in_specs,
        out_specs=out_specs,
        out_shape=out_shape,
        compiler_params=_params(("arbitrary",) if sums else ("parallel",)),
    )(*rows, *consts)
    return res[0] if len(res) == 1 else res


def _rms(h, g):
    return h * lax.rsqrt(jnp.mean(h * h, axis=-1, keepdims=True) + RMS_EPS) * g


def _sigmoid(x):
    return 1.0 / (1.0 + jnp.exp(-x))


def _silu_mul(g, u):
    return g * _sigmoid(g) * u


def _gelu(x):
    return 0.5 * x * (1.0 + jnp.tanh(math.sqrt(2.0 / math.pi) * (x + 0.044715 * (x * x * x))))


def _softplus(x):
    return jnp.maximum(x, 0.0) + jnp.log(1.0 + jnp.exp(-jnp.abs(x)))


def rms_fwd(h, g, name):
    return rowwise(lambda x, gg: _rms(x, gg), [h], [g.reshape(1, -1)], [(h.shape[1], BF16)], name=name)


def rms_bwd(h, g, dhn, dres, name):
    def fn(x, dy, dr, gg):
        _, vjp = jax.vjp(_rms, x, gg)
        dx, dg = vjp(dy.astype(F32))
        return dr + dx, dg

    d = h.shape[1]
    dh, dg = rowwise(fn, [h, dhn, dres], [g.reshape(1, -1)], [(d, F32)], [(1, d)], name=name)
    return dh, dg.reshape(-1)


def ffn_in(h, g, w_in, tag):
    m, d = h.shape
    f = w_in.shape[1] // 2
    tm, tn = _fit(m, 1024), _fit(f, 512)
    nj = f // tn

    def body(h_ref, g_ref, wg_ref, wu_ref, hn_ref, gu_ref, act_ref, hn_scr):
        @pl.when(pl.program_id(1) == 0)
        def _():
            hn = _rms(h_ref[...], g_ref[...]).astype(BF16)
            hn_scr[...] = hn
            hn_ref[...] = hn

        a = hn_scr[...]
        gate = jnp.dot(a, wg_ref[...], preferred_element_type=F32)
        up = jnp.dot(a, wu_ref[...], preferred_element_type=F32)
        gu_ref[0] = gate.astype(BF16)
        gu_ref[1] = up.astype(BF16)
        act_ref[...] = _silu_mul(gate, up).astype(BF16)

    return pl.pallas_call(
        body,
        name=f"{tag}_in",
        grid=(m // tm, nj),
        in_specs=[pl.BlockSpec((tm, d), lambda i, j: (i, 0)), pl.BlockSpec((1, d), lambda i, j: (0, 0)),
                  pl.BlockSpec((d, tn), lambda i, j: (0, j)), pl.BlockSpec((d, tn), lambda i, j: (0, nj + j))],
        out_specs=[pl.BlockSpec((tm, d), lambda i, j: (i, 0)), pl.BlockSpec((2, tm, tn), lambda i, j: (0, i, j)),
                   pl.BlockSpec((tm, tn), lambda i, j: (i, j))],
        out_shape=[jax.ShapeDtypeStruct((m, d), BF16), jax.ShapeDtypeStruct((2, m, f), BF16), jax.ShapeDtypeStruct((m, f), BF16)],
        scratch_shapes=[pltpu.VMEM((tm, d), BF16)],
        compiler_params=_params(("parallel", "arbitrary")),
    )(h, g.reshape(1, d), w_in, w_in)


def ffn_fwd(h, g, w_in, w_out, tag):
    hn, gu, act = ffn_in(h, g, w_in, tag)
    h_new = mm(act, w_out, name=f"{tag}_out", extras=(h,), epilogue=lambda acc, res: res + 0.5 * acc, tm=512, tk=2816)
    return h_new, (h, hn, gu, act)


def ffn_bwd(dh_new, saved, g, w_in, w_out, tag):
    h, hn, gu, act = saved
    m, d = h.shape
    f = w_out.shape[0]

    def act_bwd(acc, gu_blk):
        _, vjp = jax.vjp(_silu_mul, gu_blk[0].astype(F32), gu_blk[1].astype(F32))
        return jnp.stack(vjp(0.5 * acc))

    tm, tn = _fit(m, 512), _fit(f, 1408)
    pair = pl.BlockSpec((2, tm, tn), lambda i, j, k: (0, i, j))
    dgu = _mm_call(f"{tag}_dgu", (m // tm, f // tn, 1), dh_new, pl.BlockSpec((tm, d), lambda i, j, k: (i, 0)), w_out,
                   pl.BlockSpec((tn, d), lambda i, j, k: (j, 0)), _NT, [(gu, pair)],
                   [(jax.ShapeDtypeStruct((2, m, f), BF16), pair)], act_bwd, (tm, tn))[0]
    dw_out = mm(act, dh_new, ta=True, name=f"{tag}_dwout", epilogue=lambda acc: 0.5 * acc, tm=1408)

    tn, tk = _fit(f, 1408), _fit(m, 1024)
    nh = f // tn
    dw_in = _mm_call(f"{tag}_dwin", (1, 2 * nh, m // tk), hn, pl.BlockSpec((tk, d), lambda i, j, k: (k, 0)), dgu,
                     pl.BlockSpec((None, tk, tn), lambda i, j, k: (j // nh, k, j % nh)), _TN, [],
                     [(jax.ShapeDtypeStruct((d, 2 * f), F32), pl.BlockSpec((d, tn), lambda i, j, k: (0, j)))],
                     lambda acc: acc, (d, tn))[0]

    tm, tk = _fit(m, 512), _fit(f, 1408)
    nkh = f // tk
    dhn = _mm_call(f"{tag}_dhn", (m // tm, 1, 2 * nkh), dgu,
                   pl.BlockSpec((None, tm, tk), lambda i, j, k: (k // nkh, i, k % nkh)), w_in,
                   pl.BlockSpec((d, tk), lambda i, j, k: (0, k)), _NT, [],
                   [(jax.ShapeDtypeStruct((m, d), BF16), pl.BlockSpec((tm, d), lambda i, j, k: (i, 0)))],
                   lambda acc: acc, (tm, d))[0]
    dh, dg = rms_bwd(h, g, dhn, dh_new, f"{tag}_dnorm")
    return dh, dg, dw_in, dw_out


def loss_fwd_bwd(h, g, target):
    d = h.shape[1]

    def fn(x, t, gg):
        y, vjp = jax.vjp(_rms, x, gg)
        err = y - t
        dx, dg = vjp(err * (1.0 / d))
        part = 0.5 * jnp.sum(jnp.sum(err * err, axis=1, keepdims=True), axis=0, keepdims=True) * (1.0 / d)
        return dx, dg, jnp.broadcast_to(part, (1, 128))

    dh, dg, loss = rowwise(fn, [h, target], [g.reshape(1, -1)], [(d, F32)], [(1, d), (1, 128)], name="loss_head")
    return loss[0, 0], dh, dg.reshape(-1)


def _shift_down(v, d, fill):
    rows = lax.broadcasted_iota(jnp.int32, v.shape, 0)
    return jnp.where(rows < d, fill, pltpu.roll(v, d, 0))


def _shift_up(v, d, fill):
    t = v.shape[0]
    rows = lax.broadcasted_iota(jnp.int32, v.shape, 0)
    return jnp.where(rows >= t - d, fill, pltpu.roll(v, t - d, 0))


def _scan_fwd(a, x):
    d = 1
    while d < a.shape[0]:
        x = x + a * _shift_down(x, d, 0.0)
        a = a * _shift_down(a, d, 1.0)
        d *= 2
    return a, x


def _scan_bwd(b, x):
    d = 1
    while d < b.shape[0]:
        x = x + b * _shift_up(x, d, 0.0)
        b = b * _shift_up(b, d, 1.0)
        d *= 2
    return b, x


def _rows_before(cur, prev8, s):
    r = pltpu.roll(cur, s, 0)
    p = pltpu.roll(prev8, s, 0)
    rows = lax.broadcasted_iota(jnp.int32, prev8.shape, 0)
    return jnp.concatenate([jnp.where(rows < s, p, r[:8]), r[8:]], axis=0)


def _rows_after(cur, next8, s):
    t = cur.shape[0]
    r = pltpu.roll(cur, t - s, 0)
    p = pltpu.roll(next8, 8 - s, 0)
    rows = lax.broadcasted_iota(jnp.int32, next8.shape, 0)
    return jnp.concatenate([r[:t - 8], jnp.where(rows >= 8 - s, p, r[t - 8:])], axis=0)


LRU_CHUNK = 256


def _neg_expm1(y):
    small = -y * (1.0 + 0.5 * y * (1.0 + y * (1.0 / 3.0)))
    return jnp.where(y > -0.01, small, 1.0 - jnp.exp(y))


def _lru_gate(xc, pre_a, pre_x, lam):
    r = _sigmoid(pre_a)
    ig = _sigmoid(pre_x)
    log_a = (-LRU_C * r) * _softplus(-lam)
    return jnp.exp(log_a), (ig * xc) * jnp.sqrt(_neg_expm1(2.0 * log_a))


def _lru_conv(br, prev8, conv_w, conv_b):
    taps = [br] + [_rows_before(br, prev8, s) for s in range(1, LRU_CONV)]
    xc = conv_b
    for k in range(LRU_CONV):
        xc = xc + conv_w[k:k + 1, :] * taps[LRU_CONV - 1 - k]
    return xc, taps


def _lru_pre(xcb, w_ref, bias):
    nb = w_ref.shape[0]
    bw = w_ref.shape[1]
    return jnp.concatenate(
        [jnp.dot(xcb[:, n * bw:(n + 1) * bw], w_ref[n], preferred_element_type=F32) for n in range(nb)], axis=1) + bias


def lru_scan_fwd(bgr, conv_w, conv_b, wa, ba, wx, bx, lam):
    l, w2 = bgr.shape
    w = w2 // 2
    t = _tile(l, (LRU_CHUNK, 128, 64, 32, 16, 8))

    def body(bg_ref, br_ref, cw_ref, cb_ref, wa_ref, ba_ref, wx_ref, bx_ref, lam_ref, y_ref, h_ref, tail_ref, hprev_ref):
        @pl.when(pl.program_id(0) == 0)
        def _():
            tail_ref[...] = jnp.zeros_like(tail_ref)
            hprev_ref[...] = jnp.zeros_like(hprev_ref)

        br = br_ref[...]
        xc, _ = _lru_conv(br, tail_ref[...], cw_ref[...], cb_ref[...])
        xcb = xc.astype(BF16)
        a, gx = _lru_gate(xc, _lru_pre(xcb, wa_ref, ba_ref[...]), _lru_pre(xcb, wx_ref, bx_ref[...]), lam_ref[...])
        acum, x = _scan_fwd(a, gx)
        h = x + acum * hprev_ref[pl.ds(7, 1), :]
        y_ref[...] = (_gelu(bg_ref[...]) * h).astype(y_ref.dtype)
        h_ref[...] = h
        tail_ref[...] = br[t - 8:, :]
        hprev_ref[...] = h[t - 8:, :]

    def whole(x):
        nd = x.ndim
        return pl.BlockSpec(x.shape, lambda i: (0,) * nd)

    consts = [conv_w, conv_b, wa, ba, wx, bx, lam]
    return pl.pallas_call(
        body,
        name="lru_scan_fwd",
        grid=(l // t,),
        in_specs=[pl.BlockSpec((t, w), lambda i: (i, 0)), pl.BlockSpec((t, w), lambda i: (i, 1))] + [whole(c) for c in consts],
        out_specs=[pl.BlockSpec((t, w), lambda i: (i, 0)), pl.BlockSpec((t, w), lambda i: (i, 0))],
        out_shape=[jax.ShapeDtypeStruct((l, w), BF16), jax.ShapeDtypeStruct((l, w), F32)],
        scratch_shapes=[pltpu.VMEM((8, w), F32), pltpu.VMEM((8, w), F32)],
        compiler_params=_params(("arbitrary",)),
    )(bgr, bgr, *consts)


def lru_scan_bwd(dy, bgr, hseq, conv_w, conv_b, wa, ba, wx, bx, lam):
    l, w2 = bgr.shape
    w = w2 // 2
    t = _tile(l, (LRU_CHUNK, 128, 64, 32, 16, 8))
    nc = l // t
    nb, bw = wa.shape[0], wa.shape[1]

    def body(dy_ref, bg_ref, br_ref, brh_ref, h_ref, hh_ref, cw_ref, cb_ref, wa_ref, ba_ref, wx_ref, bx_ref, lam_ref,
             dbgr_ref, dcw_ref, dcb_ref, dwa_ref, dba_ref, dwx_ref, dbx_ref, dlam_ref, dxcn_ref, carry_ref):
        i = pl.program_id(0)
        has_prev = (i < nc - 1).astype(F32)

        @pl.when(i == 0)
        def _():
            dxcn_ref[...] = jnp.zeros_like(dxcn_ref)
            carry_ref[...] = jnp.zeros_like(carry_ref)

        br = br_ref[...]
        cw = cw_ref[...]
        xc, taps = _lru_conv(br, brh_ref[...] * has_prev, cw, cb_ref[...])
        xcb = xc.astype(BF16)
        (a, _), gate_vjp = jax.vjp(_lru_gate, xc, _lru_pre(xcb, wa_ref, ba_ref[...]), _lru_pre(xcb, wx_ref, bx_ref[...]),
                                   lam_ref[...])
        hs = h_ref[...]
        _, out_vjp = jax.vjp(lambda g_, h_: _gelu(g_) * h_, bg_ref[...], hs)
        dbg, dhs = out_vjp(dy_ref[...])
        bcum, x = _scan_bwd(_shift_up(a, 1, 1.0), dhs)
        dh = x + bcum * carry_ref[pl.ds(0, 1), :]
        da = dh * _rows_before(hs, hh_ref[...] * has_prev, 1)
        dxc, dpa, dpx, dlam = gate_vjp((da, dh))
        dpab, dpxb = dpa.astype(BF16), dpx.astype(BF16)
        nt = (((1,), (1,)), ((), ()))
        tn = (((0,), (0,)), ((), ()))
        dxb, dwa, dwx = [], [], []
        for n in range(nb):
            sl = slice(n * bw, (n + 1) * bw)
            dxb.append(lax.dot_general(dpab[:, sl], wa_ref[n], nt, preferred_element_type=F32)
                       + lax.dot_general(dpxb[:, sl], wx_ref[n], nt, preferred_element_type=F32))
            dwa.append(lax.dot_general(xcb[:, sl], dpab[:, sl], tn, preferred_element_type=F32))
            dwx.append(lax.dot_general(xcb[:, sl], dpxb[:, sl], tn, preferred_element_type=F32))
        dxc = dxc + jnp.concatenate(dxb, axis=1)
        ups = [dxc] + [_rows_after(dxc, dxcn_ref[...], s) for s in range(1, LRU_CONV)]
        dbr = cw[LRU_CONV - 1:LRU_CONV, :] * ups[0]
        for k in range(LRU_CONV - 1):
            dbr = dbr + cw[k:k + 1, :] * ups[LRU_CONV - 1 - k]
        dbgr_ref[:, :w] = dbg.astype(dbgr_ref.dtype)
        dbgr_ref[:, w:] = dbr.astype(dbgr_ref.dtype)
        dcw = jnp.concatenate([jnp.sum(dxc * taps[LRU_CONV - 1 - k], axis=0, keepdims=True) for k in range(LRU_CONV)], axis=0)
        sums = [(dcw_ref, dcw), (dcb_ref, jnp.sum(dxc, axis=0, keepdims=True)), (dwa_ref, jnp.stack(dwa)),
                (dba_ref, jnp.sum(dpa, axis=0, keepdims=True)), (dwx_ref, jnp.stack(dwx)),
                (dbx_ref, jnp.sum(dpx, axis=0, keepdims=True)), (dlam_ref, dlam)]

        @pl.when(i == 0)
        def _():
            for ref, val in sums:
                ref[...] = val

        @pl.when(i > 0)
        def _():
            for ref, val in sums:
                ref[...] += val

        dxcn_ref[...] = dxc[:8, :]
        carry_ref[...] = (a * dh)[:8, :]

    def whole(shape):
        nd = len(shape)
        return pl.BlockSpec(tuple(shape), lambda i: (0,) * nd)

    consts = [conv_w, conv_b, wa, ba, wx, bx, lam]
    t8 = t // 8
    rev = lambda i: nc - 1 - i
    halo = lambda i: jnp.maximum(rev(i) * t8 - 1, 0)
    in_specs = [
        pl.BlockSpec((t, w), lambda i: (rev(i), 0)),
        pl.BlockSpec((t, w), lambda i: (rev(i), 0)),
        pl.BlockSpec((t, w), lambda i: (rev(i), 1)),
        pl.BlockSpec((8, w), lambda i: (halo(i), 1)),
        pl.BlockSpec((t, w), lambda i: (rev(i), 0)),
        pl.BlockSpec((8, w), lambda i: (halo(i), 0)),
    ] + [whole(c.shape) for c in consts]
    sum_shapes = [conv_w.shape, conv_b.shape, wa.shape, ba.shape, wx.shape, bx.shape, lam.shape]
    res = pl.pallas_call(
        body,
        name="lru_scan_bwd",
        grid=(nc,),
        in_specs=in_specs,
        out_specs=[pl.BlockSpec((t, w2), lambda i: (rev(i), 0))] + [whole(s) for s in sum_shapes],
        out_shape=[jax.ShapeDtypeStruct((l, w2), BF16)] + [jax.ShapeDtypeStruct(tuple(s), F32) for s in sum_shapes],
        scratch_shapes=[pltpu.VMEM((8, w), F32), pltpu.VMEM((8, w), F32)],
        compiler_params=_params(("arbitrary",)),
    )(dy, bgr, bgr, bgr, hseq, hseq, *consts)
    return res[0], res[1:]


def lru_fwd(hn, p):
    bgr = mm(hn, p["w_in"], name="lru_in")
    y, hseq = lru_scan_fwd(bgr, p["conv_w"], p["conv_b"], p["wa"], p["ba"], p["wx"], p["bx"], p["lam"])
    return y, (hn, bgr, hseq, y)


def lru_bwd(dmixed, saved, p):
    hn, bgr, hseq, y = saved
    dy = mm(dmixed, p["w_out"], tb=True, name="lru_dy")
    dw_out = mm(y, dmixed, ta=True, name="lru_dwout")
    dbgr, (dcw, dcb, dwa, dba, dwx, dbx, dlam) = lru_scan_bwd(dy, bgr, hseq, p["conv_w"], p["conv_b"], p["wa"], p["ba"],
                                                               p["wx"], p["bx"], p["lam"])
    dw_in = mm(hn, dbgr, ta=True, name="lru_dwin")
    dhn = mm(dbgr, p["w_in"], tb=True, name="lru_dhn", out_dtypes=(BF16,), tk=2048)
    grads = dict(w_in=dw_in, conv_w=dcw, conv_b=dcb, wa=dwa, ba=dba, wx=dwx, bx=dbx, lam=dlam, w_out=dw_out)
    return dhn, grads


SB_BLOCK = 256
SB_BLOCK_Q = 1024
_NT = (((1,), (1,)), ((), ()))
_TN = (((0,), (0,)), ((), ()))


def _dot_hilo(x, tri):
    hi = x.astype(BF16)
    lo = (x - hi.astype(F32)).astype(BF16)
    return jnp.dot(hi, tri, preferred_element_type=F32) + jnp.dot(lo, tri, preferred_element_type=F32)


def _tri(n, cmp):
    r = lax.broadcasted_iota(jnp.int32, (n, n), 0)
    c = lax.broadcasted_iota(jnp.int32, (n, n), 1)
    return cmp(r, c).astype(BF16)


SB_PAIR = 2 * SB_HEAD_DIM


def _pair_masks(x2, scale=None):
    lane = lax.broadcasted_iota(jnp.int32, x2.shape, 1)
    zero = jnp.zeros_like(x2)
    a, b = jnp.where(lane < SB_HEAD_DIM, x2, zero), jnp.where(lane >= SB_HEAD_DIM, x2, zero)
    if scale is not None:
        a, b = a * scale, b * scale
    return a, b


def _causal(shape, q0, k0):
    return lax.broadcasted_iota(jnp.int32, shape, 1) + k0 < lax.broadcasted_iota(jnp.int32, shape, 0) + q0


def _sb_blocks(l):
    bk = _tile(l, (SB_BLOCK, 128))
    bq = _tile(l, (SB_BLOCK_Q, 2 * SB_BLOCK, SB_BLOCK, 128))
    return bq, bk


def sb_pair_fwd(qkv):
    l, d3 = qkv.shape
    d = d3 // 3
    npair = d // SB_PAIR
    bq, bk = _sb_blocks(l)
    ratio = bq // bk
    scale = SB_HEAD_DIM ** -0.5
    t_suf = _tri(bk, lambda r, c: r > c)

    def body(q_ref, k_ref, v_ref, tsuf_ref, o_ref, ltot_ref):
        i = pl.program_id(1)
        qs = _pair_masks(q_ref[...], scale)
        tsuf = tsuf_ref[...]

        def tile(j, carry, masked):
            rows = pl.ds(pl.multiple_of(j * bk, bk), bk)
            k2 = k_ref[rows, :]
            v2 = v_ref[rows, :]
            out = []
            for q1, (c_r, acc) in zip(qs, carry):
                z = lax.dot_general(q1, k2, _NT, preferred_element_type=F32)
                lk = -_softplus(z)
                if masked:
                    causal = _causal(z.shape, i * bq, j * bk)
                    lk = jnp.where(causal, lk, 0.0)
                w = jnp.exp(z + lk + c_r + _dot_hilo(lk, tsuf))
                if masked:
                    w = jnp.where(causal, w, 0.0)
                out.append((c_r + jnp.sum(lk, axis=1, keepdims=True), acc + jnp.dot(w.astype(BF16), v2, preferred_element_type=F32)))
            return tuple(out)

        carry = ((jnp.zeros((bq, 1), F32), jnp.zeros((bq, SB_PAIR), F32)),) * 2
        for dgl in reversed(range(ratio)):
            carry = tile(i * ratio + dgl, carry, True)
        (c_a, acc_a), (c_b, acc_b) = lax.fori_loop(0, i * ratio, lambda jj, c: tile(i * ratio - 1 - jj, c, False), carry)
        lane = lax.broadcasted_iota(jnp.int32, acc_a.shape, 1)
        o_ref[...] = jnp.where(lane < SB_HEAD_DIM, acc_a, acc_b).astype(o_ref.dtype)
        ltot_ref[...] = jnp.where(lax.broadcasted_iota(jnp.int32, (bq, 2), 1) == 0, c_a, c_b)

    return pl.pallas_call(
        body,
        name="sb_attn_fwd",
        grid=(npair, l // bq),
        in_specs=[
            pl.BlockSpec((bq, SB_PAIR), lambda p, i: (i, p)),
            pl.BlockSpec((l, SB_PAIR), lambda p, i: (0, npair + p)),
            pl.BlockSpec((l, SB_PAIR), lambda p, i: (0, 2 * npair + p)),
            pl.BlockSpec((bk, bk), lambda p, i: (0, 0)),
        ],
        out_specs=[pl.BlockSpec((bq, SB_PAIR), lambda p, i: (i, p)), pl.BlockSpec((None, bq, 2), lambda p, i: (p, i, 0))],
        out_shape=[jax.ShapeDtypeStruct((l, d), BF16), jax.ShapeDtypeStruct((npair, l, 2), F32)],
        compiler_params=_params(("parallel", "parallel")),
    )(qkv, qkv, qkv, t_suf)


def sb_pair_bwd(qkv, do, ltot):
    l, d3 = qkv.shape
    d = d3 // 3
    npair = d // SB_PAIR
    bq, bk = _sb_blocks(l)
    ratio = bq // bk
    scale = SB_HEAD_DIM ** -0.5
    t_inc = _tri(bk, lambda r, c: r <= c)
    t_exc = _tri(bk, lambda r, c: r < c)

    def body(q_ref, k_ref, v_ref, do_ref, ltot_ref, tinc_ref, texc_ref, dq_ref, dk_ref, dv_ref):
        i = pl.program_id(1)

        @pl.when(i == 0)
        def _():
            dk_ref[...] = jnp.zeros_like(dk_ref)
            dv_ref[...] = jnp.zeros_like(dv_ref)

        qs = _pair_masks(q_ref[...], scale)
        dos = _pair_masks(do_ref[...])
        lt = ltot_ref[...]
        ltots = (lt[:, 0:1], lt[:, 1:2])
        tinc = tinc_ref[...]
        texc = texc_ref[...]

        def tile(j, carry, masked):
            rows = pl.ds(pl.multiple_of(j * bk, bk), bk)
            k2 = k_ref[rows, :]
            v2 = v_ref[rows, :]
            out = []
            dk2 = dv2 = None
            for q1, do1, ltot1, (c_l, c_p, dq) in zip(qs, dos, ltots, carry):
                z = lax.dot_general(q1, k2, _NT, preferred_element_type=F32)
                lk = -_softplus(z)
                if masked:
                    causal = _causal(z.shape, i * bq, j * bk)
                    lk = jnp.where(causal, lk, 0.0)
                log_beta = z + lk
                w = jnp.exp(log_beta + (ltot1 - c_l) - _dot_hilo(lk, tinc))
                if masked:
                    w = jnp.where(causal, w, 0.0)
                g = w * lax.dot_general(do1, v2, _NT, preferred_element_type=F32)
                dz = g - jnp.exp(log_beta) * (g + c_p + _dot_hilo(g, texc))
                if masked:
                    dz = jnp.where(causal, dz, 0.0)
                dzb = dz.astype(BF16)
                dk1 = lax.dot_general(dzb, q1, _TN, preferred_element_type=F32)
                dv1 = lax.dot_general(w.astype(BF16), do1, _TN, preferred_element_type=F32)
                dk2 = dk1 if dk2 is None else dk2 + dk1
                dv2 = dv1 if dv2 is None else dv2 + dv1
                out.append((c_l + jnp.sum(lk, axis=1, keepdims=True), c_p + jnp.sum(g, axis=1, keepdims=True),
                            dq + jnp.dot(dzb, k2, preferred_element_type=F32)))
            dk_ref[rows, :] += dk2
            dv_ref[rows, :] += dv2
            return tuple(out)

        zero = jnp.zeros((bq, 1), F32)
        carry = ((zero, zero, jnp.zeros((bq, SB_PAIR), F32)),) * 2
        carry = lax.fori_loop(0, i * ratio, lambda j, c: tile(j, c, False), carry)
        for dgl in range(ratio):
            carry = tile(i * ratio + dgl, carry, True)
        (_, _, dq_a), (_, _, dq_b) = carry
        lane = lax.broadcasted_iota(jnp.int32, dq_a.shape, 1)
        dq_ref[...] = jnp.where(lane < SB_HEAD_DIM, dq_a, dq_b) * scale

    col = lambda s: pl.BlockSpec((l, SB_PAIR), lambda p, i: (0, s * npair + p))
    blk_spec = pl.BlockSpec((bq, SB_PAIR), lambda p, i: (i, p))
    tri_spec = pl.BlockSpec((bk, bk), lambda p, i: (0, 0))
    return pl.pallas_call(
        body,
        name="sb_attn_bwd",
        grid=(npair, l // bq),
        in_specs=[blk_spec, col(1), col(2), blk_spec, pl.BlockSpec((None, bq, 2), lambda p, i: (p, i, 0)), tri_spec, tri_spec],
        out_specs=[blk_spec, pl.BlockSpec((l, SB_PAIR), lambda p, i: (0, p)), pl.BlockSpec((l, SB_PAIR), lambda p, i: (0, p))],
        out_shape=[jax.ShapeDtypeStruct((l, d), F32)] * 3,
        compiler_params=_params(("parallel", "arbitrary")),
    )(qkv, qkv, qkv, do, ltot, t_inc, t_exc)


def sb_fwd(hn, p):
    qkv = mm(hn, p["w_qkv"], name="sb_qkv", out_dtypes=(BF16,), tm=2048, tn=512)
    o, ltot = sb_pair_fwd(qkv)
    return o, (hn, qkv, ltot, o)


def sb_bwd(dmixed, saved, p):
    hn, qkv, ltot, o = saved
    do = mm(dmixed, p["w_out"], tb=True, name="sb_do", out_dtypes=(BF16,))
    dw_out = mm(o, dmixed, ta=True, name="sb_dwout")
    dqkv = jnp.concatenate([g.astype(BF16) for g in sb_pair_bwd(qkv, do, ltot)], axis=1)
    dw_qkv = mm(hn, dqkv, ta=True, name="sb_dwqkv")
    dhn = mm(dqkv, p["w_qkv"], tb=True, name="sb_dhn", out_dtypes=(BF16,), tm=512, tk=3072)
    return dhn, dict(w_qkv=dw_qkv, w_out=dw_out)


S5_CHUNK = 128
S5_SLAB_GROUPS = 8


def _s5_discretise(lr, li, ldt, bre, bim):
    dt = jnp.exp(ldt)
    mag = jnp.exp(lr * dt)
    lbr = mag * jnp.cos(li * dt)
    lbi = mag * jnp.sin(li * dt)
    inv = 1.0 / (lr * lr + li * li)
    cr = ((lbr - 1.0) * lr + lbi * li) * inv
    ci = (lbi * lr - (lbr - 1.0) * li) * inv
    return lbr, lbi, cr * bre - ci * bim, cr * bim + ci * bre


def _s5_cols(lam_re, lam_im, log_dt, b_re, b_im):
    g, p = lam_re.shape
    col = lambda x: x.reshape(g * p, 1)
    ldt = jnp.broadcast_to(log_dt[:, None], (g, p))
    return col(lam_re), col(lam_im), col(ldt), b_re.reshape(g * p, -1), b_im.reshape(g * p, -1)


def _slab_b(bbar):
    sg = S5_SLAB_GROUPS
    gp, h = bbar.shape
    p = S5_STATE
    x = bbar.reshape(gp // (sg * p), sg, p, h)
    return jnp.einsum("kaph,ab->kahbp", x, jnp.eye(sg, dtype=x.dtype)).reshape(-1, sg * h, sg * p)


def _unslab_b(dslab):
    sg, p = S5_SLAB_GROUPS, S5_STATE
    nk, sh, _ = dslab.shape
    h = sh // sg
    x = dslab.reshape(nk, sg, h, sg, p)
    return jnp.einsum("kahbp,ab->kaph", x, jnp.eye(sg, dtype=x.dtype)).reshape(nk * sg * p, h)


def _slab_c(c):
    sg = S5_SLAB_GROUPS
    g, h, p = c.shape
    x = c.reshape(g // sg, sg, h, p)
    return jnp.einsum("kahp,ab->kapbh", x, jnp.eye(sg, dtype=x.dtype)).reshape(-1, sg * p, sg * h)


def _unslab_c(dslab):
    sg, p = S5_SLAB_GROUPS, S5_STATE
    nk, _, sh = dslab.shape
    h = sh // sg
    x = dslab.reshape(nk, sg, p, sg, h)
    return jnp.einsum("kapbh,ab->kahp", x, jnp.eye(sg, dtype=x.dtype)).reshape(nk * sg, h, p)


def _cmul_scan(lre, lim, xre, xim, cre, cim, reverse):
    t = xre.shape[0]
    shift = _shift_up if reverse else _shift_down
    pows = [(lre, lim)]
    for _ in range(7):
        pr, pi = pows[-1]
        pows.append((pr * lre - pi * lim, pr * lim + pi * lre))
    by_row = pows[::-1] if reverse else pows
    pw_re = jnp.concatenate([p[0] for p in by_row], axis=0)
    pw_im = jnp.concatenate([p[1] for p in by_row], axis=0)
    n_groups = t // 8
    out_re, out_im = [None] * n_groups, [None] * n_groups
    for gi in (reversed(range(n_groups)) if reverse else range(n_groups)):
        gre, gim = xre[gi * 8:(gi + 1) * 8], xim[gi * 8:(gi + 1) * 8]
        for d in (1, 2, 4):
            mr, mi = pows[d - 1]
            sre, sim = shift(gre, d, 0.0), shift(gim, d, 0.0)
            gre, gim = gre + mr * sre - mi * sim, gim + mr * sim + mi * sre
        gre, gim = gre + pw_re * cre - pw_im * cim, gim + pw_re * cim + pw_im * cre
        edge = slice(0, 1) if reverse else slice(7, 8)
        cre, cim = gre[edge], gim[edge]
        out_re[gi], out_im[gi] = gre, gim
    return jnp.concatenate(out_re, axis=0), jnp.concatenate(out_im, axis=0)


def s5_scan_fwd(u, lbr, lbi, bbd_re, bbd_im, cbd_re, cbd_imn, d_skip):
    l, w = u.shape
    n = lbr.shape[1]
    nk, cw, sw = bbd_re.shape
    t = _tile(l, (S5_CHUNK, 64, 32, 16, 8))

    def body(u_ref, lbr_ref, lbi_ref, bre_ref, bim_ref, cre_ref, cim_ref, d_ref, sre_ref, sim_ref, y_ref, z_ref, pre_ref, pim_ref):
        @pl.when(pl.program_id(0) == 0)
        def _():
            pre_ref[...] = jnp.zeros_like(pre_ref)
            pim_ref[...] = jnp.zeros_like(pim_ref)

        uu = u_ref[...]
        ub = uu.astype(BF16)
        lre, lim = lbr_ref[...], lbi_ref[...]
        xre = jnp.concatenate([jnp.dot(ub[:, k * cw:(k + 1) * cw], bre_ref[k], preferred_element_type=F32) for k in range(nk)], axis=1)
        xim = jnp.concatenate([jnp.dot(ub[:, k * cw:(k + 1) * cw], bim_ref[k], preferred_element_type=F32) for k in range(nk)], axis=1)
        sre, sim = _cmul_scan(lre, lim, xre, xim, pre_ref[pl.ds(7, 1), :], pim_ref[pl.ds(7, 1), :], False)
        sre_ref[...] = sre
        sim_ref[...] = sim
        pre_ref[...] = sre[t - 8:, :]
        pim_ref[...] = sim[t - 8:, :]
        sreb, simb = sre.astype(BF16), sim.astype(BF16)
        y = jnp.concatenate(
            [jnp.dot(sreb[:, k * sw:(k + 1) * sw], cre_ref[k], preferred_element_type=F32)
             + jnp.dot(simb[:, k * sw:(k + 1) * sw], cim_ref[k], preferred_element_type=F32) for k in range(nk)], axis=1)
        y = y + d_ref[...] * uu
        y_ref[...] = y
        z_ref[...] = _gelu(y).astype(z_ref.dtype)

    def whole(x):
        nd = x.ndim
        return pl.BlockSpec(x.shape, lambda i: (0,) * nd)

    consts = [lbr, lbi, bbd_re, bbd_im, cbd_re, cbd_imn, d_skip]
    row = lambda c: pl.BlockSpec((t, c), lambda i: (i, 0))
    return pl.pallas_call(
        body,
        name="s5_scan_fwd",
        grid=(l // t,),
        in_specs=[row(w)] + [whole(c) for c in consts],
        out_specs=[row(n), row(n), row(w), row(w)],
        out_shape=[jax.ShapeDtypeStruct((l, n), F32), jax.ShapeDtypeStruct((l, n), F32), jax.ShapeDtypeStruct((l, w), F32),
                   jax.ShapeDtypeStruct((l, w), BF16)],
        scratch_shapes=[pltpu.VMEM((8, n), F32), pltpu.VMEM((8, n), F32)],
        compiler_params=_params(("arbitrary",)),
    )(u, *consts)


def s5_scan_bwd(dz, y, u, sre, sim, lbr, lbi, bbd_re, bbd_im, cbd_re, cbd_imn, d_skip):
    l, w = u.shape
    n = lbr.shape[1]
    nk, cw, sw = bbd_re.shape
    t = _tile(l, (S5_CHUNK, 64, 32, 16, 8))
    nc = l // t

    def body(dz_ref, y_ref, u_ref, sre_ref, sim_ref, hre_ref, him_ref, lbr_ref, lbi_ref, bre_ref, bim_ref, cre_ref, cim_ref,
             d_ref, du_ref, dlr_ref, dli_ref, dbre_ref, dbim_ref, dcre_ref, dcim_ref, dd_ref, nre_ref, nim_ref):
        i = pl.program_id(0)
        has_prev = (i < nc - 1).astype(F32)

        @pl.when(i == 0)
        def _():
            nre_ref[...] = jnp.zeros_like(nre_ref)
            nim_ref[...] = jnp.zeros_like(nim_ref)

        uu = u_ref[...]
        ub = uu.astype(BF16)
        lre, lim = lbr_ref[...], lbi_ref[...]
        _, gelu_vjp = jax.vjp(_gelu, y_ref[...])
        dy = gelu_vjp(dz_ref[...].astype(F32))[0]
        dyb = dy.astype(BF16)
        gre = jnp.concatenate([lax.dot_general(dyb[:, k * cw:(k + 1) * cw], cre_ref[k], _NT, preferred_element_type=F32)
                               for k in range(nk)], axis=1)
        gim = jnp.concatenate([lax.dot_general(dyb[:, k * cw:(k + 1) * cw], cim_ref[k], _NT, preferred_element_type=F32)
                               for k in range(nk)], axis=1)
        dsre, dsim = _cmul_scan(lre, -lim, gre, gim, nre_ref[pl.ds(0, 1), :], nim_ref[pl.ds(0, 1), :], True)
        nre_ref[...] = dsre[:8, :]
        nim_ref[...] = dsim[:8, :]
        dsreb, dsimb = dsre.astype(BF16), dsim.astype(BF16)
        s_re, s_im = sre_ref[...], sim_ref[...]
        du = jnp.concatenate(
            [lax.dot_general(dsreb[:, k * sw:(k + 1) * sw], bre_ref[k], _NT, preferred_element_type=F32)
             + lax.dot_general(dsimb[:, k * sw:(k + 1) * sw], bim_ref[k], _NT, preferred_element_type=F32) for k in range(nk)],
            axis=1)
        du_ref[...] = (du + d_ref[...] * dy).astype(du_ref.dtype)
        pre = _rows_before(s_re, hre_ref[...] * has_prev, 1)
        pim = _rows_before(s_im, him_ref[...] * has_prev, 1)
        sreb, simb = s_re.astype(BF16), s_im.astype(BF16)
        sums = [
            (dlr_ref, jnp.sum(dsre * pre + dsim * pim, axis=0, keepdims=True)),
            (dli_ref, jnp.sum(dsim * pre - dsre * pim, axis=0, keepdims=True)),
            (dbre_ref, jnp.stack([lax.dot_general(ub[:, k * cw:(k + 1) * cw], dsreb[:, k * sw:(k + 1) * sw], _TN,
                                                  preferred_element_type=F32) for k in range(nk)])),
            (dbim_ref, jnp.stack([lax.dot_general(ub[:, k * cw:(k + 1) * cw], dsimb[:, k * sw:(k + 1) * sw], _TN,
                                                  preferred_element_type=F32) for k in range(nk)])),
            (dcre_ref, jnp.stack([lax.dot_general(sreb[:, k * sw:(k + 1) * sw], dyb[:, k * cw:(k + 1) * cw], _TN,
                                                  preferred_element_type=F32) for k in range(nk)])),
            (dcim_ref, jnp.stack([lax.dot_general(simb[:, k * sw:(k + 1) * sw], dyb[:, k * cw:(k + 1) * cw], _TN,
                                                  preferred_element_type=F32) for k in range(nk)])),
            (dd_ref, jnp.sum(dy * uu, axis=0, keepdims=True)),
        ]

        @pl.when(i == 0)
        def _():
            for ref, val in sums:
                ref[...] = val

        @pl.when(i > 0)
        def _():
            for ref, val in sums:
                ref[...] += val

    def whole(shape):
        nd = len(shape)
        return pl.BlockSpec(tuple(shape), lambda i: (0,) * nd)

    consts = [lbr, lbi, bbd_re, bbd_im, cbd_re, cbd_imn, d_skip]
    t8 = t // 8
    rev = lambda i: nc - 1 - i
    halo = lambda i: jnp.maximum(rev(i) * t8 - 1, 0)
    row = lambda c: pl.BlockSpec((t, c), lambda i: (rev(i), 0))
    sum_shapes = [lbr.shape, lbi.shape, bbd_re.shape, bbd_im.shape, cbd_re.shape, cbd_imn.shape, d_skip.shape]
    res = pl.pallas_call(
        body,
        name="s5_scan_bwd",
        grid=(nc,),
        in_specs=[row(w), row(w), row(w), row(n), row(n), pl.BlockSpec((8, n), lambda i: (halo(i), 0)),
                  pl.BlockSpec((8, n), lambda i: (halo(i), 0))] + [whole(c.shape) for c in consts],
        out_specs=[row(w)] + [whole(s) for s in sum_shapes],
        out_shape=[jax.ShapeDtypeStruct((l, w), BF16)] + [jax.ShapeDtypeStruct(tuple(s), F32) for s in sum_shapes],
        scratch_shapes=[pltpu.VMEM((8, n), F32), pltpu.VMEM((8, n), F32)],
        compiler_params=_params(("arbitrary",)),
    )(dz, y, u, sre, sim, sre, sim, *consts)
    return res[0], res[1:]


def _glu(vg):
    w = vg.shape[1] // 2
    return vg[:, :w] * _sigmoid(vg[:, w:])


def s5_fwd(hn, h, p):
    cols = _s5_cols(p["lam_re"], p["lam_im"], p["log_dt"], p["b_re"], p["b_im"])
    gp, hh = cols[3].shape
    lbr, lbi, bbr, bbi = rowwise(_s5_discretise, list(cols), [], [(1, F32), (1, F32), (hh, F32), (hh, F32)],
                                 name="s5_discretise", tm=512)
    consts = (lbr.reshape(1, gp), lbi.reshape(1, gp), _slab_b(bbr).astype(BF16), _slab_b(bbi).astype(BF16),
              _slab_c(p["c_re"]).astype(BF16), _slab_c(-p["c_im"]).astype(BF16), p["d"])
    u = mm(hn, p["w_in"], name="s5_in")
    sre, sim, y, z = s5_scan_fwd(u, *consts)
    vg = mm(z, p["w_out"], name="s5_out", out_dtypes=(BF16,))
    h_new = rowwise(lambda a, r: r + _glu(a.astype(F32)), [vg, h], [], [(h.shape[1], F32)], name="s5_glu")
    return h_new, (hn, u, sre, sim, y, z, vg, cols, consts)


def s5_bwd(dh_new, saved, p):
    hn, u, sre, sim, y, z, vg, cols, consts = saved

    def glu_bwd(a, dm):
        _, vjp = jax.vjp(_glu, a.astype(F32))
        return vjp(dm)[0]

    dvg = rowwise(glu_bwd, [vg, dh_new], [], [(vg.shape[1], BF16)], name="s5_dglu")
    dw_out = mm(z, dvg, ta=True, name="s5_dwout")
    dz = mm(dvg, p["w_out"], tb=True, name="s5_dz", out_dtypes=(BF16,), tk=2048)
    du, (dlbr, dlbi, dbbr, dbbi, dcre, dcimn, dd) = s5_scan_bwd(dz, y, u, sre, sim, *consts)
    dw_in = mm(hn, du, ta=True, name="s5_dwin")
    dhn = mm(du, p["w_in"], tb=True, name="s5_dhn", out_dtypes=(BF16,))
    gp = cols[0].shape[0]

    def disc_bwd(lr, li, ldt, bre, bim, g0, g1, g2, g3):
        _, vjp = jax.vjp(_s5_discretise, lr, li, ldt, bre, bim)
        return vjp((g0, g1, g2, g3))

    cot = (dlbr.reshape(gp, 1), dlbi.reshape(gp, 1), _unslab_b(dbbr), _unslab_b(dbbi))
    dlr, dli, dldt, dbre, dbim = rowwise(disc_bwd, list(cols + cot), [], [(c.shape[1], F32) for c in cols],
                                         name="s5_discretise_bwd", tm=512)
    g_, p_ = p["lam_re"].shape
    grads = dict(w_in=dw_in, lam_re=dlr.reshape(g_, p_), lam_im=dli.reshape(g_, p_), log_dt=dldt.reshape(g_, p_).sum(axis=1),
                 b_re=dbre.reshape(p["b_re"].shape), b_im=dbim.reshape(p["b_im"].shape), c_re=_unslab_c(dcre),
                 c_im=-_unslab_c(dcimn), d=dd, w_out=dw_out)
    return dhn, grads


MESH = pl.DeviceIdType.MESH
N_CHIPS = 4
N_DEVICES = 8


def _place():
    x, y, c = lax.axis_index("x"), lax.axis_index("y"), lax.axis_index("c")
    return x, y, c, [(1 - x, y), (x, 1 - y), (1 - x, 1 - y)]


def _hbm_call(body, name, ins, out_shapes, n_remote, n_local=0):
    hbm = pl.BlockSpec(memory_space=pltpu.HBM)
    scratch = [pltpu.SemaphoreType.DMA((n_remote,)), pltpu.SemaphoreType.DMA((n_remote,))]
    if n_local:
        scratch.append(pltpu.SemaphoreType.DMA((n_local,)))
    return pl.pallas_call(
        body,
        name=name,
        in_specs=[hbm] * len(ins),
        out_specs=[hbm] * len(out_shapes),
        out_shape=out_shapes,
        scratch_shapes=scratch,
    )(*ins)


def _split_dim(shape):
    return next(d for d, s in enumerate(shape) if s >= 2 and s % 2 == 0)


def gather_chips(shards):
    n = len(shards)
    cuts = [_split_dim(s.shape) for s in shards]

    def body(*refs):
        ins, outs = refs[:n], refs[n:2 * n]
        send, recv, local = refs[2 * n:]
        x, y, c, chips = _place()
        me = 2 * x + y

        def half(ref, t, which, lead=()):
            size = shards[t].shape[cuts[t]] // 2
            return ref.at[lead + (slice(None),) * cuts[t] + (pl.ds(which * size, size),)]

        def copy(t, k, block, which, to, src=None):
            dst = half(outs[t], t, which, (block,))
            return pltpu.make_async_remote_copy(dst if src is None else src, dst, send.at[6 * t + k], recv.at[6 * t + k],
                                                device_id=to, device_id_type=MESH)

        started = []
        for t in range(n):
            own = pltpu.make_async_copy(ins[t], outs[t].at[me], local.at[t])
            own.start()
            started.append(own)
        sends = [copy(t, k, me, c, (px, py, c), src=half(ins[t], t, c)) for t in range(n) for k, (px, py) in enumerate(chips)]
        for cp in sends:
            cp.start()
        passed = []
        for t in range(n):
            for k, (px, py) in enumerate(chips):
                copy(t, k, 2 * px + py, c, (px, py, c)).wait_recv()
                on = copy(t, 3 + k, 2 * px + py, c, (x, y, 1 - c))
                on.start()
                passed.append(on)
        for t in range(n):
            for k, (px, py) in enumerate(chips):
                copy(t, 3 + k, 2 * px + py, 1 - c, (x, y, 1 - c)).wait_recv()
        for cp in sends + passed:
            cp.wait_send()
        for cp in started:
            cp.wait()

    return _hbm_call(body, "gather_chips", shards, [jax.ShapeDtypeStruct((N_CHIPS,) + s.shape, s.dtype) for s in shards],
                     6 * n, n)


def scatter_chips(blocked):
    n = len(blocked)

    def body(*refs):
        ins, outs = refs[:n], refs[n:2 * n]
        send, recv = refs[2 * n:]
        x, y, c, chips = _place()
        pending = []
        for t in range(n):
            for k, (px, py) in enumerate(chips):
                cp = pltpu.make_async_remote_copy(ins[t].at[2 * px + py], outs[t].at[k], send.at[3 * t + k], recv.at[3 * t + k],
                                                  device_id=(px, py, c), device_id_type=MESH)
                cp.start()
                pending.append(cp)
        for cp in pending:
            cp.wait()

    return _hbm_call(body, "scatter_chips", blocked, [jax.ShapeDtypeStruct((3,) + b.shape[1:], b.dtype) for b in blocked], 3 * n)


def swap_cores(arrays):
    n = len(arrays)

    def body(*refs):
        ins, outs = refs[:n], refs[n:2 * n]
        send, recv = refs[2 * n:]
        x, y, c, _ = _place()
        pending = []
        for t in range(n):
            cp = pltpu.make_async_remote_copy(ins[t], outs[t], send.at[t], recv.at[t], device_id=(x, y, 1 - c),
                                              device_id_type=MESH)
            cp.start()
            pending.append(cp)
        for cp in pending:
            cp.wait()

    return _hbm_call(body, "swap_cores", arrays, [jax.ShapeDtypeStruct(a.shape, a.dtype) for a in arrays], n)


def gather_devices(buf):
    def body(in_ref, out_ref, send, recv, local):
        x, y, c, _ = _place()
        own = pltpu.make_async_copy(in_ref, out_ref.at[4 * x + 2 * y + c], local.at[0])
        own.start()
        pending = [own]
        for k in range(1, N_DEVICES):
            px = x ^ ((k >> 2) & 1)
            py = y ^ ((k >> 1) & 1)
            pc = c ^ (k & 1)
            going = pltpu.make_async_remote_copy(in_ref, out_ref.at[4 * x + 2 * y + c], send.at[k - 1], recv.at[k - 1],
                                                 device_id=(px, py, pc), device_id_type=MESH)
            going.start()
            pending.append(pltpu.make_async_remote_copy(in_ref, out_ref.at[4 * px + 2 * py + pc], send.at[k - 1], recv.at[k - 1],
                                                        device_id=(px, py, pc), device_id_type=MESH))
        for cp in pending:
            cp.wait()

    return _hbm_call(body, "gather_devices", [buf], [jax.ShapeDtypeStruct((N_DEVICES,) + buf.shape, buf.dtype)],
                     N_DEVICES - 1, 1)[0]


def _adamw(w, g, m, v):
    m = ADAM_B1 * m + (1.0 - ADAM_B1) * g
    v = ADAM_B2 * v + (1.0 - ADAM_B2) * (g * g)
    m_hat = m / (1.0 - ADAM_B1 ** ADAM_STEP)
    v_hat = v / (1.0 - ADAM_B2 ** ADAM_STEP)
    return -ADAM_LR * (m_hat / (jnp.sqrt(v_hat) + ADAM_EPS) + ADAM_WD * w), m, v


def _rows2d(a):
    return a.reshape(-1, a.shape[-1])


WEIGHTS = ["ffn1_norm", "ffn1_w_in", "ffn1_w_out", "mix_norm", "ffn2_norm", "ffn2_w_in", "ffn2_w_out", "final_norm",
           "s5_w_in", "s5_lam_re", "s5_lam_im", "s5_log_dt", "s5_b_re", "s5_b_im", "s5_c_re", "s5_c_im", "s5_d", "s5_w_out",
           "sb_w_qkv", "sb_w_out", "lru_w_in", "lru_conv_w", "lru_conv_b", "lru_w_a", "lru_b_a", "lru_w_x", "lru_b_x",
           "lru_lambda", "lru_w_out"]
INPUTS = ["x"] + WEIGHTS + ["loss_target"] + ["m_" + n for n in WEIGHTS] + ["v_" + n for n in WEIGHTS]
SHARDED_BIG = dict(ffn1_w_in=2, ffn1_w_out=1, ffn2_w_in=2, ffn2_w_out=1, s5_w_in=1, s5_w_out=2, sb_w_qkv=2, sb_w_out=1,
                   lru_w_in=2, lru_w_a=2, lru_w_x=2, lru_w_out=1)
SHARDED_SMALL = dict(s5_d=1, lru_conv_w=2, lru_conv_b=1, lru_b_a=2, lru_b_x=2, lru_lambda=1)
REPLICATED = [n for n in WEIGHTS if n not in SHARDED_BIG and n not in SHARDED_SMALL]
PACK_LANES = 128
PACK_ROW_ALIGN = 16


def _unblock(g, d):
    full = jnp.moveaxis(g, 0, d)
    return full.reshape(full.shape[:d] + (full.shape[d] * full.shape[d + 1],) + full.shape[d + 2:])


def _block(full, d):
    s = full.shape[d] // N_CHIPS
    return jnp.moveaxis(full.reshape(full.shape[:d] + (N_CHIPS, s) + full.shape[d + 1:]), d, 0)


def _pack(arrays, lead=()):
    nl = len(lead)
    flat = jnp.concatenate([a.reshape(lead + (-1,)) for a in arrays], axis=nl)
    quantum = PACK_LANES * PACK_ROW_ALIGN
    pad = (-flat.shape[nl]) % quantum
    flat = jnp.pad(flat, [(0, 0)] * nl + [(0, pad)])
    return flat.reshape(lead + (-1, PACK_LANES))


def _unpack(packed, shapes, lead=()):
    nl = len(lead)
    flat = packed.reshape(lead + (-1,))
    out, off = [], 0
    for s in shapes:
        size = math.prod(s)
        out.append(lax.slice_in_dim(flat, off, off + size, axis=nl).reshape(lead + tuple(s)))
        off += size
    return out


def _forward_backward(x, target, w, norms):
    depth = norms["ffn1_norm"].shape[0]
    n_s5 = w["s5_w_in"].shape[0]

    def mixer_params(layer):
        kind, j = layer % N_MIXERS, layer // N_MIXERS
        if kind == 0:
            return kind, j, dict(w_in=w["s5_w_in"][j], lam_re=w["s5_lam_re"][j], lam_im=w["s5_lam_im"][j], log_dt=w["s5_log_dt"][j],
                                 b_re=w["s5_b_re"][j], b_im=w["s5_b_im"][j], c_re=w["s5_c_re"][j], c_im=w["s5_c_im"][j],
                                 d=w["s5_d"][j].reshape(1, -1), w_out=w["s5_w_out"][j])
        if kind == 1:
            return kind, j, dict(w_qkv=w["sb_w_qkv"][j], w_out=w["sb_w_out"][j])
        return kind, j, dict(w_in=w["lru_w_in"][j], conv_w=w["lru_conv_w"][j], conv_b=w["lru_conv_b"][j].reshape(1, -1),
                             wa=w["lru_w_a"][j], ba=w["lru_b_a"][j].reshape(1, -1), wx=w["lru_w_x"][j],
                             bx=w["lru_b_x"][j].reshape(1, -1), lam=w["lru_lambda"][j].reshape(1, -1), w_out=w["lru_w_out"][j])

    h = x
    tape = []
    for layer in range(depth):
        h, s1 = ffn_fwd(h, norms["ffn1_norm"][layer], w["ffn1_w_in"][layer], w["ffn1_w_out"][layer], "ffn")
        kind, j, p = mixer_params(layer)
        h_mix_in = h
        hn = rms_fwd(h, norms["mix_norm"][layer], "mix_norm")
        if kind == 0:
            h, sm = s5_fwd(hn, h, p)
        elif kind == 1:
            o_flat, sm = sb_fwd(hn, p)
            h = mm(o_flat, p["w_out"], name="mix_out", extras=(h,), epilogue=lambda acc, res: res + acc)
        else:
            y, sm = lru_fwd(hn, p)
            h = mm(y, p["w_out"], name="mix_out", extras=(h,), epilogue=lambda acc, res: res + acc)
        h, s2 = ffn_fwd(h, norms["ffn2_norm"][layer], w["ffn2_w_in"][layer], w["ffn2_w_out"][layer], "ffn")
        tape.append((s1, h_mix_in, sm, s2))

    loss, dh, g_final = loss_fwd_bwd(h, norms["final_norm"], target)

    per_layer = {n: [None] * depth for n in ("ffn1_norm", "ffn1_w_in", "ffn1_w_out", "mix_norm", "ffn2_norm", "ffn2_w_in", "ffn2_w_out")}
    mix = {}
    for layer in reversed(range(depth)):
        s1, h_mix_in, sm, s2 = tape[layer]
        dh, dg, dwi, dwo = ffn_bwd(dh, s2, norms["ffn2_norm"][layer], w["ffn2_w_in"][layer], w["ffn2_w_out"][layer], "ffn")
        per_layer["ffn2_norm"][layer], per_layer["ffn2_w_in"][layer], per_layer["ffn2_w_out"][layer] = dg, dwi, dwo
        kind, j, p = mixer_params(layer)
        if kind == 0:
            dhn, g = s5_bwd(dh, sm, p)
            names = dict(w_in="s5_w_in", lam_re="s5_lam_re", lam_im="s5_lam_im", log_dt="s5_log_dt", b_re="s5_b_re", b_im="s5_b_im",
                         c_re="s5_c_re", c_im="s5_c_im", d="s5_d", w_out="s5_w_out")
        elif kind == 1:
            dhn, g = sb_bwd(dh, sm, p)
            names = dict(w_qkv="sb_w_qkv", w_out="sb_w_out")
        else:
            dhn, g = lru_bwd(dh, sm, p)
            names = dict(w_in="lru_w_in", conv_w="lru_conv_w", conv_b="lru_conv_b", wa="lru_w_a", ba="lru_b_a", wx="lru_w_x",
                         bx="lru_b_x", lam="lru_lambda", w_out="lru_w_out")
        for k, full_name in names.items():
            mix.setdefault(full_name, {})[j] = g[k].reshape(w[full_name].shape[1:])
        dh, dg = rms_bwd(h_mix_in, norms["mix_norm"][layer], dhn, dh, "mix_dnorm")
        per_layer["mix_norm"][layer] = dg
        dh, dg, dwi, dwo = ffn_bwd(dh, s1, norms["ffn1_norm"][layer], w["ffn1_w_in"][layer], w["ffn1_w_out"][layer], "ffn")
        per_layer["ffn1_norm"][layer], per_layer["ffn1_w_in"][layer], per_layer["ffn1_w_out"][layer] = dg, dwi, dwo

    grads = {n: jnp.stack(v) for n, v in per_layer.items()}
    grads["final_norm"] = g_final
    for n, by_j in mix.items():
        grads[n] = jnp.stack([by_j[j] for j in range(len(by_j))])
    return loss, dh, grads


def kernel(x, ffn1_norm, ffn1_w_in, ffn1_w_out, mix_norm, ffn2_norm, ffn2_w_in, ffn2_w_out, final_norm, s5_w_in,
           s5_lam_re, s5_lam_im, s5_log_dt, s5_b_re, s5_b_im, s5_c_re, s5_c_im, s5_d, s5_w_out, sb_w_qkv,
           sb_w_out, lru_w_in, lru_conv_w, lru_conv_b, lru_w_a, lru_b_a, lru_w_x, lru_b_x, lru_lambda,
           lru_w_out, loss_target, m_ffn1_norm, m_ffn1_w_in, m_ffn1_w_out, m_mix_norm, m_ffn2_norm,
           m_ffn2_w_in, m_ffn2_w_out, m_final_norm, m_s5_w_in, m_s5_lam_re, m_s5_lam_im, m_s5_log_dt,
           m_s5_b_re, m_s5_b_im, m_s5_c_re, m_s5_c_im, m_s5_d, m_s5_w_out, m_sb_w_qkv, m_sb_w_out, m_lru_w_in,
           m_lru_conv_w, m_lru_conv_b, m_lru_w_a, m_lru_b_a, m_lru_w_x, m_lru_b_x, m_lru_lambda, m_lru_w_out,
           v_ffn1_norm, v_ffn1_w_in, v_ffn1_w_out, v_mix_norm, v_ffn2_norm, v_ffn2_w_in, v_ffn2_w_out,
           v_final_norm, v_s5_w_in, v_s5_lam_re, v_s5_lam_im, v_s5_log_dt, v_s5_b_re, v_s5_b_im, v_s5_c_re,
           v_s5_c_im, v_s5_d, v_s5_w_out, v_sb_w_qkv, v_sb_w_out, v_lru_w_in, v_lru_conv_w, v_lru_conv_b,
           v_lru_w_a, v_lru_b_a, v_lru_w_x, v_lru_b_x, v_lru_lambda, v_lru_w_out):
    a = dict(locals())
    assert list(a) == INPUTS
    x, y, c, _ = _place()
    chip = 2 * x + y
    everyone = ("x", "y", "c")

    big = list(SHARDED_BIG)
    small = list(SHARDED_SMALL)
    small_packed = _pack([a[n] for n in small])
    gathered = gather_chips([a[n].astype(BF16) for n in big] + [small_packed])
    w = {n: _unblock(g, SHARDED_BIG[n] + 0) for n, g in zip(big, gathered[:-1])}
    small_blocks = _unpack(gathered[-1], [a[n].shape for n in small], lead=(N_CHIPS,))
    w.update({n: _unblock(g, SHARDED_SMALL[n]) for n, g in zip(small, small_blocks)})
    for n in ("s5_lam_re", "s5_lam_im", "s5_log_dt", "s5_b_re", "s5_b_im", "s5_c_re", "s5_c_im"):
        w[n] = a[n]
    norms = {n: a[n] for n in ("ffn1_norm", "mix_norm", "ffn2_norm", "final_norm")}

    loss, dx, grads = _forward_backward(a["x"][0], a["loss_target"][0], w, norms)
    loss = lax.psum(loss, everyone)

    own = {n: lax.dynamic_slice_in_dim(grads[n], chip * a[n].shape[d], a[n].shape[d], axis=d)
           for n, d in {**SHARDED_BIG, **SHARDED_SMALL}.items()}
    small_send = _pack([_block(grads[n], SHARDED_SMALL[n]) for n in small], lead=(N_CHIPS,))
    received = scatter_chips([_block(grads[n], SHARDED_BIG[n]).astype(BF16) for n in big] + [small_send])
    own_list = [own[n] for n in big] + [_pack([own[n] for n in small])]
    partial = []
    for mine, got in zip(own_list, received):
        rows = _rows2d(mine)
        got3 = got.reshape((3,) + rows.shape)
        partial.append(rowwise(lambda o, r: ((o + r[0].astype(F32)) + r[1].astype(F32)) + r[2].astype(F32), [rows, got3], [],
                               [(rows.shape[1], F32)], name="sum_chips"))
    other = swap_cores(partial)

    def adam_sharded(wv, ga, gb, m, v):
        shape = wv.shape
        res = rowwise(lambda w_, a_, b_, mm_, vv_: (a_ + b_,) + _adamw(w_, a_ + b_, mm_, vv_),
                      [_rows2d(wv), ga, gb, _rows2d(m), _rows2d(v)], [], [(shape[-1], F32)] * 4, name="adamw_sharded")
        return [r.reshape(shape) for r in res]

    out_grad, out_delta, out_m, out_v = {}, {}, {}, {}
    for i, n in enumerate(big):
        out_grad[n], out_delta[n], out_m[n], out_v[n] = adam_sharded(a[n], partial[i], other[i], a["m_" + n], a["v_" + n])
    small_shapes = [a[n].shape for n in small]
    sp = [_pack([a[pre + n] for n in small]) for pre in ("", "m_", "v_")]
    g_, d_, m_, v_ = adam_sharded(sp[0], partial[-1], other[-1], sp[1], sp[2])
    for n, gg, dd, mm_, vv in zip(small, _unpack(g_, small_shapes), _unpack(d_, small_shapes), _unpack(m_, small_shapes),
                                  _unpack(v_, small_shapes)):
        out_grad[n], out_delta[n], out_m[n], out_v[n] = gg, dd, mm_, vv

    rep_shapes = [a[n].shape for n in REPLICATED]
    rep_all = gather_devices(_pack([grads[n] for n in REPLICATED]))
    rp = [_pack([a[pre + n] for n in REPLICATED]) for pre in ("", "m_", "v_")]

    def adam_rep(w_, g8, mm_, vv_):
        g = g8[0]
        for k in range(1, N_DEVICES):
            g = g + g8[k]
        return (g,) + _adamw(w_, g, mm_, vv_)

    g_, d_, m_, v_ = rowwise(adam_rep, [rp[0], rep_all, rp[1], rp[2]], [], [(PACK_LANES, F32)] * 4, name="adamw_replicated")
    for n, gg, dd, mm_, vv in zip(REPLICATED, _unpack(g_, rep_shapes), _unpack(d_, rep_shapes), _unpack(m_, rep_shapes),
                                  _unpack(v_, rep_shapes)):
        out_grad[n], out_delta[n], out_m[n], out_v[n] = gg, dd, mm_, vv

    return (loss, dx[None], *[out_grad[n] for n in WEIGHTS], *[out_delta[n] for n in WEIGHTS], *[out_m[n] for n in WEIGHTS],
            *[out_v[n] for n in WEIGHTS])
```

```python
import functools
import math

import jax
import jax.numpy as jnp
from jax import lax
from jax.experimental import pallas as pl
from jax.experimental.pallas import tpu as pltpu

F32 = jnp.float32
BF16 = jnp.bfloat16

VMEM_LIMIT_BYTES = 56 * 1024 * 1024

RMS_EPS = 1e-6
D_FF = 2816
S5_GROUP = 16
S5_STATE = 64
SB_HEAD_DIM = 64
LRU_BLOCK_WIDTH = 256
LRU_CONV = 4
LRU_C = 8.0
N_MIXERS = 3

ADAM_LR = 0.001
ADAM_B1 = 0.9
ADAM_B2 = 0.999
ADAM_EPS = 1e-08
ADAM_WD = 0.01
ADAM_STEP = 10


def _params(semantics):
    return pltpu.CompilerParams(dimension_semantics=semantics, vmem_limit_bytes=VMEM_LIMIT_BYTES)


def _tile(dim, prefs):
    for t in prefs:
        if t <= dim and dim % t == 0:
            return t
    return dim


def _mm_call(name, grid, a, a_spec, b, b_spec, dims, extras, outs, epilogue, acc_shape):
    n_extra, n_out, n_k = len(extras), len(outs), grid[-1]

    def body(a_ref, b_ref, *rest):
        extra_refs = rest[:n_extra]
        out_refs = rest[n_extra:n_extra + n_out]
        acc_ref = rest[n_extra + n_out]
        k = pl.program_id(len(grid) - 1)
        part = lax.dot_general(a_ref[...].astype(BF16), b_ref[...].astype(BF16), dims, preferred_element_type=F32)

        @pl.when(k == 0)
        def _():
            acc_ref[...] = part

        @pl.when(k > 0)
        def _():
            acc_ref[...] += part

        @pl.when(k == n_k - 1)
        def _():
            res = epilogue(acc_ref[...], *[r[...] for r in extra_refs])
            if not isinstance(res, (tuple, list)):
                res = (res,)
            for o_ref, r in zip(out_refs, res):
                o_ref[...] = r.astype(o_ref.dtype)

    sem = ("parallel",) * (len(grid) - 1) + ("arbitrary",)
    res = pl.pallas_call(
        body,
        name=name,
        grid=grid,
        in_specs=[a_spec, b_spec] + [s for _, s in extras],
        out_specs=[s for _, s in outs],
        out_shape=[s for s, _ in outs],
        scratch_shapes=[pltpu.VMEM(acc_shape, F32)],
        compiler_params=_params(sem),
    )(a, b, *[x for x, _ in extras])
    return res


def _fit(dim, target, align=128):
    best = None
    for t in range(align, min(dim, target) + 1, align):
        if dim % t == 0:
            best = t
    return best or dim


def mm(a, b, *, name, ta=False, tb=False, extras=(), epilogue=None, out_dtypes=(F32,), tm=1024, tn=1024, tk=1024):
    m, kdim = (a.shape[1], a.shape[0]) if ta else a.shape
    n = b.shape[0] if tb else b.shape[1]
    tm = _fit(m, tm)
    tn = _fit(n, tn)
    tk = _fit(kdim, tk)
    grid = (m // tm, n // tn, kdim // tk)
    a_spec = pl.BlockSpec((tk, tm), lambda i, j, k: (k, i)) if ta else pl.BlockSpec((tm, tk), lambda i, j, k: (i, k))
    b_spec = pl.BlockSpec((tn, tk), lambda i, j, k: (j, k)) if tb else pl.BlockSpec((tk, tn), lambda i, j, k: (k, j))
    dims = (((0 if ta else 1,), (1 if tb else 0,)), ((), ()))
    o_spec = pl.BlockSpec((tm, tn), lambda i, j, k: (i, j))
    if epilogue is None:
        epilogue = lambda acc: acc
    res = _mm_call(name, grid, a, a_spec, b, b_spec, dims, [(x, o_spec) for x in extras],
                   [(jax.ShapeDtypeStruct((m, n), dt), o_spec) for dt in out_dtypes], epilogue, (tm, tn))
    return res[0] if len(res) == 1 else res


def rowwise(fn, rows, consts, outs, sums=(), *, name, tm=256):
    m = rows[0].shape[0]
    tm = _tile(m, (tm, 128, 64, 32, 16, 8))
    n_rows, n_consts, n_outs = len(rows), len(consts), len(outs)

    def body(*refs):
        in_vals = [r[...] for r in refs[:n_rows + n_consts]]
        out_refs = refs[n_rows + n_consts:n_rows + n_consts + n_outs]
        sum_refs = refs[n_rows + n_consts + n_outs:]
        res = fn(*in_vals)
        if not isinstance(res, (tuple, list)):
            res = (res,)
        for o_ref, r in zip(out_refs, res[:n_outs]):
            o_ref[...] = r.astype(o_ref.dtype)
        if sum_refs:
            first = pl.program_id(0) == 0

            @pl.when(first)
            def _():
                for s_ref, r in zip(sum_refs, res[n_outs:]):
                    s_ref[...] = r.astype(F32)

            @pl.when(jnp.logical_not(first))
            def _():
                for s_ref, r in zip(sum_refs, res[n_outs:]):
                    s_ref[...] += r.astype(F32)

    def whole(shape):
        nd = len(shape)
        return pl.BlockSpec(shape, lambda i: (0,) * nd)

    def row_spec(x):
        if x.ndim == 3:
            return pl.BlockSpec((x.shape[0], tm, x.shape[2]), lambda i: (0, i, 0))
        return pl.BlockSpec((tm, x.shape[1]), lambda i: (i, 0))

    in_specs = [row_spec(x) for x in rows] + [whole(c.shape) for c in consts]
    out_specs = [pl.BlockSpec((tm, nc), lambda i: (i, 0)) for nc, _ in outs] + [whole(tuple(s)) for s in sums]
    out_shape = [jax.ShapeDtypeStruct((m, nc), dt) for nc, dt in outs] + [jax.ShapeDtypeStruct(tuple(s), F32) for s in sums]
    res = pl.pallas_call(
        body,
        name=name,
        grid=(m // tm,),
        in_specs=in_specs,
        out_specs=out_specs,
        out_shape=out_shape,
        compiler_params=_params(("arbitrary",) if sums else ("parallel",)),
    )(*rows, *consts)
    return res[0] if len(res) == 1 else res


def _rms(h, g):
    return h * lax.rsqrt(jnp.mean(h * h, axis=-1, keepdims=True) + RMS_EPS) * g


def _sigmoid(x):
    return 1.0 / (1.0 + jnp.exp(-x))


def _silu_mul(g, u):
    return g * _sigmoid(g) * u


def _gelu(x):
    return 0.5 * x * (1.0 + jnp.tanh(math.sqrt(2.0 / math.pi) * (x + 0.044715 * (x * x * x))))


def _softplus(x):
    return jnp.maximum(x, 0.0) + jnp.log(1.0 + jnp.exp(-jnp.abs(x)))


def rms_fwd(h, g, name):
    return rowwise(lambda x, gg: _rms(x, gg), [h], [g.reshape(1, -1)], [(h.shape[1], BF16)], name=name)


def norm_input_bwd(a, a_spec, n_k, w, h, g, dres, name, tm=512):
    m, d = h.shape
    tk = w.shape[1] // n_k
    tm = _fit(m, tm)

    def body(a_ref, w_ref, h_ref, g_ref, dres_ref, dh_ref, dg_ref, acc_ref):
        i, k = pl.program_id(0), pl.program_id(1)
        part = lax.dot_general(a_ref[...].astype(BF16), w_ref[...], _NT, preferred_element_type=F32)

        @pl.when(k == 0)
        def _():
            acc_ref[...] = part

        @pl.when(k > 0)
        def _():
            acc_ref[...] += part

        @pl.when(k == n_k - 1)
        def _():
            _, vjp = jax.vjp(_rms, h_ref[...], g_ref[...])
            dx, dg = vjp(acc_ref[...])
            dh_ref[...] = dres_ref[...] + dx

            @pl.when(i == 0)
            def _():
                dg_ref[...] = dg

            @pl.when(i > 0)
            def _():
                dg_ref[...] += dg

    row = pl.BlockSpec((tm, d), lambda i, k: (i, 0))
    vec = pl.BlockSpec((1, d), lambda i, k: (0, 0))
    dh, dg = pl.pallas_call(
        body,
        name=name,
        grid=(m // tm, n_k),
        in_specs=[a_spec(tm, tk), pl.BlockSpec((d, tk), lambda i, k: (0, k)), row, vec, row],
        out_specs=[row, vec],
        out_shape=[jax.ShapeDtypeStruct((m, d), F32), jax.ShapeDtypeStruct((1, d), F32)],
        scratch_shapes=[pltpu.VMEM((tm, d), F32)],
        compiler_params=_params(("arbitrary", "arbitrary")),
    )(a, w, h, g.reshape(1, d), dres)
    return dh, dg.reshape(-1)


def _rows_spec(tm, tk):
    return pl.BlockSpec((tm, tk), lambda i, k: (i, k))


def ffn_in(h, g, w_in, tag):
    m, d = h.shape
    f = w_in.shape[1] // 2
    tm, tn = _fit(m, 512), _fit(f, 1408)
    nj = f // tn

    def body(h_ref, g_ref, wg_ref, wu_ref, hn_ref, gu_ref, act_ref, hn_scr):
        @pl.when(pl.program_id(1) == 0)
        def _():
            hn = _rms(h_ref[...], g_ref[...]).astype(BF16)
            hn_scr[...] = hn
            hn_ref[...] = hn

        a = hn_scr[...]
        gate = jnp.dot(a, wg_ref[...], preferred_element_type=F32)
        up = jnp.dot(a, wu_ref[...], preferred_element_type=F32)
        gu_ref[0] = gate.astype(BF16)
        gu_ref[1] = up.astype(BF16)
        act_ref[...] = _silu_mul(gate, up).astype(BF16)

    return pl.pallas_call(
        body,
        name=f"{tag}_in",
        grid=(m // tm, nj),
        in_specs=[pl.BlockSpec((tm, d), lambda i, j: (i, 0)), pl.BlockSpec((1, d), lambda i, j: (0, 0)),
                  pl.BlockSpec((d, tn), lambda i, j: (0, j)), pl.BlockSpec((d, tn), lambda i, j: (0, nj + j))],
        out_specs=[pl.BlockSpec((tm, d), lambda i, j: (i, 0)), pl.BlockSpec((2, tm, tn), lambda i, j: (0, i, j)),
                   pl.BlockSpec((tm, tn), lambda i, j: (i, j))],
        out_shape=[jax.ShapeDtypeStruct((m, d), BF16), jax.ShapeDtypeStruct((2, m, f), BF16), jax.ShapeDtypeStruct((m, f), BF16)],
        scratch_shapes=[pltpu.VMEM((tm, d), BF16)],
        compiler_params=_params(("parallel", "arbitrary")),
    )(h, g.reshape(1, d), w_in, w_in)


def ffn_fwd(h, g, w_in, w_out, tag):
    hn, gu, act = ffn_in(h, g, w_in, tag)
    h_new = mm(act, w_out, name=f"{tag}_out", extras=(h,), epilogue=lambda acc, res: res + 0.5 * acc, tm=512, tk=2816)
    return h_new, (h, hn, gu, act)


def ffn_bwd(dh_new, saved, g, w_in, w_out, tag):
    h, hn, gu, act = saved
    m, d = h.shape
    f = w_out.shape[0]

    def act_bwd(acc, gu_blk):
        _, vjp = jax.vjp(_silu_mul, gu_blk[0].astype(F32), gu_blk[1].astype(F32))
        return jnp.stack(vjp(0.5 * acc))

    tm, tn = _fit(m, 512), _fit(f, 1408)
    pair = pl.BlockSpec((2, tm, tn), lambda i, j, k: (0, i, j))
    dgu = _mm_call(f"{tag}_dgu", (m // tm, f // tn, 1), dh_new, pl.BlockSpec((tm, d), lambda i, j, k: (i, 0)), w_out,
                   pl.BlockSpec((tn, d), lambda i, j, k: (j, 0)), _NT, [(gu, pair)],
                   [(jax.ShapeDtypeStruct((2, m, f), BF16), pair)], act_bwd, (tm, tn))[0]
    dw_out = mm(act, dh_new, ta=True, name=f"{tag}_dwout", epilogue=lambda acc: 0.5 * acc, tm=1408)

    tn, tk = _fit(f, 1408), _fit(m, 1024)
    nh = f // tn
    dw_in = _mm_call(f"{tag}_dwin", (1, 2 * nh, m // tk), hn, pl.BlockSpec((tk, d), lambda i, j, k: (k, 0)), dgu,
                     pl.BlockSpec((None, tk, tn), lambda i, j, k: (j // nh, k, j % nh)), _TN, [],
                     [(jax.ShapeDtypeStruct((d, 2 * f), F32), pl.BlockSpec((d, tn), lambda i, j, k: (0, j)))],
                     lambda acc: acc, (d, tn))[0]

    nkh = f // _fit(f, 1408)
    dh, dg = norm_input_bwd(dgu, lambda tm, tk: pl.BlockSpec((None, tm, tk), lambda i, k: (k // nkh, i, k % nkh)), 2 * nkh,
                            w_in, h, g, dh_new, f"{tag}_dhn")
    return dh, dg, dw_in, dw_out


def loss_fwd_bwd(h, g, target):
    d = h.shape[1]

    def fn(x, t, gg):
        y, vjp = jax.vjp(_rms, x, gg)
        err = y - t
        dx, dg = vjp(err * (1.0 / d))
        part = 0.5 * jnp.sum(jnp.sum(err * err, axis=1, keepdims=True), axis=0, keepdims=True) * (1.0 / d)
        return dx, dg, jnp.broadcast_to(part, (1, 128))

    dh, dg, loss = rowwise(fn, [h, target], [g.reshape(1, -1)], [(d, F32)], [(1, d), (1, 128)], name="loss_head")
    return loss[0, 0], dh, dg.reshape(-1)


def _shift_down(v, d, fill):
    rows = lax.broadcasted_iota(jnp.int32, v.shape, 0)
    return jnp.where(rows < d, fill, pltpu.roll(v, d, 0))


def _shift_up(v, d, fill):
    t = v.shape[0]
    rows = lax.broadcasted_iota(jnp.int32, v.shape, 0)
    return jnp.where(rows >= t - d, fill, pltpu.roll(v, t - d, 0))


def _scan_fwd(a, x):
    d = 1
    while d < a.shape[0]:
        x = x + a * _shift_down(x, d, 0.0)
        a = a * _shift_down(a, d, 1.0)
        d *= 2
    return a, x


def _scan_bwd(b, x):
    d = 1
    while d < b.shape[0]:
        x = x + b * _shift_up(x, d, 0.0)
        b = b * _shift_up(b, d, 1.0)
        d *= 2
    return b, x


def _rows_before(cur, prev8, s):
    r = pltpu.roll(cur, s, 0)
    p = pltpu.roll(prev8, s, 0)
    rows = lax.broadcasted_iota(jnp.int32, prev8.shape, 0)
    return jnp.concatenate([jnp.where(rows < s, p, r[:8]), r[8:]], axis=0)


def _rows_after(cur, next8, s):
    t = cur.shape[0]
    r = pltpu.roll(cur, t - s, 0)
    p = pltpu.roll(next8, 8 - s, 0)
    rows = lax.broadcasted_iota(jnp.int32, next8.shape, 0)
    return jnp.concatenate([r[:t - 8], jnp.where(rows >= 8 - s, p, r[t - 8:])], axis=0)


LRU_CHUNK = 256


def _neg_expm1(y):
    small = -y * (1.0 + 0.5 * y * (1.0 + y * (1.0 / 3.0)))
    return jnp.where(y > -0.01, small, 1.0 - jnp.exp(y))


def _lru_gate(xc, pre_a, pre_x, lam):
    r = _sigmoid(pre_a)
    ig = _sigmoid(pre_x)
    log_a = (-LRU_C * r) * _softplus(-lam)
    return jnp.exp(log_a), (ig * xc) * jnp.sqrt(_neg_expm1(2.0 * log_a))


def _lru_conv(br, prev8, conv_w, conv_b):
    taps = [br] + [_rows_before(br, prev8, s) for s in range(1, LRU_CONV)]
    xc = conv_b
    for k in range(LRU_CONV):
        xc = xc + conv_w[k:k + 1, :] * taps[LRU_CONV - 1 - k]
    return xc, taps


def _lru_pre(xcb, w_ref, bias):
    nb = w_ref.shape[0]
    bw = w_ref.shape[1]
    return jnp.concatenate(
        [jnp.dot(xcb[:, n * bw:(n + 1) * bw], w_ref[n], preferred_element_type=F32) for n in range(nb)], axis=1) + bias


def lru_scan_fwd(bgr, conv_w, conv_b, wa, ba, wx, bx, lam):
    l, w2 = bgr.shape
    w = w2 // 2
    t = _tile(l, (LRU_CHUNK, 128, 64, 32, 16, 8))

    def body(bg_ref, br_ref, cw_ref, cb_ref, wa_ref, ba_ref, wx_ref, bx_ref, lam_ref, y_ref, h_ref, tail_ref, hprev_ref):
        @pl.when(pl.program_id(0) == 0)
        def _():
            tail_ref[...] = jnp.zeros_like(tail_ref)
            hprev_ref[...] = jnp.zeros_like(hprev_ref)

        br = br_ref[...]
        xc, _ = _lru_conv(br, tail_ref[...], cw_ref[...], cb_ref[...])
        xcb = xc.astype(BF16)
        a, gx = _lru_gate(xc, _lru_pre(xcb, wa_ref, ba_ref[...]), _lru_pre(xcb, wx_ref, bx_ref[...]), lam_ref[...])
        acum, x = _scan_fwd(a, gx)
        h = x + acum * hprev_ref[pl.ds(7, 1), :]
        y_ref[...] = (_gelu(bg_ref[...]) * h).astype(y_ref.dtype)
        h_ref[...] = h
        tail_ref[...] = br[t - 8:, :]
        hprev_ref[...] = h[t - 8:, :]

    def whole(x):
        nd = x.ndim
        return pl.BlockSpec(x.shape, lambda i: (0,) * nd)

    consts = [conv_w, conv_b, wa, ba, wx, bx, lam]
    return pl.pallas_call(
        body,
        name="lru_scan_fwd",
        grid=(l // t,),
        in_specs=[pl.BlockSpec((t, w), lambda i: (i, 0)), pl.BlockSpec((t, w), lambda i: (i, 1))] + [whole(c) for c in consts],
        out_specs=[pl.BlockSpec((t, w), lambda i: (i, 0)), pl.BlockSpec((t, w), lambda i: (i, 0))],
        out_shape=[jax.ShapeDtypeStruct((l, w), BF16), jax.ShapeDtypeStruct((l, w), F32)],
        scratch_shapes=[pltpu.VMEM((8, w), F32), pltpu.VMEM((8, w), F32)],
        compiler_params=_params(("arbitrary",)),
    )(bgr, bgr, *consts)


def lru_scan_bwd(dy, bgr, hseq, conv_w, conv_b, wa, ba, wx, bx, lam):
    l, w2 = bgr.shape
    w = w2 // 2
    t = _tile(l, (LRU_CHUNK, 128, 64, 32, 16, 8))
    nc = l // t
    nb, bw = wa.shape[0], wa.shape[1]

    def body(dy_ref, bg_ref, br_ref, brh_ref, h_ref, hh_ref, cw_ref, cb_ref, wa_ref, ba_ref, wx_ref, bx_ref, lam_ref,
             dbgr_ref, dcw_ref, dcb_ref, dwa_ref, dba_ref, dwx_ref, dbx_ref, dlam_ref, dxcn_ref, carry_ref):
        i = pl.program_id(0)
        has_prev = (i < nc - 1).astype(F32)

        @pl.when(i == 0)
        def _():
            dxcn_ref[...] = jnp.zeros_like(dxcn_ref)
            carry_ref[...] = jnp.zeros_like(carry_ref)

        br = br_ref[...]
        cw = cw_ref[...]
        xc, taps = _lru_conv(br, brh_ref[...] * has_prev, cw, cb_ref[...])
        xcb = xc.astype(BF16)
        (a, _), gate_vjp = jax.vjp(_lru_gate, xc, _lru_pre(xcb, wa_ref, ba_ref[...]), _lru_pre(xcb, wx_ref, bx_ref[...]),
                                   lam_ref[...])
        hs = h_ref[...]
        _, out_vjp = jax.vjp(lambda g_, h_: _gelu(g_) * h_, bg_ref[...], hs)
        dbg, dhs = out_vjp(dy_ref[...])
        bcum, x = _scan_bwd(_shift_up(a, 1, 1.0), dhs)
        dh = x + bcum * carry_ref[pl.ds(0, 1), :]
        da = dh * _rows_before(hs, hh_ref[...] * has_prev, 1)
        dxc, dpa, dpx, dlam = gate_vjp((da, dh))
        dpab, dpxb = dpa.astype(BF16), dpx.astype(BF16)
        nt = (((1,), (1,)), ((), ()))
        tn = (((0,), (0,)), ((), ()))
        dxb, dwa, dwx = [], [], []
        for n in range(nb):
            sl = slice(n * bw, (n + 1) * bw)
            dxb.append(lax.dot_general(dpab[:, sl], wa_ref[n], nt, preferred_element_type=F32)
                       + lax.dot_general(dpxb[:, sl], wx_ref[n], nt, preferred_element_type=F32))
            dwa.append(lax.dot_general(xcb[:, sl], dpab[:, sl], tn, preferred_element_type=F32))
            dwx.append(lax.dot_general(xcb[:, sl], dpxb[:, sl], tn, preferred_element_type=F32))
        dxc = dxc + jnp.concatenate(dxb, axis=1)
        ups = [dxc] + [_rows_after(dxc, dxcn_ref[...], s) for s in range(1, LRU_CONV)]
        dbr = cw[LRU_CONV - 1:LRU_CONV, :] * ups[0]
        for k in range(LRU_CONV - 1):
            dbr = dbr + cw[k:k + 1, :] * ups[LRU_CONV - 1 - k]
        dbgr_ref[:, :w] = dbg.astype(dbgr_ref.dtype)
        dbgr_ref[:, w:] = dbr.astype(dbgr_ref.dtype)
        dcw = jnp.concatenate([jnp.sum(dxc * taps[LRU_CONV - 1 - k], axis=0, keepdims=True) for k in range(LRU_CONV)], axis=0)
        sums = [(dcw_ref, dcw), (dcb_ref, jnp.sum(dxc, axis=0, keepdims=True)), (dwa_ref, jnp.stack(dwa)),
                (dba_ref, jnp.sum(dpa, axis=0, keepdims=True)), (dwx_ref, jnp.stack(dwx)),
                (dbx_ref, jnp.sum(dpx, axis=0, keepdims=True)), (dlam_ref, dlam)]

        @pl.when(i == 0)
        def _():
            for ref, val in sums:
                ref[...] = val

        @pl.when(i > 0)
        def _():
            for ref, val in sums:
                ref[...] += val

        dxcn_ref[...] = dxc[:8, :]
        carry_ref[...] = (a * dh)[:8, :]

    def whole(shape):
        nd = len(shape)
        return pl.BlockSpec(tuple(shape), lambda i: (0,) * nd)

    consts = [conv_w, conv_b, wa, ba, wx, bx, lam]
    t8 = t // 8
    rev = lambda i: nc - 1 - i
    halo = lambda i: jnp.maximum(rev(i) * t8 - 1, 0)
    in_specs = [
        pl.BlockSpec((t, w), lambda i: (rev(i), 0)),
        pl.BlockSpec((t, w), lambda i: (rev(i), 0)),
        pl.BlockSpec((t, w), lambda i: (rev(i), 1)),
        pl.BlockSpec((8, w), lambda i: (halo(i), 1)),
        pl.BlockSpec((t, w), lambda i: (rev(i), 0)),
        pl.BlockSpec((8, w), lambda i: (halo(i), 0)),
    ] + [whole(c.shape) for c in consts]
    sum_shapes = [conv_w.shape, conv_b.shape, wa.shape, ba.shape, wx.shape, bx.shape, lam.shape]
    res = pl.pallas_call(
        body,
        name="lru_scan_bwd",
        grid=(nc,),
        in_specs=in_specs,
        out_specs=[pl.BlockSpec((t, w2), lambda i: (rev(i), 0))] + [whole(s) for s in sum_shapes],
        out_shape=[jax.ShapeDtypeStruct((l, w2), BF16)] + [jax.ShapeDtypeStruct(tuple(s), F32) for s in sum_shapes],
        scratch_shapes=[pltpu.VMEM((8, w), F32), pltpu.VMEM((8, w), F32)],
        compiler_params=_params(("arbitrary",)),
    )(dy, bgr, bgr, bgr, hseq, hseq, *consts)
    return res[0], res[1:]


def lru_fwd(hn, p):
    bgr = mm(hn, p["w_in"], name="lru_in")
    y, hseq = lru_scan_fwd(bgr, p["conv_w"], p["conv_b"], p["wa"], p["ba"], p["wx"], p["bx"], p["lam"])
    return y, (hn, bgr, hseq, y)


def lru_bwd(dmixed, saved, p):
    hn, bgr, hseq, y = saved
    dy = mm(dmixed, p["w_out"], tb=True, name="lru_dy")
    dw_out = mm(y, dmixed, ta=True, name="lru_dwout")
    dbgr, (dcw, dcb, dwa, dba, dwx, dbx, dlam) = lru_scan_bwd(dy, bgr, hseq, p["conv_w"], p["conv_b"], p["wa"], p["ba"],
                                                               p["wx"], p["bx"], p["lam"])
    dw_in = mm(hn, dbgr, ta=True, name="lru_dwin")
    grads = dict(w_in=dw_in, conv_w=dcw, conv_b=dcb, wa=dwa, ba=dba, wx=dwx, bx=dbx, lam=dlam, w_out=dw_out)
    return (dbgr, p["w_in"]), grads


SB_BLOCK = 256
SB_BLOCK_Q = 1024
_NT = (((1,), (1,)), ((), ()))
_TN = (((0,), (0,)), ((), ()))


def _running_sums(x, tri, total_col):
    xb = x.astype(BF16)
    run = jnp.dot(xb, tri, preferred_element_type=F32)
    return run, xb, run[:, total_col:total_col + 1]


def _tri(n, cmp):
    r = lax.broadcasted_iota(jnp.int32, (n, n), 0)
    c = lax.broadcasted_iota(jnp.int32, (n, n), 1)
    return cmp(r, c).astype(BF16)


SB_PAIR = 2 * SB_HEAD_DIM


def _pair_masks(x2, scale=None):
    lane = lax.broadcasted_iota(jnp.int32, x2.shape, 1)
    zero = jnp.zeros_like(x2)
    a, b = jnp.where(lane < SB_HEAD_DIM, x2, zero), jnp.where(lane >= SB_HEAD_DIM, x2, zero)
    if scale is not None:
        a, b = a * scale, b * scale
    return a, b


def _causal(shape, q0, k0):
    return lax.broadcasted_iota(jnp.int32, shape, 1) + k0 < lax.broadcasted_iota(jnp.int32, shape, 0) + q0


def _sb_blocks(l):
    bk = _tile(l, (SB_BLOCK, 128))
    bq = _tile(l, (SB_BLOCK_Q, 2 * SB_BLOCK, SB_BLOCK, 128))
    return bq, bk


def sb_pair_fwd(qkv):
    l, d3 = qkv.shape
    d = d3 // 3
    npair = d // SB_PAIR
    bq, bk = _sb_blocks(l)
    ratio = bq // bk
    scale = SB_HEAD_DIM ** -0.5
    t_suf = _tri(bk, lambda r, c: r > c)

    def body(q_ref, k_ref, v_ref, tsuf_ref, o_ref, ltot_ref):
        i = pl.program_id(1)
        qs = _pair_masks(q_ref[...], scale)
        tsuf = tsuf_ref[...]

        def tile(j, carry, masked):
            rows = pl.ds(pl.multiple_of(j * bk, bk), bk)
            k2 = k_ref[rows, :]
            v2 = v_ref[rows, :]
            out = []
            for q1, (c_r, acc) in zip(qs, carry):
                z = lax.dot_general(q1, k2, _NT, preferred_element_type=F32)
                lk = -_softplus(z)
                if masked:
                    causal = _causal(z.shape, i * bq, j * bk)
                    lk = jnp.where(causal, lk, 0.0)
                later, lkb, later0 = _running_sums(lk, tsuf, 0)
                w = jnp.exp(z + lk + c_r + later)
                if masked:
                    w = jnp.where(causal, w, 0.0)
                out.append((c_r + later0 + lkb[:, 0:1].astype(F32), acc + jnp.dot(w.astype(BF16), v2, preferred_element_type=F32)))
            return tuple(out)

        carry = ((jnp.zeros((bq, 1), F32), jnp.zeros((bq, SB_PAIR), F32)),) * 2
        for dgl in reversed(range(ratio)):
            carry = tile(i * ratio + dgl, carry, True)
        (c_a, acc_a), (c_b, acc_b) = lax.fori_loop(0, i * ratio, lambda jj, c: tile(i * ratio - 1 - jj, c, False), carry)
        lane = lax.broadcasted_iota(jnp.int32, acc_a.shape, 1)
        o_ref[...] = jnp.where(lane < SB_HEAD_DIM, acc_a, acc_b).astype(o_ref.dtype)
        ltot_ref[...] = jnp.where(lax.broadcasted_iota(jnp.int32, (bq, 2), 1) == 0, c_a, c_b)

    return pl.pallas_call(
        body,
        name="sb_attn_fwd",
        grid=(npair, l // bq),
        in_specs=[
            pl.BlockSpec((bq, SB_PAIR), lambda p, i: (i, p)),
            pl.BlockSpec((l, SB_PAIR), lambda p, i: (0, npair + p)),
            pl.BlockSpec((l, SB_PAIR), lambda p, i: (0, 2 * npair + p)),
            pl.BlockSpec((bk, bk), lambda p, i: (0, 0)),
        ],
        out_specs=[pl.BlockSpec((bq, SB_PAIR), lambda p, i: (i, p)), pl.BlockSpec((None, bq, 2), lambda p, i: (p, i, 0))],
        out_shape=[jax.ShapeDtypeStruct((l, d), BF16), jax.ShapeDtypeStruct((npair, l, 2), F32)],
        compiler_params=_params(("parallel", "parallel")),
    )(qkv, qkv, qkv, t_suf)


def sb_pair_bwd(qkv, do, ltot):
    l, d3 = qkv.shape
    d = d3 // 3
    npair = d // SB_PAIR
    bq, bk = _sb_blocks(l)
    ratio = bq // bk
    scale = SB_HEAD_DIM ** -0.5
    t_inc = _tri(bk, lambda r, c: r <= c)
    t_exc = _tri(bk, lambda r, c: r < c)

    def body(q_ref, k_ref, v_ref, do_ref, ltot_ref, tinc_ref, texc_ref, dq_ref, dk_ref, dv_ref):
        i = pl.program_id(1)

        @pl.when(i == 0)
        def _():
            dk_ref[...] = jnp.zeros_like(dk_ref)
            dv_ref[...] = jnp.zeros_like(dv_ref)

        qs = _pair_masks(q_ref[...], scale)
        dos = _pair_masks(do_ref[...])
        lt = ltot_ref[...]
        ltots = (lt[:, 0:1], lt[:, 1:2])
        tinc = tinc_ref[...]
        texc = texc_ref[...]

        def tile(j, carry, masked):
            rows = pl.ds(pl.multiple_of(j * bk, bk), bk)
            k2 = k_ref[rows, :]
            v2 = v_ref[rows, :]
            out = []
            dk2 = dv2 = None
            for q1, do1, ltot1, (c_l, c_p, dq) in zip(qs, dos, ltots, carry):
                z = lax.dot_general(q1, k2, _NT, preferred_element_type=F32)
                lk = -_softplus(z)
                if masked:
                    causal = _causal(z.shape, i * bq, j * bk)
                    lk = jnp.where(causal, lk, 0.0)
                log_beta = z + lk
                upto, _, lk_tile = _running_sums(lk, tinc, bk - 1)
                w = jnp.exp(log_beta + (ltot1 - c_l) - upto)
                if masked:
                    w = jnp.where(causal, w, 0.0)
                g = w * lax.dot_general(do1, v2, _NT, preferred_element_type=F32)
                before, gb, before_last = _running_sums(g, texc, bk - 1)
                dz = g - jnp.exp(log_beta) * (g + c_p + before)
                if masked:
                    dz = jnp.where(causal, dz, 0.0)
                dzb = dz.astype(BF16)
                dk1 = lax.dot_general(dzb, q1, _TN, preferred_element_type=F32)
                dv1 = lax.dot_general(w.astype(BF16), do1, _TN, preferred_element_type=F32)
                dk2 = dk1 if dk2 is None else dk2 + dk1
                dv2 = dv1 if dv2 is None else dv2 + dv1
                out.append((c_l + lk_tile, c_p + before_last + gb[:, bk - 1:bk].astype(F32),
                            dq + jnp.dot(dzb, k2, preferred_element_type=F32)))
            dk_ref[rows, :] += dk2
            dv_ref[rows, :] += dv2
            return tuple(out)

        zero = jnp.zeros((bq, 1), F32)
        carry = ((zero, zero, jnp.zeros((bq, SB_PAIR), F32)),) * 2
        carry = lax.fori_loop(0, i * ratio, lambda j, c: tile(j, c, False), carry)
        for dgl in range(ratio):
            carry = tile(i * ratio + dgl, carry, True)
        (_, _, dq_a), (_, _, dq_b) = carry
        lane = lax.broadcasted_iota(jnp.int32, dq_a.shape, 1)
        dq_ref[...] = jnp.where(lane < SB_HEAD_DIM, dq_a, dq_b) * scale

    col = lambda s: pl.BlockSpec((l, SB_PAIR), lambda p, i: (0, s * npair + p))
    blk_spec = pl.BlockSpec((bq, SB_PAIR), lambda p, i: (i, p))
    tri_spec = pl.BlockSpec((bk, bk), lambda p, i: (0, 0))
    return pl.pallas_call(
        body,
        name="sb_attn_bwd",
        grid=(npair, l // bq),
        in_specs=[blk_spec, col(1), col(2), blk_spec, pl.BlockSpec((None, bq, 2), lambda p, i: (p, i, 0)), tri_spec, tri_spec],
        out_specs=[blk_spec, pl.BlockSpec((l, SB_PAIR), lambda p, i: (0, p)), pl.BlockSpec((l, SB_PAIR), lambda p, i: (0, p))],
        out_shape=[jax.ShapeDtypeStruct((l, d), F32)] * 3,
        compiler_params=_params(("parallel", "arbitrary")),
    )(qkv, qkv, qkv, do, ltot, t_inc, t_exc)


def sb_fwd(hn, p):
    qkv = mm(hn, p["w_qkv"], name="sb_qkv", out_dtypes=(BF16,), tm=2048, tn=512)
    o, ltot = sb_pair_fwd(qkv)
    return o, (hn, qkv, ltot, o)


def sb_bwd(dmixed, saved, p):
    hn, qkv, ltot, o = saved
    do = mm(dmixed, p["w_out"], tb=True, name="sb_do", out_dtypes=(BF16,))
    dw_out = mm(o, dmixed, ta=True, name="sb_dwout")
    dqkv = jnp.concatenate([g.astype(BF16) for g in sb_pair_bwd(qkv, do, ltot)], axis=1)
    dw_qkv = mm(hn, dqkv, ta=True, name="sb_dwqkv")
    return (dqkv, p["w_qkv"]), dict(w_qkv=dw_qkv, w_out=dw_out)


S5_CHUNK = 128
S5_SLAB_GROUPS = 8


def _s5_discretise(lr, li, ldt, bre, bim):
    dt = jnp.exp(ldt)
    mag = jnp.exp(lr * dt)
    lbr = mag * jnp.cos(li * dt)
    lbi = mag * jnp.sin(li * dt)
    inv = 1.0 / (lr * lr + li * li)
    cr = ((lbr - 1.0) * lr + lbi * li) * inv
    ci = (lbi * lr - (lbr - 1.0) * li) * inv
    return lbr, lbi, cr * bre - ci * bim, cr * bim + ci * bre


def _s5_cols(lam_re, lam_im, log_dt, b_re, b_im):
    g, p = lam_re.shape
    col = lambda x: x.reshape(g * p, 1)
    ldt = jnp.broadcast_to(log_dt[:, None], (g, p))
    return col(lam_re), col(lam_im), col(ldt), b_re.reshape(g * p, -1), b_im.reshape(g * p, -1)


def _slab_b(bbar):
    sg = S5_SLAB_GROUPS
    gp, h = bbar.shape
    p = S5_STATE
    x = bbar.reshape(gp // (sg * p), sg, p, h)
    return jnp.einsum("kaph,ab->kahbp", x, jnp.eye(sg, dtype=x.dtype)).reshape(-1, sg * h, sg * p)


def _unslab_b(dslab):
    sg, p = S5_SLAB_GROUPS, S5_STATE
    nk, sh, _ = dslab.shape
    h = sh // sg
    x = dslab.reshape(nk, sg, h, sg, p)
    return jnp.einsum("kahbp,ab->kaph", x, jnp.eye(sg, dtype=x.dtype)).reshape(nk * sg * p, h)


def _slab_c(c):
    sg = S5_SLAB_GROUPS
    g, h, p = c.shape
    x = c.reshape(g // sg, sg, h, p)
    return jnp.einsum("kahp,ab->kapbh", x, jnp.eye(sg, dtype=x.dtype)).reshape(-1, sg * p, sg * h)


def _unslab_c(dslab):
    sg, p = S5_SLAB_GROUPS, S5_STATE
    nk, _, sh = dslab.shape
    h = sh // sg
    x = dslab.reshape(nk, sg, p, sg, h)
    return jnp.einsum("kapbh,ab->kahp", x, jnp.eye(sg, dtype=x.dtype)).reshape(nk * sg, h, p)


def _cmul_scan(lre, lim, xre, xim, cre, cim, reverse):
    t = xre.shape[0]
    shift = _shift_up if reverse else _shift_down
    pows = [(lre, lim)]
    for _ in range(7):
        pr, pi = pows[-1]
        pows.append((pr * lre - pi * lim, pr * lim + pi * lre))
    by_row = pows[::-1] if reverse else pows
    pw_re = jnp.concatenate([p[0] for p in by_row], axis=0)
    pw_im = jnp.concatenate([p[1] for p in by_row], axis=0)
    n_groups = t // 8
    out_re, out_im = [None] * n_groups, [None] * n_groups
    for gi in (reversed(range(n_groups)) if reverse else range(n_groups)):
        gre, gim = xre[gi * 8:(gi + 1) * 8], xim[gi * 8:(gi + 1) * 8]
        for d in (1, 2, 4):
            mr, mi = pows[d - 1]
            sre, sim = shift(gre, d, 0.0), shift(gim, d, 0.0)
            gre, gim = gre + mr * sre - mi * sim, gim + mr * sim + mi * sre
        gre, gim = gre + pw_re * cre - pw_im * cim, gim + pw_re * cim + pw_im * cre
        edge = slice(0, 1) if reverse else slice(7, 8)
        cre, cim = gre[edge], gim[edge]
        out_re[gi], out_im[gi] = gre, gim
    return jnp.concatenate(out_re, axis=0), jnp.concatenate(out_im, axis=0)


def s5_scan_fwd(u, lbr, lbi, bbd_re, bbd_im, cbd_re, cbd_imn, d_skip):
    l, w = u.shape
    n = lbr.shape[1]
    nk, cw, sw = bbd_re.shape
    t = _tile(l, (S5_CHUNK, 64, 32, 16, 8))

    def body(u_ref, lbr_ref, lbi_ref, bre_ref, bim_ref, cre_ref, cim_ref, d_ref, sre_ref, sim_ref, y_ref, z_ref, pre_ref, pim_ref):
        @pl.when(pl.program_id(0) == 0)
        def _():
            pre_ref[...] = jnp.zeros_like(pre_ref)
            pim_ref[...] = jnp.zeros_like(pim_ref)

        uu = u_ref[...]
        ub = uu.astype(BF16)
        lre, lim = lbr_ref[...], lbi_ref[...]
        xre = jnp.concatenate([jnp.dot(ub[:, k * cw:(k + 1) * cw], bre_ref[k], preferred_element_type=F32) for k in range(nk)], axis=1)
        xim = jnp.concatenate([jnp.dot(ub[:, k * cw:(k + 1) * cw], bim_ref[k], preferred_element_type=F32) for k in range(nk)], axis=1)
        sre, sim = _cmul_scan(lre, lim, xre, xim, pre_ref[pl.ds(7, 1), :], pim_ref[pl.ds(7, 1), :], False)
        sre_ref[...] = sre
        sim_ref[...] = sim
        pre_ref[...] = sre[t - 8:, :]
        pim_ref[...] = sim[t - 8:, :]
        sreb, simb = sre.astype(BF16), sim.astype(BF16)
        y = jnp.concatenate(
            [jnp.dot(sreb[:, k * sw:(k + 1) * sw], cre_ref[k], preferred_element_type=F32)
             + jnp.dot(simb[:, k * sw:(k + 1) * sw], cim_ref[k], preferred_element_type=F32) for k in range(nk)], axis=1)
        y = y + d_ref[...] * uu
        y_ref[...] = y
        z_ref[...] = _gelu(y).astype(z_ref.dtype)

    def whole(x):
        nd = x.ndim
        return pl.BlockSpec(x.shape, lambda i: (0,) * nd)

    consts = [lbr, lbi, bbd_re, bbd_im, cbd_re, cbd_imn, d_skip]
    row = lambda c: pl.BlockSpec((t, c), lambda i: (i, 0))
    return pl.pallas_call(
        body,
        name="s5_scan_fwd",
        grid=(l // t,),
        in_specs=[row(w)] + [whole(c) for c in consts],
        out_specs=[row(n), row(n), row(w), row(w)],
        out_shape=[jax.ShapeDtypeStruct((l, n), F32), jax.ShapeDtypeStruct((l, n), F32), jax.ShapeDtypeStruct((l, w), F32),
                   jax.ShapeDtypeStruct((l, w), BF16)],
        scratch_shapes=[pltpu.VMEM((8, n), F32), pltpu.VMEM((8, n), F32)],
        compiler_params=_params(("arbitrary",)),
    )(u, *consts)


def s5_scan_bwd(dz, y, u, sre, sim, lbr, lbi, bbd_re, bbd_im, cbd_re, cbd_imn, d_skip):
    l, w = u.shape
    n = lbr.shape[1]
    nk, cw, sw = bbd_re.shape
    t = _tile(l, (S5_CHUNK, 64, 32, 16, 8))
    nc = l // t

    def body(dz_ref, y_ref, u_ref, sre_ref, sim_ref, hre_ref, him_ref, lbr_ref, lbi_ref, bre_ref, bim_ref, cre_ref, cim_ref,
             d_ref, du_ref, dlr_ref, dli_ref, dbre_ref, dbim_ref, dcre_ref, dcim_ref, dd_ref, nre_ref, nim_ref):
        i = pl.program_id(0)
        has_prev = (i < nc - 1).astype(F32)

        @pl.when(i == 0)
        def _():
            nre_ref[...] = jnp.zeros_like(nre_ref)
            nim_ref[...] = jnp.zeros_like(nim_ref)

        uu = u_ref[...]
        ub = uu.astype(BF16)
        lre, lim = lbr_ref[...], lbi_ref[...]
        _, gelu_vjp = jax.vjp(_gelu, y_ref[...])
        dy = gelu_vjp(dz_ref[...].astype(F32))[0]
        dyb = dy.astype(BF16)
        gre = jnp.concatenate([lax.dot_general(dyb[:, k * cw:(k + 1) * cw], cre_ref[k], _NT, preferred_element_type=F32)
                               for k in range(nk)], axis=1)
        gim = jnp.concatenate([lax.dot_general(dyb[:, k * cw:(k + 1) * cw], cim_ref[k], _NT, preferred_element_type=F32)
                               for k in range(nk)], axis=1)
        dsre, dsim = _cmul_scan(lre, -lim, gre, gim, nre_ref[pl.ds(0, 1), :], nim_ref[pl.ds(0, 1), :], True)
        nre_ref[...] = dsre[:8, :]
        nim_ref[...] = dsim[:8, :]
        dsreb, dsimb = dsre.astype(BF16), dsim.astype(BF16)
        s_re, s_im = sre_ref[...], sim_ref[...]
        du = jnp.concatenate(
            [lax.dot_general(dsreb[:, k * sw:(k + 1) * sw], bre_ref[k], _NT, preferred_element_type=F32)
             + lax.dot_general(dsimb[:, k * sw:(k + 1) * sw], bim_ref[k], _NT, preferred_element_type=F32) for k in range(nk)],
            axis=1)
        du_ref[...] = (du + d_ref[...] * dy).astype(du_ref.dtype)
        pre = _rows_before(s_re, hre_ref[...] * has_prev, 1)
        pim = _rows_before(s_im, him_ref[...] * has_prev, 1)
        sreb, simb = s_re.astype(BF16), s_im.astype(BF16)
        sums = [
            (dlr_ref, jnp.sum(dsre * pre + dsim * pim, axis=0, keepdims=True)),
            (dli_ref, jnp.sum(dsim * pre - dsre * pim, axis=0, keepdims=True)),
            (dbre_ref, jnp.stack([lax.dot_general(ub[:, k * cw:(k + 1) * cw], dsreb[:, k * sw:(k + 1) * sw], _TN,
                                                  preferred_element_type=F32) for k in range(nk)])),
            (dbim_ref, jnp.stack([lax.dot_general(ub[:, k * cw:(k + 1) * cw], dsimb[:, k * sw:(k + 1) * sw], _TN,
                                                  preferred_element_type=F32) for k in range(nk)])),
            (dcre_ref, jnp.stack([lax.dot_general(sreb[:, k * sw:(k + 1) * sw], dyb[:, k * cw:(k + 1) * cw], _TN,
                                                  preferred_element_type=F32) for k in range(nk)])),
            (dcim_ref, jnp.stack([lax.dot_general(simb[:, k * sw:(k + 1) * sw], dyb[:, k * cw:(k + 1) * cw], _TN,
                                                  preferred_element_type=F32) for k in range(nk)])),
            (dd_ref, jnp.sum(dy * uu, axis=0, keepdims=True)),
        ]

        @pl.when(i == 0)
        def _():
            for ref, val in sums:
                ref[...] = val

        @pl.when(i > 0)
        def _():
            for ref, val in sums:
                ref[...] += val

    def whole(shape):
        nd = len(shape)
        return pl.BlockSpec(tuple(shape), lambda i: (0,) * nd)

    consts = [lbr, lbi, bbd_re, bbd_im, cbd_re, cbd_imn, d_skip]
    t8 = t // 8
    rev = lambda i: nc - 1 - i
    halo = lambda i: jnp.maximum(rev(i) * t8 - 1, 0)
    row = lambda c: pl.BlockSpec((t, c), lambda i: (rev(i), 0))
    sum_shapes = [lbr.shape, lbi.shape, bbd_re.shape, bbd_im.shape, cbd_re.shape, cbd_imn.shape, d_skip.shape]
    res = pl.pallas_call(
        body,
        name="s5_scan_bwd",
        grid=(nc,),
        in_specs=[row(w), row(w), row(w), row(n), row(n), pl.BlockSpec((8, n), lambda i: (halo(i), 0)),
                  pl.BlockSpec((8, n), lambda i: (halo(i), 0))] + [whole(c.shape) for c in consts],
        out_specs=[row(w)] + [whole(s) for s in sum_shapes],
        out_shape=[jax.ShapeDtypeStruct((l, w), BF16)] + [jax.ShapeDtypeStruct(tuple(s), F32) for s in sum_shapes],
        scratch_shapes=[pltpu.VMEM((8, n), F32), pltpu.VMEM((8, n), F32)],
        compiler_params=_params(("arbitrary",)),
    )(dz, y, u, sre, sim, sre, sim, *consts)
    return res[0], res[1:]


def _glu(vg):
    w = vg.shape[1] // 2
    return vg[:, :w] * _sigmoid(vg[:, w:])


def s5_fwd(hn, h, p):
    cols = _s5_cols(p["lam_re"], p["lam_im"], p["log_dt"], p["b_re"], p["b_im"])
    gp, hh = cols[3].shape
    lbr, lbi, bbr, bbi = rowwise(_s5_discretise, list(cols), [], [(1, F32), (1, F32), (hh, F32), (hh, F32)],
                                 name="s5_discretise", tm=512)
    consts = (lbr.reshape(1, gp), lbi.reshape(1, gp), _slab_b(bbr).astype(BF16), _slab_b(bbi).astype(BF16),
              _slab_c(p["c_re"]).astype(BF16), _slab_c(-p["c_im"]).astype(BF16), p["d"])
    u = mm(hn, p["w_in"], name="s5_in")
    sre, sim, y, z = s5_scan_fwd(u, *consts)
    vg = mm(z, p["w_out"], name="s5_out", out_dtypes=(BF16,))
    h_new = rowwise(lambda a, r: r + _glu(a.astype(F32)), [vg, h], [], [(h.shape[1], F32)], name="s5_glu")
    return h_new, (hn, u, sre, sim, y, z, vg, cols, consts)


def s5_bwd(dh_new, saved, p):
    hn, u, sre, sim, y, z, vg, cols, consts = saved

    def glu_bwd(a, dm):
        _, vjp = jax.vjp(_glu, a.astype(F32))
        return vjp(dm)[0]

    dvg = rowwise(glu_bwd, [vg, dh_new], [], [(vg.shape[1], BF16)], name="s5_dglu")
    dw_out = mm(z, dvg, ta=True, name="s5_dwout")
    dz = mm(dvg, p["w_out"], tb=True, name="s5_dz", out_dtypes=(BF16,), tk=2048)
    du, (dlbr, dlbi, dbbr, dbbi, dcre, dcimn, dd) = s5_scan_bwd(dz, y, u, sre, sim, *consts)
    dw_in = mm(hn, du, ta=True, name="s5_dwin")
    gp = cols[0].shape[0]

    def disc_bwd(lr, li, ldt, bre, bim, g0, g1, g2, g3):
        _, vjp = jax.vjp(_s5_discretise, lr, li, ldt, bre, bim)
        return vjp((g0, g1, g2, g3))

    cot = (dlbr.reshape(gp, 1), dlbi.reshape(gp, 1), _unslab_b(dbbr), _unslab_b(dbbi))
    dlr, dli, dldt, dbre, dbim = rowwise(disc_bwd, list(cols + cot), [], [(c.shape[1], F32) for c in cols],
                                         name="s5_discretise_bwd", tm=512)
    g_, p_ = p["lam_re"].shape
    grads = dict(w_in=dw_in, lam_re=dlr.reshape(g_, p_), lam_im=dli.reshape(g_, p_), log_dt=dldt.reshape(g_, p_).sum(axis=1),
                 b_re=dbre.reshape(p["b_re"].shape), b_im=dbim.reshape(p["b_im"].shape), c_re=_unslab_c(dcre),
                 c_im=-_unslab_c(dcimn), d=dd, w_out=dw_out)
    return (du, p["w_in"]), grads


MESH = pl.DeviceIdType.MESH
N_CHIPS = 4
N_DEVICES = 8


def _place():
    x, y, c = lax.axis_index("x"), lax.axis_index("y"), lax.axis_index("c")
    return x, y, c, [(1 - x, y), (x, 1 - y), (1 - x, 1 - y)]


def _hbm_call(body, name, ins, out_shapes, n_remote, n_local=0):
    hbm = pl.BlockSpec(memory_space=pltpu.HBM)
    scratch = [pltpu.SemaphoreType.DMA((n_remote,)), pltpu.SemaphoreType.DMA((n_remote,))]
    if n_local:
        scratch.append(pltpu.SemaphoreType.DMA((n_local,)))
    return pl.pallas_call(
        body,
        name=name,
        in_specs=[hbm] * len(ins),
        out_specs=[hbm] * len(out_shapes),
        out_shape=out_shapes,
        scratch_shapes=scratch,
    )(*ins)


def _split_dim(shape):
    return next(d for d, s in enumerate(shape) if s >= 2 and s % 2 == 0)


def gather_chips(shards):
    n = len(shards)
    cuts = [_split_dim(s.shape) for s in shards]

    def body(*refs):
        ins, outs = refs[:n], refs[n:2 * n]
        send, recv, local = refs[2 * n:]
        x, y, c, chips = _place()
        me = 2 * x + y

        def half(ref, t, which, lead=()):
            size = shards[t].shape[cuts[t]] // 2
            return ref.at[lead + (slice(None),) * cuts[t] + (pl.ds(which * size, size),)]

        def copy(t, k, block, which, to, src=None):
            dst = half(outs[t], t, which, (block,))
            return pltpu.make_async_remote_copy(dst if src is None else src, dst, send.at[6 * t + k], recv.at[6 * t + k],
                                                device_id=to, device_id_type=MESH)

        started = []
        for t in range(n):
            own = pltpu.make_async_copy(ins[t], outs[t].at[me], local.at[t])
            own.start()
            started.append(own)
        sends = [copy(t, k, me, c, (px, py, c), src=half(ins[t], t, c)) for t in range(n) for k, (px, py) in enumerate(chips)]
        for cp in sends:
            cp.start()
        passed = []
        for t in range(n):
            for k, (px, py) in enumerate(chips):
                copy(t, k, 2 * px + py, c, (px, py, c)).wait_recv()
                on = copy(t, 3 + k, 2 * px + py, c, (x, y, 1 - c))
                on.start()
                passed.append(on)
        for t in range(n):
            for k, (px, py) in enumerate(chips):
                copy(t, 3 + k, 2 * px + py, 1 - c, (x, y, 1 - c)).wait_recv()
        for cp in sends + passed:
            cp.wait_send()
        for cp in started:
            cp.wait()

    return _hbm_call(body, "gather_chips", shards, [jax.ShapeDtypeStruct((N_CHIPS,) + s.shape, s.dtype) for s in shards],
                     6 * n, n)


def scatter_chips(blocked):
    n = len(blocked)

    def body(*refs):
        ins, outs = refs[:n], refs[n:2 * n]
        send, recv = refs[2 * n:]
        x, y, c, chips = _place()
        pending = []
        for t in range(n):
            for k, (px, py) in enumerate(chips):
                cp = pltpu.make_async_remote_copy(ins[t].at[2 * px + py], outs[t].at[k], send.at[3 * t + k], recv.at[3 * t + k],
                                                  device_id=(px, py, c), device_id_type=MESH)
                cp.start()
                pending.append(cp)
        for cp in pending:
            cp.wait()

    return _hbm_call(body, "scatter_chips", blocked, [jax.ShapeDtypeStruct((3,) + b.shape[1:], b.dtype) for b in blocked], 3 * n)


def swap_cores(arrays):
    n = len(arrays)

    def body(*refs):
        ins, outs = refs[:n], refs[n:2 * n]
        send, recv = refs[2 * n:]
        x, y, c, _ = _place()
        pending = []
        for t in range(n):
            cp = pltpu.make_async_remote_copy(ins[t], outs[t], send.at[t], recv.at[t], device_id=(x, y, 1 - c),
                                              device_id_type=MESH)
            cp.start()
            pending.append(cp)
        for cp in pending:
            cp.wait()

    return _hbm_call(body, "swap_cores", arrays, [jax.ShapeDtypeStruct(a.shape, a.dtype) for a in arrays], n)


def gather_devices(buf):
    def body(in_ref, out_ref, send, recv, local):
        x, y, c, _ = _place()
        own = pltpu.make_async_copy(in_ref, out_ref.at[4 * x + 2 * y + c], local.at[0])
        own.start()
        pending = [own]
        for k in range(1, N_DEVICES):
            px = x ^ ((k >> 2) & 1)
            py = y ^ ((k >> 1) & 1)
            pc = c ^ (k & 1)
            going = pltpu.make_async_remote_copy(in_ref, out_ref.at[4 * x + 2 * y + c], send.at[k - 1], recv.at[k - 1],
                                                 device_id=(px, py, pc), device_id_type=MESH)
            going.start()
            pending.append(pltpu.make_async_remote_copy(in_ref, out_ref.at[4 * px + 2 * py + pc], send.at[k - 1], recv.at[k - 1],
                                                        device_id=(px, py, pc), device_id_type=MESH))
        for cp in pending:
            cp.wait()

    return _hbm_call(body, "gather_devices", [buf], [jax.ShapeDtypeStruct((N_DEVICES,) + buf.shape, buf.dtype)],
                     N_DEVICES - 1, 1)[0]


def _adamw(w, g, m, v):
    m = ADAM_B1 * m + (1.0 - ADAM_B1) * g
    v = ADAM_B2 * v + (1.0 - ADAM_B2) * (g * g)
    m_hat = m / (1.0 - ADAM_B1 ** ADAM_STEP)
    v_hat = v / (1.0 - ADAM_B2 ** ADAM_STEP)
    return -ADAM_LR * (m_hat / (jnp.sqrt(v_hat) + ADAM_EPS) + ADAM_WD * w), m, v


def _rows2d(a):
    return a.reshape(-1, a.shape[-1])


WEIGHTS = ["ffn1_norm", "ffn1_w_in", "ffn1_w_out", "mix_norm", "ffn2_norm", "ffn2_w_in", "ffn2_w_out", "final_norm",
           "s5_w_in", "s5_lam_re", "s5_lam_im", "s5_log_dt", "s5_b_re", "s5_b_im", "s5_c_re", "s5_c_im", "s5_d", "s5_w_out",
           "sb_w_qkv", "sb_w_out", "lru_w_in", "lru_conv_w", "lru_conv_b", "lru_w_a", "lru_b_a", "lru_w_x", "lru_b_x",
           "lru_lambda", "lru_w_out"]
INPUTS = ["x"] + WEIGHTS + ["loss_target"] + ["m_" + n for n in WEIGHTS] + ["v_" + n for n in WEIGHTS]
SHARDED_BIG = dict(ffn1_w_in=2, ffn1_w_out=1, ffn2_w_in=2, ffn2_w_out=1, s5_w_in=1, s5_w_out=2, sb_w_qkv=2, sb_w_out=1,
                   lru_w_in=2, lru_w_a=2, lru_w_x=2, lru_w_out=1)
SHARDED_SMALL = dict(s5_d=1, lru_conv_w=2, lru_conv_b=1, lru_b_a=2, lru_b_x=2, lru_lambda=1)
REPLICATED = [n for n in WEIGHTS if n not in SHARDED_BIG and n not in SHARDED_SMALL]
PACK_LANES = 128
PACK_ROW_ALIGN = 16


def _unblock(g, d):
    full = jnp.moveaxis(g, 0, d)
    return full.reshape(full.shape[:d] + (full.shape[d] * full.shape[d + 1],) + full.shape[d + 2:])


def _block(full, d):
    s = full.shape[d] // N_CHIPS
    return jnp.moveaxis(full.reshape(full.shape[:d] + (N_CHIPS, s) + full.shape[d + 1:]), d, 0)


def _pack(arrays, lead=()):
    nl = len(lead)
    flat = jnp.concatenate([a.reshape(lead + (-1,)) for a in arrays], axis=nl)
    quantum = PACK_LANES * PACK_ROW_ALIGN
    pad = (-flat.shape[nl]) % quantum
    flat = jnp.pad(flat, [(0, 0)] * nl + [(0, pad)])
    return flat.reshape(lead + (-1, PACK_LANES))


def _unpack(packed, shapes, lead=()):
    nl = len(lead)
    flat = packed.reshape(lead + (-1,))
    out, off = [], 0
    for s in shapes:
        size = math.prod(s)
        out.append(lax.slice_in_dim(flat, off, off + size, axis=nl).reshape(lead + tuple(s)))
        off += size
    return out


def _forward_backward(x, target, w, norms):
    depth = norms["ffn1_norm"].shape[0]
    n_s5 = w["s5_w_in"].shape[0]

    def mixer_params(layer):
        kind, j = layer % N_MIXERS, layer // N_MIXERS
        if kind == 0:
            return kind, j, dict(w_in=w["s5_w_in"][j], lam_re=w["s5_lam_re"][j], lam_im=w["s5_lam_im"][j], log_dt=w["s5_log_dt"][j],
                                 b_re=w["s5_b_re"][j], b_im=w["s5_b_im"][j], c_re=w["s5_c_re"][j], c_im=w["s5_c_im"][j],
                                 d=w["s5_d"][j].reshape(1, -1), w_out=w["s5_w_out"][j])
        if kind == 1:
            return kind, j, dict(w_qkv=w["sb_w_qkv"][j], w_out=w["sb_w_out"][j])
        return kind, j, dict(w_in=w["lru_w_in"][j], conv_w=w["lru_conv_w"][j], conv_b=w["lru_conv_b"][j].reshape(1, -1),
                             wa=w["lru_w_a"][j], ba=w["lru_b_a"][j].reshape(1, -1), wx=w["lru_w_x"][j],
                             bx=w["lru_b_x"][j].reshape(1, -1), lam=w["lru_lambda"][j].reshape(1, -1), w_out=w["lru_w_out"][j])

    h = x
    tape = []
    for layer in range(depth):
        h, s1 = ffn_fwd(h, norms["ffn1_norm"][layer], w["ffn1_w_in"][layer], w["ffn1_w_out"][layer], "ffn")
        kind, j, p = mixer_params(layer)
        h_mix_in = h
        hn = rms_fwd(h, norms["mix_norm"][layer], "mix_norm")
        if kind == 0:
            h, sm = s5_fwd(hn, h, p)
        elif kind == 1:
            o_flat, sm = sb_fwd(hn, p)
            h = mm(o_flat, p["w_out"], name="mix_out", extras=(h,), epilogue=lambda acc, res: res + acc)
        else:
            y, sm = lru_fwd(hn, p)
            h = mm(y, p["w_out"], name="mix_out", extras=(h,), epilogue=lambda acc, res: res + acc)
        h, s2 = ffn_fwd(h, norms["ffn2_norm"][layer], w["ffn2_w_in"][layer], w["ffn2_w_out"][layer], "ffn")
        tape.append((s1, h_mix_in, sm, s2))

    loss, dh, g_final = loss_fwd_bwd(h, norms["final_norm"], target)

    per_layer = {n: [None] * depth for n in ("ffn1_norm", "ffn1_w_in", "ffn1_w_out", "mix_norm", "ffn2_norm", "ffn2_w_in", "ffn2_w_out")}
    mix = {}
    for layer in reversed(range(depth)):
        s1, h_mix_in, sm, s2 = tape[layer]
        dh, dg, dwi, dwo = ffn_bwd(dh, s2, norms["ffn2_norm"][layer], w["ffn2_w_in"][layer], w["ffn2_w_out"][layer], "ffn")
        per_layer["ffn2_norm"][layer], per_layer["ffn2_w_in"][layer], per_layer["ffn2_w_out"][layer] = dg, dwi, dwo
        kind, j, p = mixer_params(layer)
        if kind == 0:
            (da, w_first), g = s5_bwd(dh, sm, p)
            names = dict(w_in="s5_w_in", lam_re="s5_lam_re", lam_im="s5_lam_im", log_dt="s5_log_dt", b_re="s5_b_re", b_im="s5_b_im",
                         c_re="s5_c_re", c_im="s5_c_im", d="s5_d", w_out="s5_w_out")
        elif kind == 1:
            (da, w_first), g = sb_bwd(dh, sm, p)
            names = dict(w_qkv="sb_w_qkv", w_out="sb_w_out")
        else:
            (da, w_first), g = lru_bwd(dh, sm, p)
            names = dict(w_in="lru_w_in", conv_w="lru_conv_w", conv_b="lru_conv_b", wa="lru_w_a", ba="lru_b_a", wx="lru_w_x",
                         bx="lru_b_x", lam="lru_lambda", w_out="lru_w_out")
        for k, full_name in names.items():
            mix.setdefault(full_name, {})[j] = g[k].reshape(w[full_name].shape[1:])
        dh, dg = norm_input_bwd(da, _rows_spec, 1, w_first, h_mix_in, norms["mix_norm"][layer], dh, "mix_dhn")
        per_layer["mix_norm"][layer] = dg
        dh, dg, dwi, dwo = ffn_bwd(dh, s1, norms["ffn1_norm"][layer], w["ffn1_w_in"][layer], w["ffn1_w_out"][layer], "ffn")
        per_layer["ffn1_norm"][layer], per_layer["ffn1_w_in"][layer], per_layer["ffn1_w_out"][layer] = dg, dwi, dwo

    grads = {n: jnp.stack(v) for n, v in per_layer.items()}
    grads["final_norm"] = g_final
    for n, by_j in mix.items():
        grads[n] = jnp.stack([by_j[j] for j in range(len(by_j))])
    return loss, dh, grads


def kernel(x, ffn1_norm, ffn1_w_in, ffn1_w_out, mix_norm, ffn2_norm, ffn2_w_in, ffn2_w_out, final_norm, s5_w_in,
           s5_lam_re, s5_lam_im, s5_log_dt, s5_b_re, s5_b_im, s5_c_re, s5_c_im, s5_d, s5_w_out, sb_w_qkv,
           sb_w_out, lru_w_in, lru_conv_w, lru_conv_b, lru_w_a, lru_b_a, lru_w_x, lru_b_x, lru_lambda,
           lru_w_out, loss_target, m_ffn1_norm, m_ffn1_w_in, m_ffn1_w_out, m_mix_norm, m_ffn2_norm,
           m_ffn2_w_in, m_ffn2_w_out, m_final_norm, m_s5_w_in, m_s5_lam_re, m_s5_lam_im, m_s5_log_dt,
           m_s5_b_re, m_s5_b_im, m_s5_c_re, m_s5_c_im, m_s5_d, m_s5_w_out, m_sb_w_qkv, m_sb_w_out, m_lru_w_in,
           m_lru_conv_w, m_lru_conv_b, m_lru_w_a, m_lru_b_a, m_lru_w_x, m_lru_b_x, m_lru_lambda, m_lru_w_out,
           v_ffn1_norm, v_ffn1_w_in, v_ffn1_w_out, v_mix_norm, v_ffn2_norm, v_ffn2_w_in, v_ffn2_w_out,
           v_final_norm, v_s5_w_in, v_s5_lam_re, v_s5_lam_im, v_s5_log_dt, v_s5_b_re, v_s5_b_im, v_s5_c_re,
           v_s5_c_im, v_s5_d, v_s5_w_out, v_sb_w_qkv, v_sb_w_out, v_lru_w_in, v_lru_conv_w, v_lru_conv_b,
           v_lru_w_a, v_lru_b_a, v_lru_w_x, v_lru_b_x, v_lru_lambda, v_lru_w_out):
    a = dict(locals())
    assert list(a) == INPUTS
    x, y, c, _ = _place()
    chip = 2 * x + y
    everyone = ("x", "y", "c")

    big = list(SHARDED_BIG)
    small = list(SHARDED_SMALL)
    small_packed = _pack([a[n] for n in small])
    gathered = gather_chips([a[n].astype(BF16) for n in big] + [small_packed])
    w = {n: _unblock(g, SHARDED_BIG[n] + 0) for n, g in zip(big, gathered[:-1])}
    small_blocks = _unpack(gathered[-1], [a[n].shape for n in small], lead=(N_CHIPS,))
    w.update({n: _unblock(g, SHARDED_SMALL[n]) for n, g in zip(small, small_blocks)})
    for n in ("s5_lam_re", "s5_lam_im", "s5_log_dt", "s5_b_re", "s5_b_im", "s5_c_re", "s5_c_im"):
        w[n] = a[n]
    norms = {n: a[n] for n in ("ffn1_norm", "mix_norm", "ffn2_norm", "final_norm")}

    loss, dx, grads = _forward_backward(a["x"][0], a["loss_target"][0], w, norms)
    loss = lax.psum(loss, everyone)

    own = {n: lax.dynamic_slice_in_dim(grads[n], chip * a[n].shape[d], a[n].shape[d], axis=d)
           for n, d in {**SHARDED_BIG, **SHARDED_SMALL}.items()}
    small_send = _pack([_block(grads[n], SHARDED_SMALL[n]) for n in small], lead=(N_CHIPS,))
    received = scatter_chips([_block(grads[n], SHARDED_BIG[n]).astype(BF16) for n in big] + [small_send])
    own_list = [own[n] for n in big] + [_pack([own[n] for n in small])]
    partial = []
    for mine, got in zip(own_list, received):
        rows = _rows2d(mine)
        got3 = got.reshape((3,) + rows.shape)
        partial.append(rowwise(lambda o, r: ((o + r[0].astype(F32)) + r[1].astype(F32)) + r[2].astype(F32), [rows, got3], [],
                               [(rows.shape[1], F32)], name="sum_chips"))
    other = swap_cores(partial)

    def adam_sharded(wv, ga, gb, m, v):
        shape = wv.shape
        res = rowwise(lambda w_, a_, b_, mm_, vv_: (a_ + b_,) + _adamw(w_, a_ + b_, mm_, vv_),
                      [_rows2d(wv), ga, gb, _rows2d(m), _rows2d(v)], [], [(shape[-1], F32)] * 4, name="adamw_sharded")
        return [r.reshape(shape) for r in res]

    out_grad, out_delta, out_m, out_v = {}, {}, {}, {}
    for i, n in enumerate(big):
        out_grad[n], out_delta[n], out_m[n], out_v[n] = adam_sharded(a[n], partial[i], other[i], a["m_" + n], a["v_" + n])
    small_shapes = [a[n].shape for n in small]
    sp = [_pack([a[pre + n] for n in small]) for pre in ("", "m_", "v_")]
    g_, d_, m_, v_ = adam_sharded(sp[0], partial[-1], other[-1], sp[1], sp[2])
    for n, gg, dd, mm_, vv in zip(small, _unpack(g_, small_shapes), _unpack(d_, small_shapes), _unpack(m_, small_shapes),
                                  _unpack(v_, small_shapes)):
        out_grad[n], out_delta[n], out_m[n], out_v[n] = gg, dd, mm_, vv

    rep_shapes = [a[n].shape for n in REPLICATED]
    rep_all = gather_devices(_pack([grads[n] for n in REPLICATED]))
    rp = [_pack([a[pre + n] for n in REPLICATED]) for pre in ("", "m_", "v_")]

    def adam_rep(w_, g8, mm_, vv_):
        g = g8[0]
        for k in range(1, N_DEVICES):
            g = g + g8[k]
        return (g,) + _adamw(w_, g, mm_, vv_)

    g_, d_, m_, v_ = rowwise(adam_rep, [rp[0], rep_all, rp[1], rp[2]], [], [(PACK_LANES, F32)] * 4, name="adamw_replicated")
    for n, gg, dd, mm_, vv in zip(REPLICATED, _unpack(g_, rep_shapes), _unpack(d_, rep_shapes), _unpack(m_, rep_shapes),
                                  _unpack(v_, rep_shapes)):
        out_grad[n], out_delta[n], out_m[n], out_v[n] = gg, dd, mm_, vv

    return (loss, dx[None], *[out_grad[n] for n in WEIGHTS], *[out_delta[n] for n in WEIGHTS], *[out_m[n] for n in WEIGHTS],
            *[out_v[n] for n in WEIGHTS])
```

```python
import functools
import math

import jax
import jax.numpy as jnp
from jax import lax
from jax.experimental import pallas as pl
from jax.experimental.pallas import tpu as pltpu

F32 = jnp.float32
BF16 = jnp.bfloat16

VMEM_LIMIT_BYTES = 56 * 1024 * 1024

RMS_EPS = 1e-6
D_FF = 2816
S5_GROUP = 16
S5_STATE = 64
SB_HEAD_DIM = 64
LRU_BLOCK_WIDTH = 256
LRU_CONV = 4
LRU_C = 8.0
N_MIXERS = 3

ADAM_LR = 0.001
ADAM_B1 = 0.9
ADAM_B2 = 0.999
ADAM_EPS = 1e-08
ADAM_WD = 0.01
ADAM_STEP = 10


def _params(semantics):
    return pltpu.CompilerParams(dimension_semantics=semantics, vmem_limit_bytes=VMEM_LIMIT_BYTES)


def _tile(dim, prefs):
    for t in prefs:
        if t <= dim and dim % t == 0:
            return t
    return dim


def _mm_call(name, grid, a, a_spec, b, b_spec, dims, extras, outs, epilogue, acc_shape):
    n_extra, n_out, n_k = len(extras), len(outs), grid[-1]

    def body(a_ref, b_ref, *rest):
        extra_refs = rest[:n_extra]
        out_refs = rest[n_extra:n_extra + n_out]
        acc_ref = rest[n_extra + n_out]
        k = pl.program_id(len(grid) - 1)
        part = lax.dot_general(a_ref[...].astype(BF16), b_ref[...].astype(BF16), dims, preferred_element_type=F32)

        @pl.when(k == 0)
        def _():
            acc_ref[...] = part

        @pl.when(k > 0)
        def _():
            acc_ref[...] += part

        @pl.when(k == n_k - 1)
        def _():
            res = epilogue(acc_ref[...], *[r[...] for r in extra_refs])
            if not isinstance(res, (tuple, list)):
                res = (res,)
            for o_ref, r in zip(out_refs, res):
                o_ref[...] = r.astype(o_ref.dtype)

    sem = ("parallel",) * (len(grid) - 1) + ("arbitrary",)
    res = pl.pallas_call(
        body,
        name=name,
        grid=grid,
        in_specs=[a_spec, b_spec] + [s for _, s in extras],
        out_specs=[s for _, s in outs],
        out_shape=[s for s, _ in outs],
        scratch_shapes=[pltpu.VMEM(acc_shape, F32)],
        compiler_params=_params(sem),
    )(a, b, *[x for x, _ in extras])
    return res


def _fit(dim, target, align=128):
    best = None
    for t in range(align, min(dim, target) + 1, align):
        if dim % t == 0:
            best = t
    return best or dim


def mm(a, b, *, name, ta=False, tb=False, extras=(), epilogue=None, out_dtypes=(F32,), tm=1024, tn=1024, tk=1024):
    m, kdim = (a.shape[1], a.shape[0]) if ta else a.shape
    n = b.shape[0] if tb else b.shape[1]
    tm = _fit(m, tm)
    tn = _fit(n, tn)
    tk = _fit(kdim, tk)
    grid = (m // tm, n // tn, kdim // tk)
    a_spec = pl.BlockSpec((tk, tm), lambda i, j, k: (k, i)) if ta else pl.BlockSpec((tm, tk), lambda i, j, k: (i, k))
    b_spec = pl.BlockSpec((tn, tk), lambda i, j, k: (j, k)) if tb else pl.BlockSpec((tk, tn), lambda i, j, k: (k, j))
    dims = (((0 if ta else 1,), (1 if tb else 0,)), ((), ()))
    o_spec = pl.BlockSpec((tm, tn), lambda i, j, k: (i, j))
    if epilogue is None:
        epilogue = lambda acc: acc
    res = _mm_call(name, grid, a, a_spec, b, b_spec, dims, [(x, o_spec) for x in extras],
                   [(jax.ShapeDtypeStruct((m, n), dt), o_spec) for dt in out_dtypes], epilogue, (tm, tn))
    return res[0] if len(res) == 1 else res


def rowwise(fn, rows, consts, outs, sums=(), *, name, tm=256):
    m = rows[0].shape[0]
    pieces = [x if isinstance(x, (list, tuple)) else [x] for x in rows]
    tm = _tile(math.gcd(*[p.shape[-2] for ps in pieces for p in ps]), (tm, 128, 64, 32, 16, 8))
    starts = [[sum(q.shape[-2] for q in ps[:k]) // tm for k in range(len(ps) + 1)] for ps in pieces]
    flat = [p for ps in pieces for p in ps]
    n_rows, n_consts, n_outs = len(flat), len(consts), len(outs)

    def body(*refs):
        i = pl.program_id(0)
        in_vals, at = [], 0
        for ps, st in zip(pieces, starts):
            val = refs[at + len(ps) - 1][...]
            for k in reversed(range(len(ps) - 1)):
                val = jnp.where(i < st[k + 1], refs[at + k][...], val)
            in_vals.append(val)
            at += len(ps)
        in_vals += [r[...] for r in refs[n_rows:n_rows + n_consts]]
        out_refs = refs[n_rows + n_consts:n_rows + n_consts + n_outs]
        sum_refs = refs[n_rows + n_consts + n_outs:]
        res = fn(*in_vals)
        if not isinstance(res, (tuple, list)):
            res = (res,)
        for o_ref, r in zip(out_refs, res[:n_outs]):
            o_ref[...] = r.astype(o_ref.dtype)
        if sum_refs:
            first = pl.program_id(0) == 0

            @pl.when(first)
            def _():
                for s_ref, r in zip(sum_refs, res[n_outs:]):
                    s_ref[...] = r.astype(F32)

            @pl.when(jnp.logical_not(first))
            def _():
                for s_ref, r in zip(sum_refs, res[n_outs:]):
                    s_ref[...] += r.astype(F32)

    def whole(shape):
        nd = len(shape)
        return pl.BlockSpec(shape, lambda i: (0,) * nd)

    def row_spec(x, lo, hi):
        at = lambda i: jnp.clip(i - lo, 0, hi - lo - 1)
        if x.ndim == 3:
            return pl.BlockSpec((x.shape[0], tm, x.shape[2]), lambda i: (0, at(i), 0))
        return pl.BlockSpec((tm, x.shape[1]), lambda i: (at(i), 0))

    in_specs = [row_spec(p, st[k], st[k + 1]) for ps, st in zip(pieces, starts) for k, p in enumerate(ps)]
    in_specs += [whole(c.shape) for c in consts]
    out_specs = [pl.BlockSpec((tm, nc), lambda i: (i, 0)) for nc, _ in outs] + [whole(tuple(s)) for s in sums]
    out_shape = [jax.ShapeDtypeStruct((m, nc), dt) for nc, dt in outs] + [jax.ShapeDtypeStruct(tuple(s), F32) for s in sums]
    res = pl.pallas_call(
        body,
        name=name,
        grid=(m // tm,),
        in_specs=in_specs,
        out_specs=out_specs,
        out_shape=out_shape,
        compiler_params=_params(("arbitrary",) if sums else ("parallel",)),
    )(*flat, *consts)
    return res[0] if len(res) == 1 else res


def _rms(h, g):
    return h * lax.rsqrt(jnp.mean(h * h, axis=-1, keepdims=True) + RMS_EPS) * g


def _sigmoid(x):
    return 1.0 / (1.0 + jnp.exp(-x))


def _silu_mul(g, u):
    return g * _sigmoid(g) * u


def _gelu(x):
    return 0.5 * x * (1.0 + jnp.tanh(math.sqrt(2.0 / math.pi) * (x + 0.044715 * (x * x * x))))


def _softplus(x):
    return jnp.maximum(x, 0.0) + jnp.log(1.0 + jnp.exp(-jnp.abs(x)))


def rms_fwd(h, g, name):
    return rowwise(lambda x, gg: _rms(x, gg), [h], [g.reshape(1, -1)], [(h.shape[1], BF16)], name=name)


def norm_input_bwd(a, a_spec, n_k, w, h, g, dres, name, tm=512):
    m, d = h.shape
    tk = w.shape[1] // n_k
    tm = _fit(m, tm)

    def body(a_ref, w_ref, h_ref, g_ref, dres_ref, dh_ref, dg_ref, acc_ref):
        i, k = pl.program_id(0), pl.program_id(1)
        part = lax.dot_general(a_ref[...].astype(BF16), w_ref[...], _NT, preferred_element_type=F32)

        @pl.when(k == 0)
        def _():
            acc_ref[...] = part

        @pl.when(k > 0)
        def _():
            acc_ref[...] += part

        @pl.when(k == n_k - 1)
        def _():
            _, vjp = jax.vjp(_rms, h_ref[...], g_ref[...])
            dx, dg = vjp(acc_ref[...])
            dh_ref[...] = dres_ref[...] + dx

            @pl.when(i == 0)
            def _():
                dg_ref[...] = dg

            @pl.when(i > 0)
            def _():
                dg_ref[...] += dg

    row = pl.BlockSpec((tm, d), lambda i, k: (i, 0))
    vec = pl.BlockSpec((1, d), lambda i, k: (0, 0))
    dh, dg = pl.pallas_call(
        body,
        name=name,
        grid=(m // tm, n_k),
        in_specs=[a_spec(tm, tk), pl.BlockSpec((d, tk), lambda i, k: (0, k)), row, vec, row],
        out_specs=[row, vec],
        out_shape=[jax.ShapeDtypeStruct((m, d), F32), jax.ShapeDtypeStruct((1, d), F32)],
        scratch_shapes=[pltpu.VMEM((tm, d), F32)],
        compiler_params=_params(("arbitrary", "arbitrary")),
    )(a, w, h, g.reshape(1, d), dres)
    return dh, dg.reshape(-1)


def _rows_spec(tm, tk):
    return pl.BlockSpec((tm, tk), lambda i, k: (i, k))


def ffn_in(h, g, w_in, tag):
    m, d = h.shape
    f = w_in.shape[1] // 2
    tm, tn = _fit(m, 512), _fit(f, 1408)
    nj = f // tn

    def body(h_ref, g_ref, wg_ref, wu_ref, hn_ref, gu_ref, act_ref, hn_scr):
        @pl.when(pl.program_id(1) == 0)
        def _():
            hn = _rms(h_ref[...], g_ref[...]).astype(BF16)
            hn_scr[...] = hn
            hn_ref[...] = hn

        a = hn_scr[...]
        gate = jnp.dot(a, wg_ref[...], preferred_element_type=F32)
        up = jnp.dot(a, wu_ref[...], preferred_element_type=F32)
        gu_ref[0] = gate.astype(BF16)
        gu_ref[1] = up.astype(BF16)
        act_ref[...] = _silu_mul(gate, up).astype(BF16)

    return pl.pallas_call(
        body,
        name=f"{tag}_in",
        grid=(m // tm, nj),
        in_specs=[pl.BlockSpec((tm, d), lambda i, j: (i, 0)), pl.BlockSpec((1, d), lambda i, j: (0, 0)),
                  pl.BlockSpec((d, tn), lambda i, j: (0, j)), pl.BlockSpec((d, tn), lambda i, j: (0, nj + j))],
        out_specs=[pl.BlockSpec((tm, d), lambda i, j: (i, 0)), pl.BlockSpec((2, tm, tn), lambda i, j: (0, i, j)),
                   pl.BlockSpec((tm, tn), lambda i, j: (i, j))],
        out_shape=[jax.ShapeDtypeStruct((m, d), BF16), jax.ShapeDtypeStruct((2, m, f), BF16), jax.ShapeDtypeStruct((m, f), BF16)],
        scratch_shapes=[pltpu.VMEM((tm, d), BF16)],
        compiler_params=_params(("parallel", "arbitrary")),
    )(h, g.reshape(1, d), w_in, w_in)


def ffn_fwd(h, g, w_in, w_out, tag):
    hn, gu, act = ffn_in(h, g, w_in, tag)
    h_new = mm(act, w_out, name=f"{tag}_out", extras=(h,), epilogue=lambda acc, res: res + 0.5 * acc, tm=512, tk=2816)
    return h_new, (h, hn, gu, act)


def ffn_bwd(dh_new, saved, g, w_in, w_out, tag):
    h, hn, gu, act = saved
    m, d = h.shape
    f = w_out.shape[0]

    def act_bwd(acc, gu_blk):
        _, vjp = jax.vjp(_silu_mul, gu_blk[0].astype(F32), gu_blk[1].astype(F32))
        return jnp.stack(vjp(0.5 * acc))

    tm, tn = _fit(m, 512), _fit(f, 1408)
    pair = pl.BlockSpec((2, tm, tn), lambda i, j, k: (0, i, j))
    dgu = _mm_call(f"{tag}_dgu", (m // tm, f // tn, 1), dh_new, pl.BlockSpec((tm, d), lambda i, j, k: (i, 0)), w_out,
                   pl.BlockSpec((tn, d), lambda i, j, k: (j, 0)), _NT, [(gu, pair)],
                   [(jax.ShapeDtypeStruct((2, m, f), BF16), pair)], act_bwd, (tm, tn))[0]
    dw_out = mm(act, dh_new, ta=True, name=f"{tag}_dwout", epilogue=lambda acc: 0.5 * acc, tm=1408)

    tn, tk = _fit(f, 1408), _fit(m, 1024)
    nh = f // tn
    dw_in = _mm_call(f"{tag}_dwin", (1, 2 * nh, m // tk), hn, pl.BlockSpec((tk, d), lambda i, j, k: (k, 0)), dgu,
                     pl.BlockSpec((None, tk, tn), lambda i, j, k: (j // nh, k, j % nh)), _TN, [],
                     [(jax.ShapeDtypeStruct((d, 2 * f), F32), pl.BlockSpec((d, tn), lambda i, j, k: (0, j)))],
                     lambda acc: acc, (d, tn))[0]

    nkh = f // _fit(f, 1408)
    dh, dg = norm_input_bwd(dgu, lambda tm, tk: pl.BlockSpec((None, tm, tk), lambda i, k: (k // nkh, i, k % nkh)), 2 * nkh,
                            w_in, h, g, dh_new, f"{tag}_dhn")
    return dh, dg, dw_in, dw_out


def loss_fwd_bwd(h, g, target):
    d = h.shape[1]

    def fn(x, t, gg):
        y, vjp = jax.vjp(_rms, x, gg)
        err = y - t
        dx, dg = vjp(err * (1.0 / d))
        part = 0.5 * jnp.sum(jnp.sum(err * err, axis=1, keepdims=True), axis=0, keepdims=True) * (1.0 / d)
        return dx, dg, jnp.broadcast_to(part, (1, 128))

    dh, dg, loss = rowwise(fn, [h, target], [g.reshape(1, -1)], [(d, F32)], [(1, d), (1, 128)], name="loss_head")
    return loss[0, 0], dh, dg.reshape(-1)


def _shift_down(v, d, fill):
    rows = lax.broadcasted_iota(jnp.int32, v.shape, 0)
    return jnp.where(rows < d, fill, pltpu.roll(v, d, 0))


def _shift_up(v, d, fill):
    t = v.shape[0]
    rows = lax.broadcasted_iota(jnp.int32, v.shape, 0)
    return jnp.where(rows >= t - d, fill, pltpu.roll(v, t - d, 0))


def _scan_fwd(a, x):
    d = 1
    while d < a.shape[0]:
        x = x + a * _shift_down(x, d, 0.0)
        a = a * _shift_down(a, d, 1.0)
        d *= 2
    return a, x


def _scan_bwd(b, x):
    d = 1
    while d < b.shape[0]:
        x = x + b * _shift_up(x, d, 0.0)
        b = b * _shift_up(b, d, 1.0)
        d *= 2
    return b, x


def _rows_before(cur, prev8, s):
    r = pltpu.roll(cur, s, 0)
    p = pltpu.roll(prev8, s, 0)
    rows = lax.broadcasted_iota(jnp.int32, prev8.shape, 0)
    return jnp.concatenate([jnp.where(rows < s, p, r[:8]), r[8:]], axis=0)


def _rows_after(cur, next8, s):
    t = cur.shape[0]
    r = pltpu.roll(cur, t - s, 0)
    p = pltpu.roll(next8, 8 - s, 0)
    rows = lax.broadcasted_iota(jnp.int32, next8.shape, 0)
    return jnp.concatenate([r[:t - 8], jnp.where(rows >= 8 - s, p, r[t - 8:])], axis=0)


LRU_CHUNK = 256


def _neg_expm1(y):
    small = -y * (1.0 + 0.5 * y * (1.0 + y * (1.0 / 3.0)))
    return jnp.where(y > -0.01, small, 1.0 - jnp.exp(y))


def _lru_gate(xc, pre_a, pre_x, lam):
    r = _sigmoid(pre_a)
    ig = _sigmoid(pre_x)
    log_a = (-LRU_C * r) * _softplus(-lam)
    return jnp.exp(log_a), (ig * xc) * jnp.sqrt(_neg_expm1(2.0 * log_a))


def _lru_conv(br, prev8, conv_w, conv_b):
    taps = [br] + [_rows_before(br, prev8, s) for s in range(1, LRU_CONV)]
    xc = conv_b
    for k in range(LRU_CONV):
        xc = xc + conv_w[k:k + 1, :] * taps[LRU_CONV - 1 - k]
    return xc, taps


def _lru_pre(xcb, w_ref, bias):
    nb = w_ref.shape[0]
    bw = w_ref.shape[1]
    return jnp.concatenate(
        [jnp.dot(xcb[:, n * bw:(n + 1) * bw], w_ref[n], preferred_element_type=F32) for n in range(nb)], axis=1) + bias


def lru_scan_fwd(bgr, conv_w, conv_b, wa, ba, wx, bx, lam):
    l, w2 = bgr.shape
    w = w2 // 2
    t = _tile(l, (LRU_CHUNK, 128, 64, 32, 16, 8))

    def body(bg_ref, br_ref, cw_ref, cb_ref, wa_ref, ba_ref, wx_ref, bx_ref, lam_ref, y_ref, h_ref, tail_ref, hprev_ref):
        @pl.when(pl.program_id(0) == 0)
        def _():
            tail_ref[...] = jnp.zeros_like(tail_ref)
            hprev_ref[...] = jnp.zeros_like(hprev_ref)

        br = br_ref[...]
        xc, _ = _lru_conv(br, tail_ref[...], cw_ref[...], cb_ref[...])
        xcb = xc.astype(BF16)
        a, gx = _lru_gate(xc, _lru_pre(xcb, wa_ref, ba_ref[...]), _lru_pre(xcb, wx_ref, bx_ref[...]), lam_ref[...])
        acum, x = _scan_fwd(a, gx)
        h = x + acum * hprev_ref[pl.ds(7, 1), :]
        y_ref[...] = (_gelu(bg_ref[...]) * h).astype(y_ref.dtype)
        h_ref[...] = h
        tail_ref[...] = br[t - 8:, :]
        hprev_ref[...] = h[t - 8:, :]

    def whole(x):
        nd = x.ndim
        return pl.BlockSpec(x.shape, lambda i: (0,) * nd)

    consts = [conv_w, conv_b, wa, ba, wx, bx, lam]
    return pl.pallas_call(
        body,
        name="lru_scan_fwd",
        grid=(l // t,),
        in_specs=[pl.BlockSpec((t, w), lambda i: (i, 0)), pl.BlockSpec((t, w), lambda i: (i, 1))] + [whole(c) for c in consts],
        out_specs=[pl.BlockSpec((t, w), lambda i: (i, 0)), pl.BlockSpec((t, w), lambda i: (i, 0))],
        out_shape=[jax.ShapeDtypeStruct((l, w), BF16), jax.ShapeDtypeStruct((l, w), F32)],
        scratch_shapes=[pltpu.VMEM((8, w), F32), pltpu.VMEM((8, w), F32)],
        compiler_params=_params(("arbitrary",)),
    )(bgr, bgr, *consts)


def lru_scan_bwd(dy, bgr, hseq, conv_w, conv_b, wa, ba, wx, bx, lam):
    l, w2 = bgr.shape
    w = w2 // 2
    t = _tile(l, (LRU_CHUNK, 128, 64, 32, 16, 8))
    nc = l // t
    nb, bw = wa.shape[0], wa.shape[1]

    def body(dy_ref, bg_ref, br_ref, brh_ref, h_ref, hh_ref, cw_ref, cb_ref, wa_ref, ba_ref, wx_ref, bx_ref, lam_ref,
             dbgr_ref, dcw_ref, dcb_ref, dwa_ref, dba_ref, dwx_ref, dbx_ref, dlam_ref, dxcn_ref, carry_ref):
        i = pl.program_id(0)
        has_prev = (i < nc - 1).astype(F32)

        @pl.when(i == 0)
        def _():
            dxcn_ref[...] = jnp.zeros_like(dxcn_ref)
            carry_ref[...] = jnp.zeros_like(carry_ref)

        br = br_ref[...]
        cw = cw_ref[...]
        xc, taps = _lru_conv(br, brh_ref[...] * has_prev, cw, cb_ref[...])
        xcb = xc.astype(BF16)
        (a, _), gate_vjp = jax.vjp(_lru_gate, xc, _lru_pre(xcb, wa_ref, ba_ref[...]), _lru_pre(xcb, wx_ref, bx_ref[...]),
                                   lam_ref[...])
        hs = h_ref[...]
        _, out_vjp = jax.vjp(lambda g_, h_: _gelu(g_) * h_, bg_ref[...], hs)
        dbg, dhs = out_vjp(dy_ref[...])
        bcum, x = _scan_bwd(_shift_up(a, 1, 1.0), dhs)
        dh = x + bcum * carry_ref[pl.ds(0, 1), :]
        da = dh * _rows_before(hs, hh_ref[...] * has_prev, 1)
        dxc, dpa, dpx, dlam = gate_vjp((da, dh))
        dpab, dpxb = dpa.astype(BF16), dpx.astype(BF16)
        nt = (((1,), (1,)), ((), ()))
        tn = (((0,), (0,)), ((), ()))
        dxb, dwa, dwx = [], [], []
        for n in range(nb):
            sl = slice(n * bw, (n + 1) * bw)
            dxb.append(lax.dot_general(dpab[:, sl], wa_ref[n], nt, preferred_element_type=F32)
                       + lax.dot_general(dpxb[:, sl], wx_ref[n], nt, preferred_element_type=F32))
            dwa.append(lax.dot_general(xcb[:, sl], dpab[:, sl], tn, preferred_element_type=F32))
            dwx.append(lax.dot_general(xcb[:, sl], dpxb[:, sl], tn, preferred_element_type=F32))
        dxc = dxc + jnp.concatenate(dxb, axis=1)
        ups = [dxc] + [_rows_after(dxc, dxcn_ref[...], s) for s in range(1, LRU_CONV)]
        dbr = cw[LRU_CONV - 1:LRU_CONV, :] * ups[0]
        for k in range(LRU_CONV - 1):
            dbr = dbr + cw[k:k + 1, :] * ups[LRU_CONV - 1 - k]
        dbgr_ref[:, :w] = dbg.astype(dbgr_ref.dtype)
        dbgr_ref[:, w:] = dbr.astype(dbgr_ref.dtype)
        dcw = jnp.concatenate([jnp.sum(dxc * taps[LRU_CONV - 1 - k], axis=0, keepdims=True) for k in range(LRU_CONV)], axis=0)
        sums = [(dcw_ref, dcw), (dcb_ref, jnp.sum(dxc, axis=0, keepdims=True)), (dwa_ref, jnp.stack(dwa)),
                (dba_ref, jnp.sum(dpa, axis=0, keepdims=True)), (dwx_ref, jnp.stack(dwx)),
                (dbx_ref, jnp.sum(dpx, axis=0, keepdims=True)), (dlam_ref, dlam)]

        @pl.when(i == 0)
        def _():
            for ref, val in sums:
                ref[...] = val

        @pl.when(i > 0)
        def _():
            for ref, val in sums:
                ref[...] += val

        dxcn_ref[...] = dxc[:8, :]
        carry_ref[...] = (a * dh)[:8, :]

    def whole(shape):
        nd = len(shape)
        return pl.BlockSpec(tuple(shape), lambda i: (0,) * nd)

    consts = [conv_w, conv_b, wa, ba, wx, bx, lam]
    t8 = t // 8
    rev = lambda i: nc - 1 - i
    halo = lambda i: jnp.maximum(rev(i) * t8 - 1, 0)
    in_specs = [
        pl.BlockSpec((t, w), lambda i: (rev(i), 0)),
        pl.BlockSpec((t, w), lambda i: (rev(i), 0)),
        pl.BlockSpec((t, w), lambda i: (rev(i), 1)),
        pl.BlockSpec((8, w), lambda i: (halo(i), 1)),
        pl.BlockSpec((t, w), lambda i: (rev(i), 0)),
        pl.BlockSpec((8, w), lambda i: (halo(i), 0)),
    ] + [whole(c.shape) for c in consts]
    sum_shapes = [conv_w.shape, conv_b.shape, wa.shape, ba.shape, wx.shape, bx.shape, lam.shape]
    res = pl.pallas_call(
        body,
        name="lru_scan_bwd",
        grid=(nc,),
        in_specs=in_specs,
        out_specs=[pl.BlockSpec((t, w2), lambda i: (rev(i), 0))] + [whole(s) for s in sum_shapes],
        out_shape=[jax.ShapeDtypeStruct((l, w2), BF16)] + [jax.ShapeDtypeStruct(tuple(s), F32) for s in sum_shapes],
        scratch_shapes=[pltpu.VMEM((8, w), F32), pltpu.VMEM((8, w), F32)],
        compiler_params=_params(("arbitrary",)),
    )(dy, bgr, bgr, bgr, hseq, hseq, *consts)
    return res[0], res[1:]


def lru_fwd(hn, p):
    bgr = mm(hn, p["w_in"], name="lru_in")
    y, hseq = lru_scan_fwd(bgr, p["conv_w"], p["conv_b"], p["wa"], p["ba"], p["wx"], p["bx"], p["lam"])
    return y, (hn, bgr, hseq, y)


def lru_bwd(dmixed, saved, p):
    hn, bgr, hseq, y = saved
    dy = mm(dmixed, p["w_out"], tb=True, name="lru_dy")
    dw_out = mm(y, dmixed, ta=True, name="lru_dwout")
    dbgr, (dcw, dcb, dwa, dba, dwx, dbx, dlam) = lru_scan_bwd(dy, bgr, hseq, p["conv_w"], p["conv_b"], p["wa"], p["ba"],
                                                               p["wx"], p["bx"], p["lam"])
    dw_in = mm(hn, dbgr, ta=True, name="lru_dwin")
    grads = dict(w_in=dw_in, conv_w=dcw, conv_b=dcb, wa=dwa, ba=dba, wx=dwx, bx=dbx, lam=dlam, w_out=dw_out)
    return (dbgr, p["w_in"]), grads


SB_BLOCK = 256
SB_BLOCK_Q = 1024
_NT = (((1,), (1,)), ((), ()))
_TN = (((0,), (0,)), ((), ()))


def _running_sums(x, tri, total_col):
    xb = x.astype(BF16)
    run = jnp.dot(xb, tri, preferred_element_type=F32)
    return run, xb, run[:, total_col:total_col + 1]


def _tri(n, cmp):
    r = lax.broadcasted_iota(jnp.int32, (n, n), 0)
    c = lax.broadcasted_iota(jnp.int32, (n, n), 1)
    return cmp(r, c).astype(BF16)


SB_PAIR = 2 * SB_HEAD_DIM


def _pair_masks(x2, scale=None):
    lane = lax.broadcasted_iota(jnp.int32, x2.shape, 1)
    zero = jnp.zeros_like(x2)
    a, b = jnp.where(lane < SB_HEAD_DIM, x2, zero), jnp.where(lane >= SB_HEAD_DIM, x2, zero)
    if scale is not None:
        a, b = a * scale, b * scale
    return a, b


def _causal(shape, q0, k0):
    return lax.broadcasted_iota(jnp.int32, shape, 1) + k0 < lax.broadcasted_iota(jnp.int32, shape, 0) + q0


def _sb_blocks(l):
    bk = _tile(l, (SB_BLOCK, 128))
    bq = _tile(l, (SB_BLOCK_Q, 2 * SB_BLOCK, SB_BLOCK, 128))
    return bq, bk


def sb_pair_fwd(qkv, job=None):
    l, d3 = qkv.shape
    d = d3 // 3
    npair = d // SB_PAIR
    bq, bk = _sb_blocks(l)
    ratio = bq // bk
    scale = SB_HEAD_DIM ** -0.5
    t_suf = _tri(bk, lambda r, c: r > c)

    def body(q_ref, k_ref, v_ref, tsuf_ref, o_ref, ltot_ref):
        i = pl.program_id(1)
        qs = _pair_masks(q_ref[...], scale)
        tsuf = tsuf_ref[...]

        def tile(j, carry, masked):
            rows = pl.ds(pl.multiple_of(j * bk, bk), bk)
            k2 = k_ref[rows, :]
            v2 = v_ref[rows, :]
            out = []
            for q1, (c_r, acc) in zip(qs, carry):
                z = lax.dot_general(q1, k2, _NT, preferred_element_type=F32)
                lk = -_softplus(z)
                if masked:
                    causal = _causal(z.shape, i * bq, j * bk)
                    lk = jnp.where(causal, lk, 0.0)
                later, lkb, later0 = _running_sums(lk, tsuf, 0)
                w = jnp.exp(z + lk + c_r + later)
                if masked:
                    w = jnp.where(causal, w, 0.0)
                out.append((c_r + later0 + lkb[:, 0:1].astype(F32), acc + jnp.dot(w.astype(BF16), v2, preferred_element_type=F32)))
            return tuple(out)

        carry = ((jnp.zeros((bq, 1), F32), jnp.zeros((bq, SB_PAIR), F32)),) * 2
        for dgl in reversed(range(ratio)):
            carry = tile(i * ratio + dgl, carry, True)
        (c_a, acc_a), (c_b, acc_b) = lax.fori_loop(0, i * ratio, lambda jj, c: tile(i * ratio - 1 - jj, c, False), carry)
        lane = lax.broadcasted_iota(jnp.int32, acc_a.shape, 1)
        o_ref[...] = jnp.where(lane < SB_HEAD_DIM, acc_a, acc_b).astype(o_ref.dtype)
        ltot_ref[...] = jnp.where(lax.broadcasted_iota(jnp.int32, (bq, 2), 1) == 0, c_a, c_b)

    grid = (npair, l // bq)
    body, x_in, x_out, x_shapes, x_scratch, x_args = carry(job, body, grid, 4, 2)
    res = pl.pallas_call(
        body,
        name="sb_attn_fwd",
        grid=grid,
        in_specs=[
            pl.BlockSpec((bq, SB_PAIR), lambda p, i: (i, p)),
            pl.BlockSpec((l, SB_PAIR), lambda p, i: (0, npair + p)),
            pl.BlockSpec((l, SB_PAIR), lambda p, i: (0, 2 * npair + p)),
            pl.BlockSpec((bk, bk), lambda p, i: (0, 0)),
        ] + x_in,
        out_specs=[pl.BlockSpec((bq, SB_PAIR), lambda p, i: (i, p)), pl.BlockSpec((None, bq, 2), lambda p, i: (p, i, 0))] + x_out,
        out_shape=[jax.ShapeDtypeStruct((l, d), BF16), jax.ShapeDtypeStruct((npair, l, 2), F32)] + x_shapes,
        scratch_shapes=x_scratch,
        compiler_params=_params(("arbitrary", "arbitrary")),
    )(qkv, qkv, qkv, t_suf, *x_args)
    return res[0], res[1], res[2:]


def sb_pair_bwd(qkv, do, ltot, job=None):
    l, d3 = qkv.shape
    d = d3 // 3
    npair = d // SB_PAIR
    bq, bk = _sb_blocks(l)
    ratio = bq // bk
    scale = SB_HEAD_DIM ** -0.5
    t_inc = _tri(bk, lambda r, c: r <= c)
    t_exc = _tri(bk, lambda r, c: r < c)

    def body(q_ref, k_ref, v_ref, do_ref, ltot_ref, tinc_ref, texc_ref, dq_ref, dk_ref, dv_ref):
        i = pl.program_id(1)

        @pl.when(i == 0)
        def _():
            dk_ref[...] = jnp.zeros_like(dk_ref)
            dv_ref[...] = jnp.zeros_like(dv_ref)

        qs = _pair_masks(q_ref[...], scale)
        dos = _pair_masks(do_ref[...])
        lt = ltot_ref[...]
        ltots = (lt[:, 0:1], lt[:, 1:2])
        tinc = tinc_ref[...]
        texc = texc_ref[...]

        def tile(j, carry, masked):
            rows = pl.ds(pl.multiple_of(j * bk, bk), bk)
            k2 = k_ref[rows, :]
            v2 = v_ref[rows, :]
            out = []
            dk2 = dv2 = None
            for q1, do1, ltot1, (c_l, c_p, dq) in zip(qs, dos, ltots, carry):
                z = lax.dot_general(q1, k2, _NT, preferred_element_type=F32)
                lk = -_softplus(z)
                if masked:
                    causal = _causal(z.shape, i * bq, j * bk)
                    lk = jnp.where(causal, lk, 0.0)
                log_beta = z + lk
                upto, _, lk_tile = _running_sums(lk, tinc, bk - 1)
                w = jnp.exp(log_beta + (ltot1 - c_l) - upto)
                if masked:
                    w = jnp.where(causal, w, 0.0)
                g = w * lax.dot_general(do1, v2, _NT, preferred_element_type=F32)
                before, gb, before_last = _running_sums(g, texc, bk - 1)
                dz = g - jnp.exp(log_beta) * (g + c_p + before)
                if masked:
                    dz = jnp.where(causal, dz, 0.0)
                dzb = dz.astype(BF16)
                dk1 = lax.dot_general(dzb, q1, _TN, preferred_element_type=F32)
                dv1 = lax.dot_general(w.astype(BF16), do1, _TN, preferred_element_type=F32)
                dk2 = dk1 if dk2 is None else dk2 + dk1
                dv2 = dv1 if dv2 is None else dv2 + dv1
                out.append((c_l + lk_tile, c_p + before_last + gb[:, bk - 1:bk].astype(F32),
                            dq + jnp.dot(dzb, k2, preferred_element_type=F32)))
            dk_ref[rows, :] += dk2
            dv_ref[rows, :] += dv2
            return tuple(out)

        zero = jnp.zeros((bq, 1), F32)
        carry = ((zero, zero, jnp.zeros((bq, SB_PAIR), F32)),) * 2
        carry = lax.fori_loop(0, i * ratio, lambda j, c: tile(j, c, False), carry)
        for dgl in range(ratio):
            carry = tile(i * ratio + dgl, carry, True)
        (_, _, dq_a), (_, _, dq_b) = carry
        lane = lax.broadcasted_iota(jnp.int32, dq_a.shape, 1)
        dq_ref[...] = jnp.where(lane < SB_HEAD_DIM, dq_a, dq_b) * scale

    col = lambda s: pl.BlockSpec((l, SB_PAIR), lambda p, i: (0, s * npair + p))
    blk_spec = pl.BlockSpec((bq, SB_PAIR), lambda p, i: (i, p))
    tri_spec = pl.BlockSpec((bk, bk), lambda p, i: (0, 0))
    grid = (npair, l // bq)
    body, x_in, x_out, x_shapes, x_scratch, x_args = carry(job, body, grid, 7, 3)
    res = pl.pallas_call(
        body,
        name="sb_attn_bwd",
        grid=grid,
        in_specs=[blk_spec, col(1), col(2), blk_spec, pl.BlockSpec((None, bq, 2), lambda p, i: (p, i, 0)), tri_spec, tri_spec] + x_in,
        out_specs=[blk_spec, pl.BlockSpec((l, SB_PAIR), lambda p, i: (0, p)), pl.BlockSpec((l, SB_PAIR), lambda p, i: (0, p))] + x_out,
        out_shape=[jax.ShapeDtypeStruct((l, d), F32)] * 3 + x_shapes,
        scratch_shapes=x_scratch,
        compiler_params=_params(("arbitrary", "arbitrary")),
    )(qkv, qkv, qkv, do, ltot, t_inc, t_exc, *x_args)
    return res[:3], res[3:]


def sb_fwd(hn, p, job=None):
    qkv = mm(hn, p["w_qkv"], name="sb_qkv", out_dtypes=(BF16,), tm=2048, tn=512)
    o, ltot, carried = sb_pair_fwd(qkv, job)
    return o, (hn, qkv, ltot, o), carried


def sb_bwd(dmixed, saved, p, job=None):
    hn, qkv, ltot, o = saved
    do = mm(dmixed, p["w_out"], tb=True, name="sb_do", out_dtypes=(BF16,))
    dw_out = mm(o, dmixed, ta=True, name="sb_dwout")
    dq_dk_dv, carried = sb_pair_bwd(qkv, do, ltot, job)
    dqkv = jnp.concatenate([g.astype(BF16) for g in dq_dk_dv], axis=1)
    dw_qkv = mm(hn, dqkv, ta=True, name="sb_dwqkv")
    return (dqkv, p["w_qkv"]), dict(w_qkv=dw_qkv, w_out=dw_out), carried


S5_CHUNK = 128
S5_SLAB_GROUPS = 8


def _s5_discretise(lr, li, ldt, bre, bim):
    dt = jnp.exp(ldt)
    mag = jnp.exp(lr * dt)
    lbr = mag * jnp.cos(li * dt)
    lbi = mag * jnp.sin(li * dt)
    inv = 1.0 / (lr * lr + li * li)
    cr = ((lbr - 1.0) * lr + lbi * li) * inv
    ci = (lbi * lr - (lbr - 1.0) * li) * inv
    return lbr, lbi, cr * bre - ci * bim, cr * bim + ci * bre


def _s5_cols(lam_re, lam_im, log_dt, b_re, b_im):
    g, p = lam_re.shape
    col = lambda x: x.reshape(g * p, 1)
    ldt = jnp.broadcast_to(log_dt[:, None], (g, p))
    return col(lam_re), col(lam_im), col(ldt), b_re.reshape(g * p, -1), b_im.reshape(g * p, -1)


def _slab_b(bbar):
    sg = S5_SLAB_GROUPS
    gp, h = bbar.shape
    p = S5_STATE
    x = bbar.reshape(gp // (sg * p), sg, p, h)
    return jnp.einsum("kaph,ab->kahbp", x, jnp.eye(sg, dtype=x.dtype)).reshape(-1, sg * h, sg * p)


def _unslab_b(dslab):
    sg, p = S5_SLAB_GROUPS, S5_STATE
    nk, sh, _ = dslab.shape
    h = sh // sg
    x = dslab.reshape(nk, sg, h, sg, p)
    return jnp.einsum("kahbp,ab->kaph", x, jnp.eye(sg, dtype=x.dtype)).reshape(nk * sg * p, h)


def _slab_c(c):
    sg = S5_SLAB_GROUPS
    g, h, p = c.shape
    x = c.reshape(g // sg, sg, h, p)
    return jnp.einsum("kahp,ab->kapbh", x, jnp.eye(sg, dtype=x.dtype)).reshape(-1, sg * p, sg * h)


def _unslab_c(dslab):
    sg, p = S5_SLAB_GROUPS, S5_STATE
    nk, _, sh = dslab.shape
    h = sh // sg
    x = dslab.reshape(nk, sg, p, sg, h)
    return jnp.einsum("kapbh,ab->kahp", x, jnp.eye(sg, dtype=x.dtype)).reshape(nk * sg, h, p)


def _cmul_scan(lre, lim, xre, xim, cre, cim, reverse):
    t = xre.shape[0]
    shift = _shift_up if reverse else _shift_down
    pows = [(lre, lim)]
    for _ in range(7):
        pr, pi = pows[-1]
        pows.append((pr * lre - pi * lim, pr * lim + pi * lre))
    by_row = pows[::-1] if reverse else pows
    pw_re = jnp.concatenate([p[0] for p in by_row], axis=0)
    pw_im = jnp.concatenate([p[1] for p in by_row], axis=0)
    n_groups = t // 8
    out_re, out_im = [None] * n_groups, [None] * n_groups
    for gi in (reversed(range(n_groups)) if reverse else range(n_groups)):
        gre, gim = xre[gi * 8:(gi + 1) * 8], xim[gi * 8:(gi + 1) * 8]
        for d in (1, 2, 4):
            mr, mi = pows[d - 1]
            sre, sim = shift(gre, d, 0.0), shift(gim, d, 0.0)
            gre, gim = gre + mr * sre - mi * sim, gim + mr * sim + mi * sre
        gre, gim = gre + pw_re * cre - pw_im * cim, gim + pw_re * cim + pw_im * cre
        edge = slice(0, 1) if reverse else slice(7, 8)
        cre, cim = gre[edge], gim[edge]
        out_re[gi], out_im[gi] = gre, gim
    return jnp.concatenate(out_re, axis=0), jnp.concatenate(out_im, axis=0)


def s5_scan_fwd(u, lbr, lbi, bbd_re, bbd_im, cbd_re, cbd_imn, d_skip, job=None):
    l, w = u.shape
    n = lbr.shape[1]
    nk, cw, sw = bbd_re.shape
    t = _tile(l, (S5_CHUNK, 64, 32, 16, 8))

    def body(u_ref, lbr_ref, lbi_ref, bre_ref, bim_ref, cre_ref, cim_ref, d_ref, sre_ref, sim_ref, y_ref, z_ref, pre_ref, pim_ref):
        @pl.when(pl.program_id(0) == 0)
        def _():
            pre_ref[...] = jnp.zeros_like(pre_ref)
            pim_ref[...] = jnp.zeros_like(pim_ref)

        uu = u_ref[...]
        ub = uu.astype(BF16)
        lre, lim = lbr_ref[...], lbi_ref[...]
        xre = jnp.concatenate([jnp.dot(ub[:, k * cw:(k + 1) * cw], bre_ref[k], preferred_element_type=F32) for k in range(nk)], axis=1)
        xim = jnp.concatenate([jnp.dot(ub[:, k * cw:(k + 1) * cw], bim_ref[k], preferred_element_type=F32) for k in range(nk)], axis=1)
        sre, sim = _cmul_scan(lre, lim, xre, xim, pre_ref[pl.ds(7, 1), :], pim_ref[pl.ds(7, 1), :], False)
        sre_ref[...] = sre
        sim_ref[...] = sim
        pre_ref[...] = sre[t - 8:, :]
        pim_ref[...] = sim[t - 8:, :]
        sreb, simb = sre.astype(BF16), sim.astype(BF16)
        y = jnp.concatenate(
            [jnp.dot(sreb[:, k * sw:(k + 1) * sw], cre_ref[k], preferred_element_type=F32)
             + jnp.dot(simb[:, k * sw:(k + 1) * sw], cim_ref[k], preferred_element_type=F32) for k in range(nk)], axis=1)
        y = y + d_ref[...] * uu
        y_ref[...] = y
        z_ref[...] = _gelu(y).astype(z_ref.dtype)

    def whole(x):
        nd = x.ndim
        return pl.BlockSpec(x.shape, lambda i: (0,) * nd)

    consts = [lbr, lbi, bbd_re, bbd_im, cbd_re, cbd_imn, d_skip]
    row = lambda c: pl.BlockSpec((t, c), lambda i: (i, 0))
    body, x_in, x_out, x_shapes, x_scratch, x_args = carry(job, body, (l // t,), 1 + len(consts), 4)
    res = pl.pallas_call(
        body,
        name="s5_scan_fwd",
        grid=(l // t,),
        in_specs=[row(w)] + [whole(c) for c in consts] + x_in,
        out_specs=[row(n), row(n), row(w), row(w)] + x_out,
        out_shape=[jax.ShapeDtypeStruct((l, n), F32), jax.ShapeDtypeStruct((l, n), F32), jax.ShapeDtypeStruct((l, w), F32),
                   jax.ShapeDtypeStruct((l, w), BF16)] + x_shapes,
        scratch_shapes=[pltpu.VMEM((8, n), F32), pltpu.VMEM((8, n), F32)] + x_scratch,
        compiler_params=_params(("arbitrary",)),
    )(u, *consts, *x_args)
    return res[:4], res[4:]


def s5_scan_bwd(dz, y, u, sre, sim, lbr, lbi, bbd_re, bbd_im, cbd_re, cbd_imn, d_skip, job=None):
    l, w = u.shape
    n = lbr.shape[1]
    nk, cw, sw = bbd_re.shape
    t = _tile(l, (S5_CHUNK, 64, 32, 16, 8))
    nc = l // t

    def body(dz_ref, y_ref, u_ref, sre_ref, sim_ref, hre_ref, him_ref, lbr_ref, lbi_ref, bre_ref, bim_ref, cre_ref, cim_ref,
             d_ref, du_ref, dlr_ref, dli_ref, dbre_ref, dbim_ref, dcre_ref, dcim_ref, dd_ref, nre_ref, nim_ref):
        i = pl.program_id(0)
        has_prev = (i < nc - 1).astype(F32)

        @pl.when(i == 0)
        def _():
            nre_ref[...] = jnp.zeros_like(nre_ref)
            nim_ref[...] = jnp.zeros_like(nim_ref)

        uu = u_ref[...]
        ub = uu.astype(BF16)
        lre, lim = lbr_ref[...], lbi_ref[...]
        _, gelu_vjp = jax.vjp(_gelu, y_ref[...])
        dy = gelu_vjp(dz_ref[...].astype(F32))[0]
        dyb = dy.astype(BF16)
        gre = jnp.concatenate([lax.dot_general(dyb[:, k * cw:(k + 1) * cw], cre_ref[k], _NT, preferred_element_type=F32)
                               for k in range(nk)], axis=1)
        gim = jnp.concatenate([lax.dot_general(dyb[:, k * cw:(k + 1) * cw], cim_ref[k], _NT, preferred_element_type=F32)
                               for k in range(nk)], axis=1)
        dsre, dsim = _cmul_scan(lre, -lim, gre, gim, nre_ref[pl.ds(0, 1), :], nim_ref[pl.ds(0, 1), :], True)
        nre_ref[...] = dsre[:8, :]
        nim_ref[...] = dsim[:8, :]
        dsreb, dsimb = dsre.astype(BF16), dsim.astype(BF16)
        s_re, s_im = sre_ref[...], sim_ref[...]
        du = jnp.concatenate(
            [lax.dot_general(dsreb[:, k * sw:(k + 1) * sw], bre_ref[k], _NT, preferred_element_type=F32)
             + lax.dot_general(dsimb[:, k * sw:(k + 1) * sw], bim_ref[k], _NT, preferred_element_type=F32) for k in range(nk)],
            axis=1)
        du_ref[...] = (du + d_ref[...] * dy).astype(du_ref.dtype)
        pre = _rows_before(s_re, hre_ref[...] * has_prev, 1)
        pim = _rows_before(s_im, him_ref[...] * has_prev, 1)
        sreb, simb = s_re.astype(BF16), s_im.astype(BF16)
        sums = [
            (dlr_ref, jnp.sum(dsre * pre + dsim * pim, axis=0, keepdims=True)),
            (dli_ref, jnp.sum(dsim * pre - dsre * pim, axis=0, keepdims=True)),
            (dbre_ref, jnp.stack([lax.dot_general(ub[:, k * cw:(k + 1) * cw], dsreb[:, k * sw:(k + 1) * sw], _TN,
                                                  preferred_element_type=F32) for k in range(nk)])),
            (dbim_ref, jnp.stack([lax.dot_general(ub[:, k * cw:(k + 1) * cw], dsimb[:, k * sw:(k + 1) * sw], _TN,
                                                  preferred_element_type=F32) for k in range(nk)])),
            (dcre_ref, jnp.stack([lax.dot_general(sreb[:, k * sw:(k + 1) * sw], dyb[:, k * cw:(k + 1) * cw], _TN,
                                                  preferred_element_type=F32) for k in range(nk)])),
            (dcim_ref, jnp.stack([lax.dot_general(simb[:, k * sw:(k + 1) * sw], dyb[:, k * cw:(k + 1) * cw], _TN,
                                                  preferred_element_type=F32) for k in range(nk)])),
            (dd_ref, jnp.sum(dy * uu, axis=0, keepdims=True)),
        ]

        @pl.when(i == 0)
        def _():
            for ref, val in sums:
                ref[...] = val

        @pl.when(i > 0)
        def _():
            for ref, val in sums:
                ref[...] += val

    def whole(shape):
        nd = len(shape)
        return pl.BlockSpec(tuple(shape), lambda i: (0,) * nd)

    consts = [lbr, lbi, bbd_re, bbd_im, cbd_re, cbd_imn, d_skip]
    t8 = t // 8
    rev = lambda i: nc - 1 - i
    halo = lambda i: jnp.maximum(rev(i) * t8 - 1, 0)
    row = lambda c: pl.BlockSpec((t, c), lambda i: (rev(i), 0))
    sum_shapes = [lbr.shape, lbi.shape, bbd_re.shape, bbd_im.shape, cbd_re.shape, cbd_imn.shape, d_skip.shape]
    body, x_in, x_out, x_shapes, x_scratch, x_args = carry(job, body, (nc,), 7 + len(consts), 8)
    res = pl.pallas_call(
        body,
        name="s5_scan_bwd",
        grid=(nc,),
        in_specs=[row(w), row(w), row(w), row(n), row(n), pl.BlockSpec((8, n), lambda i: (halo(i), 0)),
                  pl.BlockSpec((8, n), lambda i: (halo(i), 0))] + [whole(c.shape) for c in consts] + x_in,
        out_specs=[row(w)] + [whole(s) for s in sum_shapes] + x_out,
        out_shape=[jax.ShapeDtypeStruct((l, w), BF16)] + [jax.ShapeDtypeStruct(tuple(s), F32) for s in sum_shapes] + x_shapes,
        scratch_shapes=[pltpu.VMEM((8, n), F32), pltpu.VMEM((8, n), F32)] + x_scratch,
        compiler_params=_params(("arbitrary",)),
    )(dz, y, u, sre, sim, sre, sim, *consts, *x_args)
    return res[0], res[1:8], res[8:]


def _glu(vg):
    w = vg.shape[1] // 2
    return vg[:, :w] * _sigmoid(vg[:, w:])


def s5_fwd(hn, h, p, job=None):
    cols = _s5_cols(p["lam_re"], p["lam_im"], p["log_dt"], p["b_re"], p["b_im"])
    gp, hh = cols[3].shape
    lbr, lbi, bbr, bbi = rowwise(_s5_discretise, list(cols), [], [(1, F32), (1, F32), (hh, F32), (hh, F32)],
                                 name="s5_discretise", tm=512)
    consts = (lbr.reshape(1, gp), lbi.reshape(1, gp), _slab_b(bbr).astype(BF16), _slab_b(bbi).astype(BF16),
              _slab_c(p["c_re"]).astype(BF16), _slab_c(-p["c_im"]).astype(BF16), p["d"])
    u = mm(hn, p["w_in"], name="s5_in")
    (sre, sim, y, z), carried = s5_scan_fwd(u, *consts, job=job)
    vg = mm(z, p["w_out"], name="s5_out", out_dtypes=(BF16,))
    h_new = rowwise(lambda a, r: r + _glu(a.astype(F32)), [vg, h], [], [(h.shape[1], F32)], name="s5_glu")
    return h_new, (hn, u, sre, sim, y, z, vg, cols, consts), carried


def s5_bwd(dh_new, saved, p, job=None):
    hn, u, sre, sim, y, z, vg, cols, consts = saved

    def glu_bwd(a, dm):
        _, vjp = jax.vjp(_glu, a.astype(F32))
        return vjp(dm)[0]

    dvg = rowwise(glu_bwd, [vg, dh_new], [], [(vg.shape[1], BF16)], name="s5_dglu")
    dw_out = mm(z, dvg, ta=True, name="s5_dwout")
    dz = mm(dvg, p["w_out"], tb=True, name="s5_dz", out_dtypes=(BF16,), tk=2048)
    du, (dlbr, dlbi, dbbr, dbbi, dcre, dcimn, dd), carried = s5_scan_bwd(dz, y, u, sre, sim, *consts, job=job)
    dw_in = mm(hn, du, ta=True, name="s5_dwin")
    gp = cols[0].shape[0]

    def disc_bwd(lr, li, ldt, bre, bim, g0, g1, g2, g3):
        _, vjp = jax.vjp(_s5_discretise, lr, li, ldt, bre, bim)
        return vjp((g0, g1, g2, g3))

    cot = (dlbr.reshape(gp, 1), dlbi.reshape(gp, 1), _unslab_b(dbbr), _unslab_b(dbbi))
    dlr, dli, dldt, dbre, dbim = rowwise(disc_bwd, list(cols + cot), [], [(c.shape[1], F32) for c in cols],
                                         name="s5_discretise_bwd", tm=512)
    g_, p_ = p["lam_re"].shape
    grads = dict(w_in=dw_in, lam_re=dlr.reshape(g_, p_), lam_im=dli.reshape(g_, p_), log_dt=dldt.reshape(g_, p_).sum(axis=1),
                 b_re=dbre.reshape(p["b_re"].shape), b_im=dbim.reshape(p["b_im"].shape), c_re=_unslab_c(dcre),
                 c_im=-_unslab_c(dcimn), d=dd, w_out=dw_out)
    return (du, p["w_in"]), grads, carried


MESH = pl.DeviceIdType.MESH
N_CHIPS = 4
N_DEVICES = 8


def _place():
    x, y, c = lax.axis_index("x"), lax.axis_index("y"), lax.axis_index("c")
    return x, y, c, [(1 - x, y), (x, 1 - y), (1 - x, 1 - y)]


def _hbm_call(body, name, ins, out_shapes, n_remote, n_local=0):
    hbm = pl.BlockSpec(memory_space=pltpu.HBM)
    scratch = [pltpu.SemaphoreType.DMA((n_remote,)), pltpu.SemaphoreType.DMA((n_remote,))]
    if n_local:
        scratch.append(pltpu.SemaphoreType.DMA((n_local,)))
    return pl.pallas_call(
        body,
        name=name,
        in_specs=[hbm] * len(ins),
        out_specs=[hbm] * len(out_shapes),
        out_shape=out_shapes,
        scratch_shapes=scratch,
    )(*ins)


def _split_dim(shape):
    return next(d for d, s in enumerate(shape) if s >= 2 and s % 2 == 0)


class Exchange:
    def __init__(self, ins, out_shapes, n_remote, n_local, start, finish):
        self.ins, self.out_shapes, self.start, self.finish = list(ins), list(out_shapes), start, finish
        self.scratch = [pltpu.SemaphoreType.DMA((n_remote,)), pltpu.SemaphoreType.DMA((n_remote,)),
                        pltpu.SemaphoreType.DMA((max(n_local, 1),))]


def run_exchange(job, name):
    n_in, n_out = len(job.ins), len(job.out_shapes)

    def body(*refs):
        ins, outs, sems = refs[:n_in], refs[n_in:n_in + n_out], refs[n_in + n_out:]
        job.start(ins, outs, *sems)
        job.finish(ins, outs, *sems)

    hbm = pl.BlockSpec(memory_space=pltpu.HBM)
    return pl.pallas_call(body, name=name, in_specs=[hbm] * n_in, out_specs=[hbm] * n_out, out_shape=job.out_shapes,
                          scratch_shapes=job.scratch)(*job.ins)


def carry(job, body, grid, n_in, n_out):
    if job is None:
        return body, [], [], [], [], []
    nji, njo = len(job.ins), len(job.out_shapes)

    def carrying(*refs):
        ins, jins = refs[:n_in], refs[n_in:n_in + nji]
        outs, jouts = refs[n_in + nji:n_in + nji + n_out], refs[n_in + nji + n_out:n_in + nji + n_out + njo]
        rest = refs[n_in + nji + n_out + njo:]
        own, sems = rest[:len(rest) - 3], rest[len(rest) - 3:]
        ids = [pl.program_id(ax) for ax in range(len(grid))]
        first = functools.reduce(jnp.logical_and, [i == 0 for i in ids])
        last = functools.reduce(jnp.logical_and, [i == g - 1 for i, g in zip(ids, grid)])

        @pl.when(first)
        def _():
            job.start(jins, jouts, *sems)

        body(*ins, *outs, *own)

        @pl.when(last)
        def _():
            job.finish(jins, jouts, *sems)

    hbm = pl.BlockSpec(memory_space=pltpu.HBM)
    return carrying, [hbm] * nji, [hbm] * njo, job.out_shapes, job.scratch, job.ins


def gather_chips(shards):
    n = len(shards)
    cuts = [_split_dim(s.shape) for s in shards]

    def parts(ins, outs, send, recv):
        x, y, c, chips = _place()

        def half(ref, t, which, lead=()):
            size = shards[t].shape[cuts[t]] // 2
            return ref.at[lead + (slice(None),) * cuts[t] + (pl.ds(which * size, size),)]

        def copy(t, k, block, which, to, src=None):
            dst = half(outs[t], t, which, (block,))
            return pltpu.make_async_remote_copy(dst if src is None else src, dst, send.at[6 * t + k], recv.at[6 * t + k],
                                                device_id=to, device_id_type=MESH)

        me = 2 * x + y
        sends = [copy(t, k, me, c, (px, py, c), src=half(ins[t], t, c)) for t in range(n) for k, (px, py) in enumerate(chips)]
        return x, y, c, chips, me, copy, sends

    def start(ins, outs, send, recv, local):
        _, _, _, _, me, _, sends = parts(ins, outs, send, recv)
        for t in range(n):
            pltpu.make_async_copy(ins[t], outs[t].at[me], local.at[t]).start()
        for cp in sends:
            cp.start()

    def finish(ins, outs, send, recv, local):
        x, y, c, chips, me, copy, sends = parts(ins, outs, send, recv)
        passed = []
        for t in range(n):
            for k, (px, py) in enumerate(chips):
                copy(t, k, 2 * px + py, c, (px, py, c)).wait_recv()
                on = copy(t, 3 + k, 2 * px + py, c, (x, y, 1 - c))
                on.start()
                passed.append(on)
        for t in range(n):
            for k, (px, py) in enumerate(chips):
                copy(t, 3 + k, 2 * px + py, 1 - c, (x, y, 1 - c)).wait_recv()
        for cp in sends + passed:
            cp.wait_send()
        for t in range(n):
            pltpu.make_async_copy(ins[t], outs[t].at[me], local.at[t]).wait()

    return Exchange(shards, [jax.ShapeDtypeStruct((N_CHIPS,) + s.shape, s.dtype) for s in shards], 6 * n, n, start, finish)


def scatter_chips(blocked):
    n = len(blocked)

    def copies(ins, outs, send, recv):
        x, y, c, chips = _place()
        return [pltpu.make_async_remote_copy(ins[t].at[2 * px + py], outs[t].at[k], send.at[3 * t + k], recv.at[3 * t + k],
                                             device_id=(px, py, c), device_id_type=MESH)
                for t in range(n) for k, (px, py) in enumerate(chips)]

    def start(ins, outs, send, recv, local):
        for cp in copies(ins, outs, send, recv):
            cp.start()

    def finish(ins, outs, send, recv, local):
        for cp in copies(ins, outs, send, recv):
            cp.wait()

    return Exchange(blocked, [jax.ShapeDtypeStruct((3,) + b.shape[1:], b.dtype) for b in blocked], 3 * n, 0, start, finish)


def swap_cores(arrays):
    n = len(arrays)

    def body(*refs):
        ins, outs = refs[:n], refs[n:2 * n]
        send, recv = refs[2 * n:]
        x, y, c, _ = _place()
        pending = []
        for t in range(n):
            cp = pltpu.make_async_remote_copy(ins[t], outs[t], send.at[t], recv.at[t], device_id=(x, y, 1 - c),
                                              device_id_type=MESH)
            cp.start()
            pending.append(cp)
        for cp in pending:
            cp.wait()

    return _hbm_call(body, "swap_cores", arrays, [jax.ShapeDtypeStruct(a.shape, a.dtype) for a in arrays], n)


def gather_devices(buf):
    def body(in_ref, out_ref, send, recv, local):
        x, y, c, _ = _place()
        own = pltpu.make_async_copy(in_ref, out_ref.at[4 * x + 2 * y + c], local.at[0])
        own.start()
        pending = [own]
        for k in range(1, N_DEVICES):
            px = x ^ ((k >> 2) & 1)
            py = y ^ ((k >> 1) & 1)
            pc = c ^ (k & 1)
            going = pltpu.make_async_remote_copy(in_ref, out_ref.at[4 * x + 2 * y + c], send.at[k - 1], recv.at[k - 1],
                                                 device_id=(px, py, pc), device_id_type=MESH)
            going.start()
            pending.append(pltpu.make_async_remote_copy(in_ref, out_ref.at[4 * px + 2 * py + pc], send.at[k - 1], recv.at[k - 1],
                                                        device_id=(px, py, pc), device_id_type=MESH))
        for cp in pending:
            cp.wait()

    return _hbm_call(body, "gather_devices", [buf], [jax.ShapeDtypeStruct((N_DEVICES,) + buf.shape, buf.dtype)],
                     N_DEVICES - 1, 1)[0]


def _adamw(w, g, m, v):
    m = ADAM_B1 * m + (1.0 - ADAM_B1) * g
    v = ADAM_B2 * v + (1.0 - ADAM_B2) * (g * g)
    m_hat = m / (1.0 - ADAM_B1 ** ADAM_STEP)
    v_hat = v / (1.0 - ADAM_B2 ** ADAM_STEP)
    return -ADAM_LR * (m_hat / (jnp.sqrt(v_hat) + ADAM_EPS) + ADAM_WD * w), m, v


def _rows2d(a):
    return a.reshape(-1, a.shape[-1])


WEIGHTS = ["ffn1_norm", "ffn1_w_in", "ffn1_w_out", "mix_norm", "ffn2_norm", "ffn2_w_in", "ffn2_w_out", "final_norm",
           "s5_w_in", "s5_lam_re", "s5_lam_im", "s5_log_dt", "s5_b_re", "s5_b_im", "s5_c_re", "s5_c_im", "s5_d", "s5_w_out",
           "sb_w_qkv", "sb_w_out", "lru_w_in", "lru_conv_w", "lru_conv_b", "lru_w_a", "lru_b_a", "lru_w_x", "lru_b_x",
           "lru_lambda", "lru_w_out"]
INPUTS = ["x"] + WEIGHTS + ["loss_target"] + ["m_" + n for n in WEIGHTS] + ["v_" + n for n in WEIGHTS]
SHARDED_BIG = dict(ffn1_w_in=2, ffn1_w_out=1, ffn2_w_in=2, ffn2_w_out=1, s5_w_in=1, s5_w_out=2, sb_w_qkv=2, sb_w_out=1,
                   lru_w_in=2, lru_w_a=2, lru_w_x=2, lru_w_out=1)
SHARDED_SMALL = dict(s5_d=1, lru_conv_w=2, lru_conv_b=1, lru_b_a=2, lru_b_x=2, lru_lambda=1)
REPLICATED = [n for n in WEIGHTS if n not in SHARDED_BIG and n not in SHARDED_SMALL]
PACK_LANES = 128
PACK_ROW_ALIGN = 16


def _unblock(g, d):
    full = jnp.moveaxis(g, 0, d)
    return full.reshape(full.shape[:d] + (full.shape[d] * full.shape[d + 1],) + full.shape[d + 2:])


def _block(full, d):
    s = full.shape[d] // N_CHIPS
    return jnp.moveaxis(full.reshape(full.shape[:d] + (N_CHIPS, s) + full.shape[d + 1:]), d, 0)


def _pack(arrays, lead=()):
    nl = len(lead)
    flat = jnp.concatenate([a.reshape(lead + (-1,)) for a in arrays], axis=nl)
    quantum = PACK_LANES * PACK_ROW_ALIGN
    pad = (-flat.shape[nl]) % quantum
    flat = jnp.pad(flat, [(0, 0)] * nl + [(0, pad)])
    return flat.reshape(lead + (-1, PACK_LANES))


def _unpack(packed, shapes, lead=()):
    nl = len(lead)
    flat = packed.reshape(lead + (-1,))
    out, off = [], 0
    for s in shapes:
        size = math.prod(s)
        out.append(lax.slice_in_dim(flat, off, off + size, axis=nl).reshape(lead + tuple(s)))
        off += size
    return out


N_GROUPS = 3
MIXER_KIND = dict(s5=0, sb=1, lru=2)
S5_NAMES = dict(w_in="s5_w_in", lam_re="s5_lam_re", lam_im="s5_lam_im", log_dt="s5_log_dt", b_re="s5_b_re", b_im="s5_b_im",
                c_re="s5_c_re", c_im="s5_c_im", d="s5_d", w_out="s5_w_out")
SB_NAMES = dict(w_qkv="sb_w_qkv", w_out="sb_w_out")
LRU_NAMES = dict(w_in="lru_w_in", conv_w="lru_conv_w", conv_b="lru_conv_b", wa="lru_w_a", ba="lru_b_a", wx="lru_w_x",
                 bx="lru_b_x", lam="lru_lambda", w_out="lru_w_out")


def _pieces_of(name, count):
    kind = None if name.startswith("ffn") else MIXER_KIND[name.split("_")[0]]
    groups = [min(i if kind is None else kind + N_MIXERS * i, N_GROUPS - 1) for i in range(count)]
    runs, lo = [], 0
    for i in range(1, count + 1):
        if i == count or groups[i] != groups[lo]:
            runs.append((lo, i, groups[lo]))
            lo = i
    return runs


class _Sharded:
    def __init__(self, a):
        self.a = a
        self.pieces = {n: _pieces_of(n, a[n].shape[0]) for n in SHARDED_BIG}
        self.by_group = [[(n, lo, hi) for n in SHARDED_BIG for lo, hi, g in self.pieces[n] if g == grp]
                         for grp in range(N_GROUPS)]
        self.weights = {}
        self.small = {}
        self.grads = {n: [None] * a[n].shape[0] for n in SHARDED_BIG}
        self.small_grads = {}
        self.received = {}
        self.small_received = None

    def gather_job(self, grp):
        arrays = [self.a[n][lo:hi].astype(BF16) for n, lo, hi in self.by_group[grp]]
        if grp == 0:
            arrays.append(_pack([self.a[n] for n in SHARDED_SMALL]))
        return gather_chips(arrays)

    def landed(self, grp, gathered):
        for (n, lo, _), g in zip(self.by_group[grp], gathered):
            self.weights[(n, lo)] = _unblock(g, SHARDED_BIG[n])
        if grp == 0:
            blocks = _unpack(gathered[-1], [self.a[n].shape for n in SHARDED_SMALL], lead=(N_CHIPS,))
            self.small = {n: _unblock(b, d) for (n, d), b in zip(SHARDED_SMALL.items(), blocks)}

    def weight(self, name, idx):
        lo = next(lo for lo, hi, _ in self.pieces[name] if lo <= idx < hi)
        return self.weights[(name, lo)][idx - lo]

    def piece_grad(self, n, lo, hi):
        return jnp.stack(self.grads[n][lo:hi])

    def scatter_job(self, grp):
        arrays = [_block(self.piece_grad(n, lo, hi), SHARDED_BIG[n]).astype(BF16) for n, lo, hi in self.by_group[grp]]
        if grp == 0:
            arrays.append(_pack([_block(self.small_grads[n], d) for n, d in SHARDED_SMALL.items()], lead=(N_CHIPS,)))
        return scatter_chips(arrays)

    def arrived(self, grp, received):
        for (n, lo, _), r in zip(self.by_group[grp], received):
            self.received[(n, lo)] = r
        if grp == 0:
            self.small_received = received[-1]


def _forward_backward(x, target, a, sh):
    depth = a["ffn1_norm"].shape[0]

    def mixer_params(layer):
        kind, j = layer % N_MIXERS, layer // N_MIXERS
        if kind == 0:
            return kind, j, dict(w_in=sh.weight("s5_w_in", j), lam_re=a["s5_lam_re"][j], lam_im=a["s5_lam_im"][j],
                                 log_dt=a["s5_log_dt"][j], b_re=a["s5_b_re"][j], b_im=a["s5_b_im"][j], c_re=a["s5_c_re"][j],
                                 c_im=a["s5_c_im"][j], d=sh.small["s5_d"][j].reshape(1, -1), w_out=sh.weight("s5_w_out", j))
        if kind == 1:
            return kind, j, dict(w_qkv=sh.weight("sb_w_qkv", j), w_out=sh.weight("sb_w_out", j))
        sm = sh.small
        return kind, j, dict(w_in=sh.weight("lru_w_in", j), conv_w=sm["lru_conv_w"][j], conv_b=sm["lru_conv_b"][j].reshape(1, -1),
                             wa=sh.weight("lru_w_a", j), ba=sm["lru_b_a"][j].reshape(1, -1), wx=sh.weight("lru_w_x", j),
                             bx=sm["lru_b_x"][j].reshape(1, -1), lam=sm["lru_lambda"][j].reshape(1, -1),
                             w_out=sh.weight("lru_w_out", j))

    def ffn_weights(which, layer):
        return sh.weight(f"{which}_w_in", layer), sh.weight(f"{which}_w_out", layer)

    sh.landed(0, run_exchange(sh.gather_job(0), "gather_chips"))
    h = x
    tape = []
    for layer in range(depth):
        h, s1 = ffn_fwd(h, a["ffn1_norm"][layer], *ffn_weights("ffn1", layer), "ffn")
        kind, j, p = mixer_params(layer)
        h_mix_in = h
        hn = rms_fwd(h, a["mix_norm"][layer], "mix_norm")
        job = sh.gather_job(layer + 1) if layer + 1 < N_GROUPS else None
        if kind == 0:
            h, sm, got = s5_fwd(hn, h, p, job)
        elif kind == 1:
            o_flat, sm, got = sb_fwd(hn, p, job)
            h = mm(o_flat, p["w_out"], name="mix_out", extras=(h,), epilogue=lambda acc, res: res + acc)
        else:
            assert job is None
            y, sm = lru_fwd(hn, p)
            h = mm(y, p["w_out"], name="mix_out", extras=(h,), epilogue=lambda acc, res: res + acc)
        if job is not None:
            sh.landed(layer + 1, got)
        h, s2 = ffn_fwd(h, a["ffn2_norm"][layer], *ffn_weights("ffn2", layer), "ffn")
        tape.append((s1, h_mix_in, sm, s2))

    loss, dh, g_final = loss_fwd_bwd(h, a["final_norm"], target)

    norm_grads = {n: [None] * depth for n in ("ffn1_norm", "mix_norm", "ffn2_norm")}
    mix = {}
    for layer in reversed(range(depth)):
        s1, h_mix_in, sm, s2 = tape[layer]
        dh, dg, dwi, dwo = ffn_bwd(dh, s2, a["ffn2_norm"][layer], *ffn_weights("ffn2", layer), "ffn")
        norm_grads["ffn2_norm"][layer], sh.grads["ffn2_w_in"][layer], sh.grads["ffn2_w_out"][layer] = dg, dwi, dwo
        kind, j, p = mixer_params(layer)
        job = sh.scatter_job(layer + 1) if layer + 1 < N_GROUPS else None
        if kind == 0:
            (da, w_first), g, got = s5_bwd(dh, sm, p, job)
            names = S5_NAMES
        elif kind == 1:
            (da, w_first), g, got = sb_bwd(dh, sm, p, job)
            names = SB_NAMES
        else:
            assert job is None
            (da, w_first), g = lru_bwd(dh, sm, p)
            names = LRU_NAMES
        if job is not None:
            sh.arrived(layer + 1, got)
        for k, full_name in names.items():
            if full_name in SHARDED_BIG:
                sh.grads[full_name][j] = g[k].reshape(sh.weight(full_name, j).shape)
            else:
                mix.setdefault(full_name, {})[j] = g[k].reshape(a[full_name].shape[1:-1] + (-1,))
        dh, dg = norm_input_bwd(da, _rows_spec, 1, w_first, h_mix_in, a["mix_norm"][layer], dh, "mix_dhn")
        norm_grads["mix_norm"][layer] = dg
        dh, dg, dwi, dwo = ffn_bwd(dh, s1, a["ffn1_norm"][layer], *ffn_weights("ffn1", layer), "ffn")
        norm_grads["ffn1_norm"][layer], sh.grads["ffn1_w_in"][layer], sh.grads["ffn1_w_out"][layer] = dg, dwi, dwo

    grads = {n: jnp.stack(v) for n, v in norm_grads.items()}
    grads["final_norm"] = g_final
    for n, by_j in mix.items():
        stacked = jnp.stack([by_j[j] for j in range(len(by_j))])
        if n in SHARDED_SMALL:
            sh.small_grads[n] = stacked
        else:
            grads[n] = stacked
    sh.arrived(0, run_exchange(sh.scatter_job(0), "scatter_chips"))
    return loss, dh, grads


def kernel(x, ffn1_norm, ffn1_w_in, ffn1_w_out, mix_norm, ffn2_norm, ffn2_w_in, ffn2_w_out, final_norm, s5_w_in,
           s5_lam_re, s5_lam_im, s5_log_dt, s5_b_re, s5_b_im, s5_c_re, s5_c_im, s5_d, s5_w_out, sb_w_qkv,
           sb_w_out, lru_w_in, lru_conv_w, lru_conv_b, lru_w_a, lru_b_a, lru_w_x, lru_b_x, lru_lambda,
           lru_w_out, loss_target, m_ffn1_norm, m_ffn1_w_in, m_ffn1_w_out, m_mix_norm, m_ffn2_norm,
           m_ffn2_w_in, m_ffn2_w_out, m_final_norm, m_s5_w_in, m_s5_lam_re, m_s5_lam_im, m_s5_log_dt,
           m_s5_b_re, m_s5_b_im, m_s5_c_re, m_s5_c_im, m_s5_d, m_s5_w_out, m_sb_w_qkv, m_sb_w_out, m_lru_w_in,
           m_lru_conv_w, m_lru_conv_b, m_lru_w_a, m_lru_b_a, m_lru_w_x, m_lru_b_x, m_lru_lambda, m_lru_w_out,
           v_ffn1_norm, v_ffn1_w_in, v_ffn1_w_out, v_mix_norm, v_ffn2_norm, v_ffn2_w_in, v_ffn2_w_out,
           v_final_norm, v_s5_w_in, v_s5_lam_re, v_s5_lam_im, v_s5_log_dt, v_s5_b_re, v_s5_b_im, v_s5_c_re,
           v_s5_c_im, v_s5_d, v_s5_w_out, v_sb_w_qkv, v_sb_w_out, v_lru_w_in, v_lru_conv_w, v_lru_conv_b,
           v_lru_w_a, v_lru_b_a, v_lru_w_x, v_lru_b_x, v_lru_lambda, v_lru_w_out):
    a = dict(locals())
    assert list(a) == INPUTS
    x, y, c, _ = _place()
    chip = 2 * x + y
    everyone = ("x", "y", "c")

    sh = _Sharded(a)
    loss, dx, grads = _forward_backward(a["x"][0], a["loss_target"][0], a, sh)
    loss = lax.psum(loss, everyone)

    small = list(SHARDED_SMALL)

    def sum_chips(mine, got):
        rows = _rows2d(mine)
        return rowwise(lambda o, r: ((o + r[0].astype(F32)) + r[1].astype(F32)) + r[2].astype(F32),
                       [rows, got.reshape((3,) + rows.shape)], [], [(rows.shape[1], F32)], name="sum_chips")

    def own_block(full, name, d):
        return lax.dynamic_slice_in_dim(full, chip * a[name].shape[d], a[name].shape[d], axis=d)

    keys = [(n, lo, hi) for n in SHARDED_BIG for lo, hi, _ in sh.pieces[n]]
    partial = [sum_chips(own_block(sh.piece_grad(n, lo, hi), n, SHARDED_BIG[n]), sh.received[(n, lo)]) for n, lo, hi in keys]
    partial.append(sum_chips(_pack([own_block(sh.small_grads[n], n, d) for n, d in SHARDED_SMALL.items()]), sh.small_received))
    other = swap_cores(partial)

    def adam_sharded(wv, ga, gb, m, v):
        shape = wv.shape
        res = rowwise(lambda w_, a_, b_, mm_, vv_: (a_ + b_,) + _adamw(w_, a_ + b_, mm_, vv_),
                      [_rows2d(wv), ga, gb, _rows2d(m), _rows2d(v)], [], [(shape[-1], F32)] * 4, name="adamw_sharded")
        return [r.reshape(shape) for r in res]

    out_grad, out_delta, out_m, out_v = {}, {}, {}, {}
    for n in SHARDED_BIG:
        at = [i for i, k in enumerate(keys) if k[0] == n]
        out_grad[n], out_delta[n], out_m[n], out_v[n] = adam_sharded(a[n], [partial[i] for i in at], [other[i] for i in at],
                                                                     a["m_" + n], a["v_" + n])
    small_shapes = [a[n].shape for n in small]
    sp = [_pack([a[pre + n] for n in small]) for pre in ("", "m_", "v_")]
    g_, d_, m_, v_ = adam_sharded(sp[0], [partial[-1]], [other[-1]], sp[1], sp[2])
    for n, gg, dd, mm_, vv in zip(small, _unpack(g_, small_shapes), _unpack(d_, small_shapes), _unpack(m_, small_shapes),
                                  _unpack(v_, small_shapes)):
        out_grad[n], out_delta[n], out_m[n], out_v[n] = gg, dd, mm_, vv

    rep_shapes = [a[n].shape for n in REPLICATED]
    rep_all = gather_devices(_pack([grads[n] for n in REPLICATED]))
    rp = [_pack([a[pre + n] for n in REPLICATED]) for pre in ("", "m_", "v_")]

    def adam_rep(w_, g8, mm_, vv_):
        g = g8[0]
        for k in range(1, N_DEVICES):
            g = g + g8[k]
        return (g,) + _adamw(w_, g, mm_, vv_)

    g_, d_, m_, v_ = rowwise(adam_rep, [rp[0], rep_all, rp[1], rp[2]], [], [(PACK_LANES, F32)] * 4, name="adamw_replicated")
    for n, gg, dd, mm_, vv in zip(REPLICATED, _unpack(g_, rep_shapes), _unpack(d_, rep_shapes), _unpack(m_, rep_shapes),
                                  _unpack(v_, rep_shapes)):
        out_grad[n], out_delta[n], out_m[n], out_v[n] = gg, dd, mm_, vv

    return (loss, dx[None], *[out_grad[n] for n in WEIGHTS], *[out_delta[n] for n in WEIGHTS], *[out_m[n] for n in WEIGHTS],
            *[out_v[n] for n in WEIGHTS])
```

```python
import functools
import math

import jax
import jax.numpy as jnp
from jax import lax
from jax.experimental import pallas as pl
from jax.experimental.pallas import tpu as pltpu

F32 = jnp.float32
BF16 = jnp.bfloat16

VMEM_LIMIT_BYTES = 56 * 1024 * 1024

RMS_EPS = 1e-6
D_FF = 2816
S5_GROUP = 16
S5_STATE = 64
SB_HEAD_DIM = 64
LRU_BLOCK_WIDTH = 256
LRU_CONV = 4
LRU_C = 8.0
N_MIXERS = 3

ADAM_LR = 0.001
ADAM_B1 = 0.9
ADAM_B2 = 0.999
ADAM_EPS = 1e-08
ADAM_WD = 0.01
ADAM_STEP = 10


def _params(semantics):
    return pltpu.CompilerParams(dimension_semantics=semantics, vmem_limit_bytes=VMEM_LIMIT_BYTES)


def _tile(dim, prefs):
    for t in prefs:
        if t <= dim and dim % t == 0:
            return t
    return dim


def _mm_call(name, grid, a, a_spec, b, b_spec, dims, extras, outs, epilogue, acc_shape):
    n_extra, n_out, n_k = len(extras), len(outs), grid[-1]

    def body(a_ref, b_ref, *rest):
        extra_refs = rest[:n_extra]
        out_refs = rest[n_extra:n_extra + n_out]
        acc_ref = rest[n_extra + n_out]
        k = pl.program_id(len(grid) - 1)
        part = lax.dot_general(a_ref[...].astype(BF16), b_ref[...].astype(BF16), dims, preferred_element_type=F32)

        @pl.when(k == 0)
        def _():
            acc_ref[...] = part

        @pl.when(k > 0)
        def _():
            acc_ref[...] += part

        @pl.when(k == n_k - 1)
        def _():
            res = epilogue(acc_ref[...], *[r[...] for r in extra_refs])
            if not isinstance(res, (tuple, list)):
                res = (res,)
            for o_ref, r in zip(out_refs, res):
                o_ref[...] = r.astype(o_ref.dtype)

    sem = ("parallel",) * (len(grid) - 1) + ("arbitrary",)
    res = pl.pallas_call(
        body,
        name=name,
        grid=grid,
        in_specs=[a_spec, b_spec] + [s for _, s in extras],
        out_specs=[s for _, s in outs],
        out_shape=[s for s, _ in outs],
        scratch_shapes=[pltpu.VMEM(acc_shape, F32)],
        compiler_params=_params(sem),
    )(a, b, *[x for x, _ in extras])
    return res


def _fit(dim, target, align=128):
    best = None
    for t in range(align, min(dim, target) + 1, align):
        if dim % t == 0:
            best = t
    return best or dim


def mm(a, b, *, name, ta=False, tb=False, extras=(), epilogue=None, out_dtypes=(F32,), tm=1024, tn=1024, tk=1024):
    m, kdim = (a.shape[1], a.shape[0]) if ta else a.shape
    n = b.shape[0] if tb else b.shape[1]
    tm = _fit(m, tm)
    tn = _fit(n, tn)
    tk = _fit(kdim, tk)
    grid = (m // tm, n // tn, kdim // tk)
    a_spec = pl.BlockSpec((tk, tm), lambda i, j, k: (k, i)) if ta else pl.BlockSpec((tm, tk), lambda i, j, k: (i, k))
    b_spec = pl.BlockSpec((tn, tk), lambda i, j, k: (j, k)) if tb else pl.BlockSpec((tk, tn), lambda i, j, k: (k, j))
    dims = (((0 if ta else 1,), (1 if tb else 0,)), ((), ()))
    o_spec = pl.BlockSpec((tm, tn), lambda i, j, k: (i, j))
    if epilogue is None:
        epilogue = lambda acc: acc
    res = _mm_call(name, grid, a, a_spec, b, b_spec, dims, [(x, o_spec) for x in extras],
                   [(jax.ShapeDtypeStruct((m, n), dt), o_spec) for dt in out_dtypes], epilogue, (tm, tn))
    return res[0] if len(res) == 1 else res


def rowwise(fn, rows, consts, outs, sums=(), *, name, tm=256):
    m = rows[0].shape[0]
    pieces = [x if isinstance(x, (list, tuple)) else [x] for x in rows]
    tm = _tile(math.gcd(*[p.shape[-2] for ps in pieces for p in ps]), (tm, 128, 64, 32, 16, 8))
    starts = [[sum(q.shape[-2] for q in ps[:k]) // tm for k in range(len(ps) + 1)] for ps in pieces]
    flat = [p for ps in pieces for p in ps]
    n_rows, n_consts, n_outs = len(flat), len(consts), len(outs)

    def body(*refs):
        i = pl.program_id(0)
        in_vals, at = [], 0
        for ps, st in zip(pieces, starts):
            val = refs[at + len(ps) - 1][...]
            for k in reversed(range(len(ps) - 1)):
                val = jnp.where(i < st[k + 1], refs[at + k][...], val)
            in_vals.append(val)
            at += len(ps)
        in_vals += [r[...] for r in refs[n_rows:n_rows + n_consts]]
        out_refs = refs[n_rows + n_consts:n_rows + n_consts + n_outs]
        sum_refs = refs[n_rows + n_consts + n_outs:]
        res = fn(*in_vals)
        if not isinstance(res, (tuple, list)):
            res = (res,)
        for o_ref, r in zip(out_refs, res[:n_outs]):
            o_ref[...] = r.astype(o_ref.dtype)
        if sum_refs:
            first = pl.program_id(0) == 0

            @pl.when(first)
            def _():
                for s_ref, r in zip(sum_refs, res[n_outs:]):
                    s_ref[...] = r.astype(F32)

            @pl.when(jnp.logical_not(first))
            def _():
                for s_ref, r in zip(sum_refs, res[n_outs:]):
                    s_ref[...] += r.astype(F32)

    def whole(shape):
        nd = len(shape)
        return pl.BlockSpec(shape, lambda i: (0,) * nd)

    def row_spec(x, lo, hi):
        at = lambda i: jnp.clip(i - lo, 0, hi - lo - 1)
        if x.ndim == 3:
            return pl.BlockSpec((x.shape[0], tm, x.shape[2]), lambda i: (0, at(i), 0))
        return pl.BlockSpec((tm, x.shape[1]), lambda i: (at(i), 0))

    in_specs = [row_spec(p, st[k], st[k + 1]) for ps, st in zip(pieces, starts) for k, p in enumerate(ps)]
    in_specs += [whole(c.shape) for c in consts]
    out_specs = [pl.BlockSpec((tm, nc), lambda i: (i, 0)) for nc, _ in outs] + [whole(tuple(s)) for s in sums]
    out_shape = [jax.ShapeDtypeStruct((m, nc), dt) for nc, dt in outs] + [jax.ShapeDtypeStruct(tuple(s), F32) for s in sums]
    res = pl.pallas_call(
        body,
        name=name,
        grid=(m // tm,),
        in_specs=in_specs,
        out_specs=out_specs,
        out_shape=out_shape,
        compiler_params=_params(("arbitrary",) if sums else ("parallel",)),
    )(*flat, *consts)
    return res[0] if len(res) == 1 else res


def _rms(h, g):
    return h * lax.rsqrt(jnp.mean(h * h, axis=-1, keepdims=True) + RMS_EPS) * g


def _sigmoid(x):
    return 1.0 / (1.0 + jnp.exp(-x))


def _silu_mul(g, u):
    return g * _sigmoid(g) * u


def _gelu(x):
    return 0.5 * x * (1.0 + jnp.tanh(math.sqrt(2.0 / math.pi) * (x + 0.044715 * (x * x * x))))


def _softplus(x):
    return jnp.maximum(x, 0.0) + jnp.log(1.0 + jnp.exp(-jnp.abs(x)))


def rms_fwd(h, g, name):
    return rowwise(lambda x, gg: _rms(x, gg), [h], [g.reshape(1, -1)], [(h.shape[1], BF16)], name=name)


def norm_input_bwd(a, a_spec, n_k, w, h, g, dres, name, tm=512):
    m, d = h.shape
    tk = w.shape[1] // n_k
    tm = _fit(m, tm)

    def body(a_ref, w_ref, h_ref, g_ref, dres_ref, dh_ref, dg_ref, acc_ref):
        i, k = pl.program_id(0), pl.program_id(1)
        part = lax.dot_general(a_ref[...].astype(BF16), w_ref[...], _NT, preferred_element_type=F32)

        @pl.when(k == 0)
        def _():
            acc_ref[...] = part

        @pl.when(k > 0)
        def _():
            acc_ref[...] += part

        @pl.when(k == n_k - 1)
        def _():
            _, vjp = jax.vjp(_rms, h_ref[...], g_ref[...])
            dx, dg = vjp(acc_ref[...])
            dh_ref[...] = dres_ref[...] + dx

            @pl.when(i == 0)
            def _():
                dg_ref[...] = dg

            @pl.when(i > 0)
            def _():
                dg_ref[...] += dg

    row = pl.BlockSpec((tm, d), lambda i, k: (i, 0))
    vec = pl.BlockSpec((1, d), lambda i, k: (0, 0))
    dh, dg = pl.pallas_call(
        body,
        name=name,
        grid=(m // tm, n_k),
        in_specs=[a_spec(tm, tk), pl.BlockSpec((d, tk), lambda i, k: (0, k)), row, vec, row],
        out_specs=[row, vec],
        out_shape=[jax.ShapeDtypeStruct((m, d), F32), jax.ShapeDtypeStruct((1, d), F32)],
        scratch_shapes=[pltpu.VMEM((tm, d), F32)],
        compiler_params=_params(("arbitrary", "arbitrary")),
    )(a, w, h, g.reshape(1, d), dres)
    return dh, dg.reshape(-1)


def _rows_spec(tm, tk):
    return pl.BlockSpec((tm, tk), lambda i, k: (i, k))


def ffn_in(h, g, w_in, tag):
    m, d = h.shape
    f = w_in.shape[1] // 2
    tm, tn = _fit(m, 512), _fit(f, 1408)
    nj = f // tn

    def body(h_ref, g_ref, wg_ref, wu_ref, hn_ref, gu_ref, act_ref, hn_scr):
        @pl.when(pl.program_id(1) == 0)
        def _():
            hn = _rms(h_ref[...], g_ref[...]).astype(BF16)
            hn_scr[...] = hn
            hn_ref[...] = hn

        a = hn_scr[...]
        gate = jnp.dot(a, wg_ref[...], preferred_element_type=F32)
        up = jnp.dot(a, wu_ref[...], preferred_element_type=F32)
        gu_ref[0] = gate.astype(BF16)
        gu_ref[1] = up.astype(BF16)
        act_ref[...] = _silu_mul(gate, up).astype(BF16)

    return pl.pallas_call(
        body,
        name=f"{tag}_in",
        grid=(m // tm, nj),
        in_specs=[pl.BlockSpec((tm, d), lambda i, j: (i, 0)), pl.BlockSpec((1, d), lambda i, j: (0, 0)),
                  pl.BlockSpec((d, tn), lambda i, j: (0, j)), pl.BlockSpec((d, tn), lambda i, j: (0, nj + j))],
        out_specs=[pl.BlockSpec((tm, d), lambda i, j: (i, 0)), pl.BlockSpec((2, tm, tn), lambda i, j: (0, i, j)),
                   pl.BlockSpec((tm, tn), lambda i, j: (i, j))],
        out_shape=[jax.ShapeDtypeStruct((m, d), BF16), jax.ShapeDtypeStruct((2, m, f), BF16), jax.ShapeDtypeStruct((m, f), BF16)],
        scratch_shapes=[pltpu.VMEM((tm, d), BF16)],
        compiler_params=_params(("parallel", "arbitrary")),
    )(h, g.reshape(1, d), w_in, w_in)


def ffn_fwd(h, g, w_in, w_out, tag):
    hn, gu, act = ffn_in(h, g, w_in, tag)
    h_new = mm(act, w_out, name=f"{tag}_out", extras=(h,), epilogue=lambda acc, res: res + 0.5 * acc, tm=512, tk=2816)
    return h_new, (h, hn, gu, act)


def ffn_bwd(dh_new, saved, g, w_in, w_out, tag):
    h, hn, gu, act = saved
    m, d = h.shape
    f = w_out.shape[0]

    def act_bwd(acc, gu_blk):
        _, vjp = jax.vjp(_silu_mul, gu_blk[0].astype(F32), gu_blk[1].astype(F32))
        return jnp.stack(vjp(0.5 * acc))

    tm, tn = _fit(m, 512), _fit(f, 1408)
    pair = pl.BlockSpec((2, tm, tn), lambda i, j, k: (0, i, j))
    dgu = _mm_call(f"{tag}_dgu", (m // tm, f // tn, 1), dh_new, pl.BlockSpec((tm, d), lambda i, j, k: (i, 0)), w_out,
                   pl.BlockSpec((tn, d), lambda i, j, k: (j, 0)), _NT, [(gu, pair)],
                   [(jax.ShapeDtypeStruct((2, m, f), BF16), pair)], act_bwd, (tm, tn))[0]
    dw_out = mm(act, dh_new, ta=True, name=f"{tag}_dwout", epilogue=lambda acc: 0.5 * acc, tm=1408)

    tn, tk = _fit(f, 1408), _fit(m, 2048)
    nh = f // tn
    dw_in = _mm_call(f"{tag}_dwin", (1, 2 * nh, m // tk), hn, pl.BlockSpec((tk, d), lambda i, j, k: (k, 0)), dgu,
                     pl.BlockSpec((None, tk, tn), lambda i, j, k: (j // nh, k, j % nh)), _TN, [],
                     [(jax.ShapeDtypeStruct((d, 2 * f), F32), pl.BlockSpec((d, tn), lambda i, j, k: (0, j)))],
                     lambda acc: acc, (d, tn))[0]

    nkh = f // _fit(f, 2816)
    dh, dg = norm_input_bwd(dgu, lambda tm, tk: pl.BlockSpec((None, tm, tk), lambda i, k: (k // nkh, i, k % nkh)), 2 * nkh,
                            w_in, h, g, dh_new, f"{tag}_dhn")
    return dh, dg, dw_in, dw_out


def loss_fwd_bwd(h, g, target):
    d = h.shape[1]

    def fn(x, t, gg):
        y, vjp = jax.vjp(_rms, x, gg)
        err = y - t
        dx, dg = vjp(err * (1.0 / d))
        part = 0.5 * jnp.sum(jnp.sum(err * err, axis=1, keepdims=True), axis=0, keepdims=True) * (1.0 / d)
        return dx, dg, jnp.broadcast_to(part, (1, 128))

    dh, dg, loss = rowwise(fn, [h, target], [g.reshape(1, -1)], [(d, F32)], [(1, d), (1, 128)], name="loss_head")
    return loss[0, 0], dh, dg.reshape(-1)


def _shift_down(v, d, fill):
    rows = lax.broadcasted_iota(jnp.int32, v.shape, 0)
    return jnp.where(rows < d, fill, pltpu.roll(v, d, 0))


def _shift_up(v, d, fill):
    t = v.shape[0]
    rows = lax.broadcasted_iota(jnp.int32, v.shape, 0)
    return jnp.where(rows >= t - d, fill, pltpu.roll(v, t - d, 0))


def _scan_fwd(a, x):
    d = 1
    while d < a.shape[0]:
        x = x + a * _shift_down(x, d, 0.0)
        a = a * _shift_down(a, d, 1.0)
        d *= 2
    return a, x


def _scan_bwd(b, x):
    d = 1
    while d < b.shape[0]:
        x = x + b * _shift_up(x, d, 0.0)
        b = b * _shift_up(b, d, 1.0)
        d *= 2
    return b, x


def _rows_before(cur, prev8, s):
    r = pltpu.roll(cur, s, 0)
    p = pltpu.roll(prev8, s, 0)
    rows = lax.broadcasted_iota(jnp.int32, prev8.shape, 0)
    return jnp.concatenate([jnp.where(rows < s, p, r[:8]), r[8:]], axis=0)


def _rows_after(cur, next8, s):
    t = cur.shape[0]
    r = pltpu.roll(cur, t - s, 0)
    p = pltpu.roll(next8, 8 - s, 0)
    rows = lax.broadcasted_iota(jnp.int32, next8.shape, 0)
    return jnp.concatenate([r[:t - 8], jnp.where(rows >= 8 - s, p, r[t - 8:])], axis=0)


LRU_CHUNK = 256


def _neg_expm1(y):
    small = -y * (1.0 + 0.5 * y * (1.0 + y * (1.0 / 3.0)))
    return jnp.where(y > -0.01, small, 1.0 - jnp.exp(y))


def _lru_gate(xc, pre_a, pre_x, lam):
    r = _sigmoid(pre_a)
    ig = _sigmoid(pre_x)
    log_a = (-LRU_C * r) * _softplus(-lam)
    return jnp.exp(log_a), (ig * xc) * jnp.sqrt(_neg_expm1(2.0 * log_a))


def _lru_conv(br, prev8, conv_w, conv_b):
    taps = [br] + [_rows_before(br, prev8, s) for s in range(1, LRU_CONV)]
    xc = conv_b
    for k in range(LRU_CONV):
        xc = xc + conv_w[k:k + 1, :] * taps[LRU_CONV - 1 - k]
    return xc, taps


def _lru_pre(xcb, w_ref, bias):
    nb = w_ref.shape[0]
    bw = w_ref.shape[1]
    return jnp.concatenate(
        [jnp.dot(xcb[:, n * bw:(n + 1) * bw], w_ref[n], preferred_element_type=F32) for n in range(nb)], axis=1) + bias


def lru_scan_fwd(bgr, conv_w, conv_b, wa, ba, wx, bx, lam):
    l, w2 = bgr.shape
    w = w2 // 2
    t = _tile(l, (LRU_CHUNK, 128, 64, 32, 16, 8))

    def body(bg_ref, br_ref, cw_ref, cb_ref, wa_ref, ba_ref, wx_ref, bx_ref, lam_ref, y_ref, h_ref, tail_ref, hprev_ref):
        @pl.when(pl.program_id(0) == 0)
        def _():
            tail_ref[...] = jnp.zeros_like(tail_ref)
            hprev_ref[...] = jnp.zeros_like(hprev_ref)

        br = br_ref[...]
        xc, _ = _lru_conv(br, tail_ref[...], cw_ref[...], cb_ref[...])
        xcb = xc.astype(BF16)
        a, gx = _lru_gate(xc, _lru_pre(xcb, wa_ref, ba_ref[...]), _lru_pre(xcb, wx_ref, bx_ref[...]), lam_ref[...])
        acum, x = _scan_fwd(a, gx)
        h = x + acum * hprev_ref[pl.ds(7, 1), :]
        y_ref[...] = (_gelu(bg_ref[...]) * h).astype(y_ref.dtype)
        h_ref[...] = h
        tail_ref[...] = br[t - 8:, :]
        hprev_ref[...] = h[t - 8:, :]

    def whole(x):
        nd = x.ndim
        return pl.BlockSpec(x.shape, lambda i: (0,) * nd)

    consts = [conv_w, conv_b, wa, ba, wx, bx, lam]
    return pl.pallas_call(
        body,
        name="lru_scan_fwd",
        grid=(l // t,),
        in_specs=[pl.BlockSpec((t, w), lambda i: (i, 0)), pl.BlockSpec((t, w), lambda i: (i, 1))] + [whole(c) for c in consts],
        out_specs=[pl.BlockSpec((t, w), lambda i: (i, 0)), pl.BlockSpec((t, w), lambda i: (i, 0))],
        out_shape=[jax.ShapeDtypeStruct((l, w), BF16), jax.ShapeDtypeStruct((l, w), F32)],
        scratch_shapes=[pltpu.VMEM((8, w), F32), pltpu.VMEM((8, w), F32)],
        compiler_params=_params(("arbitrary",)),
    )(bgr, bgr, *consts)


def lru_scan_bwd(dy, bgr, hseq, conv_w, conv_b, wa, ba, wx, bx, lam):
    l, w2 = bgr.shape
    w = w2 // 2
    t = _tile(l, (LRU_CHUNK, 128, 64, 32, 16, 8))
    nc = l // t
    nb, bw = wa.shape[0], wa.shape[1]

    def body(dy_ref, bg_ref, br_ref, brh_ref, h_ref, hh_ref, cw_ref, cb_ref, wa_ref, ba_ref, wx_ref, bx_ref, lam_ref,
             dbgr_ref, dcw_ref, dcb_ref, dwa_ref, dba_ref, dwx_ref, dbx_ref, dlam_ref, dxcn_ref, carry_ref):
        i = pl.program_id(0)
        has_prev = (i < nc - 1).astype(F32)

        @pl.when(i == 0)
        def _():
            dxcn_ref[...] = jnp.zeros_like(dxcn_ref)
            carry_ref[...] = jnp.zeros_like(carry_ref)

        br = br_ref[...]
        cw = cw_ref[...]
        xc, taps = _lru_conv(br, brh_ref[...] * has_prev, cw, cb_ref[...])
        xcb = xc.astype(BF16)
        (a, _), gate_vjp = jax.vjp(_lru_gate, xc, _lru_pre(xcb, wa_ref, ba_ref[...]), _lru_pre(xcb, wx_ref, bx_ref[...]),
                                   lam_ref[...])
        hs = h_ref[...]
        _, out_vjp = jax.vjp(lambda g_, h_: _gelu(g_) * h_, bg_ref[...], hs)
        dbg, dhs = out_vjp(dy_ref[...])
        bcum, x = _scan_bwd(_shift_up(a, 1, 1.0), dhs)
        dh = x + bcum * carry_ref[pl.ds(0, 1), :]
        da = dh * _rows_before(hs, hh_ref[...] * has_prev, 1)
        dxc, dpa, dpx, dlam = gate_vjp((da, dh))
        dpab, dpxb = dpa.astype(BF16), dpx.astype(BF16)
        nt = (((1,), (1,)), ((), ()))
        tn = (((0,), (0,)), ((), ()))
        dxb, dwa, dwx = [], [], []
        for n in range(nb):
            sl = slice(n * bw, (n + 1) * bw)
            dxb.append(lax.dot_general(dpab[:, sl], wa_ref[n], nt, preferred_element_type=F32)
                       + lax.dot_general(dpxb[:, sl], wx_ref[n], nt, preferred_element_type=F32))
            dwa.append(lax.dot_general(xcb[:, sl], dpab[:, sl], tn, preferred_element_type=F32))
            dwx.append(lax.dot_general(xcb[:, sl], dpxb[:, sl], tn, preferred_element_type=F32))
        dxc = dxc + jnp.concatenate(dxb, axis=1)
        ups = [dxc] + [_rows_after(dxc, dxcn_ref[...], s) for s in range(1, LRU_CONV)]
        dbr = cw[LRU_CONV - 1:LRU_CONV, :] * ups[0]
        for k in range(LRU_CONV - 1):
            dbr = dbr + cw[k:k + 1, :] * ups[LRU_CONV - 1 - k]
        dbgr_ref[:, :w] = dbg.astype(dbgr_ref.dtype)
        dbgr_ref[:, w:] = dbr.astype(dbgr_ref.dtype)
        dcw = jnp.concatenate([jnp.sum(dxc * taps[LRU_CONV - 1 - k], axis=0, keepdims=True) for k in range(LRU_CONV)], axis=0)
        sums = [(dcw_ref, dcw), (dcb_ref, jnp.sum(dxc, axis=0, keepdims=True)), (dwa_ref, jnp.stack(dwa)),
                (dba_ref, jnp.sum(dpa, axis=0, keepdims=True)), (dwx_ref, jnp.stack(dwx)),
                (dbx_ref, jnp.sum(dpx, axis=0, keepdims=True)), (dlam_ref, dlam)]

        @pl.when(i == 0)
        def _():
            for ref, val in sums:
                ref[...] = val

        @pl.when(i > 0)
        def _():
            for ref, val in sums:
                ref[...] += val

        dxcn_ref[...] = dxc[:8, :]
        carry_ref[...] = (a * dh)[:8, :]

    def whole(shape):
        nd = len(shape)
        return pl.BlockSpec(tuple(shape), lambda i: (0,) * nd)

    consts = [conv_w, conv_b, wa, ba, wx, bx, lam]
    t8 = t // 8
    rev = lambda i: nc - 1 - i
    halo = lambda i: jnp.maximum(rev(i) * t8 - 1, 0)
    in_specs = [
        pl.BlockSpec((t, w), lambda i: (rev(i), 0)),
        pl.BlockSpec((t, w), lambda i: (rev(i), 0)),
        pl.BlockSpec((t, w), lambda i: (rev(i), 1)),
        pl.BlockSpec((8, w), lambda i: (halo(i), 1)),
        pl.BlockSpec((t, w), lambda i: (rev(i), 0)),
        pl.BlockSpec((8, w), lambda i: (halo(i), 0)),
    ] + [whole(c.shape) for c in consts]
    sum_shapes = [conv_w.shape, conv_b.shape, wa.shape, ba.shape, wx.shape, bx.shape, lam.shape]
    res = pl.pallas_call(
        body,
        name="lru_scan_bwd",
        grid=(nc,),
        in_specs=in_specs,
        out_specs=[pl.BlockSpec((t, w2), lambda i: (rev(i), 0))] + [whole(s) for s in sum_shapes],
        out_shape=[jax.ShapeDtypeStruct((l, w2), BF16)] + [jax.ShapeDtypeStruct(tuple(s), F32) for s in sum_shapes],
        scratch_shapes=[pltpu.VMEM((8, w), F32), pltpu.VMEM((8, w), F32)],
        compiler_params=_params(("arbitrary",)),
    )(dy, bgr, bgr, bgr, hseq, hseq, *consts)
    return res[0], res[1:]


def lru_fwd(hn, p):
    bgr = mm(hn, p["w_in"], name="lru_in")
    y, hseq = lru_scan_fwd(bgr, p["conv_w"], p["conv_b"], p["wa"], p["ba"], p["wx"], p["bx"], p["lam"])
    return y, (hn, bgr, hseq, y)


def lru_bwd(dmixed, saved, p):
    hn, bgr, hseq, y = saved
    dy = mm(dmixed, p["w_out"], tb=True, name="lru_dy")
    dw_out = mm(y, dmixed, ta=True, name="lru_dwout")
    dbgr, (dcw, dcb, dwa, dba, dwx, dbx, dlam) = lru_scan_bwd(dy, bgr, hseq, p["conv_w"], p["conv_b"], p["wa"], p["ba"],
                                                               p["wx"], p["bx"], p["lam"])
    dw_in = mm(hn, dbgr, ta=True, name="lru_dwin")
    grads = dict(w_in=dw_in, conv_w=dcw, conv_b=dcb, wa=dwa, ba=dba, wx=dwx, bx=dbx, lam=dlam, w_out=dw_out)
    return (dbgr, p["w_in"]), grads


SB_BLOCK = 256
SB_BLOCK_Q = 1024
_NT = (((1,), (1,)), ((), ()))
_TN = (((0,), (0,)), ((), ()))


def _running_sums(x, tri, total_col):
    xb = x.astype(BF16)
    run = jnp.dot(xb, tri, preferred_element_type=F32)
    return run, xb, run[:, total_col:total_col + 1]


def _tri(n, cmp):
    r = lax.broadcasted_iota(jnp.int32, (n, n), 0)
    c = lax.broadcasted_iota(jnp.int32, (n, n), 1)
    return cmp(r, c).astype(BF16)


SB_PAIR = 2 * SB_HEAD_DIM


def _pair_masks(x2, scale=None):
    lane = lax.broadcasted_iota(jnp.int32, x2.shape, 1)
    zero = jnp.zeros_like(x2)
    a, b = jnp.where(lane < SB_HEAD_DIM, x2, zero), jnp.where(lane >= SB_HEAD_DIM, x2, zero)
    if scale is not None:
        a, b = a * scale, b * scale
    return a, b


def _causal(shape, q0, k0):
    return lax.broadcasted_iota(jnp.int32, shape, 1) + k0 < lax.broadcasted_iota(jnp.int32, shape, 0) + q0


def _sb_blocks(l):
    bk = _tile(l, (SB_BLOCK, 128))
    bq = _tile(l, (SB_BLOCK_Q, 2 * SB_BLOCK, SB_BLOCK, 128))
    return bq, bk


def sb_pair_fwd(qkv, job=None):
    l, d3 = qkv.shape
    d = d3 // 3
    npair = d // SB_PAIR
    bq, bk = _sb_blocks(l)
    ratio = bq // bk
    scale = SB_HEAD_DIM ** -0.5
    t_suf = _tri(bk, lambda r, c: r > c)

    def body(q_ref, k_ref, v_ref, tsuf_ref, o_ref, ltot_ref):
        i = pl.program_id(1)
        qs = _pair_masks(q_ref[...], scale)
        tsuf = tsuf_ref[...]

        def tile(j, carry, masked):
            rows = pl.ds(pl.multiple_of(j * bk, bk), bk)
            k2 = k_ref[rows, :]
            v2 = v_ref[rows, :]
            out = []
            for q1, (c_r, acc) in zip(qs, carry):
                z = lax.dot_general(q1, k2, _NT, preferred_element_type=F32)
                lk = -_softplus(z)
                if masked:
                    causal = _causal(z.shape, i * bq, j * bk)
                    lk = jnp.where(causal, lk, 0.0)
                later, lkb, later0 = _running_sums(lk, tsuf, 0)
                w = jnp.exp(z + lk + c_r + later)
                if masked:
                    w = jnp.where(causal, w, 0.0)
                out.append((c_r + later0 + lkb[:, 0:1].astype(F32), acc + jnp.dot(w.astype(BF16), v2, preferred_element_type=F32)))
            return tuple(out)

        carry = ((jnp.zeros((bq, 1), F32), jnp.zeros((bq, SB_PAIR), F32)),) * 2
        for dgl in reversed(range(ratio)):
            carry = tile(i * ratio + dgl, carry, True)
        (c_a, acc_a), (c_b, acc_b) = lax.fori_loop(0, i * ratio, lambda jj, c: tile(i * ratio - 1 - jj, c, False), carry)
        lane = lax.broadcasted_iota(jnp.int32, acc_a.shape, 1)
        o_ref[...] = jnp.where(lane < SB_HEAD_DIM, acc_a, acc_b).astype(o_ref.dtype)
        ltot_ref[...] = jnp.where(lax.broadcasted_iota(jnp.int32, (bq, 2), 1) == 0, c_a, c_b)

    grid = (npair, l // bq)
    body, x_in, x_out, x_shapes, x_scratch, x_args = carry(job, body, grid, 4, 2)
    res = pl.pallas_call(
        body,
        name="sb_attn_fwd",
        grid=grid,
        in_specs=[
            pl.BlockSpec((bq, SB_PAIR), lambda p, i: (i, p)),
            pl.BlockSpec((l, SB_PAIR), lambda p, i: (0, npair + p)),
            pl.BlockSpec((l, SB_PAIR), lambda p, i: (0, 2 * npair + p)),
            pl.BlockSpec((bk, bk), lambda p, i: (0, 0)),
        ] + x_in,
        out_specs=[pl.BlockSpec((bq, SB_PAIR), lambda p, i: (i, p)), pl.BlockSpec((None, bq, 2), lambda p, i: (p, i, 0))] + x_out,
        out_shape=[jax.ShapeDtypeStruct((l, d), BF16), jax.ShapeDtypeStruct((npair, l, 2), F32)] + x_shapes,
        scratch_shapes=x_scratch,
        compiler_params=_params(("arbitrary", "arbitrary")),
    )(qkv, qkv, qkv, t_suf, *x_args)
    return res[0], res[1], res[2:]


def sb_pair_bwd(qkv, do, ltot, job=None):
    l, d3 = qkv.shape
    d = d3 // 3
    npair = d // SB_PAIR
    bq, bk = _sb_blocks(l)
    ratio = bq // bk
    scale = SB_HEAD_DIM ** -0.5
    t_inc = _tri(bk, lambda r, c: r <= c)
    t_exc = _tri(bk, lambda r, c: r < c)

    def body(q_ref, k_ref, v_ref, do_ref, ltot_ref, tinc_ref, texc_ref, dq_ref, dk_ref, dv_ref):
        i = pl.program_id(1)

        @pl.when(i == 0)
        def _():
            dk_ref[...] = jnp.zeros_like(dk_ref)
            dv_ref[...] = jnp.zeros_like(dv_ref)

        qs = _pair_masks(q_ref[...], scale)
        dos = _pair_masks(do_ref[...])
        lt = ltot_ref[...]
        ltots = (lt[:, 0:1], lt[:, 1:2])
        tinc = tinc_ref[...]
        texc = texc_ref[...]

        def tile(j, carry, masked):
            rows = pl.ds(pl.multiple_of(j * bk, bk), bk)
            k2 = k_ref[rows, :]
            v2 = v_ref[rows, :]
            out = []
            dk2 = dv2 = None
            for q1, do1, ltot1, (c_l, c_p, dq) in zip(qs, dos, ltots, carry):
                z = lax.dot_general(q1, k2, _NT, preferred_element_type=F32)
                lk = -_softplus(z)
                if masked:
                    causal = _causal(z.shape, i * bq, j * bk)
                    lk = jnp.where(causal, lk, 0.0)
                log_beta = z + lk
                upto, _, lk_tile = _running_sums(lk, tinc, bk - 1)
                w = jnp.exp(log_beta + (ltot1 - c_l) - upto)
                if masked:
                    w = jnp.where(causal, w, 0.0)
                g = w * lax.dot_general(do1, v2, _NT, preferred_element_type=F32)
                before, gb, before_last = _running_sums(g, texc, bk - 1)
                dz = g - jnp.exp(log_beta) * (g + c_p + before)
                if masked:
                    dz = jnp.where(causal, dz, 0.0)
                dzb = dz.astype(BF16)
                dk1 = lax.dot_general(dzb, q1, _TN, preferred_element_type=F32)
                dv1 = lax.dot_general(w.astype(BF16), do1, _TN, preferred_element_type=F32)
                dk2 = dk1 if dk2 is None else dk2 + dk1
                dv2 = dv1 if dv2 is None else dv2 + dv1
                out.append((c_l + lk_tile, c_p + before_last + gb[:, bk - 1:bk].astype(F32),
                            dq + jnp.dot(dzb, k2, preferred_element_type=F32)))
            dk_ref[rows, :] += dk2
            dv_ref[rows, :] += dv2
            return tuple(out)

        zero = jnp.zeros((bq, 1), F32)
        carry = ((zero, zero, jnp.zeros((bq, SB_PAIR), F32)),) * 2
        carry = lax.fori_loop(0, i * ratio, lambda j, c: tile(j, c, False), carry)
        for dgl in range(ratio):
            carry = tile(i * ratio + dgl, carry, True)
        (_, _, dq_a), (_, _, dq_b) = carry
        lane = lax.broadcasted_iota(jnp.int32, dq_a.shape, 1)
        dq_ref[...] = jnp.where(lane < SB_HEAD_DIM, dq_a, dq_b) * scale

    col = lambda s: pl.BlockSpec((l, SB_PAIR), lambda p, i: (0, s * npair + p))
    blk_spec = pl.BlockSpec((bq, SB_PAIR), lambda p, i: (i, p))
    tri_spec = pl.BlockSpec((bk, bk), lambda p, i: (0, 0))
    grid = (npair, l // bq)
    body, x_in, x_out, x_shapes, x_scratch, x_args = carry(job, body, grid, 7, 3)
    res = pl.pallas_call(
        body,
        name="sb_attn_bwd",
        grid=grid,
        in_specs=[blk_spec, col(1), col(2), blk_spec, pl.BlockSpec((None, bq, 2), lambda p, i: (p, i, 0)), tri_spec, tri_spec] + x_in,
        out_specs=[blk_spec, pl.BlockSpec((l, SB_PAIR), lambda p, i: (0, p)), pl.BlockSpec((l, SB_PAIR), lambda p, i: (0, p))] + x_out,
        out_shape=[jax.ShapeDtypeStruct((l, d), F32)] * 3 + x_shapes,
        scratch_shapes=x_scratch,
        compiler_params=_params(("arbitrary", "arbitrary")),
    )(qkv, qkv, qkv, do, ltot, t_inc, t_exc, *x_args)
    return res[:3], res[3:]


def sb_fwd(hn, p, job=None):
    qkv = mm(hn, p["w_qkv"], name="sb_qkv", out_dtypes=(BF16,), tm=2048, tn=512)
    o, ltot, carried = sb_pair_fwd(qkv, job)
    return o, (hn, qkv, ltot, o), carried


def sb_bwd(dmixed, saved, p, job=None):
    hn, qkv, ltot, o = saved
    do = mm(dmixed, p["w_out"], tb=True, name="sb_do", out_dtypes=(BF16,))
    dw_out = mm(o, dmixed, ta=True, name="sb_dwout")
    dq_dk_dv, carried = sb_pair_bwd(qkv, do, ltot, job)
    dqkv = jnp.concatenate([g.astype(BF16) for g in dq_dk_dv], axis=1)
    dw_qkv = mm(hn, dqkv, ta=True, name="sb_dwqkv")
    return (dqkv, p["w_qkv"]), dict(w_qkv=dw_qkv, w_out=dw_out), carried


S5_CHUNK = 128
S5_SLAB_GROUPS = 8
S5_LANES = 128


def _s5_discretise(lr, li, ldt, bre, bim):
    dt = jnp.exp(ldt)
    mag = jnp.exp(lr * dt)
    lbr = mag * jnp.cos(li * dt)
    lbi = mag * jnp.sin(li * dt)
    inv = 1.0 / (lr * lr + li * li)
    cr = ((lbr - 1.0) * lr + lbi * li) * inv
    ci = (lbi * lr - (lbr - 1.0) * li) * inv
    return lbr, lbi, cr * bre - ci * bim, cr * bim + ci * bre


def _s5_cols(lam_re, lam_im, log_dt, b_re, b_im):
    g, p = lam_re.shape
    col = lambda x: x.reshape(g * p, 1)
    ldt = jnp.broadcast_to(log_dt[:, None], (g, p))
    return col(lam_re), col(lam_im), col(ldt), b_re.reshape(g * p, -1), b_im.reshape(g * p, -1)


def _slab_b(bbar):
    sg = S5_SLAB_GROUPS
    gp, h = bbar.shape
    p = S5_STATE
    x = bbar.reshape(gp // (sg * p), sg, p, h)
    return jnp.einsum("kaph,ab->kahbp", x, jnp.eye(sg, dtype=x.dtype)).reshape(-1, sg * h, sg * p)


def _unslab_b(dslab):
    sg, p = S5_SLAB_GROUPS, S5_STATE
    nk, sh, _ = dslab.shape
    h = sh // sg
    x = dslab.reshape(nk, sg, h, sg, p)
    return jnp.einsum("kahbp,ab->kaph", x, jnp.eye(sg, dtype=x.dtype)).reshape(nk * sg * p, h)


def _slab_c(c):
    sg = S5_SLAB_GROUPS
    g, h, p = c.shape
    x = c.reshape(g // sg, sg, h, p)
    return jnp.einsum("kahp,ab->kapbh", x, jnp.eye(sg, dtype=x.dtype)).reshape(-1, sg * p, sg * h)


def _unslab_c(dslab):
    sg, p = S5_SLAB_GROUPS, S5_STATE
    nk, _, sh = dslab.shape
    h = sh // sg
    x = dslab.reshape(nk, sg, p, sg, h)
    return jnp.einsum("kapbh,ab->kahp", x, jnp.eye(sg, dtype=x.dtype)).reshape(nk * sg, h, p)


def _cmul_scan(lre, lim, xre, xim, re_scr, im_scr, cre, cim, reverse):
    nb, lanes = re_scr.shape[0], re_scr.shape[2]
    ends = []
    for c in range(nb):
        sl = slice(c * lanes, (c + 1) * lanes)
        re_scr[c] = xre[:, sl]
        im_scr[c] = xim[:, sl]
        ends.append(_cmul_scan_block(lre[:, sl], lim[:, sl], re_scr.at[c], im_scr.at[c], cre[:, sl], cim[:, sl], reverse))
    return (jnp.concatenate([re_scr[c] for c in range(nb)], axis=1), jnp.concatenate([im_scr[c] for c in range(nb)], axis=1),
            jnp.concatenate([e[0] for e in ends], axis=1), jnp.concatenate([e[1] for e in ends], axis=1))


def _cmul_scan_block(lre, lim, re_ref, im_ref, cre, cim, reverse):
    t = re_ref.shape[0]
    g = t // 8
    shift = _shift_up if reverse else _shift_down
    cmul = lambda ar, ai, br, bi: (ar * br - ai * bi, ar * bi + ai * br)
    pows = [(lre, lim)]
    for _ in range(7):
        pows.append(cmul(*pows[-1], lre, lim))
    local, prev = [None] * 8, None
    for r in (reversed(range(8)) if reverse else range(8)):
        cr, ci = re_ref[pl.ds(r, g, stride=8), :], im_ref[pl.ds(r, g, stride=8), :]
        if prev is not None:
            pr, pi = cmul(lre, lim, *prev)
            cr, ci = cr + pr, ci + pi
        local[r] = prev = (cr, ci)
    yr, yi = local[0 if reverse else 7]
    edge = lax.broadcasted_iota(jnp.int32, yr.shape, 0) == (g - 1 if reverse else 0)
    mr, mi = pows[7]
    kr, ki = cmul(mr, mi, cre, cim)
    yr, yi = yr + jnp.where(edge, kr, 0.0), yi + jnp.where(edge, ki, 0.0)
    d = 1
    while d < g:
        pr, pi = cmul(mr, mi, shift(yr, d, 0.0), shift(yi, d, 0.0))
        yr, yi = yr + pr, yi + pi
        mr, mi = cmul(mr, mi, mr, mi)
        d *= 2
    er, ei = jnp.where(edge, cre, shift(yr, 1, 0.0)), jnp.where(edge, cim, shift(yi, 1, 0.0))
    for r in range(8):
        pr, pi = cmul(*pows[7 - r if reverse else r], er, ei)
        re_ref[pl.ds(r, g, stride=8), :] = local[r][0] + pr
        im_ref[pl.ds(r, g, stride=8), :] = local[r][1] + pi
    return yr, yi


def s5_scan_fwd(u, lbr, lbi, bbd_re, bbd_im, cbd_re, cbd_imn, d_skip, job=None):
    l, w = u.shape
    n = lbr.shape[1]
    nk, cw, sw = bbd_re.shape
    t = _tile(l, (S5_CHUNK, 64, 32, 16, 8))

    def body(u_ref, lbr_ref, lbi_ref, bre_ref, bim_ref, cre_ref, cim_ref, d_ref, sre_ref, sim_ref, y_ref, z_ref, pre_ref, pim_ref,
             xr_ref, xi_ref):
        @pl.when(pl.program_id(0) == 0)
        def _():
            pre_ref[...] = jnp.zeros_like(pre_ref)
            pim_ref[...] = jnp.zeros_like(pim_ref)

        uu = u_ref[...]
        ub = uu.astype(BF16)
        lre, lim = lbr_ref[...], lbi_ref[...]
        xre = jnp.concatenate([jnp.dot(ub[:, k * cw:(k + 1) * cw], bre_ref[k], preferred_element_type=F32) for k in range(nk)], axis=1)
        xim = jnp.concatenate([jnp.dot(ub[:, k * cw:(k + 1) * cw], bim_ref[k], preferred_element_type=F32) for k in range(nk)], axis=1)
        last = t // 8 - 1
        sre, sim, pre_ref[...], pim_ref[...] = _cmul_scan(lre, lim, xre, xim, xr_ref, xi_ref, pre_ref[pl.ds(last, 1), :],
                                                          pim_ref[pl.ds(last, 1), :], False)
        sre_ref[...] = sre
        sim_ref[...] = sim
        sreb, simb = sre.astype(BF16), sim.astype(BF16)
        y = jnp.concatenate(
            [jnp.dot(sreb[:, k * sw:(k + 1) * sw], cre_ref[k], preferred_element_type=F32)
             + jnp.dot(simb[:, k * sw:(k + 1) * sw], cim_ref[k], preferred_element_type=F32) for k in range(nk)], axis=1)
        y = y + d_ref[...] * uu
        y_ref[...] = y
        z_ref[...] = _gelu(y).astype(z_ref.dtype)

    def whole(x):
        nd = x.ndim
        return pl.BlockSpec(x.shape, lambda i: (0,) * nd)

    consts = [lbr, lbi, bbd_re, bbd_im, cbd_re, cbd_imn, d_skip]
    row = lambda c: pl.BlockSpec((t, c), lambda i: (i, 0))
    body, x_in, x_out, x_shapes, x_scratch, x_args = carry(job, body, (l // t,), 1 + len(consts), 4)
    res = pl.pallas_call(
        body,
        name="s5_scan_fwd",
        grid=(l // t,),
        in_specs=[row(w)] + [whole(c) for c in consts] + x_in,
        out_specs=[row(n), row(n), row(w), row(w)] + x_out,
        out_shape=[jax.ShapeDtypeStruct((l, n), F32), jax.ShapeDtypeStruct((l, n), F32), jax.ShapeDtypeStruct((l, w), F32),
                   jax.ShapeDtypeStruct((l, w), BF16)] + x_shapes,
        scratch_shapes=[pltpu.VMEM((t // 8, n), F32), pltpu.VMEM((t // 8, n), F32), pltpu.VMEM((n // S5_LANES, t, S5_LANES), F32),
                        pltpu.VMEM((n // S5_LANES, t, S5_LANES), F32)] + x_scratch,
        compiler_params=_params(("arbitrary",)),
    )(u, *consts, *x_args)
    return res[:4], res[4:]


def s5_scan_bwd(dz, y, u, sre, sim, lbr, lbi, bbd_re, bbd_im, cbd_re, cbd_imn, d_skip, job=None):
    l, w = u.shape
    n = lbr.shape[1]
    nk, cw, sw = bbd_re.shape
    t = _tile(l, (S5_CHUNK, 64, 32, 16, 8))
    nc = l // t

    def body(dz_ref, y_ref, u_ref, sre_ref, sim_ref, hre_ref, him_ref, lbr_ref, lbi_ref, bre_ref, bim_ref, cre_ref, cim_ref,
             d_ref, du_ref, dlr_ref, dli_ref, dbre_ref, dbim_ref, dcre_ref, dcim_ref, dd_ref, nre_ref, nim_ref, dsr_ref, dsi_ref):
        i = pl.program_id(0)
        has_prev = (i < nc - 1).astype(F32)

        @pl.when(i == 0)
        def _():
            nre_ref[...] = jnp.zeros_like(nre_ref)
            nim_ref[...] = jnp.zeros_like(nim_ref)

        uu = u_ref[...]
        ub = uu.astype(BF16)
        lre, lim = lbr_ref[...], lbi_ref[...]
        _, gelu_vjp = jax.vjp(_gelu, y_ref[...])
        dy = gelu_vjp(dz_ref[...].astype(F32))[0]
        dyb = dy.astype(BF16)
        gre = jnp.concatenate([lax.dot_general(dyb[:, k * cw:(k + 1) * cw], cre_ref[k], _NT, preferred_element_type=F32)
                               for k in range(nk)], axis=1)
        gim = jnp.concatenate([lax.dot_general(dyb[:, k * cw:(k + 1) * cw], cim_ref[k], _NT, preferred_element_type=F32)
                               for k in range(nk)], axis=1)
        dsre, dsim, nre_ref[...], nim_ref[...] = _cmul_scan(lre, -lim, gre, gim, dsr_ref, dsi_ref, nre_ref[pl.ds(0, 1), :],
                                                            nim_ref[pl.ds(0, 1), :], True)
        dsreb, dsimb = dsre.astype(BF16), dsim.astype(BF16)
        s_re, s_im = sre_ref[...], sim_ref[...]
        du = jnp.concatenate(
            [lax.dot_general(dsreb[:, k * sw:(k + 1) * sw], bre_ref[k], _NT, preferred_element_type=F32)
             + lax.dot_general(dsimb[:, k * sw:(k + 1) * sw], bim_ref[k], _NT, preferred_element_type=F32) for k in range(nk)],
            axis=1)
        du_ref[...] = (du + d_ref[...] * dy).astype(du_ref.dtype)
        pre = _rows_before(s_re, hre_ref[...] * has_prev, 1)
        pim = _rows_before(s_im, him_ref[...] * has_prev, 1)
        sreb, simb = s_re.astype(BF16), s_im.astype(BF16)
        sums = [
            (dlr_ref, jnp.sum(dsre * pre + dsim * pim, axis=0, keepdims=True)),
            (dli_ref, jnp.sum(dsim * pre - dsre * pim, axis=0, keepdims=True)),
            (dbre_ref, jnp.stack([lax.dot_general(ub[:, k * cw:(k + 1) * cw], dsreb[:, k * sw:(k + 1) * sw], _TN,
                                                  preferred_element_type=F32) for k in range(nk)])),
            (dbim_ref, jnp.stack([lax.dot_general(ub[:, k * cw:(k + 1) * cw], dsimb[:, k * sw:(k + 1) * sw], _TN,
                                                  preferred_element_type=F32) for k in range(nk)])),
            (dcre_ref, jnp.stack([lax.dot_general(sreb[:, k * sw:(k + 1) * sw], dyb[:, k * cw:(k + 1) * cw], _TN,
                                                  preferred_element_type=F32) for k in range(nk)])),
            (dcim_ref, jnp.stack([lax.dot_general(simb[:, k * sw:(k + 1) * sw], dyb[:, k * cw:(k + 1) * cw], _TN,
                                                  preferred_element_type=F32) for k in range(nk)])),
            (dd_ref, jnp.sum(dy * uu, axis=0, keepdims=True)),
        ]

        @pl.when(i == 0)
        def _():
            for ref, val in sums:
                ref[...] = val

        @pl.when(i > 0)
        def _():
            for ref, val in sums:
                ref[...] += val

    def whole(shape):
        nd = len(shape)
        return pl.BlockSpec(tuple(shape), lambda i: (0,) * nd)

    consts = [lbr, lbi, bbd_re, bbd_im, cbd_re, cbd_imn, d_skip]
    t8 = t // 8
    rev = lambda i: nc - 1 - i
    halo = lambda i: jnp.maximum(rev(i) * t8 - 1, 0)
    row = lambda c: pl.BlockSpec((t, c), lambda i: (rev(i), 0))
    sum_shapes = [lbr.shape, lbi.shape, bbd_re.shape, bbd_im.shape, cbd_re.shape, cbd_imn.shape, d_skip.shape]
    body, x_in, x_out, x_shapes, x_scratch, x_args = carry(job, body, (nc,), 7 + len(consts), 8)
    res = pl.pallas_call(
        body,
        name="s5_scan_bwd",
        grid=(nc,),
        in_specs=[row(w), row(w), row(w), row(n), row(n), pl.BlockSpec((8, n), lambda i: (halo(i), 0)),
                  pl.BlockSpec((8, n), lambda i: (halo(i), 0))] + [whole(c.shape) for c in consts] + x_in,
        out_specs=[row(w)] + [whole(s) for s in sum_shapes] + x_out,
        out_shape=[jax.ShapeDtypeStruct((l, w), BF16)] + [jax.ShapeDtypeStruct(tuple(s), F32) for s in sum_shapes] + x_shapes,
        scratch_shapes=[pltpu.VMEM((t8, n), F32), pltpu.VMEM((t8, n), F32), pltpu.VMEM((n // S5_LANES, t, S5_LANES), F32),
                        pltpu.VMEM((n // S5_LANES, t, S5_LANES), F32)] + x_scratch,
        compiler_params=_params(("arbitrary",)),
    )(dz, y, u, sre, sim, sre, sim, *consts, *x_args)
    return res[0], res[1:8], res[8:]


def _glu(vg):
    w = vg.shape[1] // 2
    return vg[:, :w] * _sigmoid(vg[:, w:])


def s5_fwd(hn, h, p, job=None):
    cols = _s5_cols(p["lam_re"], p["lam_im"], p["log_dt"], p["b_re"], p["b_im"])
    gp, hh = cols[3].shape
    lbr, lbi, bbr, bbi = rowwise(_s5_discretise, list(cols), [], [(1, F32), (1, F32), (hh, F32), (hh, F32)],
                                 name="s5_discretise", tm=512)
    consts = (lbr.reshape(1, gp), lbi.reshape(1, gp), _slab_b(bbr).astype(BF16), _slab_b(bbi).astype(BF16),
              _slab_c(p["c_re"]).astype(BF16), _slab_c(-p["c_im"]).astype(BF16), p["d"])
    u = mm(hn, p["w_in"], name="s5_in")
    (sre, sim, y, z), carried = s5_scan_fwd(u, *consts, job=job)
    vg = mm(z, p["w_out"], name="s5_out", out_dtypes=(BF16,))
    h_new = rowwise(lambda a, r: r + _glu(a.astype(F32)), [vg, h], [], [(h.shape[1], F32)], name="s5_glu")
    return h_new, (hn, u, sre, sim, y, z, vg, cols, consts), carried


def s5_bwd(dh_new, saved, p, job=None):
    hn, u, sre, sim, y, z, vg, cols, consts = saved

    def glu_bwd(a, dm):
        _, vjp = jax.vjp(_glu, a.astype(F32))
        return vjp(dm)[0]

    dvg = rowwise(glu_bwd, [vg, dh_new], [], [(vg.shape[1], BF16)], name="s5_dglu")
    dw_out = mm(z, dvg, ta=True, name="s5_dwout")
    dz = mm(dvg, p["w_out"], tb=True, name="s5_dz", out_dtypes=(BF16,), tk=2048)
    du, (dlbr, dlbi, dbbr, dbbi, dcre, dcimn, dd), carried = s5_scan_bwd(dz, y, u, sre, sim, *consts, job=job)
    dw_in = mm(hn, du, ta=True, name="s5_dwin")
    gp = cols[0].shape[0]

    def disc_bwd(lr, li, ldt, bre, bim, g0, g1, g2, g3):
        _, vjp = jax.vjp(_s5_discretise, lr, li, ldt, bre, bim)
        return vjp((g0, g1, g2, g3))

    cot = (dlbr.reshape(gp, 1), dlbi.reshape(gp, 1), _unslab_b(dbbr), _unslab_b(dbbi))
    dlr, dli, dldt, dbre, dbim = rowwise(disc_bwd, list(cols + cot), [], [(c.shape[1], F32) for c in cols],
                                         name="s5_discretise_bwd", tm=512)
    g_, p_ = p["lam_re"].shape
    grads = dict(w_in=dw_in, lam_re=dlr.reshape(g_, p_), lam_im=dli.reshape(g_, p_), log_dt=dldt.reshape(g_, p_).sum(axis=1),
                 b_re=dbre.reshape(p["b_re"].shape), b_im=dbim.reshape(p["b_im"].shape), c_re=_unslab_c(dcre),
                 c_im=-_unslab_c(dcimn), d=dd, w_out=dw_out)
    return (du, p["w_in"]), grads, carried


MESH = pl.DeviceIdType.MESH
N_CHIPS = 4
N_DEVICES = 8


def _place():
    x, y, c = lax.axis_index("x"), lax.axis_index("y"), lax.axis_index("c")
    return x, y, c, [(1 - x, y), (x, 1 - y), (1 - x, 1 - y)]


def _hbm_call(body, name, ins, out_shapes, n_remote, n_local=0):
    hbm = pl.BlockSpec(memory_space=pltpu.HBM)
    scratch = [pltpu.SemaphoreType.DMA((n_remote,)), pltpu.SemaphoreType.DMA((n_remote,))]
    if n_local:
        scratch.append(pltpu.SemaphoreType.DMA((n_local,)))
    return pl.pallas_call(
        body,
        name=name,
        in_specs=[hbm] * len(ins),
        out_specs=[hbm] * len(out_shapes),
        out_shape=out_shapes,
        scratch_shapes=scratch,
    )(*ins)


def _split_dim(shape):
    return next(d for d, s in enumerate(shape) if s >= 2 and s % 2 == 0)


class Exchange:
    def __init__(self, ins, out_shapes, n_remote, n_local, start, finish):
        self.ins, self.out_shapes, self.start, self.finish = list(ins), list(out_shapes), start, finish
        self.scratch = [pltpu.SemaphoreType.DMA((n_remote,)), pltpu.SemaphoreType.DMA((n_remote,)),
                        pltpu.SemaphoreType.DMA((max(n_local, 1),))]


def run_exchange(job, name):
    n_in, n_out = len(job.ins), len(job.out_shapes)

    def body(*refs):
        ins, outs, sems = refs[:n_in], refs[n_in:n_in + n_out], refs[n_in + n_out:]
        job.start(ins, outs, *sems)
        job.finish(ins, outs, *sems)

    hbm = pl.BlockSpec(memory_space=pltpu.HBM)
    return pl.pallas_call(body, name=name, in_specs=[hbm] * n_in, out_specs=[hbm] * n_out, out_shape=job.out_shapes,
                          scratch_shapes=job.scratch)(*job.ins)


def carry(job, body, grid, n_in, n_out):
    if job is None:
        return body, [], [], [], [], []
    nji, njo = len(job.ins), len(job.out_shapes)

    def carrying(*refs):
        ins, jins = refs[:n_in], refs[n_in:n_in + nji]
        outs, jouts = refs[n_in + nji:n_in + nji + n_out], refs[n_in + nji + n_out:n_in + nji + n_out + njo]
        rest = refs[n_in + nji + n_out + njo:]
        own, sems = rest[:len(rest) - 3], rest[len(rest) - 3:]
        ids = [pl.program_id(ax) for ax in range(len(grid))]
        first = functools.reduce(jnp.logical_and, [i == 0 for i in ids])
        last = functools.reduce(jnp.logical_and, [i == g - 1 for i, g in zip(ids, grid)])

        @pl.when(first)
        def _():
            job.start(jins, jouts, *sems)

        body(*ins, *outs, *own)

        @pl.when(last)
        def _():
            job.finish(jins, jouts, *sems)

    hbm = pl.BlockSpec(memory_space=pltpu.HBM)
    return carrying, [hbm] * nji, [hbm] * njo, job.out_shapes, job.scratch, job.ins


def gather_chips(shards):
    n = len(shards)
    cuts = [_split_dim(s.shape) for s in shards]

    def parts(ins, outs, send, recv):
        x, y, c, chips = _place()

        def half(ref, t, which, lead=()):
            size = shards[t].shape[cuts[t]] // 2
            return ref.at[lead + (slice(None),) * cuts[t] + (pl.ds(which * size, size),)]

        def copy(t, k, block, which, to, src=None):
            dst = half(outs[t], t, which, (block,))
            return pltpu.make_async_remote_copy(dst if src is None else src, dst, send.at[6 * t + k], recv.at[6 * t + k],
                                                device_id=to, device_id_type=MESH)

        me = 2 * x + y
        sends = [copy(t, k, me, c, (px, py, c), src=half(ins[t], t, c)) for t in range(n) for k, (px, py) in enumerate(chips)]
        return x, y, c, chips, me, copy, sends

    def start(ins, outs, send, recv, local):
        _, _, _, _, me, _, sends = parts(ins, outs, send, recv)
        for t in range(n):
            pltpu.make_async_copy(ins[t], outs[t].at[me], local.at[t]).start()
        for cp in sends:
            cp.start()

    def finish(ins, outs, send, recv, local):
        x, y, c, chips, me, copy, sends = parts(ins, outs, send, recv)
        passed = []
        for t in range(n):
            for k, (px, py) in enumerate(chips):
                copy(t, k, 2 * px + py, c, (px, py, c)).wait_recv()
                on = copy(t, 3 + k, 2 * px + py, c, (x, y, 1 - c))
                on.start()
                passed.append(on)
        for t in range(n):
            for k, (px, py) in enumerate(chips):
                copy(t, 3 + k, 2 * px + py, 1 - c, (x, y, 1 - c)).wait_recv()
        for cp in sends + passed:
            cp.wait_send()
        for t in range(n):
            pltpu.make_async_copy(ins[t], outs[t].at[me], local.at[t]).wait()

    return Exchange(shards, [jax.ShapeDtypeStruct((N_CHIPS,) + s.shape, s.dtype) for s in shards], 6 * n, n, start, finish)


def scatter_chips(blocked):
    n = len(blocked)

    def copies(ins, outs, send, recv):
        x, y, c, chips = _place()
        return [pltpu.make_async_remote_copy(ins[t].at[2 * px + py], outs[t].at[k], send.at[3 * t + k], recv.at[3 * t + k],
                                             device_id=(px, py, c), device_id_type=MESH)
                for t in range(n) for k, (px, py) in enumerate(chips)]

    def start(ins, outs, send, recv, local):
        for cp in copies(ins, outs, send, recv):
            cp.start()

    def finish(ins, outs, send, recv, local):
        for cp in copies(ins, outs, send, recv):
            cp.wait()

    return Exchange(blocked, [jax.ShapeDtypeStruct((3,) + b.shape[1:], b.dtype) for b in blocked], 3 * n, 0, start, finish)


def swap_cores(arrays):
    n = len(arrays)

    def body(*refs):
        ins, outs = refs[:n], refs[n:2 * n]
        send, recv = refs[2 * n:]
        x, y, c, _ = _place()
        pending = []
        for t in range(n):
            cp = pltpu.make_async_remote_copy(ins[t], outs[t], send.at[t], recv.at[t], device_id=(x, y, 1 - c),
                                              device_id_type=MESH)
            cp.start()
            pending.append(cp)
        for cp in pending:
            cp.wait()

    return _hbm_call(body, "swap_cores", arrays, [jax.ShapeDtypeStruct(a.shape, a.dtype) for a in arrays], n)


def gather_devices(buf):
    def body(in_ref, out_ref, send, recv, local):
        x, y, c, _ = _place()
        own = pltpu.make_async_copy(in_ref, out_ref.at[4 * x + 2 * y + c], local.at[0])
        own.start()
        pending = [own]
        for k in range(1, N_DEVICES):
            px = x ^ ((k >> 2) & 1)
            py = y ^ ((k >> 1) & 1)
            pc = c ^ (k & 1)
            going = pltpu.make_async_remote_copy(in_ref, out_ref.at[4 * x + 2 * y + c], send.at[k - 1], recv.at[k - 1],
                                                 device_id=(px, py, pc), device_id_type=MESH)
            going.start()
            pending.append(pltpu.make_async_remote_copy(in_ref, out_ref.at[4 * px + 2 * py + pc], send.at[k - 1], recv.at[k - 1],
                                                        device_id=(px, py, pc), device_id_type=MESH))
        for cp in pending:
            cp.wait()

    return _hbm_call(body, "gather_devices", [buf], [jax.ShapeDtypeStruct((N_DEVICES,) + buf.shape, buf.dtype)],
                     N_DEVICES - 1, 1)[0]


def _adamw(w, g, m, v):
    m = ADAM_B1 * m + (1.0 - ADAM_B1) * g
    v = ADAM_B2 * v + (1.0 - ADAM_B2) * (g * g)
    m_hat = m / (1.0 - ADAM_B1 ** ADAM_STEP)
    v_hat = v / (1.0 - ADAM_B2 ** ADAM_STEP)
    return -ADAM_LR * (m_hat / (jnp.sqrt(v_hat) + ADAM_EPS) + ADAM_WD * w), m, v


def _rows2d(a):
    return a.reshape(-1, a.shape[-1])


WEIGHTS = ["ffn1_norm", "ffn1_w_in", "ffn1_w_out", "mix_norm", "ffn2_norm", "ffn2_w_in", "ffn2_w_out", "final_norm",
           "s5_w_in", "s5_lam_re", "s5_lam_im", "s5_log_dt", "s5_b_re", "s5_b_im", "s5_c_re", "s5_c_im", "s5_d", "s5_w_out",
           "sb_w_qkv", "sb_w_out", "lru_w_in", "lru_conv_w", "lru_conv_b", "lru_w_a", "lru_b_a", "lru_w_x", "lru_b_x",
           "lru_lambda", "lru_w_out"]
INPUTS = ["x"] + WEIGHTS + ["loss_target"] + ["m_" + n for n in WEIGHTS] + ["v_" + n for n in WEIGHTS]
SHARDED_BIG = dict(ffn1_w_in=2, ffn1_w_out=1, ffn2_w_in=2, ffn2_w_out=1, s5_w_in=1, s5_w_out=2, sb_w_qkv=2, sb_w_out=1,
                   lru_w_in=2, lru_w_a=2, lru_w_x=2, lru_w_out=1)
SHARDED_SMALL = dict(s5_d=1, lru_conv_w=2, lru_conv_b=1, lru_b_a=2, lru_b_x=2, lru_lambda=1)
REPLICATED = [n for n in WEIGHTS if n not in SHARDED_BIG and n not in SHARDED_SMALL]
PACK_LANES = 128
PACK_ROW_ALIGN = 16
REPLICATED_ROW_TILE = 256


def _unblock(g, d):
    full = jnp.moveaxis(g, 0, d)
    return full.reshape(full.shape[:d] + (full.shape[d] * full.shape[d + 1],) + full.shape[d + 2:])


def _block(full, d):
    s = full.shape[d] // N_CHIPS
    return jnp.moveaxis(full.reshape(full.shape[:d] + (N_CHIPS, s) + full.shape[d + 1:]), d, 0)


def _pack(arrays, lead=(), row_align=PACK_ROW_ALIGN):
    nl = len(lead)
    flat = jnp.concatenate([a.reshape(lead + (-1,)) for a in arrays], axis=nl)
    quantum = PACK_LANES * row_align
    pad = (-flat.shape[nl]) % quantum
    flat = jnp.pad(flat, [(0, 0)] * nl + [(0, pad)])
    return flat.reshape(lead + (-1, PACK_LANES))


def _unpack(packed, shapes, lead=()):
    nl = len(lead)
    flat = packed.reshape(lead + (-1,))
    out, off = [], 0
    for s in shapes:
        size = math.prod(s)
        out.append(lax.slice_in_dim(flat, off, off + size, axis=nl).reshape(lead + tuple(s)))
        off += size
    return out


N_GROUPS = 3
MIXER_KIND = dict(s5=0, sb=1, lru=2)
S5_NAMES = dict(w_in="s5_w_in", lam_re="s5_lam_re", lam_im="s5_lam_im", log_dt="s5_log_dt", b_re="s5_b_re", b_im="s5_b_im",
                c_re="s5_c_re", c_im="s5_c_im", d="s5_d", w_out="s5_w_out")
SB_NAMES = dict(w_qkv="sb_w_qkv", w_out="sb_w_out")
LRU_NAMES = dict(w_in="lru_w_in", conv_w="lru_conv_w", conv_b="lru_conv_b", wa="lru_w_a", ba="lru_b_a", wx="lru_w_x",
                 bx="lru_b_x", lam="lru_lambda", w_out="lru_w_out")


def _pieces_of(name, count):
    kind = None if name.startswith("ffn") else MIXER_KIND[name.split("_")[0]]
    groups = [min(i if kind is None else kind + N_MIXERS * i, N_GROUPS - 1) for i in range(count)]
    runs, lo = [], 0
    for i in range(1, count + 1):
        if i == count or groups[i] != groups[lo]:
            runs.append((lo, i, groups[lo]))
            lo = i
    return runs


class _Sharded:
    def __init__(self, a):
        self.a = a
        self.pieces = {n: _pieces_of(n, a[n].shape[0]) for n in SHARDED_BIG}
        self.by_group = [[(n, lo, hi) for n in SHARDED_BIG for lo, hi, g in self.pieces[n] if g == grp]
                         for grp in range(N_GROUPS)]
        self.weights = {}
        self.small = {}
        self.grads = {n: [None] * a[n].shape[0] for n in SHARDED_BIG}
        self.small_grads = {}
        self.received = {}
        self.small_received = None

    def gather_job(self, grp):
        arrays = [self.a[n][lo:hi].astype(BF16) for n, lo, hi in self.by_group[grp]]
        if grp == 0:
            arrays.append(_pack([self.a[n] for n in SHARDED_SMALL]))
        return gather_chips(arrays)

    def landed(self, grp, gathered):
        for (n, lo, _), g in zip(self.by_group[grp], gathered):
            self.weights[(n, lo)] = _unblock(g, SHARDED_BIG[n])
        if grp == 0:
            blocks = _unpack(gathered[-1], [self.a[n].shape for n in SHARDED_SMALL], lead=(N_CHIPS,))
            self.small = {n: _unblock(b, d) for (n, d), b in zip(SHARDED_SMALL.items(), blocks)}

    def weight(self, name, idx):
        lo = next(lo for lo, hi, _ in self.pieces[name] if lo <= idx < hi)
        return self.weights[(name, lo)][idx - lo]

    def piece_grad(self, n, lo, hi):
        return jnp.stack(self.grads[n][lo:hi])

    def scatter_job(self, grp):
        arrays = [_block(self.piece_grad(n, lo, hi), SHARDED_BIG[n]).astype(BF16) for n, lo, hi in self.by_group[grp]]
        if grp == 0:
            arrays.append(_pack([_block(self.small_grads[n], d) for n, d in SHARDED_SMALL.items()], lead=(N_CHIPS,)))
        return scatter_chips(arrays)

    def arrived(self, grp, received):
        for (n, lo, _), r in zip(self.by_group[grp], received):
            self.received[(n, lo)] = r
        if grp == 0:
            self.small_received = received[-1]


def _forward_backward(x, target, a, sh):
    depth = a["ffn1_norm"].shape[0]

    def mixer_params(layer):
        kind, j = layer % N_MIXERS, layer // N_MIXERS
        if kind == 0:
            return kind, j, dict(w_in=sh.weight("s5_w_in", j), lam_re=a["s5_lam_re"][j], lam_im=a["s5_lam_im"][j],
                                 log_dt=a["s5_log_dt"][j], b_re=a["s5_b_re"][j], b_im=a["s5_b_im"][j], c_re=a["s5_c_re"][j],
                                 c_im=a["s5_c_im"][j], d=sh.small["s5_d"][j].reshape(1, -1), w_out=sh.weight("s5_w_out", j))
        if kind == 1:
            return kind, j, dict(w_qkv=sh.weight("sb_w_qkv", j), w_out=sh.weight("sb_w_out", j))
        sm = sh.small
        return kind, j, dict(w_in=sh.weight("lru_w_in", j), conv_w=sm["lru_conv_w"][j], conv_b=sm["lru_conv_b"][j].reshape(1, -1),
                             wa=sh.weight("lru_w_a", j), ba=sm["lru_b_a"][j].reshape(1, -1), wx=sh.weight("lru_w_x", j),
                             bx=sm["lru_b_x"][j].reshape(1, -1), lam=sm["lru_lambda"][j].reshape(1, -1),
                             w_out=sh.weight("lru_w_out", j))

    def ffn_weights(which, layer):
        return sh.weight(f"{which}_w_in", layer), sh.weight(f"{which}_w_out", layer)

    sh.landed(0, run_exchange(sh.gather_job(0), "gather_chips"))
    h = x
    tape = []
    for layer in range(depth):
        h, s1 = ffn_fwd(h, a["ffn1_norm"][layer], *ffn_weights("ffn1", layer), "ffn")
        kind, j, p = mixer_params(layer)
        h_mix_in = h
        hn = rms_fwd(h, a["mix_norm"][layer], "mix_norm")
        job = sh.gather_job(layer + 1) if layer + 1 < N_GROUPS else None
        if kind == 0:
            h, sm, got = s5_fwd(hn, h, p, job)
        elif kind == 1:
            o_flat, sm, got = sb_fwd(hn, p, job)
            h = mm(o_flat, p["w_out"], name="mix_out", extras=(h,), epilogue=lambda acc, res: res + acc)
        else:
            assert job is None
            y, sm = lru_fwd(hn, p)
            h = mm(y, p["w_out"], name="mix_out", extras=(h,), epilogue=lambda acc, res: res + acc)
        if job is not None:
            sh.landed(layer + 1, got)
        h, s2 = ffn_fwd(h, a["ffn2_norm"][layer], *ffn_weights("ffn2", layer), "ffn")
        tape.append((s1, h_mix_in, sm, s2))

    loss, dh, g_final = loss_fwd_bwd(h, a["final_norm"], target)

    norm_grads = {n: [None] * depth for n in ("ffn1_norm", "mix_norm", "ffn2_norm")}
    mix = {}
    for layer in reversed(range(depth)):
        s1, h_mix_in, sm, s2 = tape[layer]
        dh, dg, dwi, dwo = ffn_bwd(dh, s2, a["ffn2_norm"][layer], *ffn_weights("ffn2", layer), "ffn")
        norm_grads["ffn2_norm"][layer], sh.grads["ffn2_w_in"][layer], sh.grads["ffn2_w_out"][layer] = dg, dwi, dwo
        kind, j, p = mixer_params(layer)
        job = sh.scatter_job(layer + 1) if layer + 1 < N_GROUPS else None
        if kind == 0:
            (da, w_first), g, got = s5_bwd(dh, sm, p, job)
            names = S5_NAMES
        elif kind == 1:
            (da, w_first), g, got = sb_bwd(dh, sm, p, job)
            names = SB_NAMES
        else:
            assert job is None
            (da, w_first), g = lru_bwd(dh, sm, p)
            names = LRU_NAMES
        if job is not None:
            sh.arrived(layer + 1, got)
        for k, full_name in names.items():
            if full_name in SHARDED_BIG:
                sh.grads[full_name][j] = g[k].reshape(sh.weight(full_name, j).shape)
            else:
                mix.setdefault(full_name, {})[j] = g[k].reshape(a[full_name].shape[1:-1] + (-1,))
        dh, dg = norm_input_bwd(da, _rows_spec, 1, w_first, h_mix_in, a["mix_norm"][layer], dh, "mix_dhn")
        norm_grads["mix_norm"][layer] = dg
        dh, dg, dwi, dwo = ffn_bwd(dh, s1, a["ffn1_norm"][layer], *ffn_weights("ffn1", layer), "ffn")
        norm_grads["ffn1_norm"][layer], sh.grads["ffn1_w_in"][layer], sh.grads["ffn1_w_out"][layer] = dg, dwi, dwo

    grads = {n: jnp.stack(v) for n, v in norm_grads.items()}
    grads["final_norm"] = g_final
    for n, by_j in mix.items():
        stacked = jnp.stack([by_j[j] for j in range(len(by_j))])
        if n in SHARDED_SMALL:
            sh.small_grads[n] = stacked
        else:
            grads[n] = stacked
    sh.arrived(0, run_exchange(sh.scatter_job(0), "scatter_chips"))
    return loss, dh, grads


def kernel(x, ffn1_norm, ffn1_w_in, ffn1_w_out, mix_norm, ffn2_norm, ffn2_w_in, ffn2_w_out, final_norm, s5_w_in,
           s5_lam_re, s5_lam_im, s5_log_dt, s5_b_re, s5_b_im, s5_c_re, s5_c_im, s5_d, s5_w_out, sb_w_qkv,
           sb_w_out, lru_w_in, lru_conv_w, lru_conv_b, lru_w_a, lru_b_a, lru_w_x, lru_b_x, lru_lambda,
           lru_w_out, loss_target, m_ffn1_norm, m_ffn1_w_in, m_ffn1_w_out, m_mix_norm, m_ffn2_norm,
           m_ffn2_w_in, m_ffn2_w_out, m_final_norm, m_s5_w_in, m_s5_lam_re, m_s5_lam_im, m_s5_log_dt,
           m_s5_b_re, m_s5_b_im, m_s5_c_re, m_s5_c_im, m_s5_d, m_s5_w_out, m_sb_w_qkv, m_sb_w_out, m_lru_w_in,
           m_lru_conv_w, m_lru_conv_b, m_lru_w_a, m_lru_b_a, m_lru_w_x, m_lru_b_x, m_lru_lambda, m_lru_w_out,
           v_ffn1_norm, v_ffn1_w_in, v_ffn1_w_out, v_mix_norm, v_ffn2_norm, v_ffn2_w_in, v_ffn2_w_out,
           v_final_norm, v_s5_w_in, v_s5_lam_re, v_s5_lam_im, v_s5_log_dt, v_s5_b_re, v_s5_b_im, v_s5_c_re,
           v_s5_c_im, v_s5_d, v_s5_w_out, v_sb_w_qkv, v_sb_w_out, v_lru_w_in, v_lru_conv_w, v_lru_conv_b,
           v_lru_w_a, v_lru_b_a, v_lru_w_x, v_lru_b_x, v_lru_lambda, v_lru_w_out):
    a = dict(locals())
    assert list(a) == INPUTS
    x, y, c, _ = _place()
    chip = 2 * x + y
    everyone = ("x", "y", "c")

    sh = _Sharded(a)
    loss, dx, grads = _forward_backward(a["x"][0], a["loss_target"][0], a, sh)
    loss = lax.psum(loss, everyone)

    small = list(SHARDED_SMALL)

    def sum_chips(mine, got):
        rows = _rows2d(mine)
        return rowwise(lambda o, r: ((o + r[0].astype(F32)) + r[1].astype(F32)) + r[2].astype(F32),
                       [rows, got.reshape((3,) + rows.shape)], [], [(rows.shape[1], F32)], name="sum_chips")

    def own_block(full, name, d):
        return lax.dynamic_slice_in_dim(full, chip * a[name].shape[d], a[name].shape[d], axis=d)

    keys = [(n, lo, hi) for n in SHARDED_BIG for lo, hi, _ in sh.pieces[n]]
    partial = [sum_chips(own_block(sh.piece_grad(n, lo, hi), n, SHARDED_BIG[n]), sh.received[(n, lo)]) for n, lo, hi in keys]
    partial.append(sum_chips(_pack([own_block(sh.small_grads[n], n, d) for n, d in SHARDED_SMALL.items()]), sh.small_received))
    other = swap_cores(partial)

    def adam_sharded(wv, ga, gb, m, v):
        shape = wv.shape
        res = rowwise(lambda w_, a_, b_, mm_, vv_: (a_ + b_,) + _adamw(w_, a_ + b_, mm_, vv_),
                      [_rows2d(wv), ga, gb, _rows2d(m), _rows2d(v)], [], [(shape[-1], F32)] * 4, name="adamw_sharded")
        return [r.reshape(shape) for r in res]

    out_grad, out_delta, out_m, out_v = {}, {}, {}, {}
    for n in SHARDED_BIG:
        at = [i for i, k in enumerate(keys) if k[0] == n]
        out_grad[n], out_delta[n], out_m[n], out_v[n] = adam_sharded(a[n], [partial[i] for i in at], [other[i] for i in at],
                                                                     a["m_" + n], a["v_" + n])
    small_shapes = [a[n].shape for n in small]
    sp = [_pack([a[pre + n] for n in small]) for pre in ("", "m_", "v_")]
    g_, d_, m_, v_ = adam_sharded(sp[0], [partial[-1]], [other[-1]], sp[1], sp[2])
    for n, gg, dd, mm_, vv in zip(small, _unpack(g_, small_shapes), _unpack(d_, small_shapes), _unpack(m_, small_shapes),
                                  _unpack(v_, small_shapes)):
        out_grad[n], out_delta[n], out_m[n], out_v[n] = gg, dd, mm_, vv

    rep_shapes = [a[n].shape for n in REPLICATED]
    rep_all = gather_devices(_pack([grads[n] for n in REPLICATED], row_align=REPLICATED_ROW_TILE))
    rp = [_pack([a[pre + n] for n in REPLICATED], row_align=REPLICATED_ROW_TILE) for pre in ("", "m_", "v_")]

    def adam_rep(w_, g8, mm_, vv_):
        g = g8[0]
        for k in range(1, N_DEVICES):
            g = g + g8[k]
        return (g,) + _adamw(w_, g, mm_, vv_)

    g_, d_, m_, v_ = rowwise(adam_rep, [rp[0], rep_all, rp[1], rp[2]], [], [(PACK_LANES, F32)] * 4, name="adamw_replicated")
    for n, gg, dd, mm_, vv in zip(REPLICATED, _unpack(g_, rep_shapes), _unpack(d_, rep_shapes), _unpack(m_, rep_shapes),
                                  _unpack(v_, rep_shapes)):
        out_grad[n], out_delta[n], out_m[n], out_v[n] = gg, dd, mm_, vv

    return (loss, dx[None], *[out_grad[n] for n in WEIGHTS], *[out_delta[n] for n in WEIGHTS], *[out_m[n] for n in WEIGHTS],
            *[out_v[n] for n in WEIGHTS])
```

```python
import functools
import math

import jax
import jax.numpy as jnp
from jax import lax
from jax.experimental import pallas as pl
from jax.experimental.pallas import tpu as pltpu

F32 = jnp.float32
BF16 = jnp.bfloat16

VMEM_LIMIT_BYTES = 56 * 1024 * 1024

RMS_EPS = 1e-6
D_FF = 2816
S5_GROUP = 16
S5_STATE = 64
SB_HEAD_DIM = 64
LRU_BLOCK_WIDTH = 256
LRU_CONV = 4
LRU_C = 8.0
N_MIXERS = 3

ADAM_LR = 0.001
ADAM_B1 = 0.9
ADAM_B2 = 0.999
ADAM_EPS = 1e-08
ADAM_WD = 0.01
ADAM_STEP = 10


def _params(semantics):
    return pltpu.CompilerParams(dimension_semantics=semantics, vmem_limit_bytes=VMEM_LIMIT_BYTES)


def _tile(dim, prefs):
    for t in prefs:
        if t <= dim and dim % t == 0:
            return t
    return dim


def _mm_call(name, grid, a, a_spec, b, b_spec, dims, extras, outs, epilogue, acc_shape):
    n_extra, n_out, n_k = len(extras), len(outs), grid[-1]

    def body(a_ref, b_ref, *rest):
        extra_refs = rest[:n_extra]
        out_refs = rest[n_extra:n_extra + n_out]
        acc_ref = rest[n_extra + n_out]
        k = pl.program_id(len(grid) - 1)
        part = lax.dot_general(a_ref[...].astype(BF16), b_ref[...].astype(BF16), dims, preferred_element_type=F32)

        @pl.when(k == 0)
        def _():
            acc_ref[...] = part

        @pl.when(k > 0)
        def _():
            acc_ref[...] += part

        @pl.when(k == n_k - 1)
        def _():
            res = epilogue(acc_ref[...], *[r[...] for r in extra_refs])
            if not isinstance(res, (tuple, list)):
                res = (res,)
            for o_ref, r in zip(out_refs, res):
                o_ref[...] = r.astype(o_ref.dtype)

    sem = ("parallel",) * (len(grid) - 1) + ("arbitrary",)
    res = pl.pallas_call(
        body,
        name=name,
        grid=grid,
        in_specs=[a_spec, b_spec] + [s for _, s in extras],
        out_specs=[s for _, s in outs],
        out_shape=[s for s, _ in outs],
        scratch_shapes=[pltpu.VMEM(acc_shape, F32)],
        compiler_params=_params(sem),
    )(a, b, *[x for x, _ in extras])
    return res


def _fit(dim, target, align=128):
    best = None
    for t in range(align, min(dim, target) + 1, align):
        if dim % t == 0:
            best = t
    return best or dim


def mm(a, b, *, name, ta=False, tb=False, extras=(), epilogue=None, out_dtypes=(F32,), tm=1024, tn=1024, tk=1024):
    m, kdim = (a.shape[1], a.shape[0]) if ta else a.shape
    n = b.shape[0] if tb else b.shape[1]
    tm = _fit(m, tm)
    tn = _fit(n, tn)
    tk = _fit(kdim, tk)
    grid = (m // tm, n // tn, kdim // tk)
    a_spec = pl.BlockSpec((tk, tm), lambda i, j, k: (k, i)) if ta else pl.BlockSpec((tm, tk), lambda i, j, k: (i, k))
    b_spec = pl.BlockSpec((tn, tk), lambda i, j, k: (j, k)) if tb else pl.BlockSpec((tk, tn), lambda i, j, k: (k, j))
    dims = (((0 if ta else 1,), (1 if tb else 0,)), ((), ()))
    o_spec = pl.BlockSpec((tm, tn), lambda i, j, k: (i, j))
    if epilogue is None:
        epilogue = lambda acc: acc
    res = _mm_call(name, grid, a, a_spec, b, b_spec, dims, [(x, o_spec) for x in extras],
                   [(jax.ShapeDtypeStruct((m, n), dt), o_spec) for dt in out_dtypes], epilogue, (tm, tn))
    return res[0] if len(res) == 1 else res


def rowwise(fn, rows, consts, outs, sums=(), *, name, tm=256):
    m = rows[0].shape[0]
    pieces = [x if isinstance(x, (list, tuple)) else [x] for x in rows]
    tm = _tile(math.gcd(*[p.shape[-2] for ps in pieces for p in ps]), (tm, 128, 64, 32, 16, 8))
    starts = [[sum(q.shape[-2] for q in ps[:k]) // tm for k in range(len(ps) + 1)] for ps in pieces]
    flat = [p for ps in pieces for p in ps]
    n_rows, n_consts, n_outs = len(flat), len(consts), len(outs)

    def body(*refs):
        i = pl.program_id(0)
        in_vals, at = [], 0
        for ps, st in zip(pieces, starts):
            val = refs[at + len(ps) - 1][...]
            for k in reversed(range(len(ps) - 1)):
                val = jnp.where(i < st[k + 1], refs[at + k][...], val)
            in_vals.append(val)
            at += len(ps)
        in_vals += [r[...] for r in refs[n_rows:n_rows + n_consts]]
        out_refs = refs[n_rows + n_consts:n_rows + n_consts + n_outs]
        sum_refs = refs[n_rows + n_consts + n_outs:]
        res = fn(*in_vals)
        if not isinstance(res, (tuple, list)):
            res = (res,)
        for o_ref, r in zip(out_refs, res[:n_outs]):
            o_ref[...] = r.astype(o_ref.dtype)
        if sum_refs:
            first = pl.program_id(0) == 0

            @pl.when(first)
            def _():
                for s_ref, r in zip(sum_refs, res[n_outs:]):
                    s_ref[...] = r.astype(F32)

            @pl.when(jnp.logical_not(first))
            def _():
                for s_ref, r in zip(sum_refs, res[n_outs:]):
                    s_ref[...] += r.astype(F32)

    def whole(shape):
        nd = len(shape)
        return pl.BlockSpec(shape, lambda i: (0,) * nd)

    def row_spec(x, lo, hi):
        at = lambda i: jnp.clip(i - lo, 0, hi - lo - 1)
        if x.ndim == 3:
            return pl.BlockSpec((x.shape[0], tm, x.shape[2]), lambda i: (0, at(i), 0))
        return pl.BlockSpec((tm, x.shape[1]), lambda i: (at(i), 0))

    in_specs = [row_spec(p, st[k], st[k + 1]) for ps, st in zip(pieces, starts) for k, p in enumerate(ps)]
    in_specs += [whole(c.shape) for c in consts]
    out_specs = [pl.BlockSpec((tm, nc), lambda i: (i, 0)) for nc, _ in outs] + [whole(tuple(s)) for s in sums]
    out_shape = [jax.ShapeDtypeStruct((m, nc), dt) for nc, dt in outs] + [jax.ShapeDtypeStruct(tuple(s), F32) for s in sums]
    res = pl.pallas_call(
        body,
        name=name,
        grid=(m // tm,),
        in_specs=in_specs,
        out_specs=out_specs,
        out_shape=out_shape,
        compiler_params=_params(("arbitrary",) if sums else ("parallel",)),
    )(*flat, *consts)
    return res[0] if len(res) == 1 else res


def _rms(h, g):
    return h * lax.rsqrt(jnp.mean(h * h, axis=-1, keepdims=True) + RMS_EPS) * g


def _sigmoid(x):
    return 1.0 / (1.0 + jnp.exp(-x))


def _silu_mul(g, u):
    return g * _sigmoid(g) * u


def _gelu(x):
    return 0.5 * x * (1.0 + jnp.tanh(math.sqrt(2.0 / math.pi) * (x + 0.044715 * (x * x * x))))


def _softplus(x):
    return jnp.maximum(x, 0.0) + jnp.log(1.0 + jnp.exp(-jnp.abs(x)))


def rms_fwd(h, g, name):
    return rowwise(lambda x, gg: _rms(x, gg), [h], [g.reshape(1, -1)], [(h.shape[1], BF16)], name=name)


def norm_input_bwd(a, a_spec, n_k, w, h, g, dres, name, tm=512):
    m, d = h.shape
    tk = w.shape[1] // n_k
    tm = _fit(m, tm)

    def body(a_ref, w_ref, h_ref, g_ref, dres_ref, dh_ref, dg_ref, acc_ref):
        i, k = pl.program_id(0), pl.program_id(1)
        part = lax.dot_general(a_ref[...].astype(BF16), w_ref[...], _NT, preferred_element_type=F32)

        @pl.when(k == 0)
        def _():
            acc_ref[...] = part

        @pl.when(k > 0)
        def _():
            acc_ref[...] += part

        @pl.when(k == n_k - 1)
        def _():
            _, vjp = jax.vjp(_rms, h_ref[...], g_ref[...])
            dx, dg = vjp(acc_ref[...])
            dh_ref[...] = dres_ref[...] + dx

            @pl.when(i == 0)
            def _():
                dg_ref[...] = dg

            @pl.when(i > 0)
            def _():
                dg_ref[...] += dg

    row = pl.BlockSpec((tm, d), lambda i, k: (i, 0))
    vec = pl.BlockSpec((1, d), lambda i, k: (0, 0))
    dh, dg = pl.pallas_call(
        body,
        name=name,
        grid=(m // tm, n_k),
        in_specs=[a_spec(tm, tk), pl.BlockSpec((d, tk), lambda i, k: (0, k)), row, vec, row],
        out_specs=[row, vec],
        out_shape=[jax.ShapeDtypeStruct((m, d), F32), jax.ShapeDtypeStruct((1, d), F32)],
        scratch_shapes=[pltpu.VMEM((tm, d), F32)],
        compiler_params=_params(("arbitrary", "arbitrary")),
    )(a, w, h, g.reshape(1, d), dres)
    return dh, dg.reshape(-1)


def _rows_spec(tm, tk):
    return pl.BlockSpec((tm, tk), lambda i, k: (i, k))


def ffn_in(h, g, w_in, tag):
    m, d = h.shape
    f = w_in.shape[1] // 2
    tm, tn = _fit(m, 512), _fit(f, 1408)
    nj = f // tn

    def body(h_ref, g_ref, wg_ref, wu_ref, hn_ref, gu_ref, act_ref, hn_scr):
        @pl.when(pl.program_id(1) == 0)
        def _():
            hn = _rms(h_ref[...], g_ref[...]).astype(BF16)
            hn_scr[...] = hn
            hn_ref[...] = hn

        a = hn_scr[...]
        gate = jnp.dot(a, wg_ref[...], preferred_element_type=F32)
        up = jnp.dot(a, wu_ref[...], preferred_element_type=F32)
        gu_ref[0] = gate.astype(BF16)
        gu_ref[1] = up.astype(BF16)
        act_ref[...] = _silu_mul(gate, up).astype(BF16)

    return pl.pallas_call(
        body,
        name=f"{tag}_in",
        grid=(m // tm, nj),
        in_specs=[pl.BlockSpec((tm, d), lambda i, j: (i, 0)), pl.BlockSpec((1, d), lambda i, j: (0, 0)),
                  pl.BlockSpec((d, tn), lambda i, j: (0, j)), pl.BlockSpec((d, tn), lambda i, j: (0, nj + j))],
        out_specs=[pl.BlockSpec((tm, d), lambda i, j: (i, 0)), pl.BlockSpec((2, tm, tn), lambda i, j: (0, i, j)),
                   pl.BlockSpec((tm, tn), lambda i, j: (i, j))],
        out_shape=[jax.ShapeDtypeStruct((m, d), BF16), jax.ShapeDtypeStruct((2, m, f), BF16), jax.ShapeDtypeStruct((m, f), BF16)],
        scratch_shapes=[pltpu.VMEM((tm, d), BF16)],
        compiler_params=_params(("parallel", "arbitrary")),
    )(h, g.reshape(1, d), w_in, w_in)


def ffn_fwd(h, g, w_in, w_out, tag):
    hn, gu, act = ffn_in(h, g, w_in, tag)
    h_new = mm(act, w_out, name=f"{tag}_out", extras=(h,), epilogue=lambda acc, res: res + 0.5 * acc, tm=512, tk=2816)
    return h_new, (h, hn, gu, act)


def ffn_bwd(dh_new, saved, g, w_in, w_out, tag):
    h, hn, gu, act = saved
    m, d = h.shape
    f = w_out.shape[0]

    def act_bwd(acc, gu_blk):
        _, vjp = jax.vjp(_silu_mul, gu_blk[0].astype(F32), gu_blk[1].astype(F32))
        return jnp.stack(vjp(0.5 * acc))

    tm, tn = _fit(m, 512), _fit(f, 1408)
    pair = pl.BlockSpec((2, tm, tn), lambda i, j, k: (0, i, j))
    dgu = _mm_call(f"{tag}_dgu", (m // tm, f // tn, 1), dh_new, pl.BlockSpec((tm, d), lambda i, j, k: (i, 0)), w_out,
                   pl.BlockSpec((tn, d), lambda i, j, k: (j, 0)), _NT, [(gu, pair)],
                   [(jax.ShapeDtypeStruct((2, m, f), BF16), pair)], act_bwd, (tm, tn))[0]
    dw_out = mm(act, dh_new, ta=True, name=f"{tag}_dwout", epilogue=lambda acc: 0.5 * acc, tm=1408)

    tn, tk = _fit(f, 1408), _fit(m, 2048)
    nh = f // tn
    dw_in = _mm_call(f"{tag}_dwin", (1, 2 * nh, m // tk), hn, pl.BlockSpec((tk, d), lambda i, j, k: (k, 0)), dgu,
                     pl.BlockSpec((None, tk, tn), lambda i, j, k: (j // nh, k, j % nh)), _TN, [],
                     [(jax.ShapeDtypeStruct((d, 2 * f), F32), pl.BlockSpec((d, tn), lambda i, j, k: (0, j)))],
                     lambda acc: acc, (d, tn))[0]

    nkh = f // _fit(f, 2816)
    dh, dg = norm_input_bwd(dgu, lambda tm, tk: pl.BlockSpec((None, tm, tk), lambda i, k: (k // nkh, i, k % nkh)), 2 * nkh,
                            w_in, h, g, dh_new, f"{tag}_dhn")
    return dh, dg, dw_in, dw_out


def loss_fwd_bwd(h, g, target):
    d = h.shape[1]

    def fn(x, t, gg):
        y, vjp = jax.vjp(_rms, x, gg)
        err = y - t
        dx, dg = vjp(err * (1.0 / d))
        part = 0.5 * jnp.sum(jnp.sum(err * err, axis=1, keepdims=True), axis=0, keepdims=True) * (1.0 / d)
        return dx, dg, jnp.broadcast_to(part, (1, 128))

    dh, dg, loss = rowwise(fn, [h, target], [g.reshape(1, -1)], [(d, F32)], [(1, d), (1, 128)], name="loss_head")
    return loss[0, 0], dh, dg.reshape(-1)


def _shift_down(v, d, fill):
    rows = lax.broadcasted_iota(jnp.int32, v.shape, 0)
    return jnp.where(rows < d, fill, pltpu.roll(v, d, 0))


def _shift_up(v, d, fill):
    t = v.shape[0]
    rows = lax.broadcasted_iota(jnp.int32, v.shape, 0)
    return jnp.where(rows >= t - d, fill, pltpu.roll(v, t - d, 0))


def _scan_fwd(a, x):
    d = 1
    while d < a.shape[0]:
        x = x + a * _shift_down(x, d, 0.0)
        a = a * _shift_down(a, d, 1.0)
        d *= 2
    return a, x


def _scan_bwd(b, x):
    d = 1
    while d < b.shape[0]:
        x = x + b * _shift_up(x, d, 0.0)
        b = b * _shift_up(b, d, 1.0)
        d *= 2
    return b, x


def _rows_before(cur, prev8, s):
    r = pltpu.roll(cur, s, 0)
    p = pltpu.roll(prev8, s, 0)
    rows = lax.broadcasted_iota(jnp.int32, prev8.shape, 0)
    return jnp.concatenate([jnp.where(rows < s, p, r[:8]), r[8:]], axis=0)


def _rows_after(cur, next8, s):
    t = cur.shape[0]
    r = pltpu.roll(cur, t - s, 0)
    p = pltpu.roll(next8, 8 - s, 0)
    rows = lax.broadcasted_iota(jnp.int32, next8.shape, 0)
    return jnp.concatenate([r[:t - 8], jnp.where(rows >= 8 - s, p, r[t - 8:])], axis=0)


LRU_CHUNK = 256


def _neg_expm1(y):
    small = -y * (1.0 + 0.5 * y * (1.0 + y * (1.0 / 3.0)))
    return jnp.where(y > -0.01, small, 1.0 - jnp.exp(y))


def _lru_gate(xc, pre_a, pre_x, lam):
    r = _sigmoid(pre_a)
    ig = _sigmoid(pre_x)
    log_a = (-LRU_C * r) * _softplus(-lam)
    return jnp.exp(log_a), (ig * xc) * jnp.sqrt(_neg_expm1(2.0 * log_a))


def _lru_conv(br, prev8, conv_w, conv_b):
    taps = [br] + [_rows_before(br, prev8, s) for s in range(1, LRU_CONV)]
    xc = conv_b
    for k in range(LRU_CONV):
        xc = xc + conv_w[k:k + 1, :] * taps[LRU_CONV - 1 - k]
    return xc, taps


def _lru_pre(xcb, w_ref, bias):
    nb = w_ref.shape[0]
    bw = w_ref.shape[1]
    return jnp.concatenate(
        [jnp.dot(xcb[:, n * bw:(n + 1) * bw], w_ref[n], preferred_element_type=F32) for n in range(nb)], axis=1) + bias


def lru_scan_fwd(bgr, conv_w, conv_b, wa, ba, wx, bx, lam):
    l, w2 = bgr.shape
    w = w2 // 2
    t = _tile(l, (LRU_CHUNK, 128, 64, 32, 16, 8))

    def body(bg_ref, br_ref, cw_ref, cb_ref, wa_ref, ba_ref, wx_ref, bx_ref, lam_ref, y_ref, h_ref, tail_ref, hprev_ref):
        @pl.when(pl.program_id(0) == 0)
        def _():
            tail_ref[...] = jnp.zeros_like(tail_ref)
            hprev_ref[...] = jnp.zeros_like(hprev_ref)

        br = br_ref[...]
        xc, _ = _lru_conv(br, tail_ref[...], cw_ref[...], cb_ref[...])
        xcb = xc.astype(BF16)
        a, gx = _lru_gate(xc, _lru_pre(xcb, wa_ref, ba_ref[...]), _lru_pre(xcb, wx_ref, bx_ref[...]), lam_ref[...])
        acum, x = _scan_fwd(a, gx)
        h = x + acum * hprev_ref[pl.ds(7, 1), :]
        y_ref[...] = (_gelu(bg_ref[...]) * h).astype(y_ref.dtype)
        h_ref[...] = h
        tail_ref[...] = br[t - 8:, :]
        hprev_ref[...] = h[t - 8:, :]

    def whole(x):
        nd = x.ndim
        return pl.BlockSpec(x.shape, lambda i: (0,) * nd)

    consts = [conv_w, conv_b, wa, ba, wx, bx, lam]
    return pl.pallas_call(
        body,
        name="lru_scan_fwd",
        grid=(l // t,),
        in_specs=[pl.BlockSpec((t, w), lambda i: (i, 0)), pl.BlockSpec((t, w), lambda i: (i, 1))] + [whole(c) for c in consts],
        out_specs=[pl.BlockSpec((t, w), lambda i: (i, 0)), pl.BlockSpec((t, w), lambda i: (i, 0))],
        out_shape=[jax.ShapeDtypeStruct((l, w), BF16), jax.ShapeDtypeStruct((l, w), F32)],
        scratch_shapes=[pltpu.VMEM((8, w), F32), pltpu.VMEM((8, w), F32)],
        compiler_params=_params(("arbitrary",)),
    )(bgr, bgr, *consts)


def lru_scan_bwd(dy, bgr, hseq, conv_w, conv_b, wa, ba, wx, bx, lam):
    l, w2 = bgr.shape
    w = w2 // 2
    t = _tile(l, (LRU_CHUNK, 128, 64, 32, 16, 8))
    nc = l // t
    nb, bw = wa.shape[0], wa.shape[1]

    def body(dy_ref, bg_ref, br_ref, brh_ref, h_ref, hh_ref, cw_ref, cb_ref, wa_ref, ba_ref, wx_ref, bx_ref, lam_ref,
             dbgr_ref, dcw_ref, dcb_ref, dwa_ref, dba_ref, dwx_ref, dbx_ref, dlam_ref, dxcn_ref, carry_ref):
        i = pl.program_id(0)
        has_prev = (i < nc - 1).astype(F32)

        @pl.when(i == 0)
        def _():
            dxcn_ref[...] = jnp.zeros_like(dxcn_ref)
            carry_ref[...] = jnp.zeros_like(carry_ref)

        br = br_ref[...]
        cw = cw_ref[...]
        xc, taps = _lru_conv(br, brh_ref[...] * has_prev, cw, cb_ref[...])
        xcb = xc.astype(BF16)
        (a, _), gate_vjp = jax.vjp(_lru_gate, xc, _lru_pre(xcb, wa_ref, ba_ref[...]), _lru_pre(xcb, wx_ref, bx_ref[...]),
                                   lam_ref[...])
        hs = h_ref[...]
        _, out_vjp = jax.vjp(lambda g_, h_: _gelu(g_) * h_, bg_ref[...], hs)
        dbg, dhs = out_vjp(dy_ref[...])
        bcum, x = _scan_bwd(_shift_up(a, 1, 1.0), dhs)
        dh = x + bcum * carry_ref[pl.ds(0, 1), :]
        da = dh * _rows_before(hs, hh_ref[...] * has_prev, 1)
        dxc, dpa, dpx, dlam = gate_vjp((da, dh))
        dpab, dpxb = dpa.astype(BF16), dpx.astype(BF16)
        nt = (((1,), (1,)), ((), ()))
        tn = (((0,), (0,)), ((), ()))
        dxb, dwa, dwx = [], [], []
        for n in range(nb):
            sl = slice(n * bw, (n + 1) * bw)
            dxb.append(lax.dot_general(dpab[:, sl], wa_ref[n], nt, preferred_element_type=F32)
                       + lax.dot_general(dpxb[:, sl], wx_ref[n], nt, preferred_element_type=F32))
            dwa.append(lax.dot_general(xcb[:, sl], dpab[:, sl], tn, preferred_element_type=F32))
            dwx.append(lax.dot_general(xcb[:, sl], dpxb[:, sl], tn, preferred_element_type=F32))
        dxc = dxc + jnp.concatenate(dxb, axis=1)
        ups = [dxc] + [_rows_after(dxc, dxcn_ref[...], s) for s in range(1, LRU_CONV)]
        dbr = cw[LRU_CONV - 1:LRU_CONV, :] * ups[0]
        for k in range(LRU_CONV - 1):
            dbr = dbr + cw[k:k + 1, :] * ups[LRU_CONV - 1 - k]
        dbgr_ref[:, :w] = dbg.astype(dbgr_ref.dtype)
        dbgr_ref[:, w:] = dbr.astype(dbgr_ref.dtype)
        dcw = jnp.concatenate([jnp.sum(dxc * taps[LRU_CONV - 1 - k], axis=0, keepdims=True) for k in range(LRU_CONV)], axis=0)
        sums = [(dcw_ref, dcw), (dcb_ref, jnp.sum(dxc, axis=0, keepdims=True)), (dwa_ref, jnp.stack(dwa)),
                (dba_ref, jnp.sum(dpa, axis=0, keepdims=True)), (dwx_ref, jnp.stack(dwx)),
                (dbx_ref, jnp.sum(dpx, axis=0, keepdims=True)), (dlam_ref, dlam)]

        @pl.when(i == 0)
        def _():
            for ref, val in sums:
                ref[...] = val

        @pl.when(i > 0)
        def _():
            for ref, val in sums:
                ref[...] += val

        dxcn_ref[...] = dxc[:8, :]
        carry_ref[...] = (a * dh)[:8, :]

    def whole(shape):
        nd = len(shape)
        return pl.BlockSpec(tuple(shape), lambda i: (0,) * nd)

    consts = [conv_w, conv_b, wa, ba, wx, bx, lam]
    t8 = t // 8
    rev = lambda i: nc - 1 - i
    halo = lambda i: jnp.maximum(rev(i) * t8 - 1, 0)
    in_specs = [
        pl.BlockSpec((t, w), lambda i: (rev(i), 0)),
        pl.BlockSpec((t, w), lambda i: (rev(i), 0)),
        pl.BlockSpec((t, w), lambda i: (rev(i), 1)),
        pl.BlockSpec((8, w), lambda i: (halo(i), 1)),
        pl.BlockSpec((t, w), lambda i: (rev(i), 0)),
        pl.BlockSpec((8, w), lambda i: (halo(i), 0)),
    ] + [whole(c.shape) for c in consts]
    sum_shapes = [conv_w.shape, conv_b.shape, wa.shape, ba.shape, wx.shape, bx.shape, lam.shape]
    res = pl.pallas_call(
        body,
        name="lru_scan_bwd",
        grid=(nc,),
        in_specs=in_specs,
        out_specs=[pl.BlockSpec((t, w2), lambda i: (rev(i), 0))] + [whole(s) for s in sum_shapes],
        out_shape=[jax.ShapeDtypeStruct((l, w2), BF16)] + [jax.ShapeDtypeStruct(tuple(s), F32) for s in sum_shapes],
        scratch_shapes=[pltpu.VMEM((8, w), F32), pltpu.VMEM((8, w), F32)],
        compiler_params=_params(("arbitrary",)),
    )(dy, bgr, bgr, bgr, hseq, hseq, *consts)
    return res[0], res[1:]


def lru_fwd(hn, p):
    bgr = mm(hn, p["w_in"], name="lru_in")
    y, hseq = lru_scan_fwd(bgr, p["conv_w"], p["conv_b"], p["wa"], p["ba"], p["wx"], p["bx"], p["lam"])
    return y, (hn, bgr, hseq, y)


def lru_bwd(dmixed, saved, p):
    hn, bgr, hseq, y = saved
    dy = mm(dmixed, p["w_out"], tb=True, name="lru_dy")
    dw_out = mm(y, dmixed, ta=True, name="lru_dwout")
    dbgr, (dcw, dcb, dwa, dba, dwx, dbx, dlam) = lru_scan_bwd(dy, bgr, hseq, p["conv_w"], p["conv_b"], p["wa"], p["ba"],
                                                               p["wx"], p["bx"], p["lam"])
    dw_in = mm(hn, dbgr, ta=True, name="lru_dwin")
    grads = dict(w_in=dw_in, conv_w=dcw, conv_b=dcb, wa=dwa, ba=dba, wx=dwx, bx=dbx, lam=dlam, w_out=dw_out)
    return (dbgr, p["w_in"]), grads


SB_BLOCK = 256
SB_BLOCK_Q = 1024
_NT = (((1,), (1,)), ((), ()))
_TN = (((0,), (0,)), ((), ()))


def _running_sums(x, tri, total_col):
    xb = x.astype(BF16)
    run = jnp.dot(xb, tri, preferred_element_type=F32)
    return run, xb, run[:, total_col:total_col + 1]


def _tri(n, cmp):
    r = lax.broadcasted_iota(jnp.int32, (n, n), 0)
    c = lax.broadcasted_iota(jnp.int32, (n, n), 1)
    return cmp(r, c).astype(BF16)


SB_PAIR = 2 * SB_HEAD_DIM


def _pair_masks(x2, scale=None):
    lane = lax.broadcasted_iota(jnp.int32, x2.shape, 1)
    zero = jnp.zeros_like(x2)
    a, b = jnp.where(lane < SB_HEAD_DIM, x2, zero), jnp.where(lane >= SB_HEAD_DIM, x2, zero)
    if scale is not None:
        a, b = a * scale, b * scale
    return a, b


def _causal(shape, q0, k0):
    return lax.broadcasted_iota(jnp.int32, shape, 1) + k0 < lax.broadcasted_iota(jnp.int32, shape, 0) + q0


def _sb_blocks(l):
    bk = _tile(l, (SB_BLOCK, 128))
    bq = _tile(l, (SB_BLOCK_Q, 2 * SB_BLOCK, SB_BLOCK, 128))
    return bq, bk


def sb_pair_fwd(qkv, job=None):
    l, d3 = qkv.shape
    d = d3 // 3
    npair = d // SB_PAIR
    bq, bk = _sb_blocks(l)
    ratio = bq // bk
    scale = SB_HEAD_DIM ** -0.5
    t_suf = _tri(bk, lambda r, c: r > c)

    def body(q_ref, k_ref, v_ref, tsuf_ref, o_ref, ltot_ref):
        i = pl.program_id(1)
        qs = _pair_masks(q_ref[...], scale)
        tsuf = tsuf_ref[...]

        def tile(j, carry, masked, r0=0):
            rows = pl.ds(pl.multiple_of(j * bk, bk), bk)
            k2 = k_ref[rows, :]
            v2 = v_ref[rows, :]
            out = []
            for q1, (c_r, acc) in zip(qs, carry):
                z = lax.dot_general(q1[r0:], k2, _NT, preferred_element_type=F32)
                lk = -_softplus(z)
                if masked:
                    causal = _causal(z.shape, i * bq + r0, j * bk)
                    lk = jnp.where(causal, lk, 0.0)
                later, lkb, later0 = _running_sums(lk, tsuf, 0)
                w = jnp.exp(z + lk + c_r[r0:] + later)
                if masked:
                    w = jnp.where(causal, w, 0.0)
                c_new = c_r[r0:] + later0 + lkb[:, 0:1].astype(F32)
                acc_new = acc[r0:] + jnp.dot(w.astype(BF16), v2, preferred_element_type=F32)
                if r0:
                    c_new, acc_new = jnp.concatenate([c_r[:r0], c_new]), jnp.concatenate([acc[:r0], acc_new])
                out.append((c_new, acc_new))
            return tuple(out)

        carry = ((jnp.zeros((bq, 1), F32), jnp.zeros((bq, SB_PAIR), F32)),) * 2
        for dgl in reversed(range(ratio)):
            carry = tile(i * ratio + dgl, carry, True, dgl * bk)
        (c_a, acc_a), (c_b, acc_b) = lax.fori_loop(0, i * ratio, lambda jj, c: tile(i * ratio - 1 - jj, c, False), carry)
        lane = lax.broadcasted_iota(jnp.int32, acc_a.shape, 1)
        o_ref[...] = jnp.where(lane < SB_HEAD_DIM, acc_a, acc_b).astype(o_ref.dtype)
        ltot_ref[...] = jnp.where(lax.broadcasted_iota(jnp.int32, (bq, 2), 1) == 0, c_a, c_b)

    grid = (npair, l // bq)
    body, x_in, x_out, x_shapes, x_scratch, x_args = carry(job, body, grid, 4, 2)
    res = pl.pallas_call(
        body,
        name="sb_attn_fwd",
        grid=grid,
        in_specs=[
            pl.BlockSpec((bq, SB_PAIR), lambda p, i: (i, p)),
            pl.BlockSpec((l, SB_PAIR), lambda p, i: (0, npair + p)),
            pl.BlockSpec((l, SB_PAIR), lambda p, i: (0, 2 * npair + p)),
            pl.BlockSpec((bk, bk), lambda p, i: (0, 0)),
        ] + x_in,
        out_specs=[pl.BlockSpec((bq, SB_PAIR), lambda p, i: (i, p)), pl.BlockSpec((None, bq, 2), lambda p, i: (p, i, 0))] + x_out,
        out_shape=[jax.ShapeDtypeStruct((l, d), BF16), jax.ShapeDtypeStruct((npair, l, 2), F32)] + x_shapes,
        scratch_shapes=x_scratch,
        compiler_params=_params(("arbitrary", "arbitrary")),
    )(qkv, qkv, qkv, t_suf, *x_args)
    return res[0], res[1], res[2:]


def sb_pair_bwd(qkv, do, ltot, job=None):
    l, d3 = qkv.shape
    d = d3 // 3
    npair = d // SB_PAIR
    bq, bk = _sb_blocks(l)
    ratio = bq // bk
    scale = SB_HEAD_DIM ** -0.5
    t_inc = _tri(bk, lambda r, c: r <= c)
    t_exc = _tri(bk, lambda r, c: r < c)

    def body(q_ref, k_ref, v_ref, do_ref, ltot_ref, tinc_ref, texc_ref, dq_ref, dk_ref, dv_ref):
        i = pl.program_id(1)

        @pl.when(i == 0)
        def _():
            dk_ref[...] = jnp.zeros_like(dk_ref)
            dv_ref[...] = jnp.zeros_like(dv_ref)

        qs = _pair_masks(q_ref[...], scale)
        dos = _pair_masks(do_ref[...])
        lt = ltot_ref[...]
        ltots = (lt[:, 0:1], lt[:, 1:2])
        tinc = tinc_ref[...]
        texc = texc_ref[...]

        def tile(j, carry, masked, r0=0):
            rows = pl.ds(pl.multiple_of(j * bk, bk), bk)
            k2 = k_ref[rows, :]
            v2 = v_ref[rows, :]
            out = []
            dk2 = dv2 = None
            for q1, do1, ltot1, (c_l, c_p, dq) in zip(qs, dos, ltots, carry):
                q1s, do1s = q1[r0:], do1[r0:]
                z = lax.dot_general(q1s, k2, _NT, preferred_element_type=F32)
                lk = -_softplus(z)
                if masked:
                    causal = _causal(z.shape, i * bq + r0, j * bk)
                    lk = jnp.where(causal, lk, 0.0)
                log_beta = z + lk
                upto, _, lk_tile = _running_sums(lk, tinc, bk - 1)
                w = jnp.exp(log_beta + (ltot1[r0:] - c_l[r0:]) - upto)
                if masked:
                    w = jnp.where(causal, w, 0.0)
                g = w * lax.dot_general(do1s, v2, _NT, preferred_element_type=F32)
                before, gb, before_last = _running_sums(g, texc, bk - 1)
                dz = g - jnp.exp(log_beta) * (g + c_p[r0:] + before)
                if masked:
                    dz = jnp.where(causal, dz, 0.0)
                dzb = dz.astype(BF16)
                dk1 = lax.dot_general(dzb, q1s, _TN, preferred_element_type=F32)
                dv1 = lax.dot_general(w.astype(BF16), do1s, _TN, preferred_element_type=F32)
                dk2 = dk1 if dk2 is None else dk2 + dk1
                dv2 = dv1 if dv2 is None else dv2 + dv1
                new = (c_l[r0:] + lk_tile, c_p[r0:] + before_last + gb[:, bk - 1:bk].astype(F32),
                       dq[r0:] + jnp.dot(dzb, k2, preferred_element_type=F32))
                if r0:
                    new = tuple(jnp.concatenate([old[:r0], part]) for old, part in zip((c_l, c_p, dq), new))
                out.append(new)
            dk_ref[rows, :] += dk2
            dv_ref[rows, :] += dv2
            return tuple(out)

        zero = jnp.zeros((bq, 1), F32)
        carry = ((zero, zero, jnp.zeros((bq, SB_PAIR), F32)),) * 2
        carry = lax.fori_loop(0, i * ratio, lambda j, c: tile(j, c, False), carry)
        for dgl in range(ratio):
            carry = tile(i * ratio + dgl, carry, True, dgl * bk)
        (_, _, dq_a), (_, _, dq_b) = carry
        lane = lax.broadcasted_iota(jnp.int32, dq_a.shape, 1)
        dq_ref[...] = jnp.where(lane < SB_HEAD_DIM, dq_a, dq_b) * scale

    col = lambda s: pl.BlockSpec((l, SB_PAIR), lambda p, i: (0, s * npair + p))
    blk_spec = pl.BlockSpec((bq, SB_PAIR), lambda p, i: (i, p))
    tri_spec = pl.BlockSpec((bk, bk), lambda p, i: (0, 0))
    grid = (npair, l // bq)
    body, x_in, x_out, x_shapes, x_scratch, x_args = carry(job, body, grid, 7, 3)
    res = pl.pallas_call(
        body,
        name="sb_attn_bwd",
        grid=grid,
        in_specs=[blk_spec, col(1), col(2), blk_spec, pl.BlockSpec((None, bq, 2), lambda p, i: (p, i, 0)), tri_spec, tri_spec] + x_in,
        out_specs=[blk_spec, pl.BlockSpec((l, SB_PAIR), lambda p, i: (0, p)), pl.BlockSpec((l, SB_PAIR), lambda p, i: (0, p))] + x_out,
        out_shape=[jax.ShapeDtypeStruct((l, d), F32)] * 3 + x_shapes,
        scratch_shapes=x_scratch,
        compiler_params=_params(("arbitrary", "arbitrary")),
    )(qkv, qkv, qkv, do, ltot, t_inc, t_exc, *x_args)
    return res[:3], res[3:]


def sb_fwd(hn, p, job=None):
    qkv = mm(hn, p["w_qkv"], name="sb_qkv", out_dtypes=(BF16,), tm=2048, tn=512)
    o, ltot, carried = sb_pair_fwd(qkv, job)
    return o, (hn, qkv, ltot, o), carried


def sb_bwd(dmixed, saved, p, job=None):
    hn, qkv, ltot, o = saved
    do = mm(dmixed, p["w_out"], tb=True, name="sb_do", out_dtypes=(BF16,))
    dw_out = mm(o, dmixed, ta=True, name="sb_dwout")
    dq_dk_dv, carried = sb_pair_bwd(qkv, do, ltot, job)
    dqkv = jnp.concatenate([g.astype(BF16) for g in dq_dk_dv], axis=1)
    dw_qkv = mm(hn, dqkv, ta=True, name="sb_dwqkv")
    return (dqkv, p["w_qkv"]), dict(w_qkv=dw_qkv, w_out=dw_out), carried


S5_CHUNK = 256
S5_CHUNK_BWD = 128
S5_SLAB_GROUPS = 8
S5_LANES = 128


def _s5_discretise(lr, li, ldt, bre, bim):
    dt = jnp.exp(ldt)
    mag = jnp.exp(lr * dt)
    lbr = mag * jnp.cos(li * dt)
    lbi = mag * jnp.sin(li * dt)
    inv = 1.0 / (lr * lr + li * li)
    cr = ((lbr - 1.0) * lr + lbi * li) * inv
    ci = (lbi * lr - (lbr - 1.0) * li) * inv
    return lbr, lbi, cr * bre - ci * bim, cr * bim + ci * bre


def _s5_cols(lam_re, lam_im, log_dt, b_re, b_im):
    g, p = lam_re.shape
    col = lambda x: x.reshape(g * p, 1)
    ldt = jnp.broadcast_to(log_dt[:, None], (g, p))
    return col(lam_re), col(lam_im), col(ldt), b_re.reshape(g * p, -1), b_im.reshape(g * p, -1)


def _slab_b(bbar):
    sg = S5_SLAB_GROUPS
    gp, h = bbar.shape
    p = S5_STATE
    x = bbar.reshape(gp // (sg * p), sg, p, h)
    return jnp.einsum("kaph,ab->kahbp", x, jnp.eye(sg, dtype=x.dtype)).reshape(-1, sg * h, sg * p)


def _unslab_b(dslab):
    sg, p = S5_SLAB_GROUPS, S5_STATE
    nk, sh, _ = dslab.shape
    h = sh // sg
    x = dslab.reshape(nk, sg, h, sg, p)
    return jnp.einsum("kahbp,ab->kaph", x, jnp.eye(sg, dtype=x.dtype)).reshape(nk * sg * p, h)


def _slab_c(c):
    sg = S5_SLAB_GROUPS
    g, h, p = c.shape
    x = c.reshape(g // sg, sg, h, p)
    return jnp.einsum("kahp,ab->kapbh", x, jnp.eye(sg, dtype=x.dtype)).reshape(-1, sg * p, sg * h)


def _unslab_c(dslab):
    sg, p = S5_SLAB_GROUPS, S5_STATE
    nk, _, sh = dslab.shape
    h = sh // sg
    x = dslab.reshape(nk, sg, p, sg, h)
    return jnp.einsum("kapbh,ab->kahp", x, jnp.eye(sg, dtype=x.dtype)).reshape(nk * sg, h, p)


def _cmul_scan(lre, lim, xre, xim, re_scr, im_scr, cre, cim, reverse):
    nb, lanes = re_scr.shape[0], re_scr.shape[2]
    ends = []
    for c in range(nb):
        sl = slice(c * lanes, (c + 1) * lanes)
        re_scr[c] = xre[:, sl]
        im_scr[c] = xim[:, sl]
        ends.append(_cmul_scan_block(lre[:, sl], lim[:, sl], re_scr.at[c], im_scr.at[c], cre[:, sl], cim[:, sl], reverse))
    return (jnp.concatenate([re_scr[c] for c in range(nb)], axis=1), jnp.concatenate([im_scr[c] for c in range(nb)], axis=1),
            jnp.concatenate([e[0] for e in ends], axis=1), jnp.concatenate([e[1] for e in ends], axis=1))


def _cmul_scan_block(lre, lim, re_ref, im_ref, cre, cim, reverse):
    t = re_ref.shape[0]
    g = t // 8
    shift = _shift_up if reverse else _shift_down
    cmul = lambda ar, ai, br, bi: (ar * br - ai * bi, ar * bi + ai * br)
    pows = [(lre, lim)]
    for _ in range(7):
        pows.append(cmul(*pows[-1], lre, lim))
    local, prev = [None] * 8, None
    for r in (reversed(range(8)) if reverse else range(8)):
        cr, ci = re_ref[pl.ds(r, g, stride=8), :], im_ref[pl.ds(r, g, stride=8), :]
        if prev is not None:
            pr, pi = cmul(lre, lim, *prev)
            cr, ci = cr + pr, ci + pi
        local[r] = prev = (cr, ci)
    yr, yi = local[0 if reverse else 7]
    edge = lax.broadcasted_iota(jnp.int32, yr.shape, 0) == (g - 1 if reverse else 0)
    mr, mi = pows[7]
    kr, ki = cmul(mr, mi, cre, cim)
    yr, yi = yr + jnp.where(edge, kr, 0.0), yi + jnp.where(edge, ki, 0.0)
    d = 1
    while d < g:
        pr, pi = cmul(mr, mi, shift(yr, d, 0.0), shift(yi, d, 0.0))
        yr, yi = yr + pr, yi + pi
        mr, mi = cmul(mr, mi, mr, mi)
        d *= 2
    er, ei = jnp.where(edge, cre, shift(yr, 1, 0.0)), jnp.where(edge, cim, shift(yi, 1, 0.0))
    for r in range(8):
        pr, pi = cmul(*pows[7 - r if reverse else r], er, ei)
        re_ref[pl.ds(r, g, stride=8), :] = local[r][0] + pr
        im_ref[pl.ds(r, g, stride=8), :] = local[r][1] + pi
    return yr, yi


def s5_scan_fwd(u, lbr, lbi, bbd_re, bbd_im, cbd_re, cbd_imn, d_skip, job=None):
    l, w = u.shape
    n = lbr.shape[1]
    nk, cw, sw = bbd_re.shape
    t = _tile(l, (S5_CHUNK, 64, 32, 16, 8))

    def body(u_ref, lbr_ref, lbi_ref, bre_ref, bim_ref, cre_ref, cim_ref, d_ref, sre_ref, sim_ref, y_ref, z_ref, pre_ref, pim_ref,
             xr_ref, xi_ref):
        @pl.when(pl.program_id(0) == 0)
        def _():
            pre_ref[...] = jnp.zeros_like(pre_ref)
            pim_ref[...] = jnp.zeros_like(pim_ref)

        uu = u_ref[...]
        ub = uu.astype(BF16)
        lre, lim = lbr_ref[...], lbi_ref[...]
        xre = jnp.concatenate([jnp.dot(ub[:, k * cw:(k + 1) * cw], bre_ref[k], preferred_element_type=F32) for k in range(nk)], axis=1)
        xim = jnp.concatenate([jnp.dot(ub[:, k * cw:(k + 1) * cw], bim_ref[k], preferred_element_type=F32) for k in range(nk)], axis=1)
        last = t // 8 - 1
        sre, sim, pre_ref[...], pim_ref[...] = _cmul_scan(lre, lim, xre, xim, xr_ref, xi_ref, pre_ref[pl.ds(last, 1), :],
                                                          pim_ref[pl.ds(last, 1), :], False)
        sre_ref[...] = sre
        sim_ref[...] = sim
        sreb, simb = sre.astype(BF16), sim.astype(BF16)
        y = jnp.concatenate(
            [jnp.dot(sreb[:, k * sw:(k + 1) * sw], cre_ref[k], preferred_element_type=F32)
             + jnp.dot(simb[:, k * sw:(k + 1) * sw], cim_ref[k], preferred_element_type=F32) for k in range(nk)], axis=1)
        y = y + d_ref[...] * uu
        y_ref[...] = y
        z_ref[...] = _gelu(y).astype(z_ref.dtype)

    def whole(x):
        nd = x.ndim
        return pl.BlockSpec(x.shape, lambda i: (0,) * nd)

    consts = [lbr, lbi, bbd_re, bbd_im, cbd_re, cbd_imn, d_skip]
    row = lambda c: pl.BlockSpec((t, c), lambda i: (i, 0))
    body, x_in, x_out, x_shapes, x_scratch, x_args = carry(job, body, (l // t,), 1 + len(consts), 4)
    res = pl.pallas_call(
        body,
        name="s5_scan_fwd",
        grid=(l // t,),
        in_specs=[row(w)] + [whole(c) for c in consts] + x_in,
        out_specs=[row(n), row(n), row(w), row(w)] + x_out,
        out_shape=[jax.ShapeDtypeStruct((l, n), F32), jax.ShapeDtypeStruct((l, n), F32), jax.ShapeDtypeStruct((l, w), F32),
                   jax.ShapeDtypeStruct((l, w), BF16)] + x_shapes,
        scratch_shapes=[pltpu.VMEM((t // 8, n), F32), pltpu.VMEM((t // 8, n), F32), pltpu.VMEM((n // S5_LANES, t, S5_LANES), F32),
                        pltpu.VMEM((n // S5_LANES, t, S5_LANES), F32)] + x_scratch,
        compiler_params=_params(("arbitrary",)),
    )(u, *consts, *x_args)
    return res[:4], res[4:]


def s5_scan_bwd(dz, y, u, sre, sim, lbr, lbi, bbd_re, bbd_im, cbd_re, cbd_imn, d_skip, job=None):
    l, w = u.shape
    n = lbr.shape[1]
    nk, cw, sw = bbd_re.shape
    t = _tile(l, (S5_CHUNK_BWD, 64, 32, 16, 8))
    nc = l // t

    def body(dz_ref, y_ref, u_ref, sre_ref, sim_ref, hre_ref, him_ref, lbr_ref, lbi_ref, bre_ref, bim_ref, cre_ref, cim_ref,
             d_ref, du_ref, dlr_ref, dli_ref, dbre_ref, dbim_ref, dcre_ref, dcim_ref, dd_ref, nre_ref, nim_ref, dsr_ref, dsi_ref):
        i = pl.program_id(0)
        has_prev = (i < nc - 1).astype(F32)

        @pl.when(i == 0)
        def _():
            nre_ref[...] = jnp.zeros_like(nre_ref)
            nim_ref[...] = jnp.zeros_like(nim_ref)

        uu = u_ref[...]
        ub = uu.astype(BF16)
        lre, lim = lbr_ref[...], lbi_ref[...]
        _, gelu_vjp = jax.vjp(_gelu, y_ref[...])
        dy = gelu_vjp(dz_ref[...].astype(F32))[0]
        dyb = dy.astype(BF16)
        gre = jnp.concatenate([lax.dot_general(dyb[:, k * cw:(k + 1) * cw], cre_ref[k], _NT, preferred_element_type=F32)
                               for k in range(nk)], axis=1)
        gim = jnp.concatenate([lax.dot_general(dyb[:, k * cw:(k + 1) * cw], cim_ref[k], _NT, preferred_element_type=F32)
                               for k in range(nk)], axis=1)
        dsre, dsim, nre_ref[...], nim_ref[...] = _cmul_scan(lre, -lim, gre, gim, dsr_ref, dsi_ref, nre_ref[pl.ds(0, 1), :],
                                                            nim_ref[pl.ds(0, 1), :], True)
        dsreb, dsimb = dsre.astype(BF16), dsim.astype(BF16)
        s_re, s_im = sre_ref[...], sim_ref[...]
        du = jnp.concatenate(
            [lax.dot_general(dsreb[:, k * sw:(k + 1) * sw], bre_ref[k], _NT, preferred_element_type=F32)
             + lax.dot_general(dsimb[:, k * sw:(k + 1) * sw], bim_ref[k], _NT, preferred_element_type=F32) for k in range(nk)],
            axis=1)
        du_ref[...] = (du + d_ref[...] * dy).astype(du_ref.dtype)
        pre = _rows_before(s_re, hre_ref[...] * has_prev, 1)
        pim = _rows_before(s_im, him_ref[...] * has_prev, 1)
        sreb, simb = s_re.astype(BF16), s_im.astype(BF16)
        sums = [
            (dlr_ref, jnp.sum(dsre * pre + dsim * pim, axis=0, keepdims=True)),
            (dli_ref, jnp.sum(dsim * pre - dsre * pim, axis=0, keepdims=True)),
            (dbre_ref, jnp.stack([lax.dot_general(ub[:, k * cw:(k + 1) * cw], dsreb[:, k * sw:(k + 1) * sw], _TN,
                                                  preferred_element_type=F32) for k in range(nk)])),
            (dbim_ref, jnp.stack([lax.dot_general(ub[:, k * cw:(k + 1) * cw], dsimb[:, k * sw:(k + 1) * sw], _TN,
                                                  preferred_element_type=F32) for k in range(nk)])),
            (dcre_ref, jnp.stack([lax.dot_general(sreb[:, k * sw:(k + 1) * sw], dyb[:, k * cw:(k + 1) * cw], _TN,
                                                  preferred_element_type=F32) for k in range(nk)])),
            (dcim_ref, jnp.stack([lax.dot_general(simb[:, k * sw:(k + 1) * sw], dyb[:, k * cw:(k + 1) * cw], _TN,
                                                  preferred_element_type=F32) for k in range(nk)])),
            (dd_ref, jnp.sum(dy * uu, axis=0, keepdims=True)),
        ]

        @pl.when(i == 0)
        def _():
            for ref, val in sums:
                ref[...] = val

        @pl.when(i > 0)
        def _():
            for ref, val in sums:
                ref[...] += val

    def whole(shape):
        nd = len(shape)
        return pl.BlockSpec(tuple(shape), lambda i: (0,) * nd)

    consts = [lbr, lbi, bbd_re, bbd_im, cbd_re, cbd_imn, d_skip]
    t8 = t // 8
    rev = lambda i: nc - 1 - i
    halo = lambda i: jnp.maximum(rev(i) * t8 - 1, 0)
    row = lambda c: pl.BlockSpec((t, c), lambda i: (rev(i), 0))
    sum_shapes = [lbr.shape, lbi.shape, bbd_re.shape, bbd_im.shape, cbd_re.shape, cbd_imn.shape, d_skip.shape]
    body, x_in, x_out, x_shapes, x_scratch, x_args = carry(job, body, (nc,), 7 + len(consts), 8)
    res = pl.pallas_call(
        body,
        name="s5_scan_bwd",
        grid=(nc,),
        in_specs=[row(w), row(w), row(w), row(n), row(n), pl.BlockSpec((8, n), lambda i: (halo(i), 0)),
                  pl.BlockSpec((8, n), lambda i: (halo(i), 0))] + [whole(c.shape) for c in consts] + x_in,
        out_specs=[row(w)] + [whole(s) for s in sum_shapes] + x_out,
        out_shape=[jax.ShapeDtypeStruct((l, w), BF16)] + [jax.ShapeDtypeStruct(tuple(s), F32) for s in sum_shapes] + x_shapes,
        scratch_shapes=[pltpu.VMEM((t8, n), F32), pltpu.VMEM((t8, n), F32), pltpu.VMEM((n // S5_LANES, t, S5_LANES), F32),
                        pltpu.VMEM((n // S5_LANES, t, S5_LANES), F32)] + x_scratch,
        compiler_params=_params(("arbitrary",)),
    )(dz, y, u, sre, sim, sre, sim, *consts, *x_args)
    return res[0], res[1:8], res[8:]


def _glu(vg):
    w = vg.shape[1] // 2
    return vg[:, :w] * _sigmoid(vg[:, w:])


def s5_fwd(hn, h, p, job=None):
    cols = _s5_cols(p["lam_re"], p["lam_im"], p["log_dt"], p["b_re"], p["b_im"])
    gp, hh = cols[3].shape
    lbr, lbi, bbr, bbi = rowwise(_s5_discretise, list(cols), [], [(1, F32), (1, F32), (hh, F32), (hh, F32)],
                                 name="s5_discretise", tm=512)
    consts = (lbr.reshape(1, gp), lbi.reshape(1, gp), _slab_b(bbr).astype(BF16), _slab_b(bbi).astype(BF16),
              _slab_c(p["c_re"]).astype(BF16), _slab_c(-p["c_im"]).astype(BF16), p["d"])
    u = mm(hn, p["w_in"], name="s5_in")
    (sre, sim, y, z), carried = s5_scan_fwd(u, *consts, job=job)
    vg = mm(z, p["w_out"], name="s5_out", out_dtypes=(BF16,))
    h_new = rowwise(lambda a, r: r + _glu(a.astype(F32)), [vg, h], [], [(h.shape[1], F32)], name="s5_glu")
    return h_new, (hn, u, sre, sim, y, z, vg, cols, consts), carried


def s5_bwd(dh_new, saved, p, job=None):
    hn, u, sre, sim, y, z, vg, cols, consts = saved

    def glu_bwd(a, dm):
        _, vjp = jax.vjp(_glu, a.astype(F32))
        return vjp(dm)[0]

    dvg = rowwise(glu_bwd, [vg, dh_new], [], [(vg.shape[1], BF16)], name="s5_dglu")
    dw_out = mm(z, dvg, ta=True, name="s5_dwout")
    dz = mm(dvg, p["w_out"], tb=True, name="s5_dz", out_dtypes=(BF16,), tk=2048)
    du, (dlbr, dlbi, dbbr, dbbi, dcre, dcimn, dd), carried = s5_scan_bwd(dz, y, u, sre, sim, *consts, job=job)
    dw_in = mm(hn, du, ta=True, name="s5_dwin")
    gp = cols[0].shape[0]

    def disc_bwd(lr, li, ldt, bre, bim, g0, g1, g2, g3):
        _, vjp = jax.vjp(_s5_discretise, lr, li, ldt, bre, bim)
        return vjp((g0, g1, g2, g3))

    cot = (dlbr.reshape(gp, 1), dlbi.reshape(gp, 1), _unslab_b(dbbr), _unslab_b(dbbi))
    dlr, dli, dldt, dbre, dbim = rowwise(disc_bwd, list(cols + cot), [], [(c.shape[1], F32) for c in cols],
                                         name="s5_discretise_bwd", tm=512)
    g_, p_ = p["lam_re"].shape
    grads = dict(w_in=dw_in, lam_re=dlr.reshape(g_, p_), lam_im=dli.reshape(g_, p_), log_dt=dldt.reshape(g_, p_).sum(axis=1),
                 b_re=dbre.reshape(p["b_re"].shape), b_im=dbim.reshape(p["b_im"].shape), c_re=_unslab_c(dcre),
                 c_im=-_unslab_c(dcimn), d=dd, w_out=dw_out)
    return (du, p["w_in"]), grads, carried


MESH = pl.DeviceIdType.MESH
N_CHIPS = 4
N_DEVICES = 8


def _place():
    x, y, c = lax.axis_index("x"), lax.axis_index("y"), lax.axis_index("c")
    return x, y, c, [(1 - x, y), (x, 1 - y), (1 - x, 1 - y)]


def _hbm_call(body, name, ins, out_shapes, n_remote, n_local=0):
    hbm = pl.BlockSpec(memory_space=pltpu.HBM)
    scratch = [pltpu.SemaphoreType.DMA((n_remote,)), pltpu.SemaphoreType.DMA((n_remote,))]
    if n_local:
        scratch.append(pltpu.SemaphoreType.DMA((n_local,)))
    return pl.pallas_call(
        body,
        name=name,
        in_specs=[hbm] * len(ins),
        out_specs=[hbm] * len(out_shapes),
        out_shape=out_shapes,
        scratch_shapes=scratch,
    )(*ins)


def _split_dim(shape):
    return next(d for d, s in enumerate(shape) if s >= 2 and s % 2 == 0)


class Exchange:
    def __init__(self, ins, out_shapes, n_remote, n_local, start, finish):
        self.ins, self.out_shapes, self.start, self.finish = list(ins), list(out_shapes), start, finish
        self.scratch = [pltpu.SemaphoreType.DMA((n_remote,)), pltpu.SemaphoreType.DMA((n_remote,)),
                        pltpu.SemaphoreType.DMA((max(n_local, 1),))]


def run_exchange(job, name):
    n_in, n_out = len(job.ins), len(job.out_shapes)

    def body(*refs):
        ins, outs, sems = refs[:n_in], refs[n_in:n_in + n_out], refs[n_in + n_out:]
        job.start(ins, outs, *sems)
        job.finish(ins, outs, *sems)

    hbm = pl.BlockSpec(memory_space=pltpu.HBM)
    return pl.pallas_call(body, name=name, in_specs=[hbm] * n_in, out_specs=[hbm] * n_out, out_shape=job.out_shapes,
                          scratch_shapes=job.scratch)(*job.ins)


def carry(job, body, grid, n_in, n_out):
    if job is None:
        return body, [], [], [], [], []
    nji, njo = len(job.ins), len(job.out_shapes)

    def carrying(*refs):
        ins, jins = refs[:n_in], refs[n_in:n_in + nji]
        outs, jouts = refs[n_in + nji:n_in + nji + n_out], refs[n_in + nji + n_out:n_in + nji + n_out + njo]
        rest = refs[n_in + nji + n_out + njo:]
        own, sems = rest[:len(rest) - 3], rest[len(rest) - 3:]
        ids = [pl.program_id(ax) for ax in range(len(grid))]
        first = functools.reduce(jnp.logical_and, [i == 0 for i in ids])
        last = functools.reduce(jnp.logical_and, [i == g - 1 for i, g in zip(ids, grid)])

        @pl.when(first)
        def _():
            job.start(jins, jouts, *sems)

        body(*ins, *outs, *own)

        @pl.when(last)
        def _():
            job.finish(jins, jouts, *sems)

    hbm = pl.BlockSpec(memory_space=pltpu.HBM)
    return carrying, [hbm] * nji, [hbm] * njo, job.out_shapes, job.scratch, job.ins


def gather_chips(shards):
    n = len(shards)
    cuts = [_split_dim(s.shape) for s in shards]

    def parts(ins, outs, send, recv):
        x, y, c, chips = _place()

        def half(ref, t, which, lead=()):
            size = shards[t].shape[cuts[t]] // 2
            return ref.at[lead + (slice(None),) * cuts[t] + (pl.ds(which * size, size),)]

        def copy(t, k, block, which, to, src=None):
            dst = half(outs[t], t, which, (block,))
            return pltpu.make_async_remote_copy(dst if src is None else src, dst, send.at[6 * t + k], recv.at[6 * t + k],
                                                device_id=to, device_id_type=MESH)

        me = 2 * x + y
        sends = [copy(t, k, me, c, (px, py, c), src=half(ins[t], t, c)) for t in range(n) for k, (px, py) in enumerate(chips)]
        return x, y, c, chips, me, copy, sends

    def start(ins, outs, send, recv, local):
        _, _, _, _, me, _, sends = parts(ins, outs, send, recv)
        for t in range(n):
            pltpu.make_async_copy(ins[t], outs[t].at[me], local.at[t]).start()
        for cp in sends:
            cp.start()

    def finish(ins, outs, send, recv, local):
        x, y, c, chips, me, copy, sends = parts(ins, outs, send, recv)
        passed = []
        for t in range(n):
            for k, (px, py) in enumerate(chips):
                copy(t, k, 2 * px + py, c, (px, py, c)).wait_recv()
                on = copy(t, 3 + k, 2 * px + py, c, (x, y, 1 - c))
                on.start()
                passed.append(on)
        for t in range(n):
            for k, (px, py) in enumerate(chips):
                copy(t, 3 + k, 2 * px + py, 1 - c, (x, y, 1 - c)).wait_recv()
        for cp in sends + passed:
            cp.wait_send()
        for t in range(n):
            pltpu.make_async_copy(ins[t], outs[t].at[me], local.at[t]).wait()

    return Exchange(shards, [jax.ShapeDtypeStruct((N_CHIPS,) + s.shape, s.dtype) for s in shards], 6 * n, n, start, finish)


def scatter_chips(blocked):
    n = len(blocked)

    def copies(ins, outs, send, recv):
        x, y, c, chips = _place()
        return [pltpu.make_async_remote_copy(ins[t].at[2 * px + py], outs[t].at[k], send.at[3 * t + k], recv.at[3 * t + k],
                                             device_id=(px, py, c), device_id_type=MESH)
                for t in range(n) for k, (px, py) in enumerate(chips)]

    def start(ins, outs, send, recv, local):
        for cp in copies(ins, outs, send, recv):
            cp.start()

    def finish(ins, outs, send, recv, local):
        for cp in copies(ins, outs, send, recv):
            cp.wait()

    return Exchange(blocked, [jax.ShapeDtypeStruct((3,) + b.shape[1:], b.dtype) for b in blocked], 3 * n, 0, start, finish)


def swap_cores(arrays):
    n = len(arrays)

    def body(*refs):
        ins, outs = refs[:n], refs[n:2 * n]
        send, recv = refs[2 * n:]
        x, y, c, _ = _place()
        pending = []
        for t in range(n):
            cp = pltpu.make_async_remote_copy(ins[t], outs[t], send.at[t], recv.at[t], device_id=(x, y, 1 - c),
                                              device_id_type=MESH)
            cp.start()
            pending.append(cp)
        for cp in pending:
            cp.wait()

    return _hbm_call(body, "swap_cores", arrays, [jax.ShapeDtypeStruct(a.shape, a.dtype) for a in arrays], n)


def gather_devices(buf):
    def body(in_ref, out_ref, send, recv, local):
        x, y, c, _ = _place()
        own = pltpu.make_async_copy(in_ref, out_ref.at[4 * x + 2 * y + c], local.at[0])
        own.start()
        pending = [own]
        for k in range(1, N_DEVICES):
            px = x ^ ((k >> 2) & 1)
            py = y ^ ((k >> 1) & 1)
            pc = c ^ (k & 1)
            going = pltpu.make_async_remote_copy(in_ref, out_ref.at[4 * x + 2 * y + c], send.at[k - 1], recv.at[k - 1],
                                                 device_id=(px, py, pc), device_id_type=MESH)
            going.start()
            pending.append(pltpu.make_async_remote_copy(in_ref, out_ref.at[4 * px + 2 * py + pc], send.at[k - 1], recv.at[k - 1],
                                                        device_id=(px, py, pc), device_id_type=MESH))
        for cp in pending:
            cp.wait()

    return _hbm_call(body, "gather_devices", [buf], [jax.ShapeDtypeStruct((N_DEVICES,) + buf.shape, buf.dtype)],
                     N_DEVICES - 1, 1)[0]


def _adamw(w, g, m, v):
    m = ADAM_B1 * m + (1.0 - ADAM_B1) * g
    v = ADAM_B2 * v + (1.0 - ADAM_B2) * (g * g)
    m_hat = m / (1.0 - ADAM_B1 ** ADAM_STEP)
    v_hat = v / (1.0 - ADAM_B2 ** ADAM_STEP)
    return -ADAM_LR * (m_hat / (jnp.sqrt(v_hat) + ADAM_EPS) + ADAM_WD * w), m, v


def _rows2d(a):
    return a.reshape(-1, a.shape[-1])


WEIGHTS = ["ffn1_norm", "ffn1_w_in", "ffn1_w_out", "mix_norm", "ffn2_norm", "ffn2_w_in", "ffn2_w_out", "final_norm",
           "s5_w_in", "s5_lam_re", "s5_lam_im", "s5_log_dt", "s5_b_re", "s5_b_im", "s5_c_re", "s5_c_im", "s5_d", "s5_w_out",
           "sb_w_qkv", "sb_w_out", "lru_w_in", "lru_conv_w", "lru_conv_b", "lru_w_a", "lru_b_a", "lru_w_x", "lru_b_x",
           "lru_lambda", "lru_w_out"]
INPUTS = ["x"] + WEIGHTS + ["loss_target"] + ["m_" + n for n in WEIGHTS] + ["v_" + n for n in WEIGHTS]
SHARDED_BIG = dict(ffn1_w_in=2, ffn1_w_out=1, ffn2_w_in=2, ffn2_w_out=1, s5_w_in=1, s5_w_out=2, sb_w_qkv=2, sb_w_out=1,
                   lru_w_in=2, lru_w_a=2, lru_w_x=2, lru_w_out=1)
SHARDED_SMALL = dict(s5_d=1, lru_conv_w=2, lru_conv_b=1, lru_b_a=2, lru_b_x=2, lru_lambda=1)
REPLICATED = [n for n in WEIGHTS if n not in SHARDED_BIG and n not in SHARDED_SMALL]
PACK_LANES = 128
PACK_ROW_ALIGN = 16
REPLICATED_ROW_TILE = 256


def _unblock(g, d):
    full = jnp.moveaxis(g, 0, d)
    return full.reshape(full.shape[:d] + (full.shape[d] * full.shape[d + 1],) + full.shape[d + 2:])


def _block(full, d):
    s = full.shape[d] // N_CHIPS
    return jnp.moveaxis(full.reshape(full.shape[:d] + (N_CHIPS, s) + full.shape[d + 1:]), d, 0)


def _pack(arrays, lead=(), row_align=PACK_ROW_ALIGN):
    nl = len(lead)
    flat = jnp.concatenate([a.reshape(lead + (-1,)) for a in arrays], axis=nl)
    quantum = PACK_LANES * row_align
    pad = (-flat.shape[nl]) % quantum
    flat = jnp.pad(flat, [(0, 0)] * nl + [(0, pad)])
    return flat.reshape(lead + (-1, PACK_LANES))


def _unpack(packed, shapes, lead=()):
    nl = len(lead)
    flat = packed.reshape(lead + (-1,))
    out, off = [], 0
    for s in shapes:
        size = math.prod(s)
        out.append(lax.slice_in_dim(flat, off, off + size, axis=nl).reshape(lead + tuple(s)))
        off += size
    return out


N_GROUPS = 3
MIXER_KIND = dict(s5=0, sb=1, lru=2)
S5_NAMES = dict(w_in="s5_w_in", lam_re="s5_lam_re", lam_im="s5_lam_im", log_dt="s5_log_dt", b_re="s5_b_re", b_im="s5_b_im",
                c_re="s5_c_re", c_im="s5_c_im", d="s5_d", w_out="s5_w_out")
SB_NAMES = dict(w_qkv="sb_w_qkv", w_out="sb_w_out")
LRU_NAMES = dict(w_in="lru_w_in", conv_w="lru_conv_w", conv_b="lru_conv_b", wa="lru_w_a", ba="lru_b_a", wx="lru_w_x",
                 bx="lru_b_x", lam="lru_lambda", w_out="lru_w_out")


def _pieces_of(name, count):
    kind = None if name.startswith("ffn") else MIXER_KIND[name.split("_")[0]]
    groups = [min(i if kind is None else kind + N_MIXERS * i, N_GROUPS - 1) for i in range(count)]
    runs, lo = [], 0
    for i in range(1, count + 1):
        if i == count or groups[i] != groups[lo]:
            runs.append((lo, i, groups[lo]))
            lo = i
    return runs


class _Sharded:
    def __init__(self, a):
        self.a = a
        self.pieces = {n: _pieces_of(n, a[n].shape[0]) for n in SHARDED_BIG}
        self.by_group = [[(n, lo, hi) for n in SHARDED_BIG for lo, hi, g in self.pieces[n] if g == grp]
                         for grp in range(N_GROUPS)]
        self.weights = {}
        self.small = {}
        self.grads = {n: [None] * a[n].shape[0] for n in SHARDED_BIG}
        self.small_grads = {}
        self.received = {}
        self.small_received = None

    def gather_job(self, grp):
        arrays = [self.a[n][lo:hi].astype(BF16) for n, lo, hi in self.by_group[grp]]
        if grp == 0:
            arrays.append(_pack([self.a[n] for n in SHARDED_SMALL]))
        return gather_chips(arrays)

    def landed(self, grp, gathered):
        for (n, lo, _), g in zip(self.by_group[grp], gathered):
            self.weights[(n, lo)] = _unblock(g, SHARDED_BIG[n])
        if grp == 0:
            blocks = _unpack(gathered[-1], [self.a[n].shape for n in SHARDED_SMALL], lead=(N_CHIPS,))
            self.small = {n: _unblock(b, d) for (n, d), b in zip(SHARDED_SMALL.items(), blocks)}

    def weight(self, name, idx):
        lo = next(lo for lo, hi, _ in self.pieces[name] if lo <= idx < hi)
        return self.weights[(name, lo)][idx - lo]

    def piece_grad(self, n, lo, hi):
        return jnp.stack(self.grads[n][lo:hi])

    def scatter_job(self, grp):
        arrays = [_block(self.piece_grad(n, lo, hi), SHARDED_BIG[n]).astype(BF16) for n, lo, hi in self.by_group[grp]]
        if grp == 0:
            arrays.append(_pack([_block(self.small_grads[n], d) for n, d in SHARDED_SMALL.items()], lead=(N_CHIPS,)))
        return scatter_chips(arrays)

    def arrived(self, grp, received):
        for (n, lo, _), r in zip(self.by_group[grp], received):
            self.received[(n, lo)] = r
        if grp == 0:
            self.small_received = received[-1]


def _forward_backward(x, target, a, sh):
    depth = a["ffn1_norm"].shape[0]

    def mixer_params(layer):
        kind, j = layer % N_MIXERS, layer // N_MIXERS
        if kind == 0:
            return kind, j, dict(w_in=sh.weight("s5_w_in", j), lam_re=a["s5_lam_re"][j], lam_im=a["s5_lam_im"][j],
                                 log_dt=a["s5_log_dt"][j], b_re=a["s5_b_re"][j], b_im=a["s5_b_im"][j], c_re=a["s5_c_re"][j],
                                 c_im=a["s5_c_im"][j], d=sh.small["s5_d"][j].reshape(1, -1), w_out=sh.weight("s5_w_out", j))
        if kind == 1:
            return kind, j, dict(w_qkv=sh.weight("sb_w_qkv", j), w_out=sh.weight("sb_w_out", j))
        sm = sh.small
        return kind, j, dict(w_in=sh.weight("lru_w_in", j), conv_w=sm["lru_conv_w"][j], conv_b=sm["lru_conv_b"][j].reshape(1, -1),
                             wa=sh.weight("lru_w_a", j), ba=sm["lru_b_a"][j].reshape(1, -1), wx=sh.weight("lru_w_x", j),
                             bx=sm["lru_b_x"][j].reshape(1, -1), lam=sm["lru_lambda"][j].reshape(1, -1),
                             w_out=sh.weight("lru_w_out", j))

    def ffn_weights(which, layer):
        return sh.weight(f"{which}_w_in", layer), sh.weight(f"{which}_w_out", layer)

    sh.landed(0, run_exchange(sh.gather_job(0), "gather_chips"))
    h = x
    tape = []
    for layer in range(depth):
        h, s1 = ffn_fwd(h, a["ffn1_norm"][layer], *ffn_weights("ffn1", layer), "ffn")
        kind, j, p = mixer_params(layer)
        h_mix_in = h
        hn = rms_fwd(h, a["mix_norm"][layer], "mix_norm")
        job = sh.gather_job(layer + 1) if layer + 1 < N_GROUPS else None
        if kind == 0:
            h, sm, got = s5_fwd(hn, h, p, job)
        elif kind == 1:
            o_flat, sm, got = sb_fwd(hn, p, job)
            h = mm(o_flat, p["w_out"], name="mix_out", extras=(h,), epilogue=lambda acc, res: res + acc)
        else:
            assert job is None
            y, sm = lru_fwd(hn, p)
            h = mm(y, p["w_out"], name="mix_out", extras=(h,), epilogue=lambda acc, res: res + acc)
        if job is not None:
            sh.landed(layer + 1, got)
        h, s2 = ffn_fwd(h, a["ffn2_norm"][layer], *ffn_weights("ffn2", layer), "ffn")
        tape.append((s1, h_mix_in, sm, s2))

    loss, dh, g_final = loss_fwd_bwd(h, a["final_norm"], target)

    norm_grads = {n: [None] * depth for n in ("ffn1_norm", "mix_norm", "ffn2_norm")}
    mix = {}
    for layer in reversed(range(depth)):
        s1, h_mix_in, sm, s2 = tape[layer]
        dh, dg, dwi, dwo = ffn_bwd(dh, s2, a["ffn2_norm"][layer], *ffn_weights("ffn2", layer), "ffn")
        norm_grads["ffn2_norm"][layer], sh.grads["ffn2_w_in"][layer], sh.grads["ffn2_w_out"][layer] = dg, dwi, dwo
        kind, j, p = mixer_params(layer)
        job = sh.scatter_job(layer + 1) if layer + 1 < N_GROUPS else None
        if kind == 0:
            (da, w_first), g, got = s5_bwd(dh, sm, p, job)
            names = S5_NAMES
        elif kind == 1:
            (da, w_first), g, got = sb_bwd(dh, sm, p, job)
            names = SB_NAMES
        else:
            assert job is None
            (da, w_first), g = lru_bwd(dh, sm, p)
            names = LRU_NAMES
        if job is not None:
            sh.arrived(layer + 1, got)
        for k, full_name in names.items():
            if full_name in SHARDED_BIG:
                sh.grads[full_name][j] = g[k].reshape(sh.weight(full_name, j).shape)
            else:
                mix.setdefault(full_name, {})[j] = g[k].reshape(a[full_name].shape[1:-1] + (-1,))
        dh, dg = norm_input_bwd(da, _rows_spec, 1, w_first, h_mix_in, a["mix_norm"][layer], dh, "mix_dhn")
        norm_grads["mix_norm"][layer] = dg
        dh, dg, dwi, dwo = ffn_bwd(dh, s1, a["ffn1_norm"][layer], *ffn_weights("ffn1", layer), "ffn")
        norm_grads["ffn1_norm"][layer], sh.grads["ffn1_w_in"][layer], sh.grads["ffn1_w_out"][layer] = dg, dwi, dwo

    grads = {n: jnp.stack(v) for n, v in norm_grads.items()}
    grads["final_norm"] = g_final
    for n, by_j in mix.items():
        stacked = jnp.stack([by_j[j] for j in range(len(by_j))])
        if n in SHARDED_SMALL:
            sh.small_grads[n] = stacked
        else:
            grads[n] = stacked
    sh.arrived(0, run_exchange(sh.scatter_job(0), "scatter_chips"))
    return loss, dh, grads


def kernel(x, ffn1_norm, ffn1_w_in, ffn1_w_out, mix_norm, ffn2_norm, ffn2_w_in, ffn2_w_out, final_norm, s5_w_in,
           s5_lam_re, s5_lam_im, s5_log_dt, s5_b_re, s5_b_im, s5_c_re, s5_c_im, s5_d, s5_w_out, sb_w_qkv,
           sb_w_out, lru_w_in, lru_conv_w, lru_conv_b, lru_w_a, lru_b_a, lru_w_x, lru_b_x, lru_lambda,
           lru_w_out, loss_target, m_ffn1_norm, m_ffn1_w_in, m_ffn1_w_out, m_mix_norm, m_ffn2_norm,
           m_ffn2_w_in, m_ffn2_w_out, m_final_norm, m_s5_w_in, m_s5_lam_re, m_s5_lam_im, m_s5_log_dt,
           m_s5_b_re, m_s5_b_im, m_s5_c_re, m_s5_c_im, m_s5_d, m_s5_w_out, m_sb_w_qkv, m_sb_w_out, m_lru_w_in,
           m_lru_conv_w, m_lru_conv_b, m_lru_w_a, m_lru_b_a, m_lru_w_x, m_lru_b_x, m_lru_lambda, m_lru_w_out,
           v_ffn1_norm, v_ffn1_w_in, v_ffn1_w_out, v_mix_norm, v_ffn2_norm, v_ffn2_w_in, v_ffn2_w_out,
           v_final_norm, v_s5_w_in, v_s5_lam_re, v_s5_lam_im, v_s5_log_dt, v_s5_b_re, v_s5_b_im, v_s5_c_re,
           v_s5_c_im, v_s5_d, v_s5_w_out, v_sb_w_qkv, v_sb_w_out, v_lru_w_in, v_lru_conv_w, v_lru_conv_b,
           v_lru_w_a, v_lru_b_a, v_lru_w_x, v_lru_b_x, v_lru_lambda, v_lru_w_out):
    a = dict(locals())
    assert list(a) == INPUTS
    x, y, c, _ = _place()
    chip = 2 * x + y
    everyone = ("x", "y", "c")

    sh = _Sharded(a)
    loss, dx, grads = _forward_backward(a["x"][0], a["loss_target"][0], a, sh)
    loss = lax.psum(loss, everyone)

    small = list(SHARDED_SMALL)

    def sum_chips(mine, got):
        rows = _rows2d(mine)
        return rowwise(lambda o, r: ((o + r[0].astype(F32)) + r[1].astype(F32)) + r[2].astype(F32),
                       [rows, got.reshape((3,) + rows.shape)], [], [(rows.shape[1], F32)], name="sum_chips")

    def own_block(full, name, d):
        return lax.dynamic_slice_in_dim(full, chip * a[name].shape[d], a[name].shape[d], axis=d)

    keys = [(n, lo, hi) for n in SHARDED_BIG for lo, hi, _ in sh.pieces[n]]
    partial = [sum_chips(own_block(sh.piece_grad(n, lo, hi), n, SHARDED_BIG[n]), sh.received[(n, lo)]) for n, lo, hi in keys]
    partial.append(sum_chips(_pack([own_block(sh.small_grads[n], n, d) for n, d in SHARDED_SMALL.items()]), sh.small_received))
    other = swap_cores(partial)

    def adam_sharded(wv, ga, gb, m, v):
        shape = wv.shape
        res = rowwise(lambda w_, a_, b_, mm_, vv_: (a_ + b_,) + _adamw(w_, a_ + b_, mm_, vv_),
                      [_rows2d(wv), ga, gb, _rows2d(m), _rows2d(v)], [], [(shape[-1], F32)] * 4, name="adamw_sharded")
        return [r.reshape(shape) for r in res]

    out_grad, out_delta, out_m, out_v = {}, {}, {}, {}
    for n in SHARDED_BIG:
        at = [i for i, k in enumerate(keys) if k[0] == n]
        out_grad[n], out_delta[n], out_m[n], out_v[n] = adam_sharded(a[n], [partial[i] for i in at], [other[i] for i in at],
                                                                     a["m_" + n], a["v_" + n])
    small_shapes = [a[n].shape for n in small]
    sp = [_pack([a[pre + n] for n in small]) for pre in ("", "m_", "v_")]
    g_, d_, m_, v_ = adam_sharded(sp[0], [partial[-1]], [other[-1]], sp[1], sp[2])
    for n, gg, dd, mm_, vv in zip(small, _unpack(g_, small_shapes), _unpack(d_, small_shapes), _unpack(m_, small_shapes),
                                  _unpack(v_, small_shapes)):
        out_grad[n], out_delta[n], out_m[n], out_v[n] = gg, dd, mm_, vv

    rep_shapes = [a[n].shape for n in REPLICATED]
    rep_all = gather_devices(_pack([grads[n] for n in REPLICATED], row_align=REPLICATED_ROW_TILE))
    rp = [_pack([a[pre + n] for n in REPLICATED], row_align=REPLICATED_ROW_TILE) for pre in ("", "m_", "v_")]

    def adam_rep(w_, g8, mm_, vv_):
        g = g8[0]
        for k in range(1, N_DEVICES):
            g = g + g8[k]
        return (g,) + _adamw(w_, g, mm_, vv_)

    g_, d_, m_, v_ = rowwise(adam_rep, [rp[0], rep_all, rp[1], rp[2]], [], [(PACK_LANES, F32)] * 4, name="adamw_replicated")
    for n, gg, dd, mm_, vv in zip(REPLICATED, _unpack(g_, rep_shapes), _unpack(d_, rep_shapes), _unpack(m_, rep_shapes),
                                  _unpack(v_, rep_shapes)):
        out_grad[n], out_delta[n], out_m[n], out_v[n] = gg, dd, mm_, vv

    return (loss, dx[None], *[out_grad[n] for n in WEIGHTS], *[out_delta[n] for n in WEIGHTS], *[out_m[n] for n in WEIGHTS],
            *[out_v[n] for n in WEIGHTS])
```

```python
import functools
import math

import jax
import jax.numpy as jnp
from jax import lax
from jax.experimental import pallas as pl
from jax.experimental.pallas import tpu as pltpu

F32 = jnp.float32
BF16 = jnp.bfloat16

VMEM_LIMIT_BYTES = 56 * 1024 * 1024

RMS_EPS = 1e-6
D_FF = 2816
S5_GROUP = 16
S5_STATE = 64
SB_HEAD_DIM = 64
LRU_BLOCK_WIDTH = 256
LRU_CONV = 4
LRU_C = 8.0
N_MIXERS = 3

ADAM_LR = 0.001
ADAM_B1 = 0.9
ADAM_B2 = 0.999
ADAM_EPS = 1e-08
ADAM_WD = 0.01
ADAM_STEP = 10


def _params(semantics):
    return pltpu.CompilerParams(dimension_semantics=semantics, vmem_limit_bytes=VMEM_LIMIT_BYTES)


def _tile(dim, prefs):
    for t in prefs:
        if t <= dim and dim % t == 0:
            return t
    return dim


def _mm_call(name, grid, a, a_spec, b, b_spec, dims, extras, outs, epilogue, acc_shape, job=None):
    n_extra, n_out, n_k = len(extras), len(outs), grid[-1]

    def body(a_ref, b_ref, *rest):
        extra_refs = rest[:n_extra]
        out_refs = rest[n_extra:n_extra + n_out]
        acc_ref = rest[n_extra + n_out]
        k = pl.program_id(len(grid) - 1)
        part = lax.dot_general(a_ref[...].astype(BF16), b_ref[...].astype(BF16), dims, preferred_element_type=F32)

        @pl.when(k == 0)
        def _():
            acc_ref[...] = part

        @pl.when(k > 0)
        def _():
            acc_ref[...] += part

        @pl.when(k == n_k - 1)
        def _():
            res = epilogue(acc_ref[...], *[r[...] for r in extra_refs])
            if not isinstance(res, (tuple, list)):
                res = (res,)
            for o_ref, r in zip(out_refs, res):
                o_ref[...] = r.astype(o_ref.dtype)

    sem = ("arbitrary",) * len(grid) if job is not None else ("parallel",) * (len(grid) - 1) + ("arbitrary",)
    body, x_in, x_out, x_shapes, x_scratch, x_args = carry(job, body, grid, 2 + n_extra, n_out)
    res = pl.pallas_call(
        body,
        name=name,
        grid=grid,
        in_specs=[a_spec, b_spec] + [s for _, s in extras] + x_in,
        out_specs=[s for _, s in outs] + x_out,
        out_shape=[s for s, _ in outs] + x_shapes,
        scratch_shapes=[pltpu.VMEM(acc_shape, F32)] + x_scratch,
        compiler_params=_params(sem),
    )(a, b, *[x for x, _ in extras], *x_args)
    return res


def _fit(dim, target, align=128):
    best = None
    for t in range(align, min(dim, target) + 1, align):
        if dim % t == 0:
            best = t
    return best or dim


def mm(a, b, *, name, ta=False, tb=False, extras=(), epilogue=None, out_dtypes=(F32,), tm=1024, tn=1024, tk=1024):
    m, kdim = (a.shape[1], a.shape[0]) if ta else a.shape
    n = b.shape[0] if tb else b.shape[1]
    tm = _fit(m, tm)
    tn = _fit(n, tn)
    tk = _fit(kdim, tk)
    grid = (m // tm, n // tn, kdim // tk)
    a_spec = pl.BlockSpec((tk, tm), lambda i, j, k: (k, i)) if ta else pl.BlockSpec((tm, tk), lambda i, j, k: (i, k))
    b_spec = pl.BlockSpec((tn, tk), lambda i, j, k: (j, k)) if tb else pl.BlockSpec((tk, tn), lambda i, j, k: (k, j))
    dims = (((0 if ta else 1,), (1 if tb else 0,)), ((), ()))
    o_spec = pl.BlockSpec((tm, tn), lambda i, j, k: (i, j))
    if epilogue is None:
        epilogue = lambda acc: acc
    res = _mm_call(name, grid, a, a_spec, b, b_spec, dims, [(x, o_spec) for x in extras],
                   [(jax.ShapeDtypeStruct((m, n), dt), o_spec) for dt in out_dtypes], epilogue, (tm, tn))
    return res[0] if len(res) == 1 else res


def rowwise(fn, rows, consts, outs, sums=(), *, name, tm=256):
    m = rows[0].shape[0]
    pieces = [x if isinstance(x, (list, tuple)) else [x] for x in rows]
    tm = _tile(math.gcd(*[p.shape[-2] for ps in pieces for p in ps]), (tm, 128, 64, 32, 16, 8))
    starts = [[sum(q.shape[-2] for q in ps[:k]) // tm for k in range(len(ps) + 1)] for ps in pieces]
    flat = [p for ps in pieces for p in ps]
    n_rows, n_consts, n_outs = len(flat), len(consts), len(outs)

    def body(*refs):
        i = pl.program_id(0)
        in_vals, at = [], 0
        for ps, st in zip(pieces, starts):
            val = refs[at + len(ps) - 1][...]
            for k in reversed(range(len(ps) - 1)):
                val = jnp.where(i < st[k + 1], refs[at + k][...], val)
            in_vals.append(val)
            at += len(ps)
        in_vals += [r[...] for r in refs[n_rows:n_rows + n_consts]]
        out_refs = refs[n_rows + n_consts:n_rows + n_consts + n_outs]
        sum_refs = refs[n_rows + n_consts + n_outs:]
        res = fn(*in_vals)
        if not isinstance(res, (tuple, list)):
            res = (res,)
        for o_ref, r in zip(out_refs, res[:n_outs]):
            o_ref[...] = r.astype(o_ref.dtype)
        if sum_refs:
            first = pl.program_id(0) == 0

            @pl.when(first)
            def _():
                for s_ref, r in zip(sum_refs, res[n_outs:]):
                    s_ref[...] = r.astype(F32)

            @pl.when(jnp.logical_not(first))
            def _():
                for s_ref, r in zip(sum_refs, res[n_outs:]):
                    s_ref[...] += r.astype(F32)

    def whole(shape):
        nd = len(shape)
        return pl.BlockSpec(shape, lambda i: (0,) * nd)

    def row_spec(x, lo, hi):
        at = lambda i: jnp.clip(i - lo, 0, hi - lo - 1)
        if x.ndim == 3:
            return pl.BlockSpec((x.shape[0], tm, x.shape[2]), lambda i: (0, at(i), 0))
        return pl.BlockSpec((tm, x.shape[1]), lambda i: (at(i), 0))

    in_specs = [row_spec(p, st[k], st[k + 1]) for ps, st in zip(pieces, starts) for k, p in enumerate(ps)]
    in_specs += [whole(c.shape) for c in consts]
    out_specs = [pl.BlockSpec((tm, nc), lambda i: (i, 0)) for nc, _ in outs] + [whole(tuple(s)) for s in sums]
    out_shape = [jax.ShapeDtypeStruct((m, nc), dt) for nc, dt in outs] + [jax.ShapeDtypeStruct(tuple(s), F32) for s in sums]
    res = pl.pallas_call(
        body,
        name=name,
        grid=(m // tm,),
        in_specs=in_specs,
        out_specs=out_specs,
        out_shape=out_shape,
        compiler_params=_params(("arbitrary",) if sums else ("parallel",)),
    )(*flat, *consts)
    return res[0] if len(res) == 1 else res


def _rms(h, g):
    return h * lax.rsqrt(jnp.mean(h * h, axis=-1, keepdims=True) + RMS_EPS) * g


def _sigmoid(x):
    return 1.0 / (1.0 + jnp.exp(-x))


def _silu_mul(g, u):
    return g * _sigmoid(g) * u


def _gelu(x):
    return 0.5 * x * (1.0 + jnp.tanh(math.sqrt(2.0 / math.pi) * (x + 0.044715 * (x * x * x))))


def _softplus(x):
    return jnp.maximum(x, 0.0) + jnp.log(1.0 + jnp.exp(-jnp.abs(x)))


def rms_fwd(h, g, name):
    return rowwise(lambda x, gg: _rms(x, gg), [h], [g.reshape(1, -1)], [(h.shape[1], BF16)], name=name)


def norm_input_bwd(a, a_spec, n_k, w, h, g, dres, name, tm=512):
    m, d = h.shape
    tk = w.shape[1] // n_k
    tm = _fit(m, tm)

    def body(a_ref, w_ref, h_ref, g_ref, dres_ref, dh_ref, dg_ref, acc_ref):
        i, k = pl.program_id(0), pl.program_id(1)
        part = lax.dot_general(a_ref[...].astype(BF16), w_ref[...], _NT, preferred_element_type=F32)

        @pl.when(k == 0)
        def _():
            acc_ref[...] = part

        @pl.when(k > 0)
        def _():
            acc_ref[...] += part

        @pl.when(k == n_k - 1)
        def _():
            _, vjp = jax.vjp(_rms, h_ref[...], g_ref[...])
            dx, dg = vjp(acc_ref[...])
            dh_ref[...] = dres_ref[...] + dx

            @pl.when(i == 0)
            def _():
                dg_ref[...] = dg

            @pl.when(i > 0)
            def _():
                dg_ref[...] += dg

    row = pl.BlockSpec((tm, d), lambda i, k: (i, 0))
    vec = pl.BlockSpec((1, d), lambda i, k: (0, 0))
    dh, dg = pl.pallas_call(
        body,
        name=name,
        grid=(m // tm, n_k),
        in_specs=[a_spec(tm, tk), pl.BlockSpec((d, tk), lambda i, k: (0, k)), row, vec, row],
        out_specs=[row, vec],
        out_shape=[jax.ShapeDtypeStruct((m, d), F32), jax.ShapeDtypeStruct((1, d), F32)],
        scratch_shapes=[pltpu.VMEM((tm, d), F32)],
        compiler_params=_params(("arbitrary", "arbitrary")),
    )(a, w, h, g.reshape(1, d), dres)
    return dh, dg.reshape(-1)


def _rows_spec(tm, tk):
    return pl.BlockSpec((tm, tk), lambda i, k: (i, k))


def ffn_in(h, g, w_in, tag, job=None):
    m, d = h.shape
    f = w_in.shape[1] // 2
    tm, tn = _fit(m, 512), _fit(f, 1408)
    nj = f // tn

    def body(h_ref, g_ref, wg_ref, wu_ref, hn_ref, gu_ref, act_ref, hn_scr):
        @pl.when(pl.program_id(1) == 0)
        def _():
            hn = _rms(h_ref[...], g_ref[...]).astype(BF16)
            hn_scr[...] = hn
            hn_ref[...] = hn

        a = hn_scr[...]
        gate = jnp.dot(a, wg_ref[...], preferred_element_type=F32)
        up = jnp.dot(a, wu_ref[...], preferred_element_type=F32)
        gu_ref[0] = gate.astype(BF16)
        gu_ref[1] = up.astype(BF16)
        act_ref[...] = _silu_mul(gate, up).astype(BF16)

    grid = (m // tm, nj)
    body, x_in, x_out, x_shapes, x_scratch, x_args = carry(job, body, grid, 4, 3)
    res = pl.pallas_call(
        body,
        name=f"{tag}_in",
        grid=grid,
        in_specs=[pl.BlockSpec((tm, d), lambda i, j: (i, 0)), pl.BlockSpec((1, d), lambda i, j: (0, 0)),
                  pl.BlockSpec((d, tn), lambda i, j: (0, j)), pl.BlockSpec((d, tn), lambda i, j: (0, nj + j))] + x_in,
        out_specs=[pl.BlockSpec((tm, d), lambda i, j: (i, 0)), pl.BlockSpec((2, tm, tn), lambda i, j: (0, i, j)),
                   pl.BlockSpec((tm, tn), lambda i, j: (i, j))] + x_out,
        out_shape=[jax.ShapeDtypeStruct((m, d), BF16), jax.ShapeDtypeStruct((2, m, f), BF16),
                   jax.ShapeDtypeStruct((m, f), BF16)] + x_shapes,
        scratch_shapes=[pltpu.VMEM((tm, d), BF16)] + x_scratch,
        compiler_params=_params(("arbitrary", "arbitrary")),
    )(h, g.reshape(1, d), w_in, w_in, *x_args)
    return res[:3], res[3:]


def ffn_fwd(h, g, w_in, w_out, tag, job=None):
    (hn, gu, act), carried = ffn_in(h, g, w_in, tag, job)
    h_new = mm(act, w_out, name=f"{tag}_out", extras=(h,), epilogue=lambda acc, res: res + 0.5 * acc, tm=512, tk=2816)
    return h_new, (h, hn, gu, act), carried


def ffn_bwd(dh_new, saved, g, w_in, w_out, tag, jobs=(None, None)):
    h, hn, gu, act = saved
    m, d = h.shape
    f = w_out.shape[0]

    def act_bwd(acc, gu_blk):
        _, vjp = jax.vjp(_silu_mul, gu_blk[0].astype(F32), gu_blk[1].astype(F32))
        return jnp.stack(vjp(0.5 * acc))

    tm, tn = _fit(m, 512), _fit(f, 1408)
    pair = pl.BlockSpec((2, tm, tn), lambda i, j, k: (0, i, j))
    dgu, *carried0 = _mm_call(f"{tag}_dgu", (m // tm, f // tn, 1), dh_new, pl.BlockSpec((tm, d), lambda i, j, k: (i, 0)), w_out,
                              pl.BlockSpec((tn, d), lambda i, j, k: (j, 0)), _NT, [(gu, pair)],
                              [(jax.ShapeDtypeStruct((2, m, f), BF16), pair)], act_bwd, (tm, tn), jobs[0])
    dw_out = mm(act, dh_new, ta=True, name=f"{tag}_dwout", epilogue=lambda acc: 0.5 * acc, tm=1408)

    tn, tk = _fit(f, 1408), _fit(m, 2048)
    nh = f // tn
    dw_in, *carried1 = _mm_call(f"{tag}_dwin", (1, 2 * nh, m // tk), hn, pl.BlockSpec((tk, d), lambda i, j, k: (k, 0)), dgu,
                                pl.BlockSpec((None, tk, tn), lambda i, j, k: (j // nh, k, j % nh)), _TN, [],
                                [(jax.ShapeDtypeStruct((d, 2 * f), F32), pl.BlockSpec((d, tn), lambda i, j, k: (0, j)))],
                                lambda acc: acc, (d, tn), jobs[1])

    nkh = f // _fit(f, 2816)
    dh, dg = norm_input_bwd(dgu, lambda tm, tk: pl.BlockSpec((None, tm, tk), lambda i, k: (k // nkh, i, k % nkh)), 2 * nkh,
                            w_in, h, g, dh_new, f"{tag}_dhn")
    return dh, dg, dw_in, dw_out, carried0, carried1


def loss_fwd_bwd(h, g, target):
    d = h.shape[1]

    def fn(x, t, gg):
        y, vjp = jax.vjp(_rms, x, gg)
        err = y - t
        dx, dg = vjp(err * (1.0 / d))
        part = 0.5 * jnp.sum(jnp.sum(err * err, axis=1, keepdims=True), axis=0, keepdims=True) * (1.0 / d)
        return dx, dg, jnp.broadcast_to(part, (1, 128))

    dh, dg, loss = rowwise(fn, [h, target], [g.reshape(1, -1)], [(d, F32)], [(1, d), (1, 128)], name="loss_head")
    return loss[0, 0], dh, dg.reshape(-1)


def _shift_down(v, d, fill):
    rows = lax.broadcasted_iota(jnp.int32, v.shape, 0)
    return jnp.where(rows < d, fill, pltpu.roll(v, d, 0))


def _shift_up(v, d, fill):
    t = v.shape[0]
    rows = lax.broadcasted_iota(jnp.int32, v.shape, 0)
    return jnp.where(rows >= t - d, fill, pltpu.roll(v, t - d, 0))


def _scan_fwd(a, x):
    d = 1
    while d < a.shape[0]:
        x = x + a * _shift_down(x, d, 0.0)
        a = a * _shift_down(a, d, 1.0)
        d *= 2
    return a, x


def _scan_bwd(b, x):
    d = 1
    while d < b.shape[0]:
        x = x + b * _shift_up(x, d, 0.0)
        b = b * _shift_up(b, d, 1.0)
        d *= 2
    return b, x


def _rows_before(cur, prev8, s):
    r = pltpu.roll(cur, s, 0)
    p = pltpu.roll(prev8, s, 0)
    rows = lax.broadcasted_iota(jnp.int32, prev8.shape, 0)
    return jnp.concatenate([jnp.where(rows < s, p, r[:8]), r[8:]], axis=0)


def _rows_after(cur, next8, s):
    t = cur.shape[0]
    r = pltpu.roll(cur, t - s, 0)
    p = pltpu.roll(next8, 8 - s, 0)
    rows = lax.broadcasted_iota(jnp.int32, next8.shape, 0)
    return jnp.concatenate([r[:t - 8], jnp.where(rows >= 8 - s, p, r[t - 8:])], axis=0)


LRU_CHUNK = 256


def _neg_expm1(y):
    small = -y * (1.0 + 0.5 * y * (1.0 + y * (1.0 / 3.0)))
    return jnp.where(y > -0.01, small, 1.0 - jnp.exp(y))


def _lru_gate(xc, pre_a, pre_x, lam):
    r = _sigmoid(pre_a)
    ig = _sigmoid(pre_x)
    log_a = (-LRU_C * r) * _softplus(-lam)
    return jnp.exp(log_a), (ig * xc) * jnp.sqrt(_neg_expm1(2.0 * log_a))


def _lru_conv(br, prev8, conv_w, conv_b):
    taps = [br] + [_rows_before(br, prev8, s) for s in range(1, LRU_CONV)]
    xc = conv_b
    for k in range(LRU_CONV):
        xc = xc + conv_w[k:k + 1, :] * taps[LRU_CONV - 1 - k]
    return xc, taps


def _lru_pre(xcb, w_ref, bias):
    nb = w_ref.shape[0]
    bw = w_ref.shape[1]
    return jnp.concatenate(
        [jnp.dot(xcb[:, n * bw:(n + 1) * bw], w_ref[n], preferred_element_type=F32) for n in range(nb)], axis=1) + bias


def lru_scan_fwd(bgr, conv_w, conv_b, wa, ba, wx, bx, lam):
    l, w2 = bgr.shape
    w = w2 // 2
    t = _tile(l, (LRU_CHUNK, 128, 64, 32, 16, 8))

    def body(bg_ref, br_ref, cw_ref, cb_ref, wa_ref, ba_ref, wx_ref, bx_ref, lam_ref, y_ref, h_ref, tail_ref, hprev_ref):
        @pl.when(pl.program_id(0) == 0)
        def _():
            tail_ref[...] = jnp.zeros_like(tail_ref)
            hprev_ref[...] = jnp.zeros_like(hprev_ref)

        br = br_ref[...]
        xc, _ = _lru_conv(br, tail_ref[...], cw_ref[...], cb_ref[...])
        xcb = xc.astype(BF16)
        a, gx = _lru_gate(xc, _lru_pre(xcb, wa_ref, ba_ref[...]), _lru_pre(xcb, wx_ref, bx_ref[...]), lam_ref[...])
        acum, x = _scan_fwd(a, gx)
        h = x + acum * hprev_ref[pl.ds(7, 1), :]
        y_ref[...] = (_gelu(bg_ref[...]) * h).astype(y_ref.dtype)
        h_ref[...] = h
        tail_ref[...] = br[t - 8:, :]
        hprev_ref[...] = h[t - 8:, :]

    def whole(x):
        nd = x.ndim
        return pl.BlockSpec(x.shape, lambda i: (0,) * nd)

    consts = [conv_w, conv_b, wa, ba, wx, bx, lam]
    return pl.pallas_call(
        body,
        name="lru_scan_fwd",
        grid=(l // t,),
        in_specs=[pl.BlockSpec((t, w), lambda i: (i, 0)), pl.BlockSpec((t, w), lambda i: (i, 1))] + [whole(c) for c in consts],
        out_specs=[pl.BlockSpec((t, w), lambda i: (i, 0)), pl.BlockSpec((t, w), lambda i: (i, 0))],
        out_shape=[jax.ShapeDtypeStruct((l, w), BF16), jax.ShapeDtypeStruct((l, w), F32)],
        scratch_shapes=[pltpu.VMEM((8, w), F32), pltpu.VMEM((8, w), F32)],
        compiler_params=_params(("arbitrary",)),
    )(bgr, bgr, *consts)


def lru_scan_bwd(dy, bgr, hseq, conv_w, conv_b, wa, ba, wx, bx, lam):
    l, w2 = bgr.shape
    w = w2 // 2
    t = _tile(l, (LRU_CHUNK, 128, 64, 32, 16, 8))
    nc = l // t
    nb, bw = wa.shape[0], wa.shape[1]

    def body(dy_ref, bg_ref, br_ref, brh_ref, h_ref, hh_ref, cw_ref, cb_ref, wa_ref, ba_ref, wx_ref, bx_ref, lam_ref,
             dbgr_ref, dcw_ref, dcb_ref, dwa_ref, dba_ref, dwx_ref, dbx_ref, dlam_ref, dxcn_ref, carry_ref):
        i = pl.program_id(0)
        has_prev = (i < nc - 1).astype(F32)

        @pl.when(i == 0)
        def _():
            dxcn_ref[...] = jnp.zeros_like(dxcn_ref)
            carry_ref[...] = jnp.zeros_like(carry_ref)

        br = br_ref[...]
        cw = cw_ref[...]
        xc, taps = _lru_conv(br, brh_ref[...] * has_prev, cw, cb_ref[...])
        xcb = xc.astype(BF16)
        (a, _), gate_vjp = jax.vjp(_lru_gate, xc, _lru_pre(xcb, wa_ref, ba_ref[...]), _lru_pre(xcb, wx_ref, bx_ref[...]),
                                   lam_ref[...])
        hs = h_ref[...]
        _, out_vjp = jax.vjp(lambda g_, h_: _gelu(g_) * h_, bg_ref[...], hs)
        dbg, dhs = out_vjp(dy_ref[...])
        bcum, x = _scan_bwd(_shift_up(a, 1, 1.0), dhs)
        dh = x + bcum * carry_ref[pl.ds(0, 1), :]
        da = dh * _rows_before(hs, hh_ref[...] * has_prev, 1)
        dxc, dpa, dpx, dlam = gate_vjp((da, dh))
        dpab, dpxb = dpa.astype(BF16), dpx.astype(BF16)
        nt = (((1,), (1,)), ((), ()))
        tn = (((0,), (0,)), ((), ()))
        dxb, dwa, dwx = [], [], []
        for n in range(nb):
            sl = slice(n * bw, (n + 1) * bw)
            dxb.append(lax.dot_general(dpab[:, sl], wa_ref[n], nt, preferred_element_type=F32)
                       + lax.dot_general(dpxb[:, sl], wx_ref[n], nt, preferred_element_type=F32))
            dwa.append(lax.dot_general(xcb[:, sl], dpab[:, sl], tn, preferred_element_type=F32))
            dwx.append(lax.dot_general(xcb[:, sl], dpxb[:, sl], tn, preferred_element_type=F32))
        dxc = dxc + jnp.concatenate(dxb, axis=1)
        ups = [dxc] + [_rows_after(dxc, dxcn_ref[...], s) for s in range(1, LRU_CONV)]
        dbr = cw[LRU_CONV - 1:LRU_CONV, :] * ups[0]
        for k in range(LRU_CONV - 1):
            dbr = dbr + cw[k:k + 1, :] * ups[LRU_CONV - 1 - k]
        dbgr_ref[:, :w] = dbg.astype(dbgr_ref.dtype)
        dbgr_ref[:, w:] = dbr.astype(dbgr_ref.dtype)
        dcw = jnp.concatenate([jnp.sum(dxc * taps[LRU_CONV - 1 - k], axis=0, keepdims=True) for k in range(LRU_CONV)], axis=0)
        sums = [(dcw_ref, dcw), (dcb_ref, jnp.sum(dxc, axis=0, keepdims=True)), (dwa_ref, jnp.stack(dwa)),
                (dba_ref, jnp.sum(dpa, axis=0, keepdims=True)), (dwx_ref, jnp.stack(dwx)),
                (dbx_ref, jnp.sum(dpx, axis=0, keepdims=True)), (dlam_ref, dlam)]

        @pl.when(i == 0)
        def _():
            for ref, val in sums:
                ref[...] = val

        @pl.when(i > 0)
        def _():
            for ref, val in sums:
                ref[...] += val

        dxcn_ref[...] = dxc[:8, :]
        carry_ref[...] = (a * dh)[:8, :]

    def whole(shape):
        nd = len(shape)
        return pl.BlockSpec(tuple(shape), lambda i: (0,) * nd)

    consts = [conv_w, conv_b, wa, ba, wx, bx, lam]
    t8 = t // 8
    rev = lambda i: nc - 1 - i
    halo = lambda i: jnp.maximum(rev(i) * t8 - 1, 0)
    in_specs = [
        pl.BlockSpec((t, w), lambda i: (rev(i), 0)),
        pl.BlockSpec((t, w), lambda i: (rev(i), 0)),
        pl.BlockSpec((t, w), lambda i: (rev(i), 1)),
        pl.BlockSpec((8, w), lambda i: (halo(i), 1)),
        pl.BlockSpec((t, w), lambda i: (rev(i), 0)),
        pl.BlockSpec((8, w), lambda i: (halo(i), 0)),
    ] + [whole(c.shape) for c in consts]
    sum_shapes = [conv_w.shape, conv_b.shape, wa.shape, ba.shape, wx.shape, bx.shape, lam.shape]
    res = pl.pallas_call(
        body,
        name="lru_scan_bwd",
        grid=(nc,),
        in_specs=in_specs,
        out_specs=[pl.BlockSpec((t, w2), lambda i: (rev(i), 0))] + [whole(s) for s in sum_shapes],
        out_shape=[jax.ShapeDtypeStruct((l, w2), BF16)] + [jax.ShapeDtypeStruct(tuple(s), F32) for s in sum_shapes],
        scratch_shapes=[pltpu.VMEM((8, w), F32), pltpu.VMEM((8, w), F32)],
        compiler_params=_params(("arbitrary",)),
    )(dy, bgr, bgr, bgr, hseq, hseq, *consts)
    return res[0], res[1:]


def lru_fwd(hn, p):
    bgr = mm(hn, p["w_in"], name="lru_in")
    y, hseq = lru_scan_fwd(bgr, p["conv_w"], p["conv_b"], p["wa"], p["ba"], p["wx"], p["bx"], p["lam"])
    return y, (hn, bgr, hseq, y)


def lru_bwd(dmixed, saved, p):
    hn, bgr, hseq, y = saved
    dy = mm(dmixed, p["w_out"], tb=True, name="lru_dy")
    dw_out = mm(y, dmixed, ta=True, name="lru_dwout")
    dbgr, (dcw, dcb, dwa, dba, dwx, dbx, dlam) = lru_scan_bwd(dy, bgr, hseq, p["conv_w"], p["conv_b"], p["wa"], p["ba"],
                                                               p["wx"], p["bx"], p["lam"])
    dw_in = mm(hn, dbgr, ta=True, name="lru_dwin")
    grads = dict(w_in=dw_in, conv_w=dcw, conv_b=dcb, wa=dwa, ba=dba, wx=dwx, bx=dbx, lam=dlam, w_out=dw_out)
    return (dbgr, p["w_in"]), grads


SB_BLOCK = 256
SB_BLOCK_Q = 1024
_NT = (((1,), (1,)), ((), ()))
_TN = (((0,), (0,)), ((), ()))


def _running_sums(x, tri, total_col):
    xb = x.astype(BF16)
    run = jnp.dot(xb, tri, preferred_element_type=F32)
    return run, xb, run[:, total_col:total_col + 1]


def _tri(n, cmp):
    r = lax.broadcasted_iota(jnp.int32, (n, n), 0)
    c = lax.broadcasted_iota(jnp.int32, (n, n), 1)
    return cmp(r, c).astype(BF16)


SB_PAIR = 2 * SB_HEAD_DIM


def _pair_masks(x2, scale=None):
    lane = lax.broadcasted_iota(jnp.int32, x2.shape, 1)
    zero = jnp.zeros_like(x2)
    a, b = jnp.where(lane < SB_HEAD_DIM, x2, zero), jnp.where(lane >= SB_HEAD_DIM, x2, zero)
    if scale is not None:
        a, b = a * scale, b * scale
    return a, b


def _causal(shape, q0, k0):
    return lax.broadcasted_iota(jnp.int32, shape, 1) + k0 < lax.broadcasted_iota(jnp.int32, shape, 0) + q0


def _sb_blocks(l):
    bk = _tile(l, (SB_BLOCK, 128))
    bq = _tile(l, (SB_BLOCK_Q, 2 * SB_BLOCK, SB_BLOCK, 128))
    return bq, bk


def sb_pair_fwd(qkv, job=None):
    l, d3 = qkv.shape
    d = d3 // 3
    npair = d // SB_PAIR
    bq, bk = _sb_blocks(l)
    ratio = bq // bk
    scale = SB_HEAD_DIM ** -0.5
    t_suf = _tri(bk, lambda r, c: r > c)

    def body(q_ref, k_ref, v_ref, tsuf_ref, o_ref, ltot_ref):
        i = pl.program_id(1)
        qs = _pair_masks(q_ref[...], scale)
        tsuf = tsuf_ref[...]

        def tile(j, carry, masked, r0=0):
            rows = pl.ds(pl.multiple_of(j * bk, bk), bk)
            k2 = k_ref[rows, :]
            v2 = v_ref[rows, :]
            out = []
            for q1, (c_r, acc) in zip(qs, carry):
                z = lax.dot_general(q1[r0:], k2, _NT, preferred_element_type=F32)
                lk = -_softplus(z)
                if masked:
                    causal = _causal(z.shape, i * bq + r0, j * bk)
                    lk = jnp.where(causal, lk, 0.0)
                later, lkb, later0 = _running_sums(lk, tsuf, 0)
                w = jnp.exp(z + lk + c_r[r0:] + later)
                if masked:
                    w = jnp.where(causal, w, 0.0)
                c_new = c_r[r0:] + later0 + lkb[:, 0:1].astype(F32)
                acc_new = acc[r0:] + jnp.dot(w.astype(BF16), v2, preferred_element_type=F32)
                if r0:
                    c_new, acc_new = jnp.concatenate([c_r[:r0], c_new]), jnp.concatenate([acc[:r0], acc_new])
                out.append((c_new, acc_new))
            return tuple(out)

        carry = ((jnp.zeros((bq, 1), F32), jnp.zeros((bq, SB_PAIR), F32)),) * 2
        for dgl in reversed(range(ratio)):
            carry = tile(i * ratio + dgl, carry, True, dgl * bk)
        (c_a, acc_a), (c_b, acc_b) = lax.fori_loop(0, i * ratio, lambda jj, c: tile(i * ratio - 1 - jj, c, False), carry)
        lane = lax.broadcasted_iota(jnp.int32, acc_a.shape, 1)
        o_ref[...] = jnp.where(lane < SB_HEAD_DIM, acc_a, acc_b).astype(o_ref.dtype)
        ltot_ref[...] = jnp.where(lax.broadcasted_iota(jnp.int32, (bq, 2), 1) == 0, c_a, c_b)

    grid = (npair, l // bq)
    body, x_in, x_out, x_shapes, x_scratch, x_args = carry(job, body, grid, 4, 2)
    res = pl.pallas_call(
        body,
        name="sb_attn_fwd",
        grid=grid,
        in_specs=[
            pl.BlockSpec((bq, SB_PAIR), lambda p, i: (i, p)),
            pl.BlockSpec((l, SB_PAIR), lambda p, i: (0, npair + p)),
            pl.BlockSpec((l, SB_PAIR), lambda p, i: (0, 2 * npair + p)),
            pl.BlockSpec((bk, bk), lambda p, i: (0, 0)),
        ] + x_in,
        out_specs=[pl.BlockSpec((bq, SB_PAIR), lambda p, i: (i, p)), pl.BlockSpec((None, bq, 2), lambda p, i: (p, i, 0))] + x_out,
        out_shape=[jax.ShapeDtypeStruct((l, d), BF16), jax.ShapeDtypeStruct((npair, l, 2), F32)] + x_shapes,
        scratch_shapes=x_scratch,
        compiler_params=_params(("arbitrary", "arbitrary")),
    )(qkv, qkv, qkv, t_suf, *x_args)
    return res[0], res[1], res[2:]


def sb_pair_bwd(qkv, do, ltot, job=None):
    l, d3 = qkv.shape
    d = d3 // 3
    npair = d // SB_PAIR
    bq, bk = _sb_blocks(l)
    ratio = bq // bk
    scale = SB_HEAD_DIM ** -0.5
    t_inc = _tri(bk, lambda r, c: r <= c)
    t_exc = _tri(bk, lambda r, c: r < c)

    def body(q_ref, k_ref, v_ref, do_ref, ltot_ref, tinc_ref, texc_ref, dq_ref, dk_ref, dv_ref):
        i = pl.program_id(1)

        @pl.when(i == 0)
        def _():
            dk_ref[...] = jnp.zeros_like(dk_ref)
            dv_ref[...] = jnp.zeros_like(dv_ref)

        qs = _pair_masks(q_ref[...], scale)
        dos = _pair_masks(do_ref[...])
        lt = ltot_ref[...]
        ltots = (lt[:, 0:1], lt[:, 1:2])
        tinc = tinc_ref[...]
        texc = texc_ref[...]

        def tile(j, carry, masked, r0=0):
            rows = pl.ds(pl.multiple_of(j * bk, bk), bk)
            k2 = k_ref[rows, :]
            v2 = v_ref[rows, :]
            out = []
            dk2 = dv2 = None
            for q1, do1, ltot1, (c_l, c_p, dq) in zip(qs, dos, ltots, carry):
                q1s, do1s = q1[r0:], do1[r0:]
                z = lax.dot_general(q1s, k2, _NT, preferred_element_type=F32)
                lk = -_softplus(z)
                if masked:
                    causal = _causal(z.shape, i * bq + r0, j * bk)
                    lk = jnp.where(causal, lk, 0.0)
                log_beta = z + lk
                upto, _, lk_tile = _running_sums(lk, tinc, bk - 1)
                w = jnp.exp(log_beta + (ltot1[r0:] - c_l[r0:]) - upto)
                if masked:
                    w = jnp.where(causal, w, 0.0)
                g = w * lax.dot_general(do1s, v2, _NT, preferred_element_type=F32)
                before, gb, before_last = _running_sums(g, texc, bk - 1)
                dz = g - jnp.exp(log_beta) * (g + c_p[r0:] + before)
                if masked:
                    dz = jnp.where(causal, dz, 0.0)
                dzb = dz.astype(BF16)
                dk1 = lax.dot_general(dzb, q1s, _TN, preferred_element_type=F32)
                dv1 = lax.dot_general(w.astype(BF16), do1s, _TN, preferred_element_type=F32)
                dk2 = dk1 if dk2 is None else dk2 + dk1
                dv2 = dv1 if dv2 is None else dv2 + dv1
                new = (c_l[r0:] + lk_tile, c_p[r0:] + before_last + gb[:, bk - 1:bk].astype(F32),
                       dq[r0:] + jnp.dot(dzb, k2, preferred_element_type=F32))
                if r0:
                    new = tuple(jnp.concatenate([old[:r0], part]) for old, part in zip((c_l, c_p, dq), new))
                out.append(new)
            dk_ref[rows, :] += dk2
            dv_ref[rows, :] += dv2
            return tuple(out)

        zero = jnp.zeros((bq, 1), F32)
        carry = ((zero, zero, jnp.zeros((bq, SB_PAIR), F32)),) * 2
        carry = lax.fori_loop(0, i * ratio, lambda j, c: tile(j, c, False), carry)
        for dgl in range(ratio):
            carry = tile(i * ratio + dgl, carry, True, dgl * bk)
        (_, _, dq_a), (_, _, dq_b) = carry
        lane = lax.broadcasted_iota(jnp.int32, dq_a.shape, 1)
        dq_ref[...] = jnp.where(lane < SB_HEAD_DIM, dq_a, dq_b) * scale

    col = lambda s: pl.BlockSpec((l, SB_PAIR), lambda p, i: (0, s * npair + p))
    blk_spec = pl.BlockSpec((bq, SB_PAIR), lambda p, i: (i, p))
    tri_spec = pl.BlockSpec((bk, bk), lambda p, i: (0, 0))
    grid = (npair, l // bq)
    body, x_in, x_out, x_shapes, x_scratch, x_args = carry(job, body, grid, 7, 3)
    res = pl.pallas_call(
        body,
        name="sb_attn_bwd",
        grid=grid,
        in_specs=[blk_spec, col(1), col(2), blk_spec, pl.BlockSpec((None, bq, 2), lambda p, i: (p, i, 0)), tri_spec, tri_spec] + x_in,
        out_specs=[blk_spec, pl.BlockSpec((l, SB_PAIR), lambda p, i: (0, p)), pl.BlockSpec((l, SB_PAIR), lambda p, i: (0, p))] + x_out,
        out_shape=[jax.ShapeDtypeStruct((l, d), F32)] * 3 + x_shapes,
        scratch_shapes=x_scratch,
        compiler_params=_params(("arbitrary", "arbitrary")),
    )(qkv, qkv, qkv, do, ltot, t_inc, t_exc, *x_args)
    return res[:3], res[3:]


def sb_fwd(hn, p, job=None):
    qkv = mm(hn, p["w_qkv"], name="sb_qkv", out_dtypes=(BF16,), tm=2048, tn=512)
    o, ltot, carried = sb_pair_fwd(qkv, job)
    return o, (hn, qkv, ltot, o), carried


def sb_bwd(dmixed, saved, p, job=None):
    hn, qkv, ltot, o = saved
    do = mm(dmixed, p["w_out"], tb=True, name="sb_do", out_dtypes=(BF16,))
    dw_out = mm(o, dmixed, ta=True, name="sb_dwout")
    dq_dk_dv, carried = sb_pair_bwd(qkv, do, ltot, job)
    dqkv = jnp.concatenate([g.astype(BF16) for g in dq_dk_dv], axis=1)
    dw_qkv = mm(hn, dqkv, ta=True, name="sb_dwqkv")
    return (dqkv, p["w_qkv"]), dict(w_qkv=dw_qkv, w_out=dw_out), carried


S5_CHUNK = 256
S5_CHUNK_BWD = 128
S5_SLAB_GROUPS = 8
S5_LANES = 128


def _s5_discretise(lr, li, ldt, bre, bim):
    dt = jnp.exp(ldt)
    mag = jnp.exp(lr * dt)
    lbr = mag * jnp.cos(li * dt)
    lbi = mag * jnp.sin(li * dt)
    inv = 1.0 / (lr * lr + li * li)
    cr = ((lbr - 1.0) * lr + lbi * li) * inv
    ci = (lbi * lr - (lbr - 1.0) * li) * inv
    return lbr, lbi, cr * bre - ci * bim, cr * bim + ci * bre


def _s5_cols(lam_re, lam_im, log_dt, b_re, b_im):
    g, p = lam_re.shape
    col = lambda x: x.reshape(g * p, 1)
    ldt = jnp.broadcast_to(log_dt[:, None], (g, p))
    return col(lam_re), col(lam_im), col(ldt), b_re.reshape(g * p, -1), b_im.reshape(g * p, -1)


def _slab_b(bbar):
    sg = S5_SLAB_GROUPS
    gp, h = bbar.shape
    p = S5_STATE
    x = bbar.reshape(gp // (sg * p), sg, p, h)
    return jnp.einsum("kaph,ab->kahbp", x, jnp.eye(sg, dtype=x.dtype)).reshape(-1, sg * h, sg * p)


def _unslab_b(dslab):
    sg, p = S5_SLAB_GROUPS, S5_STATE
    nk, sh, _ = dslab.shape
    h = sh // sg
    x = dslab.reshape(nk, sg, h, sg, p)
    return jnp.einsum("kahbp,ab->kaph", x, jnp.eye(sg, dtype=x.dtype)).reshape(nk * sg * p, h)


def _slab_c(c):
    sg = S5_SLAB_GROUPS
    g, h, p = c.shape
    x = c.reshape(g // sg, sg, h, p)
    return jnp.einsum("kahp,ab->kapbh", x, jnp.eye(sg, dtype=x.dtype)).reshape(-1, sg * p, sg * h)


def _unslab_c(dslab):
    sg, p = S5_SLAB_GROUPS, S5_STATE
    nk, _, sh = dslab.shape
    h = sh // sg
    x = dslab.reshape(nk, sg, p, sg, h)
    return jnp.einsum("kapbh,ab->kahp", x, jnp.eye(sg, dtype=x.dtype)).reshape(nk * sg, h, p)


def _cmul_scan(lre, lim, xre, xim, re_scr, im_scr, cre, cim, reverse):
    nb, lanes = re_scr.shape[0], re_scr.shape[2]
    ends = []
    for c in range(nb):
        sl = slice(c * lanes, (c + 1) * lanes)
        re_scr[c] = xre[:, sl]
        im_scr[c] = xim[:, sl]
        ends.append(_cmul_scan_block(lre[:, sl], lim[:, sl], re_scr.at[c], im_scr.at[c], cre[:, sl], cim[:, sl], reverse))
    return (jnp.concatenate([re_scr[c] for c in range(nb)], axis=1), jnp.concatenate([im_scr[c] for c in range(nb)], axis=1),
            jnp.concatenate([e[0] for e in ends], axis=1), jnp.concatenate([e[1] for e in ends], axis=1))


def _cmul_scan_block(lre, lim, re_ref, im_ref, cre, cim, reverse):
    t = re_ref.shape[0]
    g = t // 8
    shift = _shift_up if reverse else _shift_down
    cmul = lambda ar, ai, br, bi: (ar * br - ai * bi, ar * bi + ai * br)
    pows = [(lre, lim)]
    for _ in range(7):
        pows.append(cmul(*pows[-1], lre, lim))
    local, prev = [None] * 8, None
    for r in (reversed(range(8)) if reverse else range(8)):
        cr, ci = re_ref[pl.ds(r, g, stride=8), :], im_ref[pl.ds(r, g, stride=8), :]
        if prev is not None:
            pr, pi = cmul(lre, lim, *prev)
            cr, ci = cr + pr, ci + pi
        local[r] = prev = (cr, ci)
    yr, yi = local[0 if reverse else 7]
    edge = lax.broadcasted_iota(jnp.int32, yr.shape, 0) == (g - 1 if reverse else 0)
    mr, mi = pows[7]
    kr, ki = cmul(mr, mi, cre, cim)
    yr, yi = yr + jnp.where(edge, kr, 0.0), yi + jnp.where(edge, ki, 0.0)
    d = 1
    while d < g:
        pr, pi = cmul(mr, mi, shift(yr, d, 0.0), shift(yi, d, 0.0))
        yr, yi = yr + pr, yi + pi
        mr, mi = cmul(mr, mi, mr, mi)
        d *= 2
    er, ei = jnp.where(edge, cre, shift(yr, 1, 0.0)), jnp.where(edge, cim, shift(yi, 1, 0.0))
    for r in range(8):
        pr, pi = cmul(*pows[7 - r if reverse else r], er, ei)
        re_ref[pl.ds(r, g, stride=8), :] = local[r][0] + pr
        im_ref[pl.ds(r, g, stride=8), :] = local[r][1] + pi
    return yr, yi


def s5_scan_fwd(u, lbr, lbi, bbd_re, bbd_im, cbd_re, cbd_imn, d_skip, job=None):
    l, w = u.shape
    n = lbr.shape[1]
    nk, cw, sw = bbd_re.shape
    t = _tile(l, (S5_CHUNK, 64, 32, 16, 8))

    def body(u_ref, lbr_ref, lbi_ref, bre_ref, bim_ref, cre_ref, cim_ref, d_ref, sre_ref, sim_ref, y_ref, z_ref, pre_ref, pim_ref,
             xr_ref, xi_ref):
        @pl.when(pl.program_id(0) == 0)
        def _():
            pre_ref[...] = jnp.zeros_like(pre_ref)
            pim_ref[...] = jnp.zeros_like(pim_ref)

        uu = u_ref[...]
        ub = uu.astype(BF16)
        lre, lim = lbr_ref[...], lbi_ref[...]
        xre = jnp.concatenate([jnp.dot(ub[:, k * cw:(k + 1) * cw], bre_ref[k], preferred_element_type=F32) for k in range(nk)], axis=1)
        xim = jnp.concatenate([jnp.dot(ub[:, k * cw:(k + 1) * cw], bim_ref[k], preferred_element_type=F32) for k in range(nk)], axis=1)
        last = t // 8 - 1
        sre, sim, pre_ref[...], pim_ref[...] = _cmul_scan(lre, lim, xre, xim, xr_ref, xi_ref, pre_ref[pl.ds(last, 1), :],
                                                          pim_ref[pl.ds(last, 1), :], False)
        sre_ref[...] = sre
        sim_ref[...] = sim
        sreb, simb = sre.astype(BF16), sim.astype(BF16)
        y = jnp.concatenate(
            [jnp.dot(sreb[:, k * sw:(k + 1) * sw], cre_ref[k], preferred_element_type=F32)
             + jnp.dot(simb[:, k * sw:(k + 1) * sw], cim_ref[k], preferred_element_type=F32) for k in range(nk)], axis=1)
        y = y + d_ref[...] * uu
        y_ref[...] = y
        z_ref[...] = _gelu(y).astype(z_ref.dtype)

    def whole(x):
        nd = x.ndim
        return pl.BlockSpec(x.shape, lambda i: (0,) * nd)

    consts = [lbr, lbi, bbd_re, bbd_im, cbd_re, cbd_imn, d_skip]
    row = lambda c: pl.BlockSpec((t, c), lambda i: (i, 0))
    body, x_in, x_out, x_shapes, x_scratch, x_args = carry(job, body, (l // t,), 1 + len(consts), 4)
    res = pl.pallas_call(
        body,
        name="s5_scan_fwd",
        grid=(l // t,),
        in_specs=[row(w)] + [whole(c) for c in consts] + x_in,
        out_specs=[row(n), row(n), row(w), row(w)] + x_out,
        out_shape=[jax.ShapeDtypeStruct((l, n), F32), jax.ShapeDtypeStruct((l, n), F32), jax.ShapeDtypeStruct((l, w), F32),
                   jax.ShapeDtypeStruct((l, w), BF16)] + x_shapes,
        scratch_shapes=[pltpu.VMEM((t // 8, n), F32), pltpu.VMEM((t // 8, n), F32), pltpu.VMEM((n // S5_LANES, t, S5_LANES), F32),
                        pltpu.VMEM((n // S5_LANES, t, S5_LANES), F32)] + x_scratch,
        compiler_params=_params(("arbitrary",)),
    )(u, *consts, *x_args)
    return res[:4], res[4:]


def s5_scan_bwd(dz, y, u, sre, sim, lbr, lbi, bbd_re, bbd_im, cbd_re, cbd_imn, d_skip, job=None):
    l, w = u.shape
    n = lbr.shape[1]
    nk, cw, sw = bbd_re.shape
    t = _tile(l, (S5_CHUNK_BWD, 64, 32, 16, 8))
    nc = l // t

    def body(dz_ref, y_ref, u_ref, sre_ref, sim_ref, hre_ref, him_ref, lbr_ref, lbi_ref, bre_ref, bim_ref, cre_ref, cim_ref,
             d_ref, du_ref, dlr_ref, dli_ref, dbre_ref, dbim_ref, dcre_ref, dcim_ref, dd_ref, nre_ref, nim_ref, dsr_ref, dsi_ref):
        i = pl.program_id(0)
        has_prev = (i < nc - 1).astype(F32)

        @pl.when(i == 0)
        def _():
            nre_ref[...] = jnp.zeros_like(nre_ref)
            nim_ref[...] = jnp.zeros_like(nim_ref)

        uu = u_ref[...]
        ub = uu.astype(BF16)
        lre, lim = lbr_ref[...], lbi_ref[...]
        _, gelu_vjp = jax.vjp(_gelu, y_ref[...])
        dy = gelu_vjp(dz_ref[...].astype(F32))[0]
        dyb = dy.astype(BF16)
        gre = jnp.concatenate([lax.dot_general(dyb[:, k * cw:(k + 1) * cw], cre_ref[k], _NT, preferred_element_type=F32)
                               for k in range(nk)], axis=1)
        gim = jnp.concatenate([lax.dot_general(dyb[:, k * cw:(k + 1) * cw], cim_ref[k], _NT, preferred_element_type=F32)
                               for k in range(nk)], axis=1)
        dsre, dsim, nre_ref[...], nim_ref[...] = _cmul_scan(lre, -lim, gre, gim, dsr_ref, dsi_ref, nre_ref[pl.ds(0, 1), :],
                                                            nim_ref[pl.ds(0, 1), :], True)
        dsreb, dsimb = dsre.astype(BF16), dsim.astype(BF16)
        s_re, s_im = sre_ref[...], sim_ref[...]
        du = jnp.concatenate(
            [lax.dot_general(dsreb[:, k * sw:(k + 1) * sw], bre_ref[k], _NT, preferred_element_type=F32)
             + lax.dot_general(dsimb[:, k * sw:(k + 1) * sw], bim_ref[k], _NT, preferred_element_type=F32) for k in range(nk)],
            axis=1)
        du_ref[...] = (du + d_ref[...] * dy).astype(du_ref.dtype)
        pre = _rows_before(s_re, hre_ref[...] * has_prev, 1)
        pim = _rows_before(s_im, him_ref[...] * has_prev, 1)
        sreb, simb = s_re.astype(BF16), s_im.astype(BF16)
        sums = [
            (dlr_ref, jnp.sum(dsre * pre + dsim * pim, axis=0, keepdims=True)),
            (dli_ref, jnp.sum(dsim * pre - dsre * pim, axis=0, keepdims=True)),
            (dbre_ref, jnp.stack([lax.dot_general(ub[:, k * cw:(k + 1) * cw], dsreb[:, k * sw:(k + 1) * sw], _TN,
                                                  preferred_element_type=F32) for k in range(nk)])),
            (dbim_ref, jnp.stack([lax.dot_general(ub[:, k * cw:(k + 1) * cw], dsimb[:, k * sw:(k + 1) * sw], _TN,
                                                  preferred_element_type=F32) for k in range(nk)])),
            (dcre_ref, jnp.stack([lax.dot_general(sreb[:, k * sw:(k + 1) * sw], dyb[:, k * cw:(k + 1) * cw], _TN,
                                                  preferred_element_type=F32) for k in range(nk)])),
            (dcim_ref, jnp.stack([lax.dot_general(simb[:, k * sw:(k + 1) * sw], dyb[:, k * cw:(k + 1) * cw], _TN,
                                                  preferred_element_type=F32) for k in range(nk)])),
            (dd_ref, jnp.sum(dy * uu, axis=0, keepdims=True)),
        ]

        @pl.when(i == 0)
        def _():
            for ref, val in sums:
                ref[...] = val

        @pl.when(i > 0)
        def _():
            for ref, val in sums:
                ref[...] += val

    def whole(shape):
        nd = len(shape)
        return pl.BlockSpec(tuple(shape), lambda i: (0,) * nd)

    consts = [lbr, lbi, bbd_re, bbd_im, cbd_re, cbd_imn, d_skip]
    t8 = t // 8
    rev = lambda i: nc - 1 - i
    halo = lambda i: jnp.maximum(rev(i) * t8 - 1, 0)
    row = lambda c: pl.BlockSpec((t, c), lambda i: (rev(i), 0))
    sum_shapes = [lbr.shape, lbi.shape, bbd_re.shape, bbd_im.shape, cbd_re.shape, cbd_imn.shape, d_skip.shape]
    body, x_in, x_out, x_shapes, x_scratch, x_args = carry(job, body, (nc,), 7 + len(consts), 8)
    res = pl.pallas_call(
        body,
        name="s5_scan_bwd",
        grid=(nc,),
        in_specs=[row(w), row(w), row(w), row(n), row(n), pl.BlockSpec((8, n), lambda i: (halo(i), 0)),
                  pl.BlockSpec((8, n), lambda i: (halo(i), 0))] + [whole(c.shape) for c in consts] + x_in,
        out_specs=[row(w)] + [whole(s) for s in sum_shapes] + x_out,
        out_shape=[jax.ShapeDtypeStruct((l, w), BF16)] + [jax.ShapeDtypeStruct(tuple(s), F32) for s in sum_shapes] + x_shapes,
        scratch_shapes=[pltpu.VMEM((t8, n), F32), pltpu.VMEM((t8, n), F32), pltpu.VMEM((n // S5_LANES, t, S5_LANES), F32),
                        pltpu.VMEM((n // S5_LANES, t, S5_LANES), F32)] + x_scratch,
        compiler_params=_params(("arbitrary",)),
    )(dz, y, u, sre, sim, sre, sim, *consts, *x_args)
    return res[0], res[1:8], res[8:]


def _glu(vg):
    w = vg.shape[1] // 2
    return vg[:, :w] * _sigmoid(vg[:, w:])


def s5_fwd(hn, h, p, job=None):
    cols = _s5_cols(p["lam_re"], p["lam_im"], p["log_dt"], p["b_re"], p["b_im"])
    gp, hh = cols[3].shape
    lbr, lbi, bbr, bbi = rowwise(_s5_discretise, list(cols), [], [(1, F32), (1, F32), (hh, F32), (hh, F32)],
                                 name="s5_discretise", tm=512)
    consts = (lbr.reshape(1, gp), lbi.reshape(1, gp), _slab_b(bbr).astype(BF16), _slab_b(bbi).astype(BF16),
              _slab_c(p["c_re"]).astype(BF16), _slab_c(-p["c_im"]).astype(BF16), p["d"])
    u = mm(hn, p["w_in"], name="s5_in")
    (sre, sim, y, z), carried = s5_scan_fwd(u, *consts, job=job)
    vg = mm(z, p["w_out"], name="s5_out", out_dtypes=(BF16,))
    h_new = rowwise(lambda a, r: r + _glu(a.astype(F32)), [vg, h], [], [(h.shape[1], F32)], name="s5_glu")
    return h_new, (hn, u, sre, sim, y, z, vg, cols, consts), carried


def s5_bwd(dh_new, saved, p, job=None):
    hn, u, sre, sim, y, z, vg, cols, consts = saved

    def glu_bwd(a, dm):
        _, vjp = jax.vjp(_glu, a.astype(F32))
        return vjp(dm)[0]

    dvg = rowwise(glu_bwd, [vg, dh_new], [], [(vg.shape[1], BF16)], name="s5_dglu")
    dw_out = mm(z, dvg, ta=True, name="s5_dwout")
    dz = mm(dvg, p["w_out"], tb=True, name="s5_dz", out_dtypes=(BF16,), tk=2048)
    du, (dlbr, dlbi, dbbr, dbbi, dcre, dcimn, dd), carried = s5_scan_bwd(dz, y, u, sre, sim, *consts, job=job)
    dw_in = mm(hn, du, ta=True, name="s5_dwin")
    gp = cols[0].shape[0]

    def disc_bwd(lr, li, ldt, bre, bim, g0, g1, g2, g3):
        _, vjp = jax.vjp(_s5_discretise, lr, li, ldt, bre, bim)
        return vjp((g0, g1, g2, g3))

    cot = (dlbr.reshape(gp, 1), dlbi.reshape(gp, 1), _unslab_b(dbbr), _unslab_b(dbbi))
    dlr, dli, dldt, dbre, dbim = rowwise(disc_bwd, list(cols + cot), [], [(c.shape[1], F32) for c in cols],
                                         name="s5_discretise_bwd", tm=512)
    g_, p_ = p["lam_re"].shape
    grads = dict(w_in=dw_in, lam_re=dlr.reshape(g_, p_), lam_im=dli.reshape(g_, p_), log_dt=dldt.reshape(g_, p_).sum(axis=1),
                 b_re=dbre.reshape(p["b_re"].shape), b_im=dbim.reshape(p["b_im"].shape), c_re=_unslab_c(dcre),
                 c_im=-_unslab_c(dcimn), d=dd, w_out=dw_out)
    return (du, p["w_in"]), grads, carried


MESH = pl.DeviceIdType.MESH
N_CHIPS = 4
N_DEVICES = 8


def _place():
    x, y, c = lax.axis_index("x"), lax.axis_index("y"), lax.axis_index("c")
    return x, y, c, [(1 - x, y), (x, 1 - y), (1 - x, 1 - y)]


def _hbm_call(body, name, ins, out_shapes, n_remote, n_local=0):
    hbm = pl.BlockSpec(memory_space=pltpu.HBM)
    scratch = [pltpu.SemaphoreType.DMA((n_remote,)), pltpu.SemaphoreType.DMA((n_remote,))]
    if n_local:
        scratch.append(pltpu.SemaphoreType.DMA((n_local,)))
    return pl.pallas_call(
        body,
        name=name,
        in_specs=[hbm] * len(ins),
        out_specs=[hbm] * len(out_shapes),
        out_shape=out_shapes,
        scratch_shapes=scratch,
    )(*ins)


def _split_dim(shape):
    return next(d for d, s in enumerate(shape) if s >= 2 and s % 2 == 0)


class Exchange:
    def __init__(self, ins, out_shapes, n_remote, n_local, start, finish):
        self.ins, self.out_shapes, self.start, self.finish = list(ins), list(out_shapes), start, finish
        self.scratch = [pltpu.SemaphoreType.DMA((n_remote,)), pltpu.SemaphoreType.DMA((n_remote,)),
                        pltpu.SemaphoreType.DMA((max(n_local, 1),))]


def run_exchange(job, name):
    n_in, n_out = len(job.ins), len(job.out_shapes)

    def body(*refs):
        ins, outs, sems = refs[:n_in], refs[n_in:n_in + n_out], refs[n_in + n_out:]
        job.start(ins, outs, *sems)
        job.finish(ins, outs, *sems)

    hbm = pl.BlockSpec(memory_space=pltpu.HBM)
    return pl.pallas_call(body, name=name, in_specs=[hbm] * n_in, out_specs=[hbm] * n_out, out_shape=job.out_shapes,
                          scratch_shapes=job.scratch)(*job.ins)


def carry(job, body, grid, n_in, n_out):
    if job is None:
        return body, [], [], [], [], []
    nji, njo = len(job.ins), len(job.out_shapes)

    def carrying(*refs):
        ins, jins = refs[:n_in], refs[n_in:n_in + nji]
        outs, jouts = refs[n_in + nji:n_in + nji + n_out], refs[n_in + nji + n_out:n_in + nji + n_out + njo]
        rest = refs[n_in + nji + n_out + njo:]
        own, sems = rest[:len(rest) - 3], rest[len(rest) - 3:]
        ids = [pl.program_id(ax) for ax in range(len(grid))]
        first = functools.reduce(jnp.logical_and, [i == 0 for i in ids])
        last = functools.reduce(jnp.logical_and, [i == g - 1 for i, g in zip(ids, grid)])

        @pl.when(first)
        def _():
            job.start(jins, jouts, *sems)

        body(*ins, *outs, *own)

        @pl.when(last)
        def _():
            job.finish(jins, jouts, *sems)

    hbm = pl.BlockSpec(memory_space=pltpu.HBM)
    return carrying, [hbm] * nji, [hbm] * njo, job.out_shapes, job.scratch, job.ins


def gather_chips(shards):
    n = len(shards)
    cuts = [_split_dim(s.shape) for s in shards]

    def parts(ins, outs, send, recv):
        x, y, c, chips = _place()

        def half(ref, t, which, lead=()):
            size = shards[t].shape[cuts[t]] // 2
            return ref.at[lead + (slice(None),) * cuts[t] + (pl.ds(which * size, size),)]

        def copy(t, k, block, which, to, src=None):
            dst = half(outs[t], t, which, (block,))
            return pltpu.make_async_remote_copy(dst if src is None else src, dst, send.at[6 * t + k], recv.at[6 * t + k],
                                                device_id=to, device_id_type=MESH)

        me = 2 * x + y
        sends = [copy(t, k, me, c, (px, py, c), src=half(ins[t], t, c)) for t in range(n) for k, (px, py) in enumerate(chips)]
        return x, y, c, chips, me, copy, sends

    def start(ins, outs, send, recv, local):
        _, _, _, _, me, _, sends = parts(ins, outs, send, recv)
        for t in range(n):
            pltpu.make_async_copy(ins[t], outs[t].at[me], local.at[t]).start()
        for cp in sends:
            cp.start()

    def finish(ins, outs, send, recv, local):
        x, y, c, chips, me, copy, sends = parts(ins, outs, send, recv)
        passed = []
        for t in range(n):
            for k, (px, py) in enumerate(chips):
                copy(t, k, 2 * px + py, c, (px, py, c)).wait_recv()
                on = copy(t, 3 + k, 2 * px + py, c, (x, y, 1 - c))
                on.start()
                passed.append(on)
        for t in range(n):
            for k, (px, py) in enumerate(chips):
                copy(t, 3 + k, 2 * px + py, 1 - c, (x, y, 1 - c)).wait_recv()
        for cp in sends + passed:
            cp.wait_send()
        for t in range(n):
            pltpu.make_async_copy(ins[t], outs[t].at[me], local.at[t]).wait()

    return Exchange(shards, [jax.ShapeDtypeStruct((N_CHIPS,) + s.shape, s.dtype) for s in shards], 6 * n, n, start, finish)


def scatter_chips(blocked):
    n = len(blocked)

    def copies(ins, outs, send, recv):
        x, y, c, chips = _place()
        return [pltpu.make_async_remote_copy(ins[t].at[2 * px + py], outs[t].at[k], send.at[3 * t + k], recv.at[3 * t + k],
                                             device_id=(px, py, c), device_id_type=MESH)
                for t in range(n) for k, (px, py) in enumerate(chips)]

    def start(ins, outs, send, recv, local):
        for cp in copies(ins, outs, send, recv):
            cp.start()

    def finish(ins, outs, send, recv, local):
        for cp in copies(ins, outs, send, recv):
            cp.wait()

    return Exchange(blocked, [jax.ShapeDtypeStruct((3,) + b.shape[1:], b.dtype) for b in blocked], 3 * n, 0, start, finish)


def swap_cores(arrays):
    n = len(arrays)

    def body(*refs):
        ins, outs = refs[:n], refs[n:2 * n]
        send, recv = refs[2 * n:]
        x, y, c, _ = _place()
        pending = []
        for t in range(n):
            cp = pltpu.make_async_remote_copy(ins[t], outs[t], send.at[t], recv.at[t], device_id=(x, y, 1 - c),
                                              device_id_type=MESH)
            cp.start()
            pending.append(cp)
        for cp in pending:
            cp.wait()

    return _hbm_call(body, "swap_cores", arrays, [jax.ShapeDtypeStruct(a.shape, a.dtype) for a in arrays], n)


def gather_devices(buf):
    def body(in_ref, out_ref, send, recv, local):
        x, y, c, _ = _place()
        own = pltpu.make_async_copy(in_ref, out_ref.at[4 * x + 2 * y + c], local.at[0])
        own.start()
        pending = [own]
        for k in range(1, N_DEVICES):
            px = x ^ ((k >> 2) & 1)
            py = y ^ ((k >> 1) & 1)
            pc = c ^ (k & 1)
            going = pltpu.make_async_remote_copy(in_ref, out_ref.at[4 * x + 2 * y + c], send.at[k - 1], recv.at[k - 1],
                                                 device_id=(px, py, pc), device_id_type=MESH)
            going.start()
            pending.append(pltpu.make_async_remote_copy(in_ref, out_ref.at[4 * px + 2 * py + pc], send.at[k - 1], recv.at[k - 1],
                                                        device_id=(px, py, pc), device_id_type=MESH))
        for cp in pending:
            cp.wait()

    return _hbm_call(body, "gather_devices", [buf], [jax.ShapeDtypeStruct((N_DEVICES,) + buf.shape, buf.dtype)],
                     N_DEVICES - 1, 1)[0]


def _adamw(w, g, m, v):
    m = ADAM_B1 * m + (1.0 - ADAM_B1) * g
    v = ADAM_B2 * v + (1.0 - ADAM_B2) * (g * g)
    m_hat = m / (1.0 - ADAM_B1 ** ADAM_STEP)
    v_hat = v / (1.0 - ADAM_B2 ** ADAM_STEP)
    return -ADAM_LR * (m_hat / (jnp.sqrt(v_hat) + ADAM_EPS) + ADAM_WD * w), m, v


def _rows2d(a):
    return a.reshape(-1, a.shape[-1])


WEIGHTS = ["ffn1_norm", "ffn1_w_in", "ffn1_w_out", "mix_norm", "ffn2_norm", "ffn2_w_in", "ffn2_w_out", "final_norm",
           "s5_w_in", "s5_lam_re", "s5_lam_im", "s5_log_dt", "s5_b_re", "s5_b_im", "s5_c_re", "s5_c_im", "s5_d", "s5_w_out",
           "sb_w_qkv", "sb_w_out", "lru_w_in", "lru_conv_w", "lru_conv_b", "lru_w_a", "lru_b_a", "lru_w_x", "lru_b_x",
           "lru_lambda", "lru_w_out"]
INPUTS = ["x"] + WEIGHTS + ["loss_target"] + ["m_" + n for n in WEIGHTS] + ["v_" + n for n in WEIGHTS]
SHARDED_BIG = dict(ffn1_w_in=2, ffn1_w_out=1, ffn2_w_in=2, ffn2_w_out=1, s5_w_in=1, s5_w_out=2, sb_w_qkv=2, sb_w_out=1,
                   lru_w_in=2, lru_w_a=2, lru_w_x=2, lru_w_out=1)
SHARDED_SMALL = dict(s5_d=1, lru_conv_w=2, lru_conv_b=1, lru_b_a=2, lru_b_x=2, lru_lambda=1)
REPLICATED = [n for n in WEIGHTS if n not in SHARDED_BIG and n not in SHARDED_SMALL]
PACK_LANES = 128
PACK_ROW_ALIGN = 16
REPLICATED_ROW_TILE = 256


def _unblock(g, d):
    full = jnp.moveaxis(g, 0, d)
    return full.reshape(full.shape[:d] + (full.shape[d] * full.shape[d + 1],) + full.shape[d + 2:])


def _block(full, d):
    s = full.shape[d] // N_CHIPS
    return jnp.moveaxis(full.reshape(full.shape[:d] + (N_CHIPS, s) + full.shape[d + 1:]), d, 0)


def _pack(arrays, lead=(), row_align=PACK_ROW_ALIGN):
    nl = len(lead)
    flat = jnp.concatenate([a.reshape(lead + (-1,)) for a in arrays], axis=nl)
    quantum = PACK_LANES * row_align
    pad = (-flat.shape[nl]) % quantum
    flat = jnp.pad(flat, [(0, 0)] * nl + [(0, pad)])
    return flat.reshape(lead + (-1, PACK_LANES))


def _unpack(packed, shapes, lead=()):
    nl = len(lead)
    flat = packed.reshape(lead + (-1,))
    out, off = [], 0
    for s in shapes:
        size = math.prod(s)
        out.append(lax.slice_in_dim(flat, off, off + size, axis=nl).reshape(lead + tuple(s)))
        off += size
    return out


N_GROUPS = 3
MIXER_KIND = dict(s5=0, sb=1, lru=2)
S5_NAMES = dict(w_in="s5_w_in", lam_re="s5_lam_re", lam_im="s5_lam_im", log_dt="s5_log_dt", b_re="s5_b_re", b_im="s5_b_im",
                c_re="s5_c_re", c_im="s5_c_im", d="s5_d", w_out="s5_w_out")
SB_NAMES = dict(w_qkv="sb_w_qkv", w_out="sb_w_out")
LRU_NAMES = dict(w_in="lru_w_in", conv_w="lru_conv_w", conv_b="lru_conv_b", wa="lru_w_a", ba="lru_b_a", wx="lru_w_x",
                 bx="lru_b_x", lam="lru_lambda", w_out="lru_w_out")


def _pieces_of(name, count):
    kind = None if name.startswith("ffn") else MIXER_KIND[name.split("_")[0]]
    groups = [min(i if kind is None else kind + N_MIXERS * i, N_GROUPS - 1) for i in range(count)]
    runs, lo = [], 0
    for i in range(1, count + 1):
        if i == count or groups[i] != groups[lo]:
            runs.append((lo, i, groups[lo]))
            lo = i
    return runs


class _Sharded:
    def __init__(self, a):
        self.a = a
        self.pieces = {n: _pieces_of(n, a[n].shape[0]) for n in SHARDED_BIG}
        by_group = [[(n, lo, hi) for n in SHARDED_BIG for lo, hi, g in self.pieces[n] if g == grp] for grp in range(N_GROUPS)]
        first = lambda n: n.startswith("ffn1")
        self.by_group = {0: [k for k in by_group[0] if first(k[0])], 1: by_group[1], 2: by_group[2],
                         "0b": [k for k in by_group[0] if not first(k[0])],
                         "0b1": [k for k in by_group[0] if k[0] == "ffn2_w_in"],
                         "0b2": [k for k in by_group[0] if not first(k[0]) and k[0] != "ffn2_w_in"]}
        self.weights = {}
        self.small = {}
        self.grads = {n: [None] * a[n].shape[0] for n in SHARDED_BIG}
        self.small_grads = {}
        self.received = {}
        self.small_received = None

    def gather_job(self, grp):
        arrays = [self.a[n][lo:hi].astype(BF16) for n, lo, hi in self.by_group[grp]]
        if grp == 0:
            arrays.append(_pack([self.a[n] for n in SHARDED_SMALL]))
        return gather_chips(arrays)

    def landed(self, grp, gathered):
        for (n, lo, _), g in zip(self.by_group[grp], gathered):
            self.weights[(n, lo)] = _unblock(g, SHARDED_BIG[n])
        if grp == 0:
            blocks = _unpack(gathered[-1], [self.a[n].shape for n in SHARDED_SMALL], lead=(N_CHIPS,))
            self.small = {n: _unblock(b, d) for (n, d), b in zip(SHARDED_SMALL.items(), blocks)}

    def weight(self, name, idx):
        lo = next(lo for lo, hi, _ in self.pieces[name] if lo <= idx < hi)
        return self.weights[(name, lo)][idx - lo]

    def piece_grad(self, n, lo, hi):
        return jnp.stack(self.grads[n][lo:hi])

    def scatter_job(self, grp):
        arrays = [_block(self.piece_grad(n, lo, hi), SHARDED_BIG[n]).astype(BF16) for n, lo, hi in self.by_group[grp]]
        if grp == 0:
            arrays.append(_pack([_block(self.small_grads[n], d) for n, d in SHARDED_SMALL.items()], lead=(N_CHIPS,)))
        return scatter_chips(arrays)

    def arrived(self, grp, received):
        for (n, lo, _), r in zip(self.by_group[grp], received):
            self.received[(n, lo)] = r
        if grp == 0:
            self.small_received = received[-1]


def _forward_backward(x, target, a, sh):
    depth = a["ffn1_norm"].shape[0]

    def mixer_params(layer):
        kind, j = layer % N_MIXERS, layer // N_MIXERS
        if kind == 0:
            return kind, j, dict(w_in=sh.weight("s5_w_in", j), lam_re=a["s5_lam_re"][j], lam_im=a["s5_lam_im"][j],
                                 log_dt=a["s5_log_dt"][j], b_re=a["s5_b_re"][j], b_im=a["s5_b_im"][j], c_re=a["s5_c_re"][j],
                                 c_im=a["s5_c_im"][j], d=sh.small["s5_d"][j].reshape(1, -1), w_out=sh.weight("s5_w_out", j))
        if kind == 1:
            return kind, j, dict(w_qkv=sh.weight("sb_w_qkv", j), w_out=sh.weight("sb_w_out", j))
        sm = sh.small
        return kind, j, dict(w_in=sh.weight("lru_w_in", j), conv_w=sm["lru_conv_w"][j], conv_b=sm["lru_conv_b"][j].reshape(1, -1),
                             wa=sh.weight("lru_w_a", j), ba=sm["lru_b_a"][j].reshape(1, -1), wx=sh.weight("lru_w_x", j),
                             bx=sm["lru_b_x"][j].reshape(1, -1), lam=sm["lru_lambda"][j].reshape(1, -1),
                             w_out=sh.weight("lru_w_out", j))

    def ffn_weights(which, layer):
        return sh.weight(f"{which}_w_in", layer), sh.weight(f"{which}_w_out", layer)

    sh.landed(0, run_exchange(sh.gather_job(0), "gather_chips"))
    h = x
    tape = []
    for layer in range(depth):
        job = sh.gather_job("0b") if layer == 0 else None
        h, s1, got = ffn_fwd(h, a["ffn1_norm"][layer], *ffn_weights("ffn1", layer), "ffn", job)
        if job is not None:
            sh.landed("0b", got)
        kind, j, p = mixer_params(layer)
        h_mix_in = h
        hn = rms_fwd(h, a["mix_norm"][layer], "mix_norm")
        job = sh.gather_job(layer + 1) if layer + 1 < N_GROUPS else None
        if kind == 0:
            h, sm, got = s5_fwd(hn, h, p, job)
        elif kind == 1:
            o_flat, sm, got = sb_fwd(hn, p, job)
            h = mm(o_flat, p["w_out"], name="mix_out", extras=(h,), epilogue=lambda acc, res: res + acc)
        else:
            assert job is None
            y, sm = lru_fwd(hn, p)
            h = mm(y, p["w_out"], name="mix_out", extras=(h,), epilogue=lambda acc, res: res + acc)
        if job is not None:
            sh.landed(layer + 1, got)
        h, s2, _ = ffn_fwd(h, a["ffn2_norm"][layer], *ffn_weights("ffn2", layer), "ffn")
        tape.append((s1, h_mix_in, sm, s2))

    loss, dh, g_final = loss_fwd_bwd(h, a["final_norm"], target)

    norm_grads = {n: [None] * depth for n in ("ffn1_norm", "mix_norm", "ffn2_norm")}
    mix = {}
    for layer in reversed(range(depth)):
        s1, h_mix_in, sm, s2 = tape[layer]
        dh, dg, dwi, dwo, _, _ = ffn_bwd(dh, s2, a["ffn2_norm"][layer], *ffn_weights("ffn2", layer), "ffn")
        norm_grads["ffn2_norm"][layer], sh.grads["ffn2_w_in"][layer], sh.grads["ffn2_w_out"][layer] = dg, dwi, dwo
        kind, j, p = mixer_params(layer)
        job = sh.scatter_job(layer + 1) if layer + 1 < N_GROUPS else None
        if kind == 0:
            (da, w_first), g, got = s5_bwd(dh, sm, p, job)
            names = S5_NAMES
        elif kind == 1:
            (da, w_first), g, got = sb_bwd(dh, sm, p, job)
            names = SB_NAMES
        else:
            assert job is None
            (da, w_first), g = lru_bwd(dh, sm, p)
            names = LRU_NAMES
        if job is not None:
            sh.arrived(layer + 1, got)
        for k, full_name in names.items():
            if full_name in SHARDED_BIG:
                sh.grads[full_name][j] = g[k].reshape(sh.weight(full_name, j).shape)
            else:
                mix.setdefault(full_name, {})[j] = g[k].reshape(a[full_name].shape[1:-1] + (-1,))
        dh, dg = norm_input_bwd(da, _rows_spec, 1, w_first, h_mix_in, a["mix_norm"][layer], dh, "mix_dhn")
        norm_grads["mix_norm"][layer] = dg
        jobs = (sh.scatter_job("0b1"), sh.scatter_job("0b2")) if layer == 0 else (None, None)
        dh, dg, dwi, dwo, got1, got2 = ffn_bwd(dh, s1, a["ffn1_norm"][layer], *ffn_weights("ffn1", layer), "ffn", jobs)
        if layer == 0:
            sh.arrived("0b1", got1)
            sh.arrived("0b2", got2)
        norm_grads["ffn1_norm"][layer], sh.grads["ffn1_w_in"][layer], sh.grads["ffn1_w_out"][layer] = dg, dwi, dwo

    grads = {n: jnp.stack(v) for n, v in norm_grads.items()}
    grads["final_norm"] = g_final
    for n, by_j in mix.items():
        stacked = jnp.stack([by_j[j] for j in range(len(by_j))])
        if n in SHARDED_SMALL:
            sh.small_grads[n] = stacked
        else:
            grads[n] = stacked
    sh.arrived(0, run_exchange(sh.scatter_job(0), "scatter_chips"))
    return loss, dh, grads


def kernel(x, ffn1_norm, ffn1_w_in, ffn1_w_out, mix_norm, ffn2_norm, ffn2_w_in, ffn2_w_out, final_norm, s5_w_in,
           s5_lam_re, s5_lam_im, s5_log_dt, s5_b_re, s5_b_im, s5_c_re, s5_c_im, s5_d, s5_w_out, sb_w_qkv,
           sb_w_out, lru_w_in, lru_conv_w, lru_conv_b, lru_w_a, lru_b_a, lru_w_x, lru_b_x, lru_lambda,
           lru_w_out, loss_target, m_ffn1_norm, m_ffn1_w_in, m_ffn1_w_out, m_mix_norm, m_ffn2_norm,
           m_ffn2_w_in, m_ffn2_w_out, m_final_norm, m_s5_w_in, m_s5_lam_re, m_s5_lam_im, m_s5_log_dt,
           m_s5_b_re, m_s5_b_im, m_s5_c_re, m_s5_c_im, m_s5_d, m_s5_w_out, m_sb_w_qkv, m_sb_w_out, m_lru_w_in,
           m_lru_conv_w, m_lru_conv_b, m_lru_w_a, m_lru_b_a, m_lru_w_x, m_lru_b_x, m_lru_lambda, m_lru_w_out,
           v_ffn1_norm, v_ffn1_w_in, v_ffn1_w_out, v_mix_norm, v_ffn2_norm, v_ffn2_w_in, v_ffn2_w_out,
           v_final_norm, v_s5_w_in, v_s5_lam_re, v_s5_lam_im, v_s5_log_dt, v_s5_b_re, v_s5_b_im, v_s5_c_re,
           v_s5_c_im, v_s5_d, v_s5_w_out, v_sb_w_qkv, v_sb_w_out, v_lru_w_in, v_lru_conv_w, v_lru_conv_b,
           v_lru_w_a, v_lru_b_a, v_lru_w_x, v_lru_b_x, v_lru_lambda, v_lru_w_out):
    a = dict(locals())
    assert list(a) == INPUTS
    x, y, c, _ = _place()
    chip = 2 * x + y
    everyone = ("x", "y", "c")

    sh = _Sharded(a)
    loss, dx, grads = _forward_backward(a["x"][0], a["loss_target"][0], a, sh)
    loss = lax.psum(loss, everyone)

    small = list(SHARDED_SMALL)

    def sum_chips(mine, got):
        rows = _rows2d(mine)
        return rowwise(lambda o, r: ((o + r[0].astype(F32)) + r[1].astype(F32)) + r[2].astype(F32),
                       [rows, got.reshape((3,) + rows.shape)], [], [(rows.shape[1], F32)], name="sum_chips")

    def own_block(full, name, d):
        return lax.dynamic_slice_in_dim(full, chip * a[name].shape[d], a[name].shape[d], axis=d)

    keys = [(n, lo, hi) for n in SHARDED_BIG for lo, hi, _ in sh.pieces[n]]
    partial = [sum_chips(own_block(sh.piece_grad(n, lo, hi), n, SHARDED_BIG[n]), sh.received[(n, lo)]) for n, lo, hi in keys]
    partial.append(sum_chips(_pack([own_block(sh.small_grads[n], n, d) for n, d in SHARDED_SMALL.items()]), sh.small_received))
    other = swap_cores(partial)

    def adam_sharded(wv, ga, gb, m, v):
        shape = wv.shape
        res = rowwise(lambda w_, a_, b_, mm_, vv_: (a_ + b_,) + _adamw(w_, a_ + b_, mm_, vv_),
                      [_rows2d(wv), ga, gb, _rows2d(m), _rows2d(v)], [], [(shape[-1], F32)] * 4, name="adamw_sharded")
        return [r.reshape(shape) for r in res]

    out_grad, out_delta, out_m, out_v = {}, {}, {}, {}
    for n in SHARDED_BIG:
        at = [i for i, k in enumerate(keys) if k[0] == n]
        out_grad[n], out_delta[n], out_m[n], out_v[n] = adam_sharded(a[n], [partial[i] for i in at], [other[i] for i in at],
                                                                     a["m_" + n], a["v_" + n])
    small_shapes = [a[n].shape for n in small]
    sp = [_pack([a[pre + n] for n in small]) for pre in ("", "m_", "v_")]
    g_, d_, m_, v_ = adam_sharded(sp[0], [partial[-1]], [other[-1]], sp[1], sp[2])
    for n, gg, dd, mm_, vv in zip(small, _unpack(g_, small_shapes), _unpack(d_, small_shapes), _unpack(m_, small_shapes),
                                  _unpack(v_, small_shapes)):
        out_grad[n], out_delta[n], out_m[n], out_v[n] = gg, dd, mm_, vv

    rep_shapes = [a[n].shape for n in REPLICATED]
    rep_all = gather_devices(_pack([grads[n] for n in REPLICATED], row_align=REPLICATED_ROW_TILE))
    rp = [_pack([a[pre + n] for n in REPLICATED], row_align=REPLICATED_ROW_TILE) for pre in ("", "m_", "v_")]

    def adam_rep(w_, g8, mm_, vv_):
        g = g8[0]
        for k in range(1, N_DEVICES):
            g = g + g8[k]
        return (g,) + _adamw(w_, g, mm_, vv_)

    g_, d_, m_, v_ = rowwise(adam_rep, [rp[0], rep_all, rp[1], rp[2]], [], [(PACK_LANES, F32)] * 4, name="adamw_replicated")
    for n, gg, dd, mm_, vv in zip(REPLICATED, _unpack(g_, rep_shapes), _unpack(d_, rep_shapes), _unpack(m_, rep_shapes),
                                  _unpack(v_, rep_shapes)):
        out_grad[n], out_delta[n], out_m[n], out_v[n] = gg, dd, mm_, vv

    return (loss, dx[None], *[out_grad[n] for n in WEIGHTS], *[out_delta[n] for n in WEIGHTS], *[out_m[n] for n in WEIGHTS],
            *[out_v[n] for n in WEIGHTS])
```

```python
import functools
import math

import jax
import jax.numpy as jnp
from jax import lax
from jax.experimental import pallas as pl
from jax.experimental.pallas import tpu as pltpu

F32 = jnp.float32
BF16 = jnp.bfloat16

VMEM_LIMIT_BYTES = 56 * 1024 * 1024

RMS_EPS = 1e-6
D_FF = 2816
S5_GROUP = 16
S5_STATE = 64
SB_HEAD_DIM = 64
LRU_BLOCK_WIDTH = 256
LRU_CONV = 4
LRU_C = 8.0
N_MIXERS = 3

ADAM_LR = 0.001
ADAM_B1 = 0.9
ADAM_B2 = 0.999
ADAM_EPS = 1e-08
ADAM_WD = 0.01
ADAM_STEP = 10


def _params(semantics):
    return pltpu.CompilerParams(dimension_semantics=semantics, vmem_limit_bytes=VMEM_LIMIT_BYTES)


def _tile(dim, prefs):
    for t in prefs:
        if t <= dim and dim % t == 0:
            return t
    return dim


def _mm_call(name, grid, a, a_spec, b, b_spec, dims, extras, outs, epilogue, acc_shape, job=None):
    n_extra, n_out, n_k = len(extras), len(outs), grid[-1]

    def body(a_ref, b_ref, *rest):
        extra_refs = rest[:n_extra]
        out_refs = rest[n_extra:n_extra + n_out]
        acc_ref = rest[n_extra + n_out]
        k = pl.program_id(len(grid) - 1)
        part = lax.dot_general(a_ref[...].astype(BF16), b_ref[...].astype(BF16), dims, preferred_element_type=F32)

        @pl.when(k == 0)
        def _():
            acc_ref[...] = part

        @pl.when(k > 0)
        def _():
            acc_ref[...] += part

        @pl.when(k == n_k - 1)
        def _():
            res = epilogue(acc_ref[...], *[r[...] for r in extra_refs])
            if not isinstance(res, (tuple, list)):
                res = (res,)
            for o_ref, r in zip(out_refs, res):
                o_ref[...] = r.astype(o_ref.dtype)

    sem = ("arbitrary",) * len(grid) if job is not None else ("parallel",) * (len(grid) - 1) + ("arbitrary",)
    body, x_in, x_out, x_shapes, x_scratch, x_args = carry(job, body, grid, 2 + n_extra, n_out)
    res = pl.pallas_call(
        body,
        name=name,
        grid=grid,
        in_specs=[a_spec, b_spec] + [s for _, s in extras] + x_in,
        out_specs=[s for _, s in outs] + x_out,
        out_shape=[s for s, _ in outs] + x_shapes,
        scratch_shapes=[pltpu.VMEM(acc_shape, F32)] + x_scratch,
        compiler_params=_params(sem),
    )(a, b, *[x for x, _ in extras], *x_args)
    return res


def _fit(dim, target, align=128):
    best = None
    for t in range(align, min(dim, target) + 1, align):
        if dim % t == 0:
            best = t
    return best or dim


def mm(a, b, *, name, ta=False, tb=False, extras=(), epilogue=None, out_dtypes=(F32,), tm=1024, tn=1024, tk=1024):
    m, kdim = (a.shape[1], a.shape[0]) if ta else a.shape
    n = b.shape[0] if tb else b.shape[1]
    tm = _fit(m, tm)
    tn = _fit(n, tn)
    tk = _fit(kdim, tk)
    grid = (m // tm, n // tn, kdim // tk)
    a_spec = pl.BlockSpec((tk, tm), lambda i, j, k: (k, i)) if ta else pl.BlockSpec((tm, tk), lambda i, j, k: (i, k))
    b_spec = pl.BlockSpec((tn, tk), lambda i, j, k: (j, k)) if tb else pl.BlockSpec((tk, tn), lambda i, j, k: (k, j))
    dims = (((0 if ta else 1,), (1 if tb else 0,)), ((), ()))
    o_spec = pl.BlockSpec((tm, tn), lambda i, j, k: (i, j))
    if epilogue is None:
        epilogue = lambda acc: acc
    res = _mm_call(name, grid, a, a_spec, b, b_spec, dims, [(x, o_spec) for x in extras],
                   [(jax.ShapeDtypeStruct((m, n), dt), o_spec) for dt in out_dtypes], epilogue, (tm, tn))
    return res[0] if len(res) == 1 else res


def rowwise(fn, rows, consts, outs, sums=(), *, name, tm=256):
    m = rows[0].shape[0]
    pieces = [x if isinstance(x, (list, tuple)) else [x] for x in rows]
    tm = _tile(math.gcd(*[p.shape[-2] for ps in pieces for p in ps]), (tm, 128, 64, 32, 16, 8))
    starts = [[sum(q.shape[-2] for q in ps[:k]) // tm for k in range(len(ps) + 1)] for ps in pieces]
    flat = [p for ps in pieces for p in ps]
    n_rows, n_consts, n_outs = len(flat), len(consts), len(outs)

    def body(*refs):
        i = pl.program_id(0)
        in_vals, at = [], 0
        for ps, st in zip(pieces, starts):
            val = refs[at + len(ps) - 1][...]
            for k in reversed(range(len(ps) - 1)):
                val = jnp.where(i < st[k + 1], refs[at + k][...], val)
            in_vals.append(val)
            at += len(ps)
        in_vals += [r[...] for r in refs[n_rows:n_rows + n_consts]]
        out_refs = refs[n_rows + n_consts:n_rows + n_consts + n_outs]
        sum_refs = refs[n_rows + n_consts + n_outs:]
        res = fn(*in_vals)
        if not isinstance(res, (tuple, list)):
            res = (res,)
        for o_ref, r in zip(out_refs, res[:n_outs]):
            o_ref[...] = r.astype(o_ref.dtype)
        if sum_refs:
            first = pl.program_id(0) == 0

            @pl.when(first)
            def _():
                for s_ref, r in zip(sum_refs, res[n_outs:]):
                    s_ref[...] = r.astype(F32)

            @pl.when(jnp.logical_not(first))
            def _():
                for s_ref, r in zip(sum_refs, res[n_outs:]):
                    s_ref[...] += r.astype(F32)

    def whole(shape):
        nd = len(shape)
        return pl.BlockSpec(shape, lambda i: (0,) * nd)

    def row_spec(x, lo, hi):
        at = lambda i: jnp.clip(i - lo, 0, hi - lo - 1)
        if x.ndim == 3:
            return pl.BlockSpec((x.shape[0], tm, x.shape[2]), lambda i: (0, at(i), 0))
        return pl.BlockSpec((tm, x.shape[1]), lambda i: (at(i), 0))

    in_specs = [row_spec(p, st[k], st[k + 1]) for ps, st in zip(pieces, starts) for k, p in enumerate(ps)]
    in_specs += [whole(c.shape) for c in consts]
    out_specs = [pl.BlockSpec((tm, nc), lambda i: (i, 0)) for nc, _ in outs] + [whole(tuple(s)) for s in sums]
    out_shape = [jax.ShapeDtypeStruct((m, nc), dt) for nc, dt in outs] + [jax.ShapeDtypeStruct(tuple(s), F32) for s in sums]
    res = pl.pallas_call(
        body,
        name=name,
        grid=(m // tm,),
        in_specs=in_specs,
        out_specs=out_specs,
        out_shape=out_shape,
        compiler_params=_params(("arbitrary",) if sums else ("parallel",)),
    )(*flat, *consts)
    return res[0] if len(res) == 1 else res


def _rms(h, g):
    return h * lax.rsqrt(jnp.mean(h * h, axis=-1, keepdims=True) + RMS_EPS) * g


def _sigmoid(x):
    return 1.0 / (1.0 + jnp.exp(-x))


def _silu_mul(g, u):
    return g * _sigmoid(g) * u


def _gelu(x):
    return 0.5 * x * (1.0 + jnp.tanh(math.sqrt(2.0 / math.pi) * (x + 0.044715 * (x * x * x))))


def _softplus(x):
    return jnp.maximum(x, 0.0) + jnp.log(1.0 + jnp.exp(-jnp.abs(x)))


def rms_fwd(h, g, name):
    return rowwise(lambda x, gg: _rms(x, gg), [h], [g.reshape(1, -1)], [(h.shape[1], BF16)], name=name)


def norm_input_bwd(a, a_spec, n_k, w, h, g, dres, name, tm=512):
    m, d = h.shape
    tk = w.shape[1] // n_k
    tm = _fit(m, tm)

    def body(a_ref, w_ref, h_ref, g_ref, dres_ref, dh_ref, dg_ref, acc_ref):
        i, k = pl.program_id(0), pl.program_id(1)
        part = lax.dot_general(a_ref[...].astype(BF16), w_ref[...], _NT, preferred_element_type=F32)

        @pl.when(k == 0)
        def _():
            acc_ref[...] = part

        @pl.when(k > 0)
        def _():
            acc_ref[...] += part

        @pl.when(k == n_k - 1)
        def _():
            _, vjp = jax.vjp(_rms, h_ref[...], g_ref[...])
            dx, dg = vjp(acc_ref[...])
            dh_ref[...] = dres_ref[...] + dx

            @pl.when(i == 0)
            def _():
                dg_ref[...] = dg

            @pl.when(i > 0)
            def _():
                dg_ref[...] += dg

    row = pl.BlockSpec((tm, d), lambda i, k: (i, 0))
    vec = pl.BlockSpec((1, d), lambda i, k: (0, 0))
    dh, dg = pl.pallas_call(
        body,
        name=name,
        grid=(m // tm, n_k),
        in_specs=[a_spec(tm, tk), pl.BlockSpec((d, tk), lambda i, k: (0, k)), row, vec, row],
        out_specs=[row, vec],
        out_shape=[jax.ShapeDtypeStruct((m, d), F32), jax.ShapeDtypeStruct((1, d), F32)],
        scratch_shapes=[pltpu.VMEM((tm, d), F32)],
        compiler_params=_params(("arbitrary", "arbitrary")),
    )(a, w, h, g.reshape(1, d), dres)
    return dh, dg.reshape(-1)


def _rows_spec(tm, tk):
    return pl.BlockSpec((tm, tk), lambda i, k: (i, k))


def ffn_in(h, g, w_in, tag, job=None):
    m, d = h.shape
    f = w_in.shape[1] // 2
    tm, tn = _fit(m, 512), _fit(f, 1408)
    nj = f // tn

    def body(h_ref, g_ref, wg_ref, wu_ref, hn_ref, gu_ref, act_ref, hn_scr):
        @pl.when(pl.program_id(1) == 0)
        def _():
            hn = _rms(h_ref[...], g_ref[...]).astype(BF16)
            hn_scr[...] = hn
            hn_ref[...] = hn

        a = hn_scr[...]
        gate = jnp.dot(a, wg_ref[...], preferred_element_type=F32)
        up = jnp.dot(a, wu_ref[...], preferred_element_type=F32)
        gu_ref[0] = gate.astype(BF16)
        gu_ref[1] = up.astype(BF16)
        act_ref[...] = _silu_mul(gate, up).astype(BF16)

    grid = (m // tm, nj)
    body, x_in, x_out, x_shapes, x_scratch, x_args = carry(job, body, grid, 4, 3)
    res = pl.pallas_call(
        body,
        name=f"{tag}_in",
        grid=grid,
        in_specs=[pl.BlockSpec((tm, d), lambda i, j: (i, 0)), pl.BlockSpec((1, d), lambda i, j: (0, 0)),
                  pl.BlockSpec((d, tn), lambda i, j: (0, j)), pl.BlockSpec((d, tn), lambda i, j: (0, nj + j))] + x_in,
        out_specs=[pl.BlockSpec((tm, d), lambda i, j: (i, 0)), pl.BlockSpec((2, tm, tn), lambda i, j: (0, i, j)),
                   pl.BlockSpec((tm, tn), lambda i, j: (i, j))] + x_out,
        out_shape=[jax.ShapeDtypeStruct((m, d), BF16), jax.ShapeDtypeStruct((2, m, f), BF16),
                   jax.ShapeDtypeStruct((m, f), BF16)] + x_shapes,
        scratch_shapes=[pltpu.VMEM((tm, d), BF16)] + x_scratch,
        compiler_params=_params(("arbitrary", "arbitrary")),
    )(h, g.reshape(1, d), w_in, w_in, *x_args)
    return res[:3], res[3:]


def ffn_fwd(h, g, w_in, w_out, tag, job=None):
    (hn, gu, act), carried = ffn_in(h, g, w_in, tag, job)
    h_new = mm(act, w_out, name=f"{tag}_out", extras=(h,), epilogue=lambda acc, res: res + 0.5 * acc, tm=512, tk=2816)
    return h_new, (h, hn, gu, act), carried


def ffn_bwd(dh_new, saved, g, w_in, w_out, tag, jobs=(None, None)):
    h, hn, gu, act = saved
    m, d = h.shape
    f = w_out.shape[0]

    def act_bwd(acc, gu_blk):
        _, vjp = jax.vjp(_silu_mul, gu_blk[0].astype(F32), gu_blk[1].astype(F32))
        return jnp.stack(vjp(0.5 * acc))

    tm, tn = _fit(m, 512), _fit(f, 1408)
    pair = pl.BlockSpec((2, tm, tn), lambda i, j, k: (0, i, j))
    dgu, *carried0 = _mm_call(f"{tag}_dgu", (m // tm, f // tn, 1), dh_new, pl.BlockSpec((tm, d), lambda i, j, k: (i, 0)), w_out,
                              pl.BlockSpec((tn, d), lambda i, j, k: (j, 0)), _NT, [(gu, pair)],
                              [(jax.ShapeDtypeStruct((2, m, f), BF16), pair)], act_bwd, (tm, tn), jobs[0])
    dw_out = mm(act, dh_new, ta=True, name=f"{tag}_dwout", epilogue=lambda acc: 0.5 * acc, tm=1408)

    tn, tk = _fit(f, 1408), _fit(m, 2048)
    nh = f // tn
    dw_in, *carried1 = _mm_call(f"{tag}_dwin", (1, 2 * nh, m // tk), hn, pl.BlockSpec((tk, d), lambda i, j, k: (k, 0)), dgu,
                                pl.BlockSpec((None, tk, tn), lambda i, j, k: (j // nh, k, j % nh)), _TN, [],
                                [(jax.ShapeDtypeStruct((d, 2 * f), F32), pl.BlockSpec((d, tn), lambda i, j, k: (0, j)))],
                                lambda acc: acc, (d, tn), jobs[1])

    nkh = f // _fit(f, 2816)
    dh, dg = norm_input_bwd(dgu, lambda tm, tk: pl.BlockSpec((None, tm, tk), lambda i, k: (k // nkh, i, k % nkh)), 2 * nkh,
                            w_in, h, g, dh_new, f"{tag}_dhn")
    return dh, dg, dw_in, dw_out, carried0, carried1


def loss_fwd_bwd(h, g, target):
    d = h.shape[1]

    def fn(x, t, gg):
        y, vjp = jax.vjp(_rms, x, gg)
        err = y - t
        dx, dg = vjp(err * (1.0 / d))
        part = 0.5 * jnp.sum(jnp.sum(err * err, axis=1, keepdims=True), axis=0, keepdims=True) * (1.0 / d)
        return dx, dg, jnp.broadcast_to(part, (1, 128))

    dh, dg, loss = rowwise(fn, [h, target], [g.reshape(1, -1)], [(d, F32)], [(1, d), (1, 128)], name="loss_head")
    return loss[0, 0], dh, dg.reshape(-1)


def _shift_down(v, d, fill):
    rows = lax.broadcasted_iota(jnp.int32, v.shape, 0)
    return jnp.where(rows < d, fill, pltpu.roll(v, d, 0))


def _shift_up(v, d, fill):
    t = v.shape[0]
    rows = lax.broadcasted_iota(jnp.int32, v.shape, 0)
    return jnp.where(rows >= t - d, fill, pltpu.roll(v, t - d, 0))


def _scan_fwd(a, x):
    d = 1
    while d < a.shape[0]:
        x = x + a * _shift_down(x, d, 0.0)
        a = a * _shift_down(a, d, 1.0)
        d *= 2
    return a, x


def _scan_bwd(b, x):
    d = 1
    while d < b.shape[0]:
        x = x + b * _shift_up(x, d, 0.0)
        b = b * _shift_up(b, d, 1.0)
        d *= 2
    return b, x


def _rows_before(cur, prev8, s):
    r = pltpu.roll(cur, s, 0)
    p = pltpu.roll(prev8, s, 0)
    rows = lax.broadcasted_iota(jnp.int32, prev8.shape, 0)
    return jnp.concatenate([jnp.where(rows < s, p, r[:8]), r[8:]], axis=0)


def _rows_after(cur, next8, s):
    t = cur.shape[0]
    r = pltpu.roll(cur, t - s, 0)
    p = pltpu.roll(next8, 8 - s, 0)
    rows = lax.broadcasted_iota(jnp.int32, next8.shape, 0)
    return jnp.concatenate([r[:t - 8], jnp.where(rows >= 8 - s, p, r[t - 8:])], axis=0)


LRU_CHUNK = 256


def _neg_expm1(y):
    small = -y * (1.0 + 0.5 * y * (1.0 + y * (1.0 / 3.0)))
    return jnp.where(y > -0.01, small, 1.0 - jnp.exp(y))


def _lru_gate(xc, pre_a, pre_x, lam):
    r = _sigmoid(pre_a)
    ig = _sigmoid(pre_x)
    log_a = (-LRU_C * r) * _softplus(-lam)
    return jnp.exp(log_a), (ig * xc) * jnp.sqrt(_neg_expm1(2.0 * log_a))


def _lru_conv(br, prev8, conv_w, conv_b):
    taps = [br] + [_rows_before(br, prev8, s) for s in range(1, LRU_CONV)]
    xc = conv_b
    for k in range(LRU_CONV):
        xc = xc + conv_w[k:k + 1, :] * taps[LRU_CONV - 1 - k]
    return xc, taps


def _lru_pre(xcb, w_ref, bias):
    nb = w_ref.shape[0]
    bw = w_ref.shape[1]
    return jnp.concatenate(
        [jnp.dot(xcb[:, n * bw:(n + 1) * bw], w_ref[n], preferred_element_type=F32) for n in range(nb)], axis=1) + bias


def lru_scan_fwd(bgr, conv_w, conv_b, wa, ba, wx, bx, lam):
    l, w2 = bgr.shape
    w = w2 // 2
    t = _tile(l, (LRU_CHUNK, 128, 64, 32, 16, 8))

    def body(bg_ref, br_ref, cw_ref, cb_ref, wa_ref, ba_ref, wx_ref, bx_ref, lam_ref, y_ref, h_ref, tail_ref, hprev_ref):
        @pl.when(pl.program_id(0) == 0)
        def _():
            tail_ref[...] = jnp.zeros_like(tail_ref)
            hprev_ref[...] = jnp.zeros_like(hprev_ref)

        br = br_ref[...]
        xc, _ = _lru_conv(br, tail_ref[...], cw_ref[...], cb_ref[...])
        xcb = xc.astype(BF16)
        a, gx = _lru_gate(xc, _lru_pre(xcb, wa_ref, ba_ref[...]), _lru_pre(xcb, wx_ref, bx_ref[...]), lam_ref[...])
        acum, x = _scan_fwd(a, gx)
        h = x + acum * hprev_ref[pl.ds(7, 1), :]
        y_ref[...] = (_gelu(bg_ref[...]) * h).astype(y_ref.dtype)
        h_ref[...] = h
        tail_ref[...] = br[t - 8:, :]
        hprev_ref[...] = h[t - 8:, :]

    def whole(x):
        nd = x.ndim
        return pl.BlockSpec(x.shape, lambda i: (0,) * nd)

    consts = [conv_w, conv_b, wa, ba, wx, bx, lam]
    return pl.pallas_call(
        body,
        name="lru_scan_fwd",
        grid=(l // t,),
        in_specs=[pl.BlockSpec((t, w), lambda i: (i, 0)), pl.BlockSpec((t, w), lambda i: (i, 1))] + [whole(c) for c in consts],
        out_specs=[pl.BlockSpec((t, w), lambda i: (i, 0)), pl.BlockSpec((t, w), lambda i: (i, 0))],
        out_shape=[jax.ShapeDtypeStruct((l, w), BF16), jax.ShapeDtypeStruct((l, w), F32)],
        scratch_shapes=[pltpu.VMEM((8, w), F32), pltpu.VMEM((8, w), F32)],
        compiler_params=_params(("arbitrary",)),
    )(bgr, bgr, *consts)


def lru_scan_bwd(dy, bgr, hseq, conv_w, conv_b, wa, ba, wx, bx, lam):
    l, w2 = bgr.shape
    w = w2 // 2
    t = _tile(l, (LRU_CHUNK, 128, 64, 32, 16, 8))
    nc = l // t
    nb, bw = wa.shape[0], wa.shape[1]

    def body(dy_ref, bg_ref, br_ref, brh_ref, h_ref, hh_ref, cw_ref, cb_ref, wa_ref, ba_ref, wx_ref, bx_ref, lam_ref,
             dbgr_ref, dcw_ref, dcb_ref, dwa_ref, dba_ref, dwx_ref, dbx_ref, dlam_ref, dxcn_ref, carry_ref):
        i = pl.program_id(0)
        has_prev = (i < nc - 1).astype(F32)

        @pl.when(i == 0)
        def _():
            dxcn_ref[...] = jnp.zeros_like(dxcn_ref)
            carry_ref[...] = jnp.zeros_like(carry_ref)

        br = br_ref[...]
        cw = cw_ref[...]
        xc, taps = _lru_conv(br, brh_ref[...] * has_prev, cw, cb_ref[...])
        xcb = xc.astype(BF16)
        (a, _), gate_vjp = jax.vjp(_lru_gate, xc, _lru_pre(xcb, wa_ref, ba_ref[...]), _lru_pre(xcb, wx_ref, bx_ref[...]),
                                   lam_ref[...])
        hs = h_ref[...]
        _, out_vjp = jax.vjp(lambda g_, h_: _gelu(g_) * h_, bg_ref[...], hs)
        dbg, dhs = out_vjp(dy_ref[...])
        bcum, x = _scan_bwd(_shift_up(a, 1, 1.0), dhs)
        dh = x + bcum * carry_ref[pl.ds(0, 1), :]
        da = dh * _rows_before(hs, hh_ref[...] * has_prev, 1)
        dxc, dpa, dpx, dlam = gate_vjp((da, dh))
        dpab, dpxb = dpa.astype(BF16), dpx.astype(BF16)
        nt = (((1,), (1,)), ((), ()))
        tn = (((0,), (0,)), ((), ()))
        dxb, dwa, dwx = [], [], []
        for n in range(nb):
            sl = slice(n * bw, (n + 1) * bw)
            dxb.append(lax.dot_general(dpab[:, sl], wa_ref[n], nt, preferred_element_type=F32)
                       + lax.dot_general(dpxb[:, sl], wx_ref[n], nt, preferred_element_type=F32))
            dwa.append(lax.dot_general(xcb[:, sl], dpab[:, sl], tn, preferred_element_type=F32))
            dwx.append(lax.dot_general(xcb[:, sl], dpxb[:, sl], tn, preferred_element_type=F32))
        dxc = dxc + jnp.concatenate(dxb, axis=1)
        ups = [dxc] + [_rows_after(dxc, dxcn_ref[...], s) for s in range(1, LRU_CONV)]
        dbr = cw[LRU_CONV - 1:LRU_CONV, :] * ups[0]
        for k in range(LRU_CONV - 1):
            dbr = dbr + cw[k:k + 1, :] * ups[LRU_CONV - 1 - k]
        dbgr_ref[:, :w] = dbg.astype(dbgr_ref.dtype)
        dbgr_ref[:, w:] = dbr.astype(dbgr_ref.dtype)
        dcw = jnp.concatenate([jnp.sum(dxc * taps[LRU_CONV - 1 - k], axis=0, keepdims=True) for k in range(LRU_CONV)], axis=0)
        sums = [(dcw_ref, dcw), (dcb_ref, jnp.sum(dxc, axis=0, keepdims=True)), (dwa_ref, jnp.stack(dwa)),
                (dba_ref, jnp.sum(dpa, axis=0, keepdims=True)), (dwx_ref, jnp.stack(dwx)),
                (dbx_ref, jnp.sum(dpx, axis=0, keepdims=True)), (dlam_ref, dlam)]

        @pl.when(i == 0)
        def _():
            for ref, val in sums:
                ref[...] = val

        @pl.when(i > 0)
        def _():
            for ref, val in sums:
                ref[...] += val

        dxcn_ref[...] = dxc[:8, :]
        carry_ref[...] = (a * dh)[:8, :]

    def whole(shape):
        nd = len(shape)
        return pl.BlockSpec(tuple(shape), lambda i: (0,) * nd)

    consts = [conv_w, conv_b, wa, ba, wx, bx, lam]
    t8 = t // 8
    rev = lambda i: nc - 1 - i
    halo = lambda i: jnp.maximum(rev(i) * t8 - 1, 0)
    in_specs = [
        pl.BlockSpec((t, w), lambda i: (rev(i), 0)),
        pl.BlockSpec((t, w), lambda i: (rev(i), 0)),
        pl.BlockSpec((t, w), lambda i: (rev(i), 1)),
        pl.BlockSpec((8, w), lambda i: (halo(i), 1)),
        pl.BlockSpec((t, w), lambda i: (rev(i), 0)),
        pl.BlockSpec((8, w), lambda i: (halo(i), 0)),
    ] + [whole(c.shape) for c in consts]
    sum_shapes = [conv_w.shape, conv_b.shape, wa.shape, ba.shape, wx.shape, bx.shape, lam.shape]
    res = pl.pallas_call(
        body,
        name="lru_scan_bwd",
        grid=(nc,),
        in_specs=in_specs,
        out_specs=[pl.BlockSpec((t, w2), lambda i: (rev(i), 0))] + [whole(s) for s in sum_shapes],
        out_shape=[jax.ShapeDtypeStruct((l, w2), BF16)] + [jax.ShapeDtypeStruct(tuple(s), F32) for s in sum_shapes],
        scratch_shapes=[pltpu.VMEM((8, w), F32), pltpu.VMEM((8, w), F32)],
        compiler_params=_params(("arbitrary",)),
    )(dy, bgr, bgr, bgr, hseq, hseq, *consts)
    return res[0], res[1:]


def lru_fwd(hn, p):
    bgr = mm(hn, p["w_in"], name="lru_in")
    y, hseq = lru_scan_fwd(bgr, p["conv_w"], p["conv_b"], p["wa"], p["ba"], p["wx"], p["bx"], p["lam"])
    return y, (hn, bgr, hseq, y)


def lru_bwd(dmixed, saved, p):
    hn, bgr, hseq, y = saved
    dy = mm(dmixed, p["w_out"], tb=True, name="lru_dy")
    dw_out = mm(y, dmixed, ta=True, name="lru_dwout")
    dbgr, (dcw, dcb, dwa, dba, dwx, dbx, dlam) = lru_scan_bwd(dy, bgr, hseq, p["conv_w"], p["conv_b"], p["wa"], p["ba"],
                                                               p["wx"], p["bx"], p["lam"])
    dw_in = mm(hn, dbgr, ta=True, name="lru_dwin")
    grads = dict(w_in=dw_in, conv_w=dcw, conv_b=dcb, wa=dwa, ba=dba, wx=dwx, bx=dbx, lam=dlam, w_out=dw_out)
    return (dbgr, p["w_in"]), grads


SB_BLOCK = 256
SB_BLOCK_Q = 1024
_NT = (((1,), (1,)), ((), ()))
_TN = (((0,), (0,)), ((), ()))


def _running_sums(x, tri, total_col):
    xb = x.astype(BF16)
    run = jnp.dot(xb, tri, preferred_element_type=F32)
    return run, xb, run[:, total_col:total_col + 1]


def _tri(n, cmp):
    r = lax.broadcasted_iota(jnp.int32, (n, n), 0)
    c = lax.broadcasted_iota(jnp.int32, (n, n), 1)
    return cmp(r, c).astype(BF16)


SB_PAIR = 2 * SB_HEAD_DIM


def _pair_masks(x2, scale=None):
    lane = lax.broadcasted_iota(jnp.int32, x2.shape, 1)
    zero = jnp.zeros_like(x2)
    a, b = jnp.where(lane < SB_HEAD_DIM, x2, zero), jnp.where(lane >= SB_HEAD_DIM, x2, zero)
    if scale is not None:
        a, b = a * scale, b * scale
    return a, b


def _causal(shape, q0, k0):
    return lax.broadcasted_iota(jnp.int32, shape, 1) + k0 < lax.broadcasted_iota(jnp.int32, shape, 0) + q0


def _sb_blocks(l):
    bk = _tile(l, (SB_BLOCK, 128))
    bq = _tile(l, (SB_BLOCK_Q, 2 * SB_BLOCK, SB_BLOCK, 128))
    return bq, bk


def sb_pair_fwd(qkv, job=None):
    l, d3 = qkv.shape
    d = d3 // 3
    npair = d // SB_PAIR
    bq, bk = _sb_blocks(l)
    ratio = bq // bk
    scale = SB_HEAD_DIM ** -0.5
    t_suf = _tri(bk, lambda r, c: r > c)

    def body(q_ref, k_ref, v_ref, tsuf_ref, o_ref, ltot_ref):
        i = pl.program_id(1)
        qs = _pair_masks(q_ref[...], scale)
        tsuf = tsuf_ref[...]

        def tile(j, carry, masked, r0=0):
            rows = pl.ds(pl.multiple_of(j * bk, bk), bk)
            k2 = k_ref[rows, :]
            v2 = v_ref[rows, :]
            out = []
            for q1, (c_r, acc) in zip(qs, carry):
                z = lax.dot_general(q1[r0:], k2, _NT, preferred_element_type=F32)
                lk = -_softplus(z)
                if masked:
                    causal = _causal(z.shape, i * bq + r0, j * bk)
                    lk = jnp.where(causal, lk, 0.0)
                later, lkb, later0 = _running_sums(lk, tsuf, 0)
                w = jnp.exp(z + lk + c_r[r0:] + later)
                if masked:
                    w = jnp.where(causal, w, 0.0)
                c_new = c_r[r0:] + later0 + lkb[:, 0:1].astype(F32)
                acc_new = acc[r0:] + jnp.dot(w.astype(BF16), v2, preferred_element_type=F32)
                if r0:
                    c_new, acc_new = jnp.concatenate([c_r[:r0], c_new]), jnp.concatenate([acc[:r0], acc_new])
                out.append((c_new, acc_new))
            return tuple(out)

        carry = ((jnp.zeros((bq, 1), F32), jnp.zeros((bq, SB_PAIR), F32)),) * 2
        for dgl in reversed(range(ratio)):
            carry = tile(i * ratio + dgl, carry, True, dgl * bk)
        (c_a, acc_a), (c_b, acc_b) = lax.fori_loop(0, i * ratio, lambda jj, c: tile(i * ratio - 1 - jj, c, False), carry)
        lane = lax.broadcasted_iota(jnp.int32, acc_a.shape, 1)
        o_ref[...] = jnp.where(lane < SB_HEAD_DIM, acc_a, acc_b).astype(o_ref.dtype)
        ltot_ref[...] = jnp.where(lax.broadcasted_iota(jnp.int32, (bq, 2), 1) == 0, c_a, c_b)

    grid = (npair, l // bq)
    body, x_in, x_out, x_shapes, x_scratch, x_args = carry(job, body, grid, 4, 2)
    res = pl.pallas_call(
        body,
        name="sb_attn_fwd",
        grid=grid,
        in_specs=[
            pl.BlockSpec((bq, SB_PAIR), lambda p, i: (i, p)),
            pl.BlockSpec((l, SB_PAIR), lambda p, i: (0, npair + p)),
            pl.BlockSpec((l, SB_PAIR), lambda p, i: (0, 2 * npair + p)),
            pl.BlockSpec((bk, bk), lambda p, i: (0, 0)),
        ] + x_in,
        out_specs=[pl.BlockSpec((bq, SB_PAIR), lambda p, i: (i, p)), pl.BlockSpec((None, bq, 2), lambda p, i: (p, i, 0))] + x_out,
        out_shape=[jax.ShapeDtypeStruct((l, d), BF16), jax.ShapeDtypeStruct((npair, l, 2), F32)] + x_shapes,
        scratch_shapes=x_scratch,
        compiler_params=_params(("arbitrary", "arbitrary")),
    )(qkv, qkv, qkv, t_suf, *x_args)
    return res[0], res[1], res[2:]


def sb_pair_bwd(qkv, do, ltot, job=None):
    l, d3 = qkv.shape
    d = d3 // 3
    npair = d // SB_PAIR
    bq, bk = _sb_blocks(l)
    ratio = bq // bk
    scale = SB_HEAD_DIM ** -0.5
    t_inc = _tri(bk, lambda r, c: r <= c)
    t_exc = _tri(bk, lambda r, c: r < c)

    def body(q_ref, k_ref, v_ref, do_ref, ltot_ref, tinc_ref, texc_ref, dq_ref, dk_ref, dv_ref):
        i = pl.program_id(1)

        @pl.when(i == 0)
        def _():
            dk_ref[...] = jnp.zeros_like(dk_ref)
            dv_ref[...] = jnp.zeros_like(dv_ref)

        qs = _pair_masks(q_ref[...], scale)
        dos = _pair_masks(do_ref[...])
        lt = ltot_ref[...]
        ltots = (lt[:, 0:1], lt[:, 1:2])
        tinc = tinc_ref[...]
        texc = texc_ref[...]

        def tile(j, carry, masked, r0=0):
            rows = pl.ds(pl.multiple_of(j * bk, bk), bk)
            k2 = k_ref[rows, :]
            v2 = v_ref[rows, :]
            out = []
            dk2 = dv2 = None
            for q1, do1, ltot1, (c_l, c_p, dq) in zip(qs, dos, ltots, carry):
                q1s, do1s = q1[r0:], do1[r0:]
                z = lax.dot_general(q1s, k2, _NT, preferred_element_type=F32)
                lk = -_softplus(z)
                if masked:
                    causal = _causal(z.shape, i * bq + r0, j * bk)
                    lk = jnp.where(causal, lk, 0.0)
                log_beta = z + lk
                upto, _, lk_tile = _running_sums(lk, tinc, bk - 1)
                w = jnp.exp(log_beta + (ltot1[r0:] - c_l[r0:]) - upto)
                if masked:
                    w = jnp.where(causal, w, 0.0)
                g = w * lax.dot_general(do1s, v2, _NT, preferred_element_type=F32)
                before, gb, before_last = _running_sums(g, texc, bk - 1)
                dz = g - jnp.exp(log_beta) * (g + c_p[r0:] + before)
                if masked:
                    dz = jnp.where(causal, dz, 0.0)
                dzb = dz.astype(BF16)
                dk1 = lax.dot_general(dzb, q1s, _TN, preferred_element_type=F32)
                dv1 = lax.dot_general(w.astype(BF16), do1s, _TN, preferred_element_type=F32)
                dk2 = dk1 if dk2 is None else dk2 + dk1
                dv2 = dv1 if dv2 is None else dv2 + dv1
                new = (c_l[r0:] + lk_tile, c_p[r0:] + before_last + gb[:, bk - 1:bk].astype(F32),
                       dq[r0:] + jnp.dot(dzb, k2, preferred_element_type=F32))
                if r0:
                    new = tuple(jnp.concatenate([old[:r0], part]) for old, part in zip((c_l, c_p, dq), new))
                out.append(new)
            dk_ref[rows, :] += dk2
            dv_ref[rows, :] += dv2
            return tuple(out)

        zero = jnp.zeros((bq, 1), F32)
        carry = ((zero, zero, jnp.zeros((bq, SB_PAIR), F32)),) * 2
        carry = lax.fori_loop(0, i * ratio, lambda j, c: tile(j, c, False), carry)
        for dgl in range(ratio):
            carry = tile(i * ratio + dgl, carry, True, dgl * bk)
        (_, _, dq_a), (_, _, dq_b) = carry
        lane = lax.broadcasted_iota(jnp.int32, dq_a.shape, 1)
        dq_ref[...] = jnp.where(lane < SB_HEAD_DIM, dq_a, dq_b) * scale

    col = lambda s: pl.BlockSpec((l, SB_PAIR), lambda p, i: (0, s * npair + p))
    blk_spec = pl.BlockSpec((bq, SB_PAIR), lambda p, i: (i, p))
    tri_spec = pl.BlockSpec((bk, bk), lambda p, i: (0, 0))
    grid = (npair, l // bq)
    body, x_in, x_out, x_shapes, x_scratch, x_args = carry(job, body, grid, 7, 3)
    res = pl.pallas_call(
        body,
        name="sb_attn_bwd",
        grid=grid,
        in_specs=[blk_spec, col(1), col(2), blk_spec, pl.BlockSpec((None, bq, 2), lambda p, i: (p, i, 0)), tri_spec, tri_spec] + x_in,
        out_specs=[blk_spec, pl.BlockSpec((l, SB_PAIR), lambda p, i: (0, p)), pl.BlockSpec((l, SB_PAIR), lambda p, i: (0, p))] + x_out,
        out_shape=[jax.ShapeDtypeStruct((l, d), F32)] * 3 + x_shapes,
        scratch_shapes=x_scratch,
        compiler_params=_params(("arbitrary", "arbitrary")),
    )(qkv, qkv, qkv, do, ltot, t_inc, t_exc, *x_args)
    return res[:3], res[3:]


def sb_fwd(hn, p, job=None):
    qkv = mm(hn, p["w_qkv"], name="sb_qkv", out_dtypes=(BF16,), tm=2048, tn=512)
    o, ltot, carried = sb_pair_fwd(qkv, job)
    return o, (hn, qkv, ltot, o), carried


def sb_bwd(dmixed, saved, p, job=None):
    hn, qkv, ltot, o = saved
    do = mm(dmixed, p["w_out"], tb=True, name="sb_do", out_dtypes=(BF16,))
    dw_out = mm(o, dmixed, ta=True, name="sb_dwout")
    dq_dk_dv, carried = sb_pair_bwd(qkv, do, ltot, job)
    dqkv = jnp.concatenate([g.astype(BF16) for g in dq_dk_dv], axis=1)
    dw_qkv = mm(hn, dqkv, ta=True, name="sb_dwqkv")
    return (dqkv, p["w_qkv"]), dict(w_qkv=dw_qkv, w_out=dw_out), carried


S5_CHUNK = 256
S5_CHUNK_BWD = 128
S5_SLAB_GROUPS = 8
S5_LANES = 128


def _s5_discretise(lr, li, ldt, bre, bim):
    dt = jnp.exp(ldt)
    mag = jnp.exp(lr * dt)
    lbr = mag * jnp.cos(li * dt)
    lbi = mag * jnp.sin(li * dt)
    inv = 1.0 / (lr * lr + li * li)
    cr = ((lbr - 1.0) * lr + lbi * li) * inv
    ci = (lbi * lr - (lbr - 1.0) * li) * inv
    return lbr, lbi, cr * bre - ci * bim, cr * bim + ci * bre


def _s5_cols(lam_re, lam_im, log_dt, b_re, b_im):
    g, p = lam_re.shape
    col = lambda x: x.reshape(g * p, 1)
    ldt = jnp.broadcast_to(log_dt[:, None], (g, p))
    return col(lam_re), col(lam_im), col(ldt), b_re.reshape(g * p, -1), b_im.reshape(g * p, -1)


def _slab_b(bbar):
    sg = S5_SLAB_GROUPS
    gp, h = bbar.shape
    p = S5_STATE
    x = bbar.reshape(gp // (sg * p), sg, p, h)
    return jnp.einsum("kaph,ab->kahbp", x, jnp.eye(sg, dtype=x.dtype)).reshape(-1, sg * h, sg * p)


def _unslab_b(dslab):
    sg, p = S5_SLAB_GROUPS, S5_STATE
    nk, sh, _ = dslab.shape
    h = sh // sg
    x = dslab.reshape(nk, sg, h, sg, p)
    return jnp.einsum("kahbp,ab->kaph", x, jnp.eye(sg, dtype=x.dtype)).reshape(nk * sg * p, h)


def _slab_c(c):
    sg = S5_SLAB_GROUPS
    g, h, p = c.shape
    x = c.reshape(g // sg, sg, h, p)
    return jnp.einsum("kahp,ab->kapbh", x, jnp.eye(sg, dtype=x.dtype)).reshape(-1, sg * p, sg * h)


def _unslab_c(dslab):
    sg, p = S5_SLAB_GROUPS, S5_STATE
    nk, _, sh = dslab.shape
    h = sh // sg
    x = dslab.reshape(nk, sg, p, sg, h)
    return jnp.einsum("kapbh,ab->kahp", x, jnp.eye(sg, dtype=x.dtype)).reshape(nk * sg, h, p)


def _cmul_scan(lre, lim, xre, xim, re_scr, im_scr, cre, cim, reverse):
    nb, lanes = re_scr.shape[0], re_scr.shape[2]
    ends = []
    for c in range(nb):
        sl = slice(c * lanes, (c + 1) * lanes)
        re_scr[c] = xre[:, sl]
        im_scr[c] = xim[:, sl]
        ends.append(_cmul_scan_block(lre[:, sl], lim[:, sl], re_scr.at[c], im_scr.at[c], cre[:, sl], cim[:, sl], reverse))
    return (jnp.concatenate([re_scr[c] for c in range(nb)], axis=1), jnp.concatenate([im_scr[c] for c in range(nb)], axis=1),
            jnp.concatenate([e[0] for e in ends], axis=1), jnp.concatenate([e[1] for e in ends], axis=1))


def _cmul_scan_block(lre, lim, re_ref, im_ref, cre, cim, reverse):
    t = re_ref.shape[0]
    g = t // 8
    shift = _shift_up if reverse else _shift_down
    cmul = lambda ar, ai, br, bi: (ar * br - ai * bi, ar * bi + ai * br)
    pows = [(lre, lim)]
    for _ in range(7):
        pows.append(cmul(*pows[-1], lre, lim))
    local, prev = [None] * 8, None
    for r in (reversed(range(8)) if reverse else range(8)):
        cr, ci = re_ref[pl.ds(r, g, stride=8), :], im_ref[pl.ds(r, g, stride=8), :]
        if prev is not None:
            pr, pi = cmul(lre, lim, *prev)
            cr, ci = cr + pr, ci + pi
        local[r] = prev = (cr, ci)
    yr, yi = local[0 if reverse else 7]
    edge = lax.broadcasted_iota(jnp.int32, yr.shape, 0) == (g - 1 if reverse else 0)
    mr, mi = pows[7]
    kr, ki = cmul(mr, mi, cre, cim)
    yr, yi = yr + jnp.where(edge, kr, 0.0), yi + jnp.where(edge, ki, 0.0)
    d = 1
    while d < g:
        pr, pi = cmul(mr, mi, shift(yr, d, 0.0), shift(yi, d, 0.0))
        yr, yi = yr + pr, yi + pi
        mr, mi = cmul(mr, mi, mr, mi)
        d *= 2
    er, ei = jnp.where(edge, cre, shift(yr, 1, 0.0)), jnp.where(edge, cim, shift(yi, 1, 0.0))
    for r in range(8):
        pr, pi = cmul(*pows[7 - r if reverse else r], er, ei)
        re_ref[pl.ds(r, g, stride=8), :] = local[r][0] + pr
        im_ref[pl.ds(r, g, stride=8), :] = local[r][1] + pi
    return yr, yi


def s5_scan_fwd(u, lbr, lbi, bbd_re, bbd_im, cbd_re, cbd_imn, d_skip, job=None):
    l, w = u.shape
    n = lbr.shape[1]
    nk, cw, sw = bbd_re.shape
    t = _tile(l, (S5_CHUNK, 64, 32, 16, 8))

    def body(u_ref, lbr_ref, lbi_ref, bre_ref, bim_ref, cre_ref, cim_ref, d_ref, sre_ref, sim_ref, y_ref, z_ref, pre_ref, pim_ref,
             xr_ref, xi_ref):
        @pl.when(pl.program_id(0) == 0)
        def _():
            pre_ref[...] = jnp.zeros_like(pre_ref)
            pim_ref[...] = jnp.zeros_like(pim_ref)

        uu = u_ref[...]
        ub = uu.astype(BF16)
        lre, lim = lbr_ref[...], lbi_ref[...]
        xre = jnp.concatenate([jnp.dot(ub[:, k * cw:(k + 1) * cw], bre_ref[k], preferred_element_type=F32) for k in range(nk)], axis=1)
        xim = jnp.concatenate([jnp.dot(ub[:, k * cw:(k + 1) * cw], bim_ref[k], preferred_element_type=F32) for k in range(nk)], axis=1)
        last = t // 8 - 1
        sre, sim, pre_ref[...], pim_ref[...] = _cmul_scan(lre, lim, xre, xim, xr_ref, xi_ref, pre_ref[pl.ds(last, 1), :],
                                                          pim_ref[pl.ds(last, 1), :], False)
        sre_ref[...] = sre
        sim_ref[...] = sim
        sreb, simb = sre.astype(BF16), sim.astype(BF16)
        y = jnp.concatenate(
            [jnp.dot(sreb[:, k * sw:(k + 1) * sw], cre_ref[k], preferred_element_type=F32)
             + jnp.dot(simb[:, k * sw:(k + 1) * sw], cim_ref[k], preferred_element_type=F32) for k in range(nk)], axis=1)
        y = y + d_ref[...] * uu
        y_ref[...] = y
        z_ref[...] = _gelu(y).astype(z_ref.dtype)

    def whole(x):
        nd = x.ndim
        return pl.BlockSpec(x.shape, lambda i: (0,) * nd)

    consts = [lbr, lbi, bbd_re, bbd_im, cbd_re, cbd_imn, d_skip]
    row = lambda c: pl.BlockSpec((t, c), lambda i: (i, 0))
    body, x_in, x_out, x_shapes, x_scratch, x_args = carry(job, body, (l // t,), 1 + len(consts), 4)
    res = pl.pallas_call(
        body,
        name="s5_scan_fwd",
        grid=(l // t,),
        in_specs=[row(w)] + [whole(c) for c in consts] + x_in,
        out_specs=[row(n), row(n), row(w), row(w)] + x_out,
        out_shape=[jax.ShapeDtypeStruct((l, n), F32), jax.ShapeDtypeStruct((l, n), F32), jax.ShapeDtypeStruct((l, w), F32),
                   jax.ShapeDtypeStruct((l, w), BF16)] + x_shapes,
        scratch_shapes=[pltpu.VMEM((t // 8, n), F32), pltpu.VMEM((t // 8, n), F32), pltpu.VMEM((n // S5_LANES, t, S5_LANES), F32),
                        pltpu.VMEM((n // S5_LANES, t, S5_LANES), F32)] + x_scratch,
        compiler_params=_params(("arbitrary",)),
    )(u, *consts, *x_args)
    return res[:4], res[4:]


def s5_scan_bwd(dz, y, u, sre, sim, lbr, lbi, bbd_re, bbd_im, cbd_re, cbd_imn, d_skip, job=None):
    l, w = u.shape
    n = lbr.shape[1]
    nk, cw, sw = bbd_re.shape
    t = _tile(l, (S5_CHUNK_BWD, 64, 32, 16, 8))
    nc = l // t

    def body(dz_ref, y_ref, u_ref, sre_ref, sim_ref, hre_ref, him_ref, lbr_ref, lbi_ref, bre_ref, bim_ref, cre_ref, cim_ref,
             d_ref, du_ref, dlr_ref, dli_ref, dbre_ref, dbim_ref, dcre_ref, dcim_ref, dd_ref, nre_ref, nim_ref, dsr_ref, dsi_ref):
        i = pl.program_id(0)
        has_prev = (i < nc - 1).astype(F32)

        @pl.when(i == 0)
        def _():
            nre_ref[...] = jnp.zeros_like(nre_ref)
            nim_ref[...] = jnp.zeros_like(nim_ref)

        uu = u_ref[...]
        ub = uu.astype(BF16)
        lre, lim = lbr_ref[...], lbi_ref[...]
        _, gelu_vjp = jax.vjp(_gelu, y_ref[...])
        dy = gelu_vjp(dz_ref[...].astype(F32))[0]
        dyb = dy.astype(BF16)
        gre = jnp.concatenate([lax.dot_general(dyb[:, k * cw:(k + 1) * cw], cre_ref[k], _NT, preferred_element_type=F32)
                               for k in range(nk)], axis=1)
        gim = jnp.concatenate([lax.dot_general(dyb[:, k * cw:(k + 1) * cw], cim_ref[k], _NT, preferred_element_type=F32)
                               for k in range(nk)], axis=1)
        dsre, dsim, nre_ref[...], nim_ref[...] = _cmul_scan(lre, -lim, gre, gim, dsr_ref, dsi_ref, nre_ref[pl.ds(0, 1), :],
                                                            nim_ref[pl.ds(0, 1), :], True)
        dsreb, dsimb = dsre.astype(BF16), dsim.astype(BF16)
        s_re, s_im = sre_ref[...], sim_ref[...]
        du = jnp.concatenate(
            [lax.dot_general(dsreb[:, k * sw:(k + 1) * sw], bre_ref[k], _NT, preferred_element_type=F32)
             + lax.dot_general(dsimb[:, k * sw:(k + 1) * sw], bim_ref[k], _NT, preferred_element_type=F32) for k in range(nk)],
            axis=1)
        du_ref[...] = (du + d_ref[...] * dy).astype(du_ref.dtype)
        pre = _rows_before(s_re, hre_ref[...] * has_prev, 1)
        pim = _rows_before(s_im, him_ref[...] * has_prev, 1)
        sreb, simb = s_re.astype(BF16), s_im.astype(BF16)
        sums = [
            (dlr_ref, jnp.sum(dsre * pre + dsim * pim, axis=0, keepdims=True)),
            (dli_ref, jnp.sum(dsim * pre - dsre * pim, axis=0, keepdims=True)),
            (dbre_ref, jnp.stack([lax.dot_general(ub[:, k * cw:(k + 1) * cw], dsreb[:, k * sw:(k + 1) * sw], _TN,
                                                  preferred_element_type=F32) for k in range(nk)])),
            (dbim_ref, jnp.stack([lax.dot_general(ub[:, k * cw:(k + 1) * cw], dsimb[:, k * sw:(k + 1) * sw], _TN,
                                                  preferred_element_type=F32) for k in range(nk)])),
            (dcre_ref, jnp.stack([lax.dot_general(sreb[:, k * sw:(k + 1) * sw], dyb[:, k * cw:(k + 1) * cw], _TN,
                                                  preferred_element_type=F32) for k in range(nk)])),
            (dcim_ref, jnp.stack([lax.dot_general(simb[:, k * sw:(k + 1) * sw], dyb[:, k * cw:(k + 1) * cw], _TN,
                                                  preferred_element_type=F32) for k in range(nk)])),
            (dd_ref, jnp.sum(dy * uu, axis=0, keepdims=True)),
        ]

        @pl.when(i == 0)
        def _():
            for ref, val in sums:
                ref[...] = val

        @pl.when(i > 0)
        def _():
            for ref, val in sums:
                ref[...] += val

    def whole(shape):
        nd = len(shape)
        return pl.BlockSpec(tuple(shape), lambda i: (0,) * nd)

    consts = [lbr, lbi, bbd_re, bbd_im, cbd_re, cbd_imn, d_skip]
    t8 = t // 8
    rev = lambda i: nc - 1 - i
    halo = lambda i: jnp.maximum(rev(i) * t8 - 1, 0)
    row = lambda c: pl.BlockSpec((t, c), lambda i: (rev(i), 0))
    sum_shapes = [lbr.shape, lbi.shape, bbd_re.shape, bbd_im.shape, cbd_re.shape, cbd_imn.shape, d_skip.shape]
    body, x_in, x_out, x_shapes, x_scratch, x_args = carry(job, body, (nc,), 7 + len(consts), 8)
    res = pl.pallas_call(
        body,
        name="s5_scan_bwd",
        grid=(nc,),
        in_specs=[row(w), row(w), row(w), row(n), row(n), pl.BlockSpec((8, n), lambda i: (halo(i), 0)),
                  pl.BlockSpec((8, n), lambda i: (halo(i), 0))] + [whole(c.shape) for c in consts] + x_in,
        out_specs=[row(w)] + [whole(s) for s in sum_shapes] + x_out,
        out_shape=[jax.ShapeDtypeStruct((l, w), BF16)] + [jax.ShapeDtypeStruct(tuple(s), F32) for s in sum_shapes] + x_shapes,
        scratch_shapes=[pltpu.VMEM((t8, n), F32), pltpu.VMEM((t8, n), F32), pltpu.VMEM((n // S5_LANES, t, S5_LANES), F32),
                        pltpu.VMEM((n // S5_LANES, t, S5_LANES), F32)] + x_scratch,
        compiler_params=_params(("arbitrary",)),
    )(dz, y, u, sre, sim, sre, sim, *consts, *x_args)
    return res[0], res[1:8], res[8:]


def _glu(vg):
    w = vg.shape[1] // 2
    return vg[:, :w] * _sigmoid(vg[:, w:])


def s5_fwd(hn, h, p, job=None):
    cols = _s5_cols(p["lam_re"], p["lam_im"], p["log_dt"], p["b_re"], p["b_im"])
    gp, hh = cols[3].shape
    lbr, lbi, bbr, bbi = rowwise(_s5_discretise, list(cols), [], [(1, F32), (1, F32), (hh, F32), (hh, F32)],
                                 name="s5_discretise", tm=512)
    consts = (lbr.reshape(1, gp), lbi.reshape(1, gp), _slab_b(bbr).astype(BF16), _slab_b(bbi).astype(BF16),
              _slab_c(p["c_re"]).astype(BF16), _slab_c(-p["c_im"]).astype(BF16), p["d"])
    u = mm(hn, p["w_in"], name="s5_in")
    (sre, sim, y, z), carried = s5_scan_fwd(u, *consts, job=job)
    vg = mm(z, p["w_out"], name="s5_out", out_dtypes=(BF16,))
    h_new = rowwise(lambda a, r: r + _glu(a.astype(F32)), [vg, h], [], [(h.shape[1], F32)], name="s5_glu")
    return h_new, (hn, u, sre, sim, y, z, vg, cols, consts), carried


def s5_bwd(dh_new, saved, p, job=None):
    hn, u, sre, sim, y, z, vg, cols, consts = saved

    def glu_bwd(a, dm):
        _, vjp = jax.vjp(_glu, a.astype(F32))
        return vjp(dm)[0]

    dvg = rowwise(glu_bwd, [vg, dh_new], [], [(vg.shape[1], BF16)], name="s5_dglu")
    dw_out = mm(z, dvg, ta=True, name="s5_dwout")
    dz = mm(dvg, p["w_out"], tb=True, name="s5_dz", out_dtypes=(BF16,), tk=2048)
    du, (dlbr, dlbi, dbbr, dbbi, dcre, dcimn, dd), carried = s5_scan_bwd(dz, y, u, sre, sim, *consts, job=job)
    dw_in = mm(hn, du, ta=True, name="s5_dwin")
    gp = cols[0].shape[0]

    def disc_bwd(lr, li, ldt, bre, bim, g0, g1, g2, g3):
        _, vjp = jax.vjp(_s5_discretise, lr, li, ldt, bre, bim)
        return vjp((g0, g1, g2, g3))

    cot = (dlbr.reshape(gp, 1), dlbi.reshape(gp, 1), _unslab_b(dbbr), _unslab_b(dbbi))
    dlr, dli, dldt, dbre, dbim = rowwise(disc_bwd, list(cols + cot), [], [(c.shape[1], F32) for c in cols],
                                         name="s5_discretise_bwd", tm=512)
    g_, p_ = p["lam_re"].shape
    grads = dict(w_in=dw_in, lam_re=dlr.reshape(g_, p_), lam_im=dli.reshape(g_, p_), log_dt=dldt.reshape(g_, p_).sum(axis=1),
                 b_re=dbre.reshape(p["b_re"].shape), b_im=dbim.reshape(p["b_im"].shape), c_re=_unslab_c(dcre),
                 c_im=-_unslab_c(dcimn), d=dd, w_out=dw_out)
    return (du, p["w_in"]), grads, carried


MESH = pl.DeviceIdType.MESH
N_CHIPS = 4
N_DEVICES = 8


def _place():
    x, y, c = lax.axis_index("x"), lax.axis_index("y"), lax.axis_index("c")
    return x, y, c, [(1 - x, y), (x, 1 - y), (1 - x, 1 - y)]


def _hbm_call(body, name, ins, out_shapes, n_remote, n_local=0):
    hbm = pl.BlockSpec(memory_space=pltpu.HBM)
    scratch = [pltpu.SemaphoreType.DMA((n_remote,)), pltpu.SemaphoreType.DMA((n_remote,))]
    if n_local:
        scratch.append(pltpu.SemaphoreType.DMA((n_local,)))
    return pl.pallas_call(
        body,
        name=name,
        in_specs=[hbm] * len(ins),
        out_specs=[hbm] * len(out_shapes),
        out_shape=out_shapes,
        scratch_shapes=scratch,
    )(*ins)


def _split_dim(shape):
    return next(d for d, s in enumerate(shape) if s >= 2 and s % 2 == 0)


class Exchange:
    def __init__(self, ins, out_shapes, n_remote, n_local, start, finish):
        self.ins, self.out_shapes, self.start, self.finish = list(ins), list(out_shapes), start, finish
        self.scratch = [pltpu.SemaphoreType.DMA((n_remote,)), pltpu.SemaphoreType.DMA((n_remote,)),
                        pltpu.SemaphoreType.DMA((max(n_local, 1),))]


def both(first, second):
    n1, m1, s1 = len(first.ins), len(first.out_shapes), len(first.scratch)
    job = Exchange(first.ins + second.ins, first.out_shapes + second.out_shapes, 1, 1,
                   lambda ins, outs, *sems: (first.start(ins[:n1], outs[:m1], *sems[:s1]),
                                             second.start(ins[n1:], outs[m1:], *sems[s1:])),
                   lambda ins, outs, *sems: (first.finish(ins[:n1], outs[:m1], *sems[:s1]),
                                             second.finish(ins[n1:], outs[m1:], *sems[s1:])))
    job.scratch = first.scratch + second.scratch
    return job


def run_exchange(job, name):
    n_in, n_out = len(job.ins), len(job.out_shapes)

    def body(*refs):
        ins, outs, sems = refs[:n_in], refs[n_in:n_in + n_out], refs[n_in + n_out:]
        job.start(ins, outs, *sems)
        job.finish(ins, outs, *sems)

    hbm = pl.BlockSpec(memory_space=pltpu.HBM)
    return pl.pallas_call(body, name=name, in_specs=[hbm] * n_in, out_specs=[hbm] * n_out, out_shape=job.out_shapes,
                          scratch_shapes=job.scratch)(*job.ins)


def carry(job, body, grid, n_in, n_out):
    if job is None:
        return body, [], [], [], [], []
    nji, njo = len(job.ins), len(job.out_shapes)

    def carrying(*refs):
        ins, jins = refs[:n_in], refs[n_in:n_in + nji]
        outs, jouts = refs[n_in + nji:n_in + nji + n_out], refs[n_in + nji + n_out:n_in + nji + n_out + njo]
        rest = refs[n_in + nji + n_out + njo:]
        n_sems = len(job.scratch)
        own, sems = rest[:len(rest) - n_sems], rest[len(rest) - n_sems:]
        ids = [pl.program_id(ax) for ax in range(len(grid))]
        first = functools.reduce(jnp.logical_and, [i == 0 for i in ids])
        last = functools.reduce(jnp.logical_and, [i == g - 1 for i, g in zip(ids, grid)])

        @pl.when(first)
        def _():
            job.start(jins, jouts, *sems)

        body(*ins, *outs, *own)

        @pl.when(last)
        def _():
            job.finish(jins, jouts, *sems)

    hbm = pl.BlockSpec(memory_space=pltpu.HBM)
    return carrying, [hbm] * nji, [hbm] * njo, job.out_shapes, job.scratch, job.ins


def gather_chips(shards):
    n = len(shards)
    cuts = [_split_dim(s.shape) for s in shards]

    def parts(ins, outs, send, recv):
        x, y, c, chips = _place()

        def half(ref, t, which, lead=()):
            size = shards[t].shape[cuts[t]] // 2
            return ref.at[lead + (slice(None),) * cuts[t] + (pl.ds(which * size, size),)]

        def copy(t, k, block, which, to, src=None):
            dst = half(outs[t], t, which, (block,))
            return pltpu.make_async_remote_copy(dst if src is None else src, dst, send.at[6 * t + k], recv.at[6 * t + k],
                                                device_id=to, device_id_type=MESH)

        me = 2 * x + y
        sends = [copy(t, k, me, c, (px, py, c), src=half(ins[t], t, c)) for t in range(n) for k, (px, py) in enumerate(chips)]
        return x, y, c, chips, me, copy, sends

    def start(ins, outs, send, recv, local):
        _, _, _, _, me, _, sends = parts(ins, outs, send, recv)
        for t in range(n):
            pltpu.make_async_copy(ins[t], outs[t].at[me], local.at[t]).start()
        for cp in sends:
            cp.start()

    def finish(ins, outs, send, recv, local):
        x, y, c, chips, me, copy, sends = parts(ins, outs, send, recv)
        passed = []
        for t in range(n):
            for k, (px, py) in enumerate(chips):
                copy(t, k, 2 * px + py, c, (px, py, c)).wait_recv()
                on = copy(t, 3 + k, 2 * px + py, c, (x, y, 1 - c))
                on.start()
                passed.append(on)
        for t in range(n):
            for k, (px, py) in enumerate(chips):
                copy(t, 3 + k, 2 * px + py, 1 - c, (x, y, 1 - c)).wait_recv()
        for cp in sends + passed:
            cp.wait_send()
        for t in range(n):
            pltpu.make_async_copy(ins[t], outs[t].at[me], local.at[t]).wait()

    return Exchange(shards, [jax.ShapeDtypeStruct((N_CHIPS,) + s.shape, s.dtype) for s in shards], 6 * n, n, start, finish)


def scatter_chips(blocked):
    n = len(blocked)

    def copies(ins, outs, send, recv):
        x, y, c, chips = _place()
        return [pltpu.make_async_remote_copy(ins[t].at[2 * px + py], outs[t].at[k], send.at[3 * t + k], recv.at[3 * t + k],
                                             device_id=(px, py, c), device_id_type=MESH)
                for t in range(n) for k, (px, py) in enumerate(chips)]

    def start(ins, outs, send, recv, local):
        for cp in copies(ins, outs, send, recv):
            cp.start()

    def finish(ins, outs, send, recv, local):
        for cp in copies(ins, outs, send, recv):
            cp.wait()

    return Exchange(blocked, [jax.ShapeDtypeStruct((3,) + b.shape[1:], b.dtype) for b in blocked], 3 * n, 0, start, finish)


def swap_cores(arrays):
    n = len(arrays)

    def body(*refs):
        ins, outs = refs[:n], refs[n:2 * n]
        send, recv = refs[2 * n:]
        x, y, c, _ = _place()
        pending = []
        for t in range(n):
            cp = pltpu.make_async_remote_copy(ins[t], outs[t], send.at[t], recv.at[t], device_id=(x, y, 1 - c),
                                              device_id_type=MESH)
            cp.start()
            pending.append(cp)
        for cp in pending:
            cp.wait()

    return _hbm_call(body, "swap_cores", arrays, [jax.ShapeDtypeStruct(a.shape, a.dtype) for a in arrays], n)


def gather_devices(buf):
    def copies(in_ref, out_ref, send, recv, local):
        x, y, c, _ = _place()
        me = 4 * x + 2 * y + c
        going, landing = [], []
        for k in range(1, N_DEVICES):
            peer = (x ^ ((k >> 2) & 1), y ^ ((k >> 1) & 1), c ^ (k & 1))
            going.append(pltpu.make_async_remote_copy(in_ref, out_ref.at[me], send.at[k - 1], recv.at[k - 1],
                                                      device_id=peer, device_id_type=MESH))
            landing.append(pltpu.make_async_remote_copy(in_ref, out_ref.at[4 * peer[0] + 2 * peer[1] + peer[2]], send.at[k - 1],
                                                        recv.at[k - 1], device_id=peer, device_id_type=MESH))
        return pltpu.make_async_copy(in_ref, out_ref.at[me], local.at[0]), going, landing

    def start(ins, outs, send, recv, local):
        own, going, _ = copies(ins[0], outs[0], send, recv, local)
        own.start()
        for cp in going:
            cp.start()

    def finish(ins, outs, send, recv, local):
        own, _, landing = copies(ins[0], outs[0], send, recv, local)
        own.wait()
        for cp in landing:
            cp.wait()

    return Exchange([buf], [jax.ShapeDtypeStruct((N_DEVICES,) + buf.shape, buf.dtype)], N_DEVICES - 1, 1, start, finish)


def _adamw(w, g, m, v):
    m = ADAM_B1 * m + (1.0 - ADAM_B1) * g
    v = ADAM_B2 * v + (1.0 - ADAM_B2) * (g * g)
    m_hat = m / (1.0 - ADAM_B1 ** ADAM_STEP)
    v_hat = v / (1.0 - ADAM_B2 ** ADAM_STEP)
    return -ADAM_LR * (m_hat / (jnp.sqrt(v_hat) + ADAM_EPS) + ADAM_WD * w), m, v


def _rows2d(a):
    return a.reshape(-1, a.shape[-1])


WEIGHTS = ["ffn1_norm", "ffn1_w_in", "ffn1_w_out", "mix_norm", "ffn2_norm", "ffn2_w_in", "ffn2_w_out", "final_norm",
           "s5_w_in", "s5_lam_re", "s5_lam_im", "s5_log_dt", "s5_b_re", "s5_b_im", "s5_c_re", "s5_c_im", "s5_d", "s5_w_out",
           "sb_w_qkv", "sb_w_out", "lru_w_in", "lru_conv_w", "lru_conv_b", "lru_w_a", "lru_b_a", "lru_w_x", "lru_b_x",
           "lru_lambda", "lru_w_out"]
INPUTS = ["x"] + WEIGHTS + ["loss_target"] + ["m_" + n for n in WEIGHTS] + ["v_" + n for n in WEIGHTS]
SHARDED_BIG = dict(ffn1_w_in=2, ffn1_w_out=1, ffn2_w_in=2, ffn2_w_out=1, s5_w_in=1, s5_w_out=2, sb_w_qkv=2, sb_w_out=1,
                   lru_w_in=2, lru_w_a=2, lru_w_x=2, lru_w_out=1)
SHARDED_SMALL = dict(s5_d=1, lru_conv_w=2, lru_conv_b=1, lru_b_a=2, lru_b_x=2, lru_lambda=1)
REPLICATED = [n for n in WEIGHTS if n not in SHARDED_BIG and n not in SHARDED_SMALL]
PACK_LANES = 128
PACK_ROW_ALIGN = 16
REPLICATED_ROW_TILE = 256


def _unblock(g, d):
    full = jnp.moveaxis(g, 0, d)
    return full.reshape(full.shape[:d] + (full.shape[d] * full.shape[d + 1],) + full.shape[d + 2:])


def _block(full, d):
    s = full.shape[d] // N_CHIPS
    return jnp.moveaxis(full.reshape(full.shape[:d] + (N_CHIPS, s) + full.shape[d + 1:]), d, 0)


def _pack(arrays, lead=(), row_align=PACK_ROW_ALIGN):
    nl = len(lead)
    flat = jnp.concatenate([a.reshape(lead + (-1,)) for a in arrays], axis=nl)
    quantum = PACK_LANES * row_align
    pad = (-flat.shape[nl]) % quantum
    flat = jnp.pad(flat, [(0, 0)] * nl + [(0, pad)])
    return flat.reshape(lead + (-1, PACK_LANES))


def _unpack(packed, shapes, lead=()):
    nl = len(lead)
    flat = packed.reshape(lead + (-1,))
    out, off = [], 0
    for s in shapes:
        size = math.prod(s)
        out.append(lax.slice_in_dim(flat, off, off + size, axis=nl).reshape(lead + tuple(s)))
        off += size
    return out


N_GROUPS = 3
MIXER_KIND = dict(s5=0, sb=1, lru=2)
S5_NAMES = dict(w_in="s5_w_in", lam_re="s5_lam_re", lam_im="s5_lam_im", log_dt="s5_log_dt", b_re="s5_b_re", b_im="s5_b_im",
                c_re="s5_c_re", c_im="s5_c_im", d="s5_d", w_out="s5_w_out")
SB_NAMES = dict(w_qkv="sb_w_qkv", w_out="sb_w_out")
LRU_NAMES = dict(w_in="lru_w_in", conv_w="lru_conv_w", conv_b="lru_conv_b", wa="lru_w_a", ba="lru_b_a", wx="lru_w_x",
                 bx="lru_b_x", lam="lru_lambda", w_out="lru_w_out")


def _pieces_of(name, count):
    kind = None if name.startswith("ffn") else MIXER_KIND[name.split("_")[0]]
    groups = [min(i if kind is None else kind + N_MIXERS * i, N_GROUPS - 1) for i in range(count)]
    runs, lo = [], 0
    for i in range(1, count + 1):
        if i == count or groups[i] != groups[lo]:
            runs.append((lo, i, groups[lo]))
            lo = i
    return runs


class _Sharded:
    def __init__(self, a):
        self.a = a
        self.pieces = {n: _pieces_of(n, a[n].shape[0]) for n in SHARDED_BIG}
        by_group = [[(n, lo, hi) for n in SHARDED_BIG for lo, hi, g in self.pieces[n] if g == grp] for grp in range(N_GROUPS)]
        first = lambda n: n.startswith("ffn1")
        self.by_group = {0: [k for k in by_group[0] if first(k[0])], 1: by_group[1], 2: by_group[2],
                         "0b": [k for k in by_group[0] if not first(k[0])],
                         "0b1": [k for k in by_group[0] if k[0] == "ffn2_w_in"],
                         "0b2": [k for k in by_group[0] if not first(k[0]) and k[0] != "ffn2_w_in"]}
        self.weights = {}
        self.small = {}
        self.grads = {n: [None] * a[n].shape[0] for n in SHARDED_BIG}
        self.small_grads = {}
        self.received = {}
        self.small_received = None

    def gather_job(self, grp):
        arrays = [self.a[n][lo:hi].astype(BF16) for n, lo, hi in self.by_group[grp]]
        if grp == 0:
            arrays.append(_pack([self.a[n] for n in SHARDED_SMALL]))
        return gather_chips(arrays)

    def landed(self, grp, gathered):
        for (n, lo, _), g in zip(self.by_group[grp], gathered):
            self.weights[(n, lo)] = _unblock(g, SHARDED_BIG[n])
        if grp == 0:
            blocks = _unpack(gathered[-1], [self.a[n].shape for n in SHARDED_SMALL], lead=(N_CHIPS,))
            self.small = {n: _unblock(b, d) for (n, d), b in zip(SHARDED_SMALL.items(), blocks)}

    def weight(self, name, idx):
        lo = next(lo for lo, hi, _ in self.pieces[name] if lo <= idx < hi)
        return self.weights[(name, lo)][idx - lo]

    def piece_grad(self, n, lo, hi):
        return jnp.stack(self.grads[n][lo:hi])

    def scatter_job(self, grp):
        arrays = [_block(self.piece_grad(n, lo, hi), SHARDED_BIG[n]).astype(BF16) for n, lo, hi in self.by_group[grp]]
        if grp == 0:
            arrays.append(_pack([_block(self.small_grads[n], d) for n, d in SHARDED_SMALL.items()], lead=(N_CHIPS,)))
        return scatter_chips(arrays)

    def arrived(self, grp, received):
        for (n, lo, _), r in zip(self.by_group[grp], received):
            self.received[(n, lo)] = r
        if grp == 0:
            self.small_received = received[-1]


def _forward_backward(x, target, a, sh):
    depth = a["ffn1_norm"].shape[0]

    def mixer_params(layer):
        kind, j = layer % N_MIXERS, layer // N_MIXERS
        if kind == 0:
            return kind, j, dict(w_in=sh.weight("s5_w_in", j), lam_re=a["s5_lam_re"][j], lam_im=a["s5_lam_im"][j],
                                 log_dt=a["s5_log_dt"][j], b_re=a["s5_b_re"][j], b_im=a["s5_b_im"][j], c_re=a["s5_c_re"][j],
                                 c_im=a["s5_c_im"][j], d=sh.small["s5_d"][j].reshape(1, -1), w_out=sh.weight("s5_w_out", j))
        if kind == 1:
            return kind, j, dict(w_qkv=sh.weight("sb_w_qkv", j), w_out=sh.weight("sb_w_out", j))
        sm = sh.small
        return kind, j, dict(w_in=sh.weight("lru_w_in", j), conv_w=sm["lru_conv_w"][j], conv_b=sm["lru_conv_b"][j].reshape(1, -1),
                             wa=sh.weight("lru_w_a", j), ba=sm["lru_b_a"][j].reshape(1, -1), wx=sh.weight("lru_w_x", j),
                             bx=sm["lru_b_x"][j].reshape(1, -1), lam=sm["lru_lambda"][j].reshape(1, -1),
                             w_out=sh.weight("lru_w_out", j))

    def ffn_weights(which, layer):
        return sh.weight(f"{which}_w_in", layer), sh.weight(f"{which}_w_out", layer)

    sh.landed(0, run_exchange(sh.gather_job(0), "gather_chips"))
    h = x
    tape = []
    for layer in range(depth):
        job = sh.gather_job("0b") if layer == 0 else None
        h, s1, got = ffn_fwd(h, a["ffn1_norm"][layer], *ffn_weights("ffn1", layer), "ffn", job)
        if job is not None:
            sh.landed("0b", got)
        kind, j, p = mixer_params(layer)
        h_mix_in = h
        hn = rms_fwd(h, a["mix_norm"][layer], "mix_norm")
        job = sh.gather_job(layer + 1) if layer + 1 < N_GROUPS else None
        if kind == 0:
            h, sm, got = s5_fwd(hn, h, p, job)
        elif kind == 1:
            o_flat, sm, got = sb_fwd(hn, p, job)
            h = mm(o_flat, p["w_out"], name="mix_out", extras=(h,), epilogue=lambda acc, res: res + acc)
        else:
            assert job is None
            y, sm = lru_fwd(hn, p)
            h = mm(y, p["w_out"], name="mix_out", extras=(h,), epilogue=lambda acc, res: res + acc)
        if job is not None:
            sh.landed(layer + 1, got)
        h, s2, _ = ffn_fwd(h, a["ffn2_norm"][layer], *ffn_weights("ffn2", layer), "ffn")
        tape.append((s1, h_mix_in, sm, s2))

    loss, dh, g_final = loss_fwd_bwd(h, a["final_norm"], target)

    norm_grads = {n: [None] * depth for n in ("ffn1_norm", "mix_norm", "ffn2_norm")}
    mix = {}
    for layer in reversed(range(depth)):
        s1, h_mix_in, sm, s2 = tape[layer]
        dh, dg, dwi, dwo, _, _ = ffn_bwd(dh, s2, a["ffn2_norm"][layer], *ffn_weights("ffn2", layer), "ffn")
        norm_grads["ffn2_norm"][layer], sh.grads["ffn2_w_in"][layer], sh.grads["ffn2_w_out"][layer] = dg, dwi, dwo
        kind, j, p = mixer_params(layer)
        job = sh.scatter_job(layer + 1) if layer + 1 < N_GROUPS else None
        if kind == 0:
            (da, w_first), g, got = s5_bwd(dh, sm, p, job)
            names = S5_NAMES
        elif kind == 1:
            (da, w_first), g, got = sb_bwd(dh, sm, p, job)
            names = SB_NAMES
        else:
            assert job is None
            (da, w_first), g = lru_bwd(dh, sm, p)
            names = LRU_NAMES
        if job is not None:
            sh.arrived(layer + 1, got)
        for k, full_name in names.items():
            if full_name in SHARDED_BIG:
                sh.grads[full_name][j] = g[k].reshape(sh.weight(full_name, j).shape)
            else:
                mix.setdefault(full_name, {})[j] = g[k].reshape(a[full_name].shape[1:-1] + (-1,))
        dh, dg = norm_input_bwd(da, _rows_spec, 1, w_first, h_mix_in, a["mix_norm"][layer], dh, "mix_dhn")
        norm_grads["mix_norm"][layer] = dg
        jobs = (sh.scatter_job("0b1"), sh.scatter_job("0b2")) if layer == 0 else (None, None)
        dh, dg, dwi, dwo, got1, got2 = ffn_bwd(dh, s1, a["ffn1_norm"][layer], *ffn_weights("ffn1", layer), "ffn", jobs)
        if layer == 0:
            sh.arrived("0b1", got1)
            sh.arrived("0b2", got2)
        norm_grads["ffn1_norm"][layer], sh.grads["ffn1_w_in"][layer], sh.grads["ffn1_w_out"][layer] = dg, dwi, dwo

    grads = {n: jnp.stack(v) for n, v in norm_grads.items()}
    grads["final_norm"] = g_final
    for n, by_j in mix.items():
        stacked = jnp.stack([by_j[j] for j in range(len(by_j))])
        if n in SHARDED_SMALL:
            sh.small_grads[n] = stacked
        else:
            grads[n] = stacked
    rep_job = gather_devices(_pack([grads[n] for n in REPLICATED], row_align=REPLICATED_ROW_TILE))
    *got, rep_all = run_exchange(both(sh.scatter_job(0), rep_job), "scatter_chips")
    sh.arrived(0, got)
    return loss, dh, grads, rep_all


def kernel(x, ffn1_norm, ffn1_w_in, ffn1_w_out, mix_norm, ffn2_norm, ffn2_w_in, ffn2_w_out, final_norm, s5_w_in,
           s5_lam_re, s5_lam_im, s5_log_dt, s5_b_re, s5_b_im, s5_c_re, s5_c_im, s5_d, s5_w_out, sb_w_qkv,
           sb_w_out, lru_w_in, lru_conv_w, lru_conv_b, lru_w_a, lru_b_a, lru_w_x, lru_b_x, lru_lambda,
           lru_w_out, loss_target, m_ffn1_norm, m_ffn1_w_in, m_ffn1_w_out, m_mix_norm, m_ffn2_norm,
           m_ffn2_w_in, m_ffn2_w_out, m_final_norm, m_s5_w_in, m_s5_lam_re, m_s5_lam_im, m_s5_log_dt,
           m_s5_b_re, m_s5_b_im, m_s5_c_re, m_s5_c_im, m_s5_d, m_s5_w_out, m_sb_w_qkv, m_sb_w_out, m_lru_w_in,
           m_lru_conv_w, m_lru_conv_b, m_lru_w_a, m_lru_b_a, m_lru_w_x, m_lru_b_x, m_lru_lambda, m_lru_w_out,
           v_ffn1_norm, v_ffn1_w_in, v_ffn1_w_out, v_mix_norm, v_ffn2_norm, v_ffn2_w_in, v_ffn2_w_out,
           v_final_norm, v_s5_w_in, v_s5_lam_re, v_s5_lam_im, v_s5_log_dt, v_s5_b_re, v_s5_b_im, v_s5_c_re,
           v_s5_c_im, v_s5_d, v_s5_w_out, v_sb_w_qkv, v_sb_w_out, v_lru_w_in, v_lru_conv_w, v_lru_conv_b,
           v_lru_w_a, v_lru_b_a, v_lru_w_x, v_lru_b_x, v_lru_lambda, v_lru_w_out):
    a = dict(locals())
    assert list(a) == INPUTS
    x, y, c, _ = _place()
    chip = 2 * x + y
    everyone = ("x", "y", "c")

    sh = _Sharded(a)
    loss, dx, grads, rep_all = _forward_backward(a["x"][0], a["loss_target"][0], a, sh)
    loss = lax.psum(loss, everyone)

    small = list(SHARDED_SMALL)

    def sum_chips(mine, got):
        rows = _rows2d(mine)
        return rowwise(lambda o, r: ((o + r[0].astype(F32)) + r[1].astype(F32)) + r[2].astype(F32),
                       [rows, got.reshape((3,) + rows.shape)], [], [(rows.shape[1], F32)], name="sum_chips")

    def own_block(full, name, d):
        return lax.dynamic_slice_in_dim(full, chip * a[name].shape[d], a[name].shape[d], axis=d)

    keys = [(n, lo, hi) for n in SHARDED_BIG for lo, hi, _ in sh.pieces[n]]
    partial = [sum_chips(own_block(sh.piece_grad(n, lo, hi), n, SHARDED_BIG[n]), sh.received[(n, lo)]) for n, lo, hi in keys]
    partial.append(sum_chips(_pack([own_block(sh.small_grads[n], n, d) for n, d in SHARDED_SMALL.items()]), sh.small_received))
    other = swap_cores(partial)

    def adam_sharded(wv, ga, gb, m, v):
        shape = wv.shape
        res = rowwise(lambda w_, a_, b_, mm_, vv_: (a_ + b_,) + _adamw(w_, a_ + b_, mm_, vv_),
                      [_rows2d(wv), ga, gb, _rows2d(m), _rows2d(v)], [], [(shape[-1], F32)] * 4, name="adamw_sharded")
        return [r.reshape(shape) for r in res]

    out_grad, out_delta, out_m, out_v = {}, {}, {}, {}
    for n in SHARDED_BIG:
        at = [i for i, k in enumerate(keys) if k[0] == n]
        out_grad[n], out_delta[n], out_m[n], out_v[n] = adam_sharded(a[n], [partial[i] for i in at], [other[i] for i in at],
                                                                     a["m_" + n], a["v_" + n])
    small_shapes = [a[n].shape for n in small]
    sp = [_pack([a[pre + n] for n in small]) for pre in ("", "m_", "v_")]
    g_, d_, m_, v_ = adam_sharded(sp[0], [partial[-1]], [other[-1]], sp[1], sp[2])
    for n, gg, dd, mm_, vv in zip(small, _unpack(g_, small_shapes), _unpack(d_, small_shapes), _unpack(m_, small_shapes),
                                  _unpack(v_, small_shapes)):
        out_grad[n], out_delta[n], out_m[n], out_v[n] = gg, dd, mm_, vv

    rep_shapes = [a[n].shape for n in REPLICATED]
    rp = [_pack([a[pre + n] for n in REPLICATED], row_align=REPLICATED_ROW_TILE) for pre in ("", "m_", "v_")]

    def adam_rep(w_, g8, mm_, vv_):
        g = g8[0]
        for k in range(1, N_DEVICES):
            g = g + g8[k]
        return (g,) + _adamw(w_, g, mm_, vv_)

    g_, d_, m_, v_ = rowwise(adam_rep, [rp[0], rep_all, rp[1], rp[2]], [], [(PACK_LANES, F32)] * 4, name="adamw_replicated")
    for n, gg, dd, mm_, vv in zip(REPLICATED, _unpack(g_, rep_shapes), _unpack(d_, rep_shapes), _unpack(m_, rep_shapes),
                                  _unpack(v_, rep_shapes)):
        out_grad[n], out_delta[n], out_m[n], out_v[n] = gg, dd, mm_, vv

    return (loss, dx[None], *[out_grad[n] for n in WEIGHTS], *[out_delta[n] for n in WEIGHTS], *[out_m[n] for n in WEIGHTS],
            *[out_v[n] for n in WEIGHTS])
```

```python
import functools
import math

import jax
import jax.numpy as jnp
from jax import lax
from jax.experimental import pallas as pl
from jax.experimental.pallas import tpu as pltpu

F32 = jnp.float32
BF16 = jnp.bfloat16

VMEM_LIMIT_BYTES = 56 * 1024 * 1024

RMS_EPS = 1e-6
D_FF = 2816
S5_GROUP = 16
S5_STATE = 64
SB_HEAD_DIM = 64
LRU_BLOCK_WIDTH = 256
LRU_CONV = 4
LRU_C = 8.0
N_MIXERS = 3

ADAM_LR = 0.001
ADAM_B1 = 0.9
ADAM_B2 = 0.999
ADAM_EPS = 1e-08
ADAM_WD = 0.01
ADAM_STEP = 10


def _params(semantics):
    return pltpu.CompilerParams(dimension_semantics=semantics, vmem_limit_bytes=VMEM_LIMIT_BYTES)


def _tile(dim, prefs):
    for t in prefs:
        if t <= dim and dim % t == 0:
            return t
    return dim


def _mm_call(name, grid, a, a_spec, b, b_spec, dims, extras, outs, epilogue, acc_shape, job=None):
    n_extra, n_out, n_k = len(extras), len(outs), grid[-1]

    def body(a_ref, b_ref, *rest):
        extra_refs = rest[:n_extra]
        out_refs = rest[n_extra:n_extra + n_out]
        acc_ref = rest[n_extra + n_out]
        k = pl.program_id(len(grid) - 1)
        part = lax.dot_general(a_ref[...].astype(BF16), b_ref[...].astype(BF16), dims, preferred_element_type=F32)

        @pl.when(k == 0)
        def _():
            acc_ref[...] = part

        @pl.when(k > 0)
        def _():
            acc_ref[...] += part

        @pl.when(k == n_k - 1)
        def _():
            res = epilogue(acc_ref[...], *[r[...] for r in extra_refs])
            if not isinstance(res, (tuple, list)):
                res = (res,)
            for o_ref, r in zip(out_refs, res):
                o_ref[...] = r.astype(o_ref.dtype)

    sem = ("arbitrary",) * len(grid) if job is not None else ("parallel",) * (len(grid) - 1) + ("arbitrary",)
    body, x_in, x_out, x_shapes, x_scratch, x_args = carry(job, body, grid, 2 + n_extra, n_out)
    res = pl.pallas_call(
        body,
        name=name,
        grid=grid,
        in_specs=[a_spec, b_spec] + [s for _, s in extras] + x_in,
        out_specs=[s for _, s in outs] + x_out,
        out_shape=[s for s, _ in outs] + x_shapes,
        scratch_shapes=[pltpu.VMEM(acc_shape, F32)] + x_scratch,
        compiler_params=_params(sem),
    )(a, b, *[x for x, _ in extras], *x_args)
    return res


def _fit(dim, target, align=128):
    best = None
    for t in range(align, min(dim, target) + 1, align):
        if dim % t == 0:
            best = t
    return best or dim


def mm(a, b, *, name, ta=False, tb=False, extras=(), epilogue=None, out_dtypes=(F32,), tm=1024, tn=1024, tk=1024):
    m, kdim = (a.shape[1], a.shape[0]) if ta else a.shape
    n = b.shape[0] if tb else b.shape[1]
    tm = _fit(m, tm)
    tn = _fit(n, tn)
    tk = _fit(kdim, tk)
    grid = (m // tm, n // tn, kdim // tk)
    a_spec = pl.BlockSpec((tk, tm), lambda i, j, k: (k, i)) if ta else pl.BlockSpec((tm, tk), lambda i, j, k: (i, k))
    b_spec = pl.BlockSpec((tn, tk), lambda i, j, k: (j, k)) if tb else pl.BlockSpec((tk, tn), lambda i, j, k: (k, j))
    dims = (((0 if ta else 1,), (1 if tb else 0,)), ((), ()))
    o_spec = pl.BlockSpec((tm, tn), lambda i, j, k: (i, j))
    if epilogue is None:
        epilogue = lambda acc: acc
    res = _mm_call(name, grid, a, a_spec, b, b_spec, dims, [(x, o_spec) for x in extras],
                   [(jax.ShapeDtypeStruct((m, n), dt), o_spec) for dt in out_dtypes], epilogue, (tm, tn))
    return res[0] if len(res) == 1 else res


def rowwise(fn, rows, consts, outs, sums=(), *, name, tm=256):
    m = rows[0].shape[0]
    pieces = [x if isinstance(x, (list, tuple)) else [x] for x in rows]
    tm = _tile(math.gcd(*[p.shape[-2] for ps in pieces for p in ps]), (tm, 128, 64, 32, 16, 8))
    starts = [[sum(q.shape[-2] for q in ps[:k]) // tm for k in range(len(ps) + 1)] for ps in pieces]
    flat = [p for ps in pieces for p in ps]
    n_rows, n_consts, n_outs = len(flat), len(consts), len(outs)

    def body(*refs):
        i = pl.program_id(0)
        in_vals, at = [], 0
        for ps, st in zip(pieces, starts):
            val = refs[at + len(ps) - 1][...]
            for k in reversed(range(len(ps) - 1)):
                val = jnp.where(i < st[k + 1], refs[at + k][...], val)
            in_vals.append(val)
            at += len(ps)
        in_vals += [r[...] for r in refs[n_rows:n_rows + n_consts]]
        out_refs = refs[n_rows + n_consts:n_rows + n_consts + n_outs]
        sum_refs = refs[n_rows + n_consts + n_outs:]
        res = fn(*in_vals)
        if not isinstance(res, (tuple, list)):
            res = (res,)
        for o_ref, r in zip(out_refs, res[:n_outs]):
            o_ref[...] = r.astype(o_ref.dtype)
        if sum_refs:
            first = pl.program_id(0) == 0

            @pl.when(first)
            def _():
                for s_ref, r in zip(sum_refs, res[n_outs:]):
                    s_ref[...] = r.astype(F32)

            @pl.when(jnp.logical_not(first))
            def _():
                for s_ref, r in zip(sum_refs, res[n_outs:]):
                    s_ref[...] += r.astype(F32)

    def whole(shape):
        nd = len(shape)
        return pl.BlockSpec(shape, lambda i: (0,) * nd)

    def row_spec(x, lo, hi):
        at = lambda i: jnp.clip(i - lo, 0, hi - lo - 1)
        if x.ndim == 3:
            return pl.BlockSpec((x.shape[0], tm, x.shape[2]), lambda i: (0, at(i), 0))
        return pl.BlockSpec((tm, x.shape[1]), lambda i: (at(i), 0))

    in_specs = [row_spec(p, st[k], st[k + 1]) for ps, st in zip(pieces, starts) for k, p in enumerate(ps)]
    in_specs += [whole(c.shape) for c in consts]
    out_specs = [pl.BlockSpec((tm, nc), lambda i: (i, 0)) for nc, _ in outs] + [whole(tuple(s)) for s in sums]
    out_shape = [jax.ShapeDtypeStruct((m, nc), dt) for nc, dt in outs] + [jax.ShapeDtypeStruct(tuple(s), F32) for s in sums]
    res = pl.pallas_call(
        body,
        name=name,
        grid=(m // tm,),
        in_specs=in_specs,
        out_specs=out_specs,
        out_shape=out_shape,
        compiler_params=_params(("arbitrary",) if sums else ("parallel",)),
    )(*flat, *consts)
    return res[0] if len(res) == 1 else res


def _rms(h, g):
    return h * lax.rsqrt(jnp.mean(h * h, axis=-1, keepdims=True) + RMS_EPS) * g


def _sigmoid(x):
    return 0.5 * jnp.tanh(0.5 * x) + 0.5


def _silu_mul(g, u):
    return g * _sigmoid(g) * u


def _gelu(x):
    return 0.5 * x * (1.0 + jnp.tanh(math.sqrt(2.0 / math.pi) * (x + 0.044715 * (x * x * x))))


def _softplus(x):
    return jnp.maximum(x, 0.0) + jnp.log(1.0 + jnp.exp(-jnp.abs(x)))


def rms_fwd(h, g, name):
    return rowwise(lambda x, gg: _rms(x, gg), [h], [g.reshape(1, -1)], [(h.shape[1], BF16)], name=name)


def norm_input_bwd(a, a_spec, n_k, w, h, g, dres, name, tm=512):
    m, d = h.shape
    tk = w.shape[1] // n_k
    tm = _fit(m, tm)

    def body(a_ref, w_ref, h_ref, g_ref, dres_ref, dh_ref, dg_ref, acc_ref):
        i, k = pl.program_id(0), pl.program_id(1)
        part = lax.dot_general(a_ref[...].astype(BF16), w_ref[...], _NT, preferred_element_type=F32)

        @pl.when(k == 0)
        def _():
            acc_ref[...] = part

        @pl.when(k > 0)
        def _():
            acc_ref[...] += part

        @pl.when(k == n_k - 1)
        def _():
            _, vjp = jax.vjp(_rms, h_ref[...], g_ref[...])
            dx, dg = vjp(acc_ref[...])
            dh_ref[...] = dres_ref[...] + dx

            @pl.when(i == 0)
            def _():
                dg_ref[...] = dg

            @pl.when(i > 0)
            def _():
                dg_ref[...] += dg

    row = pl.BlockSpec((tm, d), lambda i, k: (i, 0))
    vec = pl.BlockSpec((1, d), lambda i, k: (0, 0))
    dh, dg = pl.pallas_call(
        body,
        name=name,
        grid=(m // tm, n_k),
        in_specs=[a_spec(tm, tk), pl.BlockSpec((d, tk), lambda i, k: (0, k)), row, vec, row],
        out_specs=[row, vec],
        out_shape=[jax.ShapeDtypeStruct((m, d), F32), jax.ShapeDtypeStruct((1, d), F32)],
        scratch_shapes=[pltpu.VMEM((tm, d), F32)],
        compiler_params=_params(("arbitrary", "arbitrary")),
    )(a, w, h, g.reshape(1, d), dres)
    return dh, dg.reshape(-1)


def _rows_spec(tm, tk):
    return pl.BlockSpec((tm, tk), lambda i, k: (i, k))


def ffn_in(h, g, w_in, tag, job=None):
    m, d = h.shape
    f = w_in.shape[1] // 2
    tm, tn = _fit(m, 512), _fit(f, 1408)
    nj = f // tn

    def body(h_ref, g_ref, wg_ref, wu_ref, hn_ref, gu_ref, act_ref, hn_scr):
        @pl.when(pl.program_id(1) == 0)
        def _():
            hn = _rms(h_ref[...], g_ref[...]).astype(BF16)
            hn_scr[...] = hn
            hn_ref[...] = hn

        a = hn_scr[...]
        gate = jnp.dot(a, wg_ref[...], preferred_element_type=F32)
        up = jnp.dot(a, wu_ref[...], preferred_element_type=F32)
        gu_ref[0] = gate.astype(BF16)
        gu_ref[1] = up.astype(BF16)
        act_ref[...] = _silu_mul(gate, up).astype(BF16)

    grid = (m // tm, nj)
    body, x_in, x_out, x_shapes, x_scratch, x_args = carry(job, body, grid, 4, 3)
    res = pl.pallas_call(
        body,
        name=f"{tag}_in",
        grid=grid,
        in_specs=[pl.BlockSpec((tm, d), lambda i, j: (i, 0)), pl.BlockSpec((1, d), lambda i, j: (0, 0)),
                  pl.BlockSpec((d, tn), lambda i, j: (0, j)), pl.BlockSpec((d, tn), lambda i, j: (0, nj + j))] + x_in,
        out_specs=[pl.BlockSpec((tm, d), lambda i, j: (i, 0)), pl.BlockSpec((2, tm, tn), lambda i, j: (0, i, j)),
                   pl.BlockSpec((tm, tn), lambda i, j: (i, j))] + x_out,
        out_shape=[jax.ShapeDtypeStruct((m, d), BF16), jax.ShapeDtypeStruct((2, m, f), BF16),
                   jax.ShapeDtypeStruct((m, f), BF16)] + x_shapes,
        scratch_shapes=[pltpu.VMEM((tm, d), BF16)] + x_scratch,
        compiler_params=_params(("arbitrary", "arbitrary")),
    )(h, g.reshape(1, d), w_in, w_in, *x_args)
    return res[:3], res[3:]


def ffn_fwd(h, g, w_in, w_out, tag, job=None):
    (hn, gu, act), carried = ffn_in(h, g, w_in, tag, job)
    h_new = mm(act, w_out, name=f"{tag}_out", extras=(h,), epilogue=lambda acc, res: res + 0.5 * acc, tm=512, tk=2816)
    return h_new, (h, hn, gu, act), carried


def ffn_bwd(dh_new, saved, g, w_in, w_out, tag, jobs=(None, None)):
    h, hn, gu, act = saved
    m, d = h.shape
    f = w_out.shape[0]

    def act_bwd(acc, gu_blk):
        _, vjp = jax.vjp(_silu_mul, gu_blk[0].astype(F32), gu_blk[1].astype(F32))
        return jnp.stack(vjp(0.5 * acc))

    tm, tn = _fit(m, 512), _fit(f, 1408)
    pair = pl.BlockSpec((2, tm, tn), lambda i, j, k: (0, i, j))
    dgu, *carried0 = _mm_call(f"{tag}_dgu", (m // tm, f // tn, 1), dh_new, pl.BlockSpec((tm, d), lambda i, j, k: (i, 0)), w_out,
                              pl.BlockSpec((tn, d), lambda i, j, k: (j, 0)), _NT, [(gu, pair)],
                              [(jax.ShapeDtypeStruct((2, m, f), BF16), pair)], act_bwd, (tm, tn), jobs[0])
    dw_out = mm(act, dh_new, ta=True, name=f"{tag}_dwout", epilogue=lambda acc: 0.5 * acc, tm=1408)

    tn, tk = _fit(f, 1408), _fit(m, 2048)
    nh = f // tn
    dw_in, *carried1 = _mm_call(f"{tag}_dwin", (1, 2 * nh, m // tk), hn, pl.BlockSpec((tk, d), lambda i, j, k: (k, 0)), dgu,
                                pl.BlockSpec((None, tk, tn), lambda i, j, k: (j // nh, k, j % nh)), _TN, [],
                                [(jax.ShapeDtypeStruct((d, 2 * f), F32), pl.BlockSpec((d, tn), lambda i, j, k: (0, j)))],
                                lambda acc: acc, (d, tn), jobs[1])

    nkh = f // _fit(f, 2816)
    dh, dg = norm_input_bwd(dgu, lambda tm, tk: pl.BlockSpec((None, tm, tk), lambda i, k: (k // nkh, i, k % nkh)), 2 * nkh,
                            w_in, h, g, dh_new, f"{tag}_dhn")
    return dh, dg, dw_in, dw_out, carried0, carried1


def loss_fwd_bwd(h, g, target):
    d = h.shape[1]

    def fn(x, t, gg):
        y, vjp = jax.vjp(_rms, x, gg)
        err = y - t
        dx, dg = vjp(err * (1.0 / d))
        part = 0.5 * jnp.sum(jnp.sum(err * err, axis=1, keepdims=True), axis=0, keepdims=True) * (1.0 / d)
        return dx, dg, jnp.broadcast_to(part, (1, 128))

    dh, dg, loss = rowwise(fn, [h, target], [g.reshape(1, -1)], [(d, F32)], [(1, d), (1, 128)], name="loss_head")
    return loss[0, 0], dh, dg.reshape(-1)


def _shift_down(v, d, fill):
    rows = lax.broadcasted_iota(jnp.int32, v.shape, 0)
    return jnp.where(rows < d, fill, pltpu.roll(v, d, 0))


def _shift_up(v, d, fill):
    t = v.shape[0]
    rows = lax.broadcasted_iota(jnp.int32, v.shape, 0)
    return jnp.where(rows >= t - d, fill, pltpu.roll(v, t - d, 0))


def _scan_fwd(a, x):
    d = 1
    while d < a.shape[0]:
        x = x + a * _shift_down(x, d, 0.0)
        a = a * _shift_down(a, d, 1.0)
        d *= 2
    return a, x


def _scan_bwd(b, x):
    d = 1
    while d < b.shape[0]:
        x = x + b * _shift_up(x, d, 0.0)
        b = b * _shift_up(b, d, 1.0)
        d *= 2
    return b, x


def _rows_before(cur, prev8, s):
    r = pltpu.roll(cur, s, 0)
    p = pltpu.roll(prev8, s, 0)
    rows = lax.broadcasted_iota(jnp.int32, prev8.shape, 0)
    return jnp.concatenate([jnp.where(rows < s, p, r[:8]), r[8:]], axis=0)


def _rows_after(cur, next8, s):
    t = cur.shape[0]
    r = pltpu.roll(cur, t - s, 0)
    p = pltpu.roll(next8, 8 - s, 0)
    rows = lax.broadcasted_iota(jnp.int32, next8.shape, 0)
    return jnp.concatenate([r[:t - 8], jnp.where(rows >= 8 - s, p, r[t - 8:])], axis=0)


LRU_CHUNK = 256


def _neg_expm1(y):
    small = -y * (1.0 + 0.5 * y * (1.0 + y * (1.0 / 3.0)))
    return jnp.where(y > -0.01, small, 1.0 - jnp.exp(y))


def _lru_gate(xc, pre_a, pre_x, lam):
    r = _sigmoid(pre_a)
    ig = _sigmoid(pre_x)
    log_a = (-LRU_C * r) * _softplus(-lam)
    return jnp.exp(log_a), (ig * xc) * jnp.sqrt(_neg_expm1(2.0 * log_a))


def _lru_conv(br, prev8, conv_w, conv_b):
    taps = [br] + [_rows_before(br, prev8, s) for s in range(1, LRU_CONV)]
    xc = conv_b
    for k in range(LRU_CONV):
        xc = xc + conv_w[k:k + 1, :] * taps[LRU_CONV - 1 - k]
    return xc, taps


def _lru_pre(xcb, w_ref, bias):
    nb = w_ref.shape[0]
    bw = w_ref.shape[1]
    return jnp.concatenate(
        [jnp.dot(xcb[:, n * bw:(n + 1) * bw], w_ref[n], preferred_element_type=F32) for n in range(nb)], axis=1) + bias


def lru_scan_fwd(bgr, conv_w, conv_b, wa, ba, wx, bx, lam):
    l, w2 = bgr.shape
    w = w2 // 2
    t = _tile(l, (LRU_CHUNK, 128, 64, 32, 16, 8))

    def body(bg_ref, br_ref, cw_ref, cb_ref, wa_ref, ba_ref, wx_ref, bx_ref, lam_ref, y_ref, h_ref, tail_ref, hprev_ref):
        @pl.when(pl.program_id(0) == 0)
        def _():
            tail_ref[...] = jnp.zeros_like(tail_ref)
            hprev_ref[...] = jnp.zeros_like(hprev_ref)

        br = br_ref[...]
        xc, _ = _lru_conv(br, tail_ref[...], cw_ref[...], cb_ref[...])
        xcb = xc.astype(BF16)
        a, gx = _lru_gate(xc, _lru_pre(xcb, wa_ref, ba_ref[...]), _lru_pre(xcb, wx_ref, bx_ref[...]), lam_ref[...])
        acum, x = _scan_fwd(a, gx)
        h = x + acum * hprev_ref[pl.ds(7, 1), :]
        y_ref[...] = (_gelu(bg_ref[...]) * h).astype(y_ref.dtype)
        h_ref[...] = h
        tail_ref[...] = br[t - 8:, :]
        hprev_ref[...] = h[t - 8:, :]

    def whole(x):
        nd = x.ndim
        return pl.BlockSpec(x.shape, lambda i: (0,) * nd)

    consts = [conv_w, conv_b, wa, ba, wx, bx, lam]
    return pl.pallas_call(
        body,
        name="lru_scan_fwd",
        grid=(l // t,),
        in_specs=[pl.BlockSpec((t, w), lambda i: (i, 0)), pl.BlockSpec((t, w), lambda i: (i, 1))] + [whole(c) for c in consts],
        out_specs=[pl.BlockSpec((t, w), lambda i: (i, 0)), pl.BlockSpec((t, w), lambda i: (i, 0))],
        out_shape=[jax.ShapeDtypeStruct((l, w), BF16), jax.ShapeDtypeStruct((l, w), F32)],
        scratch_shapes=[pltpu.VMEM((8, w), F32), pltpu.VMEM((8, w), F32)],
        compiler_params=_params(("arbitrary",)),
    )(bgr, bgr, *consts)


def lru_scan_bwd(dy, bgr, hseq, conv_w, conv_b, wa, ba, wx, bx, lam):
    l, w2 = bgr.shape
    w = w2 // 2
    t = _tile(l, (LRU_CHUNK, 128, 64, 32, 16, 8))
    nc = l // t
    nb, bw = wa.shape[0], wa.shape[1]

    def body(dy_ref, bg_ref, br_ref, brh_ref, h_ref, hh_ref, cw_ref, cb_ref, wa_ref, ba_ref, wx_ref, bx_ref, lam_ref,
             dbgr_ref, dcw_ref, dcb_ref, dwa_ref, dba_ref, dwx_ref, dbx_ref, dlam_ref, dxcn_ref, carry_ref):
        i = pl.program_id(0)
        has_prev = (i < nc - 1).astype(F32)

        @pl.when(i == 0)
        def _():
            dxcn_ref[...] = jnp.zeros_like(dxcn_ref)
            carry_ref[...] = jnp.zeros_like(carry_ref)

        br = br_ref[...]
        cw = cw_ref[...]
        xc, taps = _lru_conv(br, brh_ref[...] * has_prev, cw, cb_ref[...])
        xcb = xc.astype(BF16)
        (a, _), gate_vjp = jax.vjp(_lru_gate, xc, _lru_pre(xcb, wa_ref, ba_ref[...]), _lru_pre(xcb, wx_ref, bx_ref[...]),
                                   lam_ref[...])
        hs = h_ref[...]
        _, out_vjp = jax.vjp(lambda g_, h_: _gelu(g_) * h_, bg_ref[...], hs)
        dbg, dhs = out_vjp(dy_ref[...])
        bcum, x = _scan_bwd(_shift_up(a, 1, 1.0), dhs)
        dh = x + bcum * carry_ref[pl.ds(0, 1), :]
        da = dh * _rows_before(hs, hh_ref[...] * has_prev, 1)
        dxc, dpa, dpx, dlam = gate_vjp((da, dh))
        dpab, dpxb = dpa.astype(BF16), dpx.astype(BF16)
        nt = (((1,), (1,)), ((), ()))
        tn = (((0,), (0,)), ((), ()))
        dxb, dwa, dwx = [], [], []
        for n in range(nb):
            sl = slice(n * bw, (n + 1) * bw)
            dxb.append(lax.dot_general(dpab[:, sl], wa_ref[n], nt, preferred_element_type=F32)
                       + lax.dot_general(dpxb[:, sl], wx_ref[n], nt, preferred_element_type=F32))
            dwa.append(lax.dot_general(xcb[:, sl], dpab[:, sl], tn, preferred_element_type=F32))
            dwx.append(lax.dot_general(xcb[:, sl], dpxb[:, sl], tn, preferred_element_type=F32))
        dxc = dxc + jnp.concatenate(dxb, axis=1)
        ups = [dxc] + [_rows_after(dxc, dxcn_ref[...], s) for s in range(1, LRU_CONV)]
        dbr = cw[LRU_CONV - 1:LRU_CONV, :] * ups[0]
        for k in range(LRU_CONV - 1):
            dbr = dbr + cw[k:k + 1, :] * ups[LRU_CONV - 1 - k]
        dbgr_ref[:, :w] = dbg.astype(dbgr_ref.dtype)
        dbgr_ref[:, w:] = dbr.astype(dbgr_ref.dtype)
        dcw = jnp.concatenate([jnp.sum(dxc * taps[LRU_CONV - 1 - k], axis=0, keepdims=True) for k in range(LRU_CONV)], axis=0)
        sums = [(dcw_ref, dcw), (dcb_ref, jnp.sum(dxc, axis=0, keepdims=True)), (dwa_ref, jnp.stack(dwa)),
                (dba_ref, jnp.sum(dpa, axis=0, keepdims=True)), (dwx_ref, jnp.stack(dwx)),
                (dbx_ref, jnp.sum(dpx, axis=0, keepdims=True)), (dlam_ref, dlam)]

        @pl.when(i == 0)
        def _():
            for ref, val in sums:
                ref[...] = val

        @pl.when(i > 0)
        def _():
            for ref, val in sums:
                ref[...] += val

        dxcn_ref[...] = dxc[:8, :]
        carry_ref[...] = (a * dh)[:8, :]

    def whole(shape):
        nd = len(shape)
        return pl.BlockSpec(tuple(shape), lambda i: (0,) * nd)

    consts = [conv_w, conv_b, wa, ba, wx, bx, lam]
    t8 = t // 8
    rev = lambda i: nc - 1 - i
    halo = lambda i: jnp.maximum(rev(i) * t8 - 1, 0)
    in_specs = [
        pl.BlockSpec((t, w), lambda i: (rev(i), 0)),
        pl.BlockSpec((t, w), lambda i: (rev(i), 0)),
        pl.BlockSpec((t, w), lambda i: (rev(i), 1)),
        pl.BlockSpec((8, w), lambda i: (halo(i), 1)),
        pl.BlockSpec((t, w), lambda i: (rev(i), 0)),
        pl.BlockSpec((8, w), lambda i: (halo(i), 0)),
    ] + [whole(c.shape) for c in consts]
    sum_shapes = [conv_w.shape, conv_b.shape, wa.shape, ba.shape, wx.shape, bx.shape, lam.shape]
    res = pl.pallas_call(
        body,
        name="lru_scan_bwd",
        grid=(nc,),
        in_specs=in_specs,
        out_specs=[pl.BlockSpec((t, w2), lambda i: (rev(i), 0))] + [whole(s) for s in sum_shapes],
        out_shape=[jax.ShapeDtypeStruct((l, w2), BF16)] + [jax.ShapeDtypeStruct(tuple(s), F32) for s in sum_shapes],
        scratch_shapes=[pltpu.VMEM((8, w), F32), pltpu.VMEM((8, w), F32)],
        compiler_params=_params(("arbitrary",)),
    )(dy, bgr, bgr, bgr, hseq, hseq, *consts)
    return res[0], res[1:]


def lru_fwd(hn, p):
    bgr = mm(hn, p["w_in"], name="lru_in")
    y, hseq = lru_scan_fwd(bgr, p["conv_w"], p["conv_b"], p["wa"], p["ba"], p["wx"], p["bx"], p["lam"])
    return y, (hn, bgr, hseq, y)


def lru_bwd(dmixed, saved, p):
    hn, bgr, hseq, y = saved
    dy = mm(dmixed, p["w_out"], tb=True, name="lru_dy")
    dw_out = mm(y, dmixed, ta=True, name="lru_dwout")
    dbgr, (dcw, dcb, dwa, dba, dwx, dbx, dlam) = lru_scan_bwd(dy, bgr, hseq, p["conv_w"], p["conv_b"], p["wa"], p["ba"],
                                                               p["wx"], p["bx"], p["lam"])
    dw_in = mm(hn, dbgr, ta=True, name="lru_dwin")
    grads = dict(w_in=dw_in, conv_w=dcw, conv_b=dcb, wa=dwa, ba=dba, wx=dwx, bx=dbx, lam=dlam, w_out=dw_out)
    return (dbgr, p["w_in"]), grads


SB_BLOCK = 256
SB_BLOCK_Q = 1024
_NT = (((1,), (1,)), ((), ()))
_TN = (((0,), (0,)), ((), ()))


def _running_sums(x, tri, total_col):
    xb = x.astype(BF16)
    run = jnp.dot(xb, tri, preferred_element_type=F32)
    return run, xb, run[:, total_col:total_col + 1]


def _tri(n, cmp):
    r = lax.broadcasted_iota(jnp.int32, (n, n), 0)
    c = lax.broadcasted_iota(jnp.int32, (n, n), 1)
    return cmp(r, c).astype(BF16)


SB_PAIR = 2 * SB_HEAD_DIM


def _pair_masks(x2, scale=None):
    lane = lax.broadcasted_iota(jnp.int32, x2.shape, 1)
    zero = jnp.zeros_like(x2)
    a, b = jnp.where(lane < SB_HEAD_DIM, x2, zero), jnp.where(lane >= SB_HEAD_DIM, x2, zero)
    if scale is not None:
        a, b = a * scale, b * scale
    return a, b


def _causal(shape, q0, k0):
    return lax.broadcasted_iota(jnp.int32, shape, 1) + k0 < lax.broadcasted_iota(jnp.int32, shape, 0) + q0


def _sb_blocks(l):
    bk = _tile(l, (SB_BLOCK, 128))
    bq = _tile(l, (SB_BLOCK_Q, 2 * SB_BLOCK, SB_BLOCK, 128))
    return bq, bk


def sb_pair_fwd(qkv, job=None):
    l, d3 = qkv.shape
    d = d3 // 3
    npair = d // SB_PAIR
    bq, bk = _sb_blocks(l)
    ratio = bq // bk
    scale = SB_HEAD_DIM ** -0.5
    t_suf = _tri(bk, lambda r, c: r > c)

    def body(q_ref, k_ref, v_ref, tsuf_ref, o_ref, ltot_ref):
        i = pl.program_id(1)
        qs = _pair_masks(q_ref[...], scale)
        tsuf = tsuf_ref[...]

        def tile(j, carry, masked, r0=0):
            rows = pl.ds(pl.multiple_of(j * bk, bk), bk)
            k2 = k_ref[rows, :]
            v2 = v_ref[rows, :]
            out = []
            for q1, (c_r, acc) in zip(qs, carry):
                z = lax.dot_general(q1[r0:], k2, _NT, preferred_element_type=F32)
                lk = -_softplus(z)
                if masked:
                    causal = _causal(z.shape, i * bq + r0, j * bk)
                    lk = jnp.where(causal, lk, 0.0)
                later, lkb, later0 = _running_sums(lk, tsuf, 0)
                w = jnp.exp(z + lk + c_r[r0:] + later)
                if masked:
                    w = jnp.where(causal, w, 0.0)
                c_new = c_r[r0:] + later0 + lkb[:, 0:1].astype(F32)
                acc_new = acc[r0:] + jnp.dot(w.astype(BF16), v2, preferred_element_type=F32)
                if r0:
                    c_new, acc_new = jnp.concatenate([c_r[:r0], c_new]), jnp.concatenate([acc[:r0], acc_new])
                out.append((c_new, acc_new))
            return tuple(out)

        carry = ((jnp.zeros((bq, 1), F32), jnp.zeros((bq, SB_PAIR), F32)),) * 2
        for dgl in reversed(range(ratio)):
            carry = tile(i * ratio + dgl, carry, True, dgl * bk)
        (c_a, acc_a), (c_b, acc_b) = lax.fori_loop(0, i * ratio, lambda jj, c: tile(i * ratio - 1 - jj, c, False), carry)
        lane = lax.broadcasted_iota(jnp.int32, acc_a.shape, 1)
        o_ref[...] = jnp.where(lane < SB_HEAD_DIM, acc_a, acc_b).astype(o_ref.dtype)
        ltot_ref[...] = jnp.where(lax.broadcasted_iota(jnp.int32, (bq, 2), 1) == 0, c_a, c_b)

    grid = (npair, l // bq)
    body, x_in, x_out, x_shapes, x_scratch, x_args = carry(job, body, grid, 4, 2)
    res = pl.pallas_call(
        body,
        name="sb_attn_fwd",
        grid=grid,
        in_specs=[
            pl.BlockSpec((bq, SB_PAIR), lambda p, i: (i, p)),
            pl.BlockSpec((l, SB_PAIR), lambda p, i: (0, npair + p)),
            pl.BlockSpec((l, SB_PAIR), lambda p, i: (0, 2 * npair + p)),
            pl.BlockSpec((bk, bk), lambda p, i: (0, 0)),
        ] + x_in,
        out_specs=[pl.BlockSpec((bq, SB_PAIR), lambda p, i: (i, p)), pl.BlockSpec((None, bq, 2), lambda p, i: (p, i, 0))] + x_out,
        out_shape=[jax.ShapeDtypeStruct((l, d), BF16), jax.ShapeDtypeStruct((npair, l, 2), F32)] + x_shapes,
        scratch_shapes=x_scratch,
        compiler_params=_params(("arbitrary", "arbitrary")),
    )(qkv, qkv, qkv, t_suf, *x_args)
    return res[0], res[1], res[2:]


def sb_pair_bwd(qkv, do, ltot, job=None):
    l, d3 = qkv.shape
    d = d3 // 3
    npair = d // SB_PAIR
    bq, bk = _sb_blocks(l)
    ratio = bq // bk
    scale = SB_HEAD_DIM ** -0.5
    t_inc = _tri(bk, lambda r, c: r <= c)
    t_exc = _tri(bk, lambda r, c: r < c)

    def body(q_ref, k_ref, v_ref, do_ref, ltot_ref, tinc_ref, texc_ref, dq_ref, dk_ref, dv_ref):
        i = pl.program_id(1)

        @pl.when(i == 0)
        def _():
            dk_ref[...] = jnp.zeros_like(dk_ref)
            dv_ref[...] = jnp.zeros_like(dv_ref)

        qs = _pair_masks(q_ref[...], scale)
        dos = _pair_masks(do_ref[...])
        lt = ltot_ref[...]
        ltots = (lt[:, 0:1], lt[:, 1:2])
        tinc = tinc_ref[...]
        texc = texc_ref[...]

        def tile(j, carry, masked, r0=0):
            rows = pl.ds(pl.multiple_of(j * bk, bk), bk)
            k2 = k_ref[rows, :]
            v2 = v_ref[rows, :]
            out = []
            dk2 = dv2 = None
            for q1, do1, ltot1, (c_l, c_p, dq) in zip(qs, dos, ltots, carry):
                q1s, do1s = q1[r0:], do1[r0:]
                z = lax.dot_general(q1s, k2, _NT, preferred_element_type=F32)
                lk = -_softplus(z)
                if masked:
                    causal = _causal(z.shape, i * bq + r0, j * bk)
                    lk = jnp.where(causal, lk, 0.0)
                log_beta = z + lk
                upto, _, lk_tile = _running_sums(lk, tinc, bk - 1)
                w = jnp.exp(log_beta + (ltot1[r0:] - c_l[r0:]) - upto)
                if masked:
                    w = jnp.where(causal, w, 0.0)
                g = w * lax.dot_general(do1s, v2, _NT, preferred_element_type=F32)
                before, gb, before_last = _running_sums(g, texc, bk - 1)
                dz = g - jnp.exp(log_beta) * (g + c_p[r0:] + before)
                if masked:
                    dz = jnp.where(causal, dz, 0.0)
                dzb = dz.astype(BF16)
                dk1 = lax.dot_general(dzb, q1s, _TN, preferred_element_type=F32)
                dv1 = lax.dot_general(w.astype(BF16), do1s, _TN, preferred_element_type=F32)
                dk2 = dk1 if dk2 is None else dk2 + dk1
                dv2 = dv1 if dv2 is None else dv2 + dv1
                new = (c_l[r0:] + lk_tile, c_p[r0:] + before_last + gb[:, bk - 1:bk].astype(F32),
                       dq[r0:] + jnp.dot(dzb, k2, preferred_element_type=F32))
                if r0:
                    new = tuple(jnp.concatenate([old[:r0], part]) for old, part in zip((c_l, c_p, dq), new))
                out.append(new)
            dk_ref[rows, :] += dk2
            dv_ref[rows, :] += dv2
            return tuple(out)

        zero = jnp.zeros((bq, 1), F32)
        carry = ((zero, zero, jnp.zeros((bq, SB_PAIR), F32)),) * 2
        carry = lax.fori_loop(0, i * ratio, lambda j, c: tile(j, c, False), carry)
        for dgl in range(ratio):
            carry = tile(i * ratio + dgl, carry, True, dgl * bk)
        (_, _, dq_a), (_, _, dq_b) = carry
        lane = lax.broadcasted_iota(jnp.int32, dq_a.shape, 1)
        dq_ref[...] = jnp.where(lane < SB_HEAD_DIM, dq_a, dq_b) * scale

    col = lambda s: pl.BlockSpec((l, SB_PAIR), lambda p, i: (0, s * npair + p))
    blk_spec = pl.BlockSpec((bq, SB_PAIR), lambda p, i: (i, p))
    tri_spec = pl.BlockSpec((bk, bk), lambda p, i: (0, 0))
    grid = (npair, l // bq)
    body, x_in, x_out, x_shapes, x_scratch, x_args = carry(job, body, grid, 7, 3)
    res = pl.pallas_call(
        body,
        name="sb_attn_bwd",
        grid=grid,
        in_specs=[blk_spec, col(1), col(2), blk_spec, pl.BlockSpec((None, bq, 2), lambda p, i: (p, i, 0)), tri_spec, tri_spec] + x_in,
        out_specs=[blk_spec, pl.BlockSpec((l, SB_PAIR), lambda p, i: (0, p)), pl.BlockSpec((l, SB_PAIR), lambda p, i: (0, p))] + x_out,
        out_shape=[jax.ShapeDtypeStruct((l, d), F32)] * 3 + x_shapes,
        scratch_shapes=x_scratch,
        compiler_params=_params(("arbitrary", "arbitrary")),
    )(qkv, qkv, qkv, do, ltot, t_inc, t_exc, *x_args)
    return res[:3], res[3:]


def sb_fwd(hn, p, job=None):
    qkv = mm(hn, p["w_qkv"], name="sb_qkv", out_dtypes=(BF16,), tm=2048, tn=512)
    o, ltot, carried = sb_pair_fwd(qkv, job)
    return o, (hn, qkv, ltot, o), carried


def sb_bwd(dmixed, saved, p, job=None):
    hn, qkv, ltot, o = saved
    do = mm(dmixed, p["w_out"], tb=True, name="sb_do", out_dtypes=(BF16,))
    dw_out = mm(o, dmixed, ta=True, name="sb_dwout")
    dq_dk_dv, carried = sb_pair_bwd(qkv, do, ltot, job)
    dqkv = jnp.concatenate([g.astype(BF16) for g in dq_dk_dv], axis=1)
    dw_qkv = mm(hn, dqkv, ta=True, name="sb_dwqkv")
    return (dqkv, p["w_qkv"]), dict(w_qkv=dw_qkv, w_out=dw_out), carried


S5_CHUNK = 256
S5_CHUNK_BWD = 128
S5_SLAB_GROUPS = 8
S5_LANES = 128


def _s5_discretise(lr, li, ldt, bre, bim):
    dt = jnp.exp(ldt)
    mag = jnp.exp(lr * dt)
    lbr = mag * jnp.cos(li * dt)
    lbi = mag * jnp.sin(li * dt)
    inv = 1.0 / (lr * lr + li * li)
    cr = ((lbr - 1.0) * lr + lbi * li) * inv
    ci = (lbi * lr - (lbr - 1.0) * li) * inv
    return lbr, lbi, cr * bre - ci * bim, cr * bim + ci * bre


def _s5_cols(lam_re, lam_im, log_dt, b_re, b_im):
    g, p = lam_re.shape
    col = lambda x: x.reshape(g * p, 1)
    ldt = jnp.broadcast_to(log_dt[:, None], (g, p))
    return col(lam_re), col(lam_im), col(ldt), b_re.reshape(g * p, -1), b_im.reshape(g * p, -1)


def _slab_b(bbar):
    sg = S5_SLAB_GROUPS
    gp, h = bbar.shape
    p = S5_STATE
    x = bbar.reshape(gp // (sg * p), sg, p, h)
    return jnp.einsum("kaph,ab->kahbp", x, jnp.eye(sg, dtype=x.dtype)).reshape(-1, sg * h, sg * p)


def _unslab_b(dslab):
    sg, p = S5_SLAB_GROUPS, S5_STATE
    nk, sh, _ = dslab.shape
    h = sh // sg
    x = dslab.reshape(nk, sg, h, sg, p)
    return jnp.einsum("kahbp,ab->kaph", x, jnp.eye(sg, dtype=x.dtype)).reshape(nk * sg * p, h)


def _slab_c(c):
    sg = S5_SLAB_GROUPS
    g, h, p = c.shape
    x = c.reshape(g // sg, sg, h, p)
    return jnp.einsum("kahp,ab->kapbh", x, jnp.eye(sg, dtype=x.dtype)).reshape(-1, sg * p, sg * h)


def _unslab_c(dslab):
    sg, p = S5_SLAB_GROUPS, S5_STATE
    nk, _, sh = dslab.shape
    h = sh // sg
    x = dslab.reshape(nk, sg, p, sg, h)
    return jnp.einsum("kapbh,ab->kahp", x, jnp.eye(sg, dtype=x.dtype)).reshape(nk * sg, h, p)


def _cmul_scan(lre, lim, xre, xim, re_scr, im_scr, cre, cim, reverse):
    nb, lanes = re_scr.shape[0], re_scr.shape[2]
    ends = []
    for c in range(nb):
        sl = slice(c * lanes, (c + 1) * lanes)
        re_scr[c] = xre[:, sl]
        im_scr[c] = xim[:, sl]
        ends.append(_cmul_scan_block(lre[:, sl], lim[:, sl], re_scr.at[c], im_scr.at[c], cre[:, sl], cim[:, sl], reverse))
    return (jnp.concatenate([re_scr[c] for c in range(nb)], axis=1), jnp.concatenate([im_scr[c] for c in range(nb)], axis=1),
            jnp.concatenate([e[0] for e in ends], axis=1), jnp.concatenate([e[1] for e in ends], axis=1))


def _cmul_scan_block(lre, lim, re_ref, im_ref, cre, cim, reverse):
    t = re_ref.shape[0]
    g = t // 8
    shift = _shift_up if reverse else _shift_down
    cmul = lambda ar, ai, br, bi: (ar * br - ai * bi, ar * bi + ai * br)
    pows = [(lre, lim)]
    for _ in range(7):
        pows.append(cmul(*pows[-1], lre, lim))
    local, prev = [None] * 8, None
    for r in (reversed(range(8)) if reverse else range(8)):
        cr, ci = re_ref[pl.ds(r, g, stride=8), :], im_ref[pl.ds(r, g, stride=8), :]
        if prev is not None:
            pr, pi = cmul(lre, lim, *prev)
            cr, ci = cr + pr, ci + pi
        local[r] = prev = (cr, ci)
    yr, yi = local[0 if reverse else 7]
    edge = lax.broadcasted_iota(jnp.int32, yr.shape, 0) == (g - 1 if reverse else 0)
    mr, mi = pows[7]
    kr, ki = cmul(mr, mi, cre, cim)
    yr, yi = yr + jnp.where(edge, kr, 0.0), yi + jnp.where(edge, ki, 0.0)
    d = 1
    while d < g:
        pr, pi = cmul(mr, mi, shift(yr, d, 0.0), shift(yi, d, 0.0))
        yr, yi = yr + pr, yi + pi
        mr, mi = cmul(mr, mi, mr, mi)
        d *= 2
    er, ei = jnp.where(edge, cre, shift(yr, 1, 0.0)), jnp.where(edge, cim, shift(yi, 1, 0.0))
    for r in range(8):
        pr, pi = cmul(*pows[7 - r if reverse else r], er, ei)
        re_ref[pl.ds(r, g, stride=8), :] = local[r][0] + pr
        im_ref[pl.ds(r, g, stride=8), :] = local[r][1] + pi
    return yr, yi


def s5_scan_fwd(u, lbr, lbi, bbd_re, bbd_im, cbd_re, cbd_imn, d_skip, job=None):
    l, w = u.shape
    n = lbr.shape[1]
    nk, cw, sw = bbd_re.shape
    t = _tile(l, (S5_CHUNK, 64, 32, 16, 8))

    def body(u_ref, lbr_ref, lbi_ref, bre_ref, bim_ref, cre_ref, cim_ref, d_ref, sre_ref, sim_ref, y_ref, z_ref, pre_ref, pim_ref,
             xr_ref, xi_ref):
        @pl.when(pl.program_id(0) == 0)
        def _():
            pre_ref[...] = jnp.zeros_like(pre_ref)
            pim_ref[...] = jnp.zeros_like(pim_ref)

        uu = u_ref[...]
        ub = uu.astype(BF16)
        lre, lim = lbr_ref[...], lbi_ref[...]
        xre = jnp.concatenate([jnp.dot(ub[:, k * cw:(k + 1) * cw], bre_ref[k], preferred_element_type=F32) for k in range(nk)], axis=1)
        xim = jnp.concatenate([jnp.dot(ub[:, k * cw:(k + 1) * cw], bim_ref[k], preferred_element_type=F32) for k in range(nk)], axis=1)
        last = t // 8 - 1
        sre, sim, pre_ref[...], pim_ref[...] = _cmul_scan(lre, lim, xre, xim, xr_ref, xi_ref, pre_ref[pl.ds(last, 1), :],
                                                          pim_ref[pl.ds(last, 1), :], False)
        sre_ref[...] = sre
        sim_ref[...] = sim
        sreb, simb = sre.astype(BF16), sim.astype(BF16)
        y = jnp.concatenate(
            [jnp.dot(sreb[:, k * sw:(k + 1) * sw], cre_ref[k], preferred_element_type=F32)
             + jnp.dot(simb[:, k * sw:(k + 1) * sw], cim_ref[k], preferred_element_type=F32) for k in range(nk)], axis=1)
        y = y + d_ref[...] * uu
        y_ref[...] = y
        z_ref[...] = _gelu(y).astype(z_ref.dtype)

    def whole(x):
        nd = x.ndim
        return pl.BlockSpec(x.shape, lambda i: (0,) * nd)

    consts = [lbr, lbi, bbd_re, bbd_im, cbd_re, cbd_imn, d_skip]
    row = lambda c: pl.BlockSpec((t, c), lambda i: (i, 0))
    body, x_in, x_out, x_shapes, x_scratch, x_args = carry(job, body, (l // t,), 1 + len(consts), 4)
    res = pl.pallas_call(
        body,
        name="s5_scan_fwd",
        grid=(l // t,),
        in_specs=[row(w)] + [whole(c) for c in consts] + x_in,
        out_specs=[row(n), row(n), row(w), row(w)] + x_out,
        out_shape=[jax.ShapeDtypeStruct((l, n), F32), jax.ShapeDtypeStruct((l, n), F32), jax.ShapeDtypeStruct((l, w), F32),
                   jax.ShapeDtypeStruct((l, w), BF16)] + x_shapes,
        scratch_shapes=[pltpu.VMEM((t // 8, n), F32), pltpu.VMEM((t // 8, n), F32), pltpu.VMEM((n // S5_LANES, t, S5_LANES), F32),
                        pltpu.VMEM((n // S5_LANES, t, S5_LANES), F32)] + x_scratch,
        compiler_params=_params(("arbitrary",)),
    )(u, *consts, *x_args)
    return res[:4], res[4:]


def s5_scan_bwd(dz, y, u, sre, sim, lbr, lbi, bbd_re, bbd_im, cbd_re, cbd_imn, d_skip, job=None):
    l, w = u.shape
    n = lbr.shape[1]
    nk, cw, sw = bbd_re.shape
    t = _tile(l, (S5_CHUNK_BWD, 64, 32, 16, 8))
    nc = l // t

    def body(dz_ref, y_ref, u_ref, sre_ref, sim_ref, hre_ref, him_ref, lbr_ref, lbi_ref, bre_ref, bim_ref, cre_ref, cim_ref,
             d_ref, du_ref, dlr_ref, dli_ref, dbre_ref, dbim_ref, dcre_ref, dcim_ref, dd_ref, nre_ref, nim_ref, dsr_ref, dsi_ref):
        i = pl.program_id(0)
        has_prev = (i < nc - 1).astype(F32)

        @pl.when(i == 0)
        def _():
            nre_ref[...] = jnp.zeros_like(nre_ref)
            nim_ref[...] = jnp.zeros_like(nim_ref)

        uu = u_ref[...]
        ub = uu.astype(BF16)
        lre, lim = lbr_ref[...], lbi_ref[...]
        _, gelu_vjp = jax.vjp(_gelu, y_ref[...])
        dy = gelu_vjp(dz_ref[...].astype(F32))[0]
        dyb = dy.astype(BF16)
        gre = jnp.concatenate([lax.dot_general(dyb[:, k * cw:(k + 1) * cw], cre_ref[k], _NT, preferred_element_type=F32)
                               for k in range(nk)], axis=1)
        gim = jnp.concatenate([lax.dot_general(dyb[:, k * cw:(k + 1) * cw], cim_ref[k], _NT, preferred_element_type=F32)
                               for k in range(nk)], axis=1)
        dsre, dsim, nre_ref[...], nim_ref[...] = _cmul_scan(lre, -lim, gre, gim, dsr_ref, dsi_ref, nre_ref[pl.ds(0, 1), :],
                                                            nim_ref[pl.ds(0, 1), :], True)
        dsreb, dsimb = dsre.astype(BF16), dsim.astype(BF16)
        s_re, s_im = sre_ref[...], sim_ref[...]
        du = jnp.concatenate(
            [lax.dot_general(dsreb[:, k * sw:(k + 1) * sw], bre_ref[k], _NT, preferred_element_type=F32)
             + lax.dot_general(dsimb[:, k * sw:(k + 1) * sw], bim_ref[k], _NT, preferred_element_type=F32) for k in range(nk)],
            axis=1)
        du_ref[...] = (du + d_ref[...] * dy).astype(du_ref.dtype)
        pre = _rows_before(s_re, hre_ref[...] * has_prev, 1)
        pim = _rows_before(s_im, him_ref[...] * has_prev, 1)
        sreb, simb = s_re.astype(BF16), s_im.astype(BF16)
        sums = [
            (dlr_ref, jnp.sum(dsre * pre + dsim * pim, axis=0, keepdims=True)),
            (dli_ref, jnp.sum(dsim * pre - dsre * pim, axis=0, keepdims=True)),
            (dbre_ref, jnp.stack([lax.dot_general(ub[:, k * cw:(k + 1) * cw], dsreb[:, k * sw:(k + 1) * sw], _TN,
                                                  preferred_element_type=F32) for k in range(nk)])),
            (dbim_ref, jnp.stack([lax.dot_general(ub[:, k * cw:(k + 1) * cw], dsimb[:, k * sw:(k + 1) * sw], _TN,
                                                  preferred_element_type=F32) for k in range(nk)])),
            (dcre_ref, jnp.stack([lax.dot_general(sreb[:, k * sw:(k + 1) * sw], dyb[:, k * cw:(k + 1) * cw], _TN,
                                                  preferred_element_type=F32) for k in range(nk)])),
            (dcim_ref, jnp.stack([lax.dot_general(simb[:, k * sw:(k + 1) * sw], dyb[:, k * cw:(k + 1) * cw], _TN,
                                                  preferred_element_type=F32) for k in range(nk)])),
            (dd_ref, jnp.sum(dy * uu, axis=0, keepdims=True)),
        ]

        @pl.when(i == 0)
        def _():
            for ref, val in sums:
                ref[...] = val

        @pl.when(i > 0)
        def _():
            for ref, val in sums:
                ref[...] += val

    def whole(shape):
        nd = len(shape)
        return pl.BlockSpec(tuple(shape), lambda i: (0,) * nd)

    consts = [lbr, lbi, bbd_re, bbd_im, cbd_re, cbd_imn, d_skip]
    t8 = t // 8
    rev = lambda i: nc - 1 - i
    halo = lambda i: jnp.maximum(rev(i) * t8 - 1, 0)
    row = lambda c: pl.BlockSpec((t, c), lambda i: (rev(i), 0))
    sum_shapes = [lbr.shape, lbi.shape, bbd_re.shape, bbd_im.shape, cbd_re.shape, cbd_imn.shape, d_skip.shape]
    body, x_in, x_out, x_shapes, x_scratch, x_args = carry(job, body, (nc,), 7 + len(consts), 8)
    res = pl.pallas_call(
        body,
        name="s5_scan_bwd",
        grid=(nc,),
        in_specs=[row(w), row(w), row(w), row(n), row(n), pl.BlockSpec((8, n), lambda i: (halo(i), 0)),
                  pl.BlockSpec((8, n), lambda i: (halo(i), 0))] + [whole(c.shape) for c in consts] + x_in,
        out_specs=[row(w)] + [whole(s) for s in sum_shapes] + x_out,
        out_shape=[jax.ShapeDtypeStruct((l, w), BF16)] + [jax.ShapeDtypeStruct(tuple(s), F32) for s in sum_shapes] + x_shapes,
        scratch_shapes=[pltpu.VMEM((t8, n), F32), pltpu.VMEM((t8, n), F32), pltpu.VMEM((n // S5_LANES, t, S5_LANES), F32),
                        pltpu.VMEM((n // S5_LANES, t, S5_LANES), F32)] + x_scratch,
        compiler_params=_params(("arbitrary",)),
    )(dz, y, u, sre, sim, sre, sim, *consts, *x_args)
    return res[0], res[1:8], res[8:]


def _glu(vg):
    w = vg.shape[1] // 2
    return vg[:, :w] * _sigmoid(vg[:, w:])


def s5_fwd(hn, h, p, job=None):
    cols = _s5_cols(p["lam_re"], p["lam_im"], p["log_dt"], p["b_re"], p["b_im"])
    gp, hh = cols[3].shape
    lbr, lbi, bbr, bbi = rowwise(_s5_discretise, list(cols), [], [(1, F32), (1, F32), (hh, F32), (hh, F32)],
                                 name="s5_discretise", tm=512)
    consts = (lbr.reshape(1, gp), lbi.reshape(1, gp), _slab_b(bbr).astype(BF16), _slab_b(bbi).astype(BF16),
              _slab_c(p["c_re"]).astype(BF16), _slab_c(-p["c_im"]).astype(BF16), p["d"])
    u = mm(hn, p["w_in"], name="s5_in")
    (sre, sim, y, z), carried = s5_scan_fwd(u, *consts, job=job)
    vg = mm(z, p["w_out"], name="s5_out", out_dtypes=(BF16,))
    h_new = rowwise(lambda a, r: r + _glu(a.astype(F32)), [vg, h], [], [(h.shape[1], F32)], name="s5_glu")
    return h_new, (hn, u, sre, sim, y, z, vg, cols, consts), carried


def s5_bwd(dh_new, saved, p, job=None):
    hn, u, sre, sim, y, z, vg, cols, consts = saved

    def glu_bwd(a, dm):
        _, vjp = jax.vjp(_glu, a.astype(F32))
        return vjp(dm)[0]

    dvg = rowwise(glu_bwd, [vg, dh_new], [], [(vg.shape[1], BF16)], name="s5_dglu")
    dw_out = mm(z, dvg, ta=True, name="s5_dwout")
    dz = mm(dvg, p["w_out"], tb=True, name="s5_dz", out_dtypes=(BF16,), tk=2048)
    du, (dlbr, dlbi, dbbr, dbbi, dcre, dcimn, dd), carried = s5_scan_bwd(dz, y, u, sre, sim, *consts, job=job)
    dw_in = mm(hn, du, ta=True, name="s5_dwin")
    gp = cols[0].shape[0]

    def disc_bwd(lr, li, ldt, bre, bim, g0, g1, g2, g3):
        _, vjp = jax.vjp(_s5_discretise, lr, li, ldt, bre, bim)
        return vjp((g0, g1, g2, g3))

    cot = (dlbr.reshape(gp, 1), dlbi.reshape(gp, 1), _unslab_b(dbbr), _unslab_b(dbbi))
    dlr, dli, dldt, dbre, dbim = rowwise(disc_bwd, list(cols + cot), [], [(c.shape[1], F32) for c in cols],
                                         name="s5_discretise_bwd", tm=512)
    g_, p_ = p["lam_re"].shape
    grads = dict(w_in=dw_in, lam_re=dlr.reshape(g_, p_), lam_im=dli.reshape(g_, p_), log_dt=dldt.reshape(g_, p_).sum(axis=1),
                 b_re=dbre.reshape(p["b_re"].shape), b_im=dbim.reshape(p["b_im"].shape), c_re=_unslab_c(dcre),
                 c_im=-_unslab_c(dcimn), d=dd, w_out=dw_out)
    return (du, p["w_in"]), grads, carried


MESH = pl.DeviceIdType.MESH
N_CHIPS = 4
N_DEVICES = 8


def _place():
    x, y, c = lax.axis_index("x"), lax.axis_index("y"), lax.axis_index("c")
    return x, y, c, [(1 - x, y), (x, 1 - y), (1 - x, 1 - y)]


def _hbm_call(body, name, ins, out_shapes, n_remote, n_local=0):
    hbm = pl.BlockSpec(memory_space=pltpu.HBM)
    scratch = [pltpu.SemaphoreType.DMA((n_remote,)), pltpu.SemaphoreType.DMA((n_remote,))]
    if n_local:
        scratch.append(pltpu.SemaphoreType.DMA((n_local,)))
    return pl.pallas_call(
        body,
        name=name,
        in_specs=[hbm] * len(ins),
        out_specs=[hbm] * len(out_shapes),
        out_shape=out_shapes,
        scratch_shapes=scratch,
    )(*ins)


def _split_dim(shape):
    return next(d for d, s in enumerate(shape) if s >= 2 and s % 2 == 0)


class Exchange:
    def __init__(self, ins, out_shapes, n_remote, n_local, start, finish):
        self.ins, self.out_shapes, self.start, self.finish = list(ins), list(out_shapes), start, finish
        self.scratch = [pltpu.SemaphoreType.DMA((n_remote,)), pltpu.SemaphoreType.DMA((n_remote,)),
                        pltpu.SemaphoreType.DMA((max(n_local, 1),))]


def both(first, second):
    n1, m1, s1 = len(first.ins), len(first.out_shapes), len(first.scratch)
    job = Exchange(first.ins + second.ins, first.out_shapes + second.out_shapes, 1, 1,
                   lambda ins, outs, *sems: (first.start(ins[:n1], outs[:m1], *sems[:s1]),
                                             second.start(ins[n1:], outs[m1:], *sems[s1:])),
                   lambda ins, outs, *sems: (first.finish(ins[:n1], outs[:m1], *sems[:s1]),
                                             second.finish(ins[n1:], outs[m1:], *sems[s1:])))
    job.scratch = first.scratch + second.scratch
    return job


def run_exchange(job, name):
    n_in, n_out = len(job.ins), len(job.out_shapes)

    def body(*refs):
        ins, outs, sems = refs[:n_in], refs[n_in:n_in + n_out], refs[n_in + n_out:]
        job.start(ins, outs, *sems)
        job.finish(ins, outs, *sems)

    hbm = pl.BlockSpec(memory_space=pltpu.HBM)
    return pl.pallas_call(body, name=name, in_specs=[hbm] * n_in, out_specs=[hbm] * n_out, out_shape=job.out_shapes,
                          scratch_shapes=job.scratch)(*job.ins)


def carry(job, body, grid, n_in, n_out):
    if job is None:
        return body, [], [], [], [], []
    nji, njo = len(job.ins), len(job.out_shapes)

    def carrying(*refs):
        ins, jins = refs[:n_in], refs[n_in:n_in + nji]
        outs, jouts = refs[n_in + nji:n_in + nji + n_out], refs[n_in + nji + n_out:n_in + nji + n_out + njo]
        rest = refs[n_in + nji + n_out + njo:]
        n_sems = len(job.scratch)
        own, sems = rest[:len(rest) - n_sems], rest[len(rest) - n_sems:]
        ids = [pl.program_id(ax) for ax in range(len(grid))]
        first = functools.reduce(jnp.logical_and, [i == 0 for i in ids])
        last = functools.reduce(jnp.logical_and, [i == g - 1 for i, g in zip(ids, grid)])

        @pl.when(first)
        def _():
            job.start(jins, jouts, *sems)

        body(*ins, *outs, *own)

        @pl.when(last)
        def _():
            job.finish(jins, jouts, *sems)

    hbm = pl.BlockSpec(memory_space=pltpu.HBM)
    return carrying, [hbm] * nji, [hbm] * njo, job.out_shapes, job.scratch, job.ins


def gather_chips(shards):
    n = len(shards)
    cuts = [_split_dim(s.shape) for s in shards]

    def parts(ins, outs, send, recv):
        x, y, c, chips = _place()

        def half(ref, t, which, lead=()):
            size = shards[t].shape[cuts[t]] // 2
            return ref.at[lead + (slice(None),) * cuts[t] + (pl.ds(which * size, size),)]

        def copy(t, k, block, which, to, src=None):
            dst = half(outs[t], t, which, (block,))
            return pltpu.make_async_remote_copy(dst if src is None else src, dst, send.at[6 * t + k], recv.at[6 * t + k],
                                                device_id=to, device_id_type=MESH)

        me = 2 * x + y
        sends = [copy(t, k, me, c, (px, py, c), src=half(ins[t], t, c)) for t in range(n) for k, (px, py) in enumerate(chips)]
        return x, y, c, chips, me, copy, sends

    def start(ins, outs, send, recv, local):
        _, _, _, _, me, _, sends = parts(ins, outs, send, recv)
        for t in range(n):
            pltpu.make_async_copy(ins[t], outs[t].at[me], local.at[t]).start()
        for cp in sends:
            cp.start()

    def finish(ins, outs, send, recv, local):
        x, y, c, chips, me, copy, sends = parts(ins, outs, send, recv)
        passed = []
        for t in range(n):
            for k, (px, py) in enumerate(chips):
                copy(t, k, 2 * px + py, c, (px, py, c)).wait_recv()
                on = copy(t, 3 + k, 2 * px + py, c, (x, y, 1 - c))
                on.start()
                passed.append(on)
        for t in range(n):
            for k, (px, py) in enumerate(chips):
                copy(t, 3 + k, 2 * px + py, 1 - c, (x, y, 1 - c)).wait_recv()
        for cp in sends + passed:
            cp.wait_send()
        for t in range(n):
            pltpu.make_async_copy(ins[t], outs[t].at[me], local.at[t]).wait()

    return Exchange(shards, [jax.ShapeDtypeStruct((N_CHIPS,) + s.shape, s.dtype) for s in shards], 6 * n, n, start, finish)


def scatter_chips(blocked):
    n = len(blocked)

    def copies(ins, outs, send, recv):
        x, y, c, chips = _place()
        return [pltpu.make_async_remote_copy(ins[t].at[2 * px + py], outs[t].at[k], send.at[3 * t + k], recv.at[3 * t + k],
                                             device_id=(px, py, c), device_id_type=MESH)
                for t in range(n) for k, (px, py) in enumerate(chips)]

    def start(ins, outs, send, recv, local):
        for cp in copies(ins, outs, send, recv):
            cp.start()

    def finish(ins, outs, send, recv, local):
        for cp in copies(ins, outs, send, recv):
            cp.wait()

    return Exchange(blocked, [jax.ShapeDtypeStruct((3,) + b.shape[1:], b.dtype) for b in blocked], 3 * n, 0, start, finish)


def swap_cores(arrays):
    n = len(arrays)

    def body(*refs):
        ins, outs = refs[:n], refs[n:2 * n]
        send, recv = refs[2 * n:]
        x, y, c, _ = _place()
        pending = []
        for t in range(n):
            cp = pltpu.make_async_remote_copy(ins[t], outs[t], send.at[t], recv.at[t], device_id=(x, y, 1 - c),
                                              device_id_type=MESH)
            cp.start()
            pending.append(cp)
        for cp in pending:
            cp.wait()

    return _hbm_call(body, "swap_cores", arrays, [jax.ShapeDtypeStruct(a.shape, a.dtype) for a in arrays], n)


def gather_devices(buf):
    def copies(in_ref, out_ref, send, recv, local):
        x, y, c, _ = _place()
        me = 4 * x + 2 * y + c
        going, landing = [], []
        for k in range(1, N_DEVICES):
            peer = (x ^ ((k >> 2) & 1), y ^ ((k >> 1) & 1), c ^ (k & 1))
            going.append(pltpu.make_async_remote_copy(in_ref, out_ref.at[me], send.at[k - 1], recv.at[k - 1],
                                                      device_id=peer, device_id_type=MESH))
            landing.append(pltpu.make_async_remote_copy(in_ref, out_ref.at[4 * peer[0] + 2 * peer[1] + peer[2]], send.at[k - 1],
                                                        recv.at[k - 1], device_id=peer, device_id_type=MESH))
        return pltpu.make_async_copy(in_ref, out_ref.at[me], local.at[0]), going, landing

    def start(ins, outs, send, recv, local):
        own, going, _ = copies(ins[0], outs[0], send, recv, local)
        own.start()
        for cp in going:
            cp.start()

    def finish(ins, outs, send, recv, local):
        own, _, landing = copies(ins[0], outs[0], send, recv, local)
        own.wait()
        for cp in landing:
            cp.wait()

    return Exchange([buf], [jax.ShapeDtypeStruct((N_DEVICES,) + buf.shape, buf.dtype)], N_DEVICES - 1, 1, start, finish)


def _adamw(w, g, m, v):
    m = ADAM_B1 * m + (1.0 - ADAM_B1) * g
    v = ADAM_B2 * v + (1.0 - ADAM_B2) * (g * g)
    m_hat = m / (1.0 - ADAM_B1 ** ADAM_STEP)
    v_hat = v / (1.0 - ADAM_B2 ** ADAM_STEP)
    return -ADAM_LR * (m_hat / (jnp.sqrt(v_hat) + ADAM_EPS) + ADAM_WD * w), m, v


def _rows2d(a):
    return a.reshape(-1, a.shape[-1])


WEIGHTS = ["ffn1_norm", "ffn1_w_in", "ffn1_w_out", "mix_norm", "ffn2_norm", "ffn2_w_in", "ffn2_w_out", "final_norm",
           "s5_w_in", "s5_lam_re", "s5_lam_im", "s5_log_dt", "s5_b_re", "s5_b_im", "s5_c_re", "s5_c_im", "s5_d", "s5_w_out",
           "sb_w_qkv", "sb_w_out", "lru_w_in", "lru_conv_w", "lru_conv_b", "lru_w_a", "lru_b_a", "lru_w_x", "lru_b_x",
           "lru_lambda", "lru_w_out"]
INPUTS = ["x"] + WEIGHTS + ["loss_target"] + ["m_" + n for n in WEIGHTS] + ["v_" + n for n in WEIGHTS]
SHARDED_BIG = dict(ffn1_w_in=2, ffn1_w_out=1, ffn2_w_in=2, ffn2_w_out=1, s5_w_in=1, s5_w_out=2, sb_w_qkv=2, sb_w_out=1,
                   lru_w_in=2, lru_w_a=2, lru_w_x=2, lru_w_out=1)
SHARDED_SMALL = dict(s5_d=1, lru_conv_w=2, lru_conv_b=1, lru_b_a=2, lru_b_x=2, lru_lambda=1)
REPLICATED = [n for n in WEIGHTS if n not in SHARDED_BIG and n not in SHARDED_SMALL]
PACK_LANES = 128
PACK_ROW_ALIGN = 16
REPLICATED_ROW_TILE = 256


def _unblock(g, d):
    full = jnp.moveaxis(g, 0, d)
    return full.reshape(full.shape[:d] + (full.shape[d] * full.shape[d + 1],) + full.shape[d + 2:])


def _block(full, d):
    s = full.shape[d] // N_CHIPS
    return jnp.moveaxis(full.reshape(full.shape[:d] + (N_CHIPS, s) + full.shape[d + 1:]), d, 0)


def _pack(arrays, lead=(), row_align=PACK_ROW_ALIGN):
    nl = len(lead)
    flat = jnp.concatenate([a.reshape(lead + (-1,)) for a in arrays], axis=nl)
    quantum = PACK_LANES * row_align
    pad = (-flat.shape[nl]) % quantum
    flat = jnp.pad(flat, [(0, 0)] * nl + [(0, pad)])
    return flat.reshape(lead + (-1, PACK_LANES))


def _unpack(packed, shapes, lead=()):
    nl = len(lead)
    flat = packed.reshape(lead + (-1,))
    out, off = [], 0
    for s in shapes:
        size = math.prod(s)
        out.append(lax.slice_in_dim(flat, off, off + size, axis=nl).reshape(lead + tuple(s)))
        off += size
    return out


N_GROUPS = 3
MIXER_KIND = dict(s5=0, sb=1, lru=2)
S5_NAMES = dict(w_in="s5_w_in", lam_re="s5_lam_re", lam_im="s5_lam_im", log_dt="s5_log_dt", b_re="s5_b_re", b_im="s5_b_im",
                c_re="s5_c_re", c_im="s5_c_im", d="s5_d", w_out="s5_w_out")
SB_NAMES = dict(w_qkv="sb_w_qkv", w_out="sb_w_out")
LRU_NAMES = dict(w_in="lru_w_in", conv_w="lru_conv_w", conv_b="lru_conv_b", wa="lru_w_a", ba="lru_b_a", wx="lru_w_x",
                 bx="lru_b_x", lam="lru_lambda", w_out="lru_w_out")


def _pieces_of(name, count):
    kind = None if name.startswith("ffn") else MIXER_KIND[name.split("_")[0]]
    groups = [min(i if kind is None else kind + N_MIXERS * i, N_GROUPS - 1) for i in range(count)]
    runs, lo = [], 0
    for i in range(1, count + 1):
        if i == count or groups[i] != groups[lo]:
            runs.append((lo, i, groups[lo]))
            lo = i
    return runs


class _Sharded:
    def __init__(self, a):
        self.a = a
        self.pieces = {n: _pieces_of(n, a[n].shape[0]) for n in SHARDED_BIG}
        by_group = [[(n, lo, hi) for n in SHARDED_BIG for lo, hi, g in self.pieces[n] if g == grp] for grp in range(N_GROUPS)]
        first = lambda n: n.startswith("ffn1")
        self.by_group = {0: [k for k in by_group[0] if first(k[0])], 1: by_group[1], 2: by_group[2],
                         "0b": [k for k in by_group[0] if not first(k[0])],
                         "0b1": [k for k in by_group[0] if k[0] == "ffn2_w_in"],
                         "0b2": [k for k in by_group[0] if not first(k[0]) and k[0] != "ffn2_w_in"]}
        self.weights = {}
        self.small = {}
        self.grads = {n: [None] * a[n].shape[0] for n in SHARDED_BIG}
        self.small_grads = {}
        self.received = {}
        self.small_received = None

    def gather_job(self, grp):
        arrays = [self.a[n][lo:hi].astype(BF16) for n, lo, hi in self.by_group[grp]]
        if grp == 0:
            arrays.append(_pack([self.a[n] for n in SHARDED_SMALL]))
        return gather_chips(arrays)

    def landed(self, grp, gathered):
        for (n, lo, _), g in zip(self.by_group[grp], gathered):
            self.weights[(n, lo)] = _unblock(g, SHARDED_BIG[n])
        if grp == 0:
            blocks = _unpack(gathered[-1], [self.a[n].shape for n in SHARDED_SMALL], lead=(N_CHIPS,))
            self.small = {n: _unblock(b, d) for (n, d), b in zip(SHARDED_SMALL.items(), blocks)}

    def weight(self, name, idx):
        lo = next(lo for lo, hi, _ in self.pieces[name] if lo <= idx < hi)
        return self.weights[(name, lo)][idx - lo]

    def piece_grad(self, n, lo, hi):
        return jnp.stack(self.grads[n][lo:hi])

    def scatter_job(self, grp):
        arrays = [_block(self.piece_grad(n, lo, hi), SHARDED_BIG[n]).astype(BF16) for n, lo, hi in self.by_group[grp]]
        if grp == 0:
            arrays.append(_pack([_block(self.small_grads[n], d) for n, d in SHARDED_SMALL.items()], lead=(N_CHIPS,)))
        return scatter_chips(arrays)

    def arrived(self, grp, received):
        for (n, lo, _), r in zip(self.by_group[grp], received):
            self.received[(n, lo)] = r
        if grp == 0:
            self.small_received = received[-1]


def _forward_backward(x, target, a, sh):
    depth = a["ffn1_norm"].shape[0]

    def mixer_params(layer):
        kind, j = layer % N_MIXERS, layer // N_MIXERS
        if kind == 0:
            return kind, j, dict(w_in=sh.weight("s5_w_in", j), lam_re=a["s5_lam_re"][j], lam_im=a["s5_lam_im"][j],
                                 log_dt=a["s5_log_dt"][j], b_re=a["s5_b_re"][j], b_im=a["s5_b_im"][j], c_re=a["s5_c_re"][j],
                                 c_im=a["s5_c_im"][j], d=sh.small["s5_d"][j].reshape(1, -1), w_out=sh.weight("s5_w_out", j))
        if kind == 1:
            return kind, j, dict(w_qkv=sh.weight("sb_w_qkv", j), w_out=sh.weight("sb_w_out", j))
        sm = sh.small
        return kind, j, dict(w_in=sh.weight("lru_w_in", j), conv_w=sm["lru_conv_w"][j], conv_b=sm["lru_conv_b"][j].reshape(1, -1),
                             wa=sh.weight("lru_w_a", j), ba=sm["lru_b_a"][j].reshape(1, -1), wx=sh.weight("lru_w_x", j),
                             bx=sm["lru_b_x"][j].reshape(1, -1), lam=sm["lru_lambda"][j].reshape(1, -1),
                             w_out=sh.weight("lru_w_out", j))

    def ffn_weights(which, layer):
        return sh.weight(f"{which}_w_in", layer), sh.weight(f"{which}_w_out", layer)

    sh.landed(0, run_exchange(sh.gather_job(0), "gather_chips"))
    h = x
    tape = []
    for layer in range(depth):
        job = sh.gather_job("0b") if layer == 0 else None
        h, s1, got = ffn_fwd(h, a["ffn1_norm"][layer], *ffn_weights("ffn1", layer), "ffn", job)
        if job is not None:
            sh.landed("0b", got)
        kind, j, p = mixer_params(layer)
        h_mix_in = h
        hn = rms_fwd(h, a["mix_norm"][layer], "mix_norm")
        job = sh.gather_job(layer + 1) if layer + 1 < N_GROUPS else None
        if kind == 0:
            h, sm, got = s5_fwd(hn, h, p, job)
        elif kind == 1:
            o_flat, sm, got = sb_fwd(hn, p, job)
            h = mm(o_flat, p["w_out"], name="mix_out", extras=(h,), epilogue=lambda acc, res: res + acc)
        else:
            assert job is None
            y, sm = lru_fwd(hn, p)
            h = mm(y, p["w_out"], name="mix_out", extras=(h,), epilogue=lambda acc, res: res + acc)
        if job is not None:
            sh.landed(layer + 1, got)
        h, s2, _ = ffn_fwd(h, a["ffn2_norm"][layer], *ffn_weights("ffn2", layer), "ffn")
        tape.append((s1, h_mix_in, sm, s2))

    loss, dh, g_final = loss_fwd_bwd(h, a["final_norm"], target)

    norm_grads = {n: [None] * depth for n in ("ffn1_norm", "mix_norm", "ffn2_norm")}
    mix = {}
    for layer in reversed(range(depth)):
        s1, h_mix_in, sm, s2 = tape[layer]
        dh, dg, dwi, dwo, _, _ = ffn_bwd(dh, s2, a["ffn2_norm"][layer], *ffn_weights("ffn2", layer), "ffn")
        norm_grads["ffn2_norm"][layer], sh.grads["ffn2_w_in"][layer], sh.grads["ffn2_w_out"][layer] = dg, dwi, dwo
        kind, j, p = mixer_params(layer)
        job = sh.scatter_job(layer + 1) if layer + 1 < N_GROUPS else None
        if kind == 0:
            (da, w_first), g, got = s5_bwd(dh, sm, p, job)
            names = S5_NAMES
        elif kind == 1:
            (da, w_first), g, got = sb_bwd(dh, sm, p, job)
            names = SB_NAMES
        else:
            assert job is None
            (da, w_first), g = lru_bwd(dh, sm, p)
            names = LRU_NAMES
        if job is not None:
            sh.arrived(layer + 1, got)
        for k, full_name in names.items():
            if full_name in SHARDED_BIG:
                sh.grads[full_name][j] = g[k].reshape(sh.weight(full_name, j).shape)
            else:
                mix.setdefault(full_name, {})[j] = g[k].reshape(a[full_name].shape[1:-1] + (-1,))
        dh, dg = norm_input_bwd(da, _rows_spec, 1, w_first, h_mix_in, a["mix_norm"][layer], dh, "mix_dhn")
        norm_grads["mix_norm"][layer] = dg
        jobs = (sh.scatter_job("0b1"), sh.scatter_job("0b2")) if layer == 0 else (None, None)
        dh, dg, dwi, dwo, got1, got2 = ffn_bwd(dh, s1, a["ffn1_norm"][layer], *ffn_weights("ffn1", layer), "ffn", jobs)
        if layer == 0:
            sh.arrived("0b1", got1)
            sh.arrived("0b2", got2)
        norm_grads["ffn1_norm"][layer], sh.grads["ffn1_w_in"][layer], sh.grads["ffn1_w_out"][layer] = dg, dwi, dwo

    grads = {n: jnp.stack(v) for n, v in norm_grads.items()}
    grads["final_norm"] = g_final
    for n, by_j in mix.items():
        stacked = jnp.stack([by_j[j] for j in range(len(by_j))])
        if n in SHARDED_SMALL:
            sh.small_grads[n] = stacked
        else:
            grads[n] = stacked
    rep_job = gather_devices(_pack([grads[n] for n in REPLICATED], row_align=REPLICATED_ROW_TILE))
    *got, rep_all = run_exchange(both(sh.scatter_job(0), rep_job), "scatter_chips")
    sh.arrived(0, got)
    return loss, dh, grads, rep_all


def kernel(x, ffn1_norm, ffn1_w_in, ffn1_w_out, mix_norm, ffn2_norm, ffn2_w_in, ffn2_w_out, final_norm, s5_w_in,
           s5_lam_re, s5_lam_im, s5_log_dt, s5_b_re, s5_b_im, s5_c_re, s5_c_im, s5_d, s5_w_out, sb_w_qkv,
           sb_w_out, lru_w_in, lru_conv_w, lru_conv_b, lru_w_a, lru_b_a, lru_w_x, lru_b_x, lru_lambda,
           lru_w_out, loss_target, m_ffn1_norm, m_ffn1_w_in, m_ffn1_w_out, m_mix_norm, m_ffn2_norm,
           m_ffn2_w_in, m_ffn2_w_out, m_final_norm, m_s5_w_in, m_s5_lam_re, m_s5_lam_im, m_s5_log_dt,
           m_s5_b_re, m_s5_b_im, m_s5_c_re, m_s5_c_im, m_s5_d, m_s5_w_out, m_sb_w_qkv, m_sb_w_out, m_lru_w_in,
           m_lru_conv_w, m_lru_conv_b, m_lru_w_a, m_lru_b_a, m_lru_w_x, m_lru_b_x, m_lru_lambda, m_lru_w_out,
           v_ffn1_norm, v_ffn1_w_in, v_ffn1_w_out, v_mix_norm, v_ffn2_norm, v_ffn2_w_in, v_ffn2_w_out,
           v_final_norm, v_s5_w_in, v_s5_lam_re, v_s5_lam_im, v_s5_log_dt, v_s5_b_re, v_s5_b_im, v_s5_c_re,
           v_s5_c_im, v_s5_d, v_s5_w_out, v_sb_w_qkv, v_sb_w_out, v_lru_w_in, v_lru_conv_w, v_lru_conv_b,
           v_lru_w_a, v_lru_b_a, v_lru_w_x, v_lru_b_x, v_lru_lambda, v_lru_w_out):
    a = dict(locals())
    assert list(a) == INPUTS
    x, y, c, _ = _place()
    chip = 2 * x + y
    everyone = ("x", "y", "c")

    sh = _Sharded(a)
    loss, dx, grads, rep_all = _forward_backward(a["x"][0], a["loss_target"][0], a, sh)
    loss = lax.psum(loss, everyone)

    small = list(SHARDED_SMALL)

    def sum_chips(mine, got):
        rows = _rows2d(mine)
        return rowwise(lambda o, r: ((o + r[0].astype(F32)) + r[1].astype(F32)) + r[2].astype(F32),
                       [rows, got.reshape((3,) + rows.shape)], [], [(rows.shape[1], F32)], name="sum_chips")

    def own_block(full, name, d):
        return lax.dynamic_slice_in_dim(full, chip * a[name].shape[d], a[name].shape[d], axis=d)

    keys = [(n, lo, hi) for n in SHARDED_BIG for lo, hi, _ in sh.pieces[n]]
    partial = [sum_chips(own_block(sh.piece_grad(n, lo, hi), n, SHARDED_BIG[n]), sh.received[(n, lo)]) for n, lo, hi in keys]
    partial.append(sum_chips(_pack([own_block(sh.small_grads[n], n, d) for n, d in SHARDED_SMALL.items()]), sh.small_received))
    other = swap_cores(partial)

    def adam_sharded(wv, ga, gb, m, v):
        shape = wv.shape
        res = rowwise(lambda w_, a_, b_, mm_, vv_: (a_ + b_,) + _adamw(w_, a_ + b_, mm_, vv_),
                      [_rows2d(wv), ga, gb, _rows2d(m), _rows2d(v)], [], [(shape[-1], F32)] * 4, name="adamw_sharded")
        return [r.reshape(shape) for r in res]

    out_grad, out_delta, out_m, out_v = {}, {}, {}, {}
    for n in SHARDED_BIG:
        at = [i for i, k in enumerate(keys) if k[0] == n]
        out_grad[n], out_delta[n], out_m[n], out_v[n] = adam_sharded(a[n], [partial[i] for i in at], [other[i] for i in at],
                                                                     a["m_" + n], a["v_" + n])
    small_shapes = [a[n].shape for n in small]
    sp = [_pack([a[pre + n] for n in small]) for pre in ("", "m_", "v_")]
    g_, d_, m_, v_ = adam_sharded(sp[0], [partial[-1]], [other[-1]], sp[1], sp[2])
    for n, gg, dd, mm_, vv in zip(small, _unpack(g_, small_shapes), _unpack(d_, small_shapes), _unpack(m_, small_shapes),
                                  _unpack(v_, small_shapes)):
        out_grad[n], out_delta[n], out_m[n], out_v[n] = gg, dd, mm_, vv

    rep_shapes = [a[n].shape for n in REPLICATED]
    rp = [_pack([a[pre + n] for n in REPLICATED], row_align=REPLICATED_ROW_TILE) for pre in ("", "m_", "v_")]

    def adam_rep(w_, g8, mm_, vv_):
        g = g8[0]
        for k in range(1, N_DEVICES):
            g = g + g8[k]
        return (g,) + _adamw(w_, g, mm_, vv_)

    g_, d_, m_, v_ = rowwise(adam_rep, [rp[0], rep_all, rp[1], rp[2]], [], [(PACK_LANES, F32)] * 4, name="adamw_replicated")
    for n, gg, dd, mm_, vv in zip(REPLICATED, _unpack(g_, rep_shapes), _unpack(d_, rep_shapes), _unpack(m_, rep_shapes),
                                  _unpack(v_, rep_shapes)):
        out_grad[n], out_delta[n], out_m[n], out_v[n] = gg, dd, mm_, vv

    return (loss, dx[None], *[out_grad[n] for n in WEIGHTS], *[out_delta[n] for n in WEIGHTS], *[out_m[n] for n in WEIGHTS],
            *[out_v[n] for n in WEIGHTS])
```
